```python
import jax, jax.numpy as jnp
from jax import lax
import numpy as np

D_MODEL = 1024
BATCH = 4
SEQ = 4096
DEPTH = 2

CHUNK = 128
EPS = 1e-6
A_GROUPS = 4
A_GROUP_DIM = 128
A_WIDTH = A_GROUPS * A_GROUP_DIM
B_HEADS = 4
B_KEY_DIM = 64
B_VAL_DIM = 128
B_QK_WIDTH = B_HEADS * B_KEY_DIM
B_V_WIDTH = B_HEADS * B_VAL_DIM
B_GATE_RANK = 16
B_GATE_NORMALIZER = 16.0
EVEN_SPLITS = (A_WIDTH, A_WIDTH, B_QK_WIDTH, B_QK_WIDTH, B_GATE_RANK, B_V_WIDTH, B_V_WIDTH)
EVEN_IN = sum(EVEN_SPLITS)
EVEN_MIX_WIDTH = A_WIDTH + B_V_WIDTH
C_HEADS = 16
C_HEAD_DIM = 64
C_WIDTH = C_HEADS * C_HEAD_DIM
ODD_SPLITS = (C_WIDTH, C_WIDTH, C_WIDTH, C_WIDTH, C_HEADS)
ODD_IN = sum(ODD_SPLITS)
D_FF_DENSE = 2816
N_EXPERTS = 8
TOP_K = 2
D_FF_EXPERT = 3584
N_EVEN = (DEPTH + 1) // 2
N_ODD = DEPTH // 2

kernel_name = "hybrid_gmlp_gla_fox_moe_trunk"


def _split(z, sizes):
    idx = np.cumsum(np.array(sizes))[:-1].tolist()
    return jnp.split(z, idx, axis=-1)


def rmsnorm(x, g):
    xf = x.astype(jnp.float32)
    y = xf * lax.rsqrt(jnp.mean(xf * xf, axis=-1, keepdims=True) + EPS)
    return (y * g.astype(jnp.float32)).astype(x.dtype)


def layernorm(x, g, b):
    xf = x.astype(jnp.float32)
    mu = jnp.mean(xf, axis=-1, keepdims=True)
    xc = xf - mu
    y = xc * lax.rsqrt(jnp.mean(xc * xc, axis=-1, keepdims=True) + EPS)
    return (y * g.astype(jnp.float32) + b.astype(jnp.float32)).astype(x.dtype)


def swiglu(h, w1, w3, w2):
    return (jax.nn.silu(h @ w1) * (h @ w3)) @ w2


def chunk_spatial_gating(u, v, w_s, b_s):
    bsz, seq, _ = u.shape
    n = seq // CHUNK
    mask = jnp.tril(jnp.ones((CHUNK, CHUNK), dtype=bool))
    w = jnp.where(mask[None], w_s, jnp.zeros_like(w_s))
    v5 = v.reshape(bsz, n, CHUNK, A_GROUPS, A_GROUP_DIM)
    mixed = jnp.einsum('gts,bnsgd->bntgd', w, v5) + b_s.T[None, None, :, :, None]
    return (u.reshape(bsz, n, CHUNK, A_GROUPS, A_GROUP_DIM) * mixed).reshape(bsz, seq, A_WIDTH)


def gla_chunked(q, k, v, log_a):
    bsz, seq, nh, dk = q.shape
    dv = v.shape[-1]
    n = seq // CHUNK
    def to_chunks(t):
        return jnp.swapaxes(t.reshape(bsz, n, CHUNK, nh, t.shape[-1]), 0, 1)
    g = jnp.cumsum(log_a.astype(jnp.float32).reshape(bsz, n, CHUNK, nh, dk), axis=2)
    g = jnp.swapaxes(g, 0, 1)
    causal = jnp.tril(jnp.ones((CHUNK, CHUNK), dtype=bool))

    def step(state, inp):
        qc, kc, vc, gc = inp
        qf = qc.astype(jnp.float32)
        kf = kc.astype(jnp.float32)
        vf = vc.astype(jnp.float32)
        o_inter = jnp.einsum('bthk,bhkv->bthv', qf * jnp.exp(gc), state)
        diff = gc[:, :, None] - gc[:, None, :]
        diff = jnp.where(causal[None, :, :, None, None], diff, -jnp.inf)
        attn = jnp.einsum('bthk,bshk,btshk->btsh', qf, kf, jnp.exp(diff))
        o_intra = jnp.einsum('btsh,bshv->bthv', attn, vf)
        g_last = gc[:, -1]
        k_dec = kf * jnp.exp(g_last[:, None] - gc)
        new_state = jnp.exp(g_last)[..., None] * state + jnp.einsum('bshk,bshv->bhkv', k_dec, vf)
        return new_state, o_inter + o_intra

    state0 = jnp.zeros((bsz, nh, dk, dv), jnp.float32)
    _, out = lax.scan(step, state0, (to_chunks(q), to_chunks(k), to_chunks(v), g))
    return jnp.swapaxes(out, 0, 1).reshape(bsz, seq, nh, dv).astype(v.dtype)


def forgetting_attention(q, k, v, log_f):
    bsz, seq, nh, hd = q.shape
    n = seq // CHUNK
    scale = hd ** -0.5
    c = jnp.cumsum(log_f, axis=1)
    c_key = jnp.swapaxes(c, 1, 2)
    q_blocks = jnp.swapaxes(q.reshape(bsz, n, CHUNK, nh, hd), 0, 1)
    c_blocks = jnp.swapaxes(c.reshape(bsz, n, CHUNK, nh), 0, 1)
    s_pos = jnp.arange(seq)

    def block(args):
        qb, cb, bi = args
        logits = jnp.einsum('bthd,bshd->bhts', qb, k).astype(jnp.float32) * scale
        logits = logits + jnp.swapaxes(cb, 1, 2)[..., None] - c_key[:, :, None, :]
        t_pos = bi * CHUNK + jnp.arange(CHUNK)
        mask = s_pos[None, :] <= t_pos[:, None]
        logits = jnp.where(mask[None, None], logits, -jnp.inf)
        p = jax.nn.softmax(logits, axis=-1)
        return jnp.einsum('bhts,bshd->bthd', p.astype(v.dtype), v)

    out = lax.map(block, (q_blocks, c_blocks, jnp.arange(n)))
    return jnp.swapaxes(out, 0, 1).reshape(bsz, seq, nh * hd)


def mixer_even(h, w_in, gate_up, gate_bias, w_s, b_s, ln_g, ln_b, head_g, w_o):
    bsz, seq, _ = h.shape
    z = h @ w_in
    u, v, q, k, g_lr, vb, og = _split(z, EVEN_SPLITS)
    u = jax.nn.gelu(u)
    v = layernorm(jax.nn.gelu(v), ln_g, ln_b)
    a_out = chunk_spatial_gating(u, v, w_s, b_s)
    gate_logit = (g_lr @ gate_up + gate_bias).astype(jnp.float32)
    log_a = jax.nn.log_sigmoid(gate_logit) / B_GATE_NORMALIZER
    qh = q.reshape(bsz, seq, B_HEADS, B_KEY_DIM) * (B_KEY_DIM ** -0.5)
    kh = k.reshape(bsz, seq, B_HEADS, B_KEY_DIM)
    vh = vb.reshape(bsz, seq, B_HEADS, B_VAL_DIM)
    o = gla_chunked(qh, kh, vh, log_a.reshape(bsz, seq, B_HEADS, B_KEY_DIM))
    o = rmsnorm(o, head_g) * jax.nn.silu(og.reshape(bsz, seq, B_HEADS, B_VAL_DIM))
    b_out = o.reshape(bsz, seq, B_V_WIDTH)
    return jnp.concatenate([a_out, b_out], axis=-1) @ w_o


def mixer_odd(h, w_in, forget_bias, q_g, k_g, w_o):
    bsz, seq, _ = h.shape
    z = h @ w_in
    q, k, v, og, f = _split(z, ODD_SPLITS)
    q = rmsnorm(q.reshape(bsz, seq, C_HEADS, C_HEAD_DIM), q_g)
    k = rmsnorm(k.reshape(bsz, seq, C_HEADS, C_HEAD_DIM), k_g)
    v = v.reshape(bsz, seq, C_HEADS, C_HEAD_DIM)
    log_f = jax.nn.log_sigmoid(f.astype(jnp.float32) + forget_bias.astype(jnp.float32))
    o = forgetting_attention(q, k, v, log_f)
    o = o * jax.nn.sigmoid(og)
    return o @ w_o


def moe_swiglu(h, router, w1, w3, w2):
    logits = (h @ router).astype(jnp.float32)
    top_vals, top_idx = lax.top_k(logits, TOP_K)
    top_w = jax.nn.softmax(top_vals, axis=-1)
    gates = jnp.sum(jax.nn.one_hot(top_idx, N_EXPERTS, dtype=jnp.float32) * top_w[..., None], axis=-2)
    gates = gates.astype(h.dtype)
    out = jnp.zeros_like(h)
    for e in range(N_EXPERTS):
        out = out + gates[..., e:e + 1] * swiglu(h, w1[e], w3[e], w2[e])
    return out


def setup_inputs(seed: int = 0) -> dict:
    key = jax.random.key(seed)
    ks = iter(jax.random.split(key, 32))
    f32 = jnp.float32
    def nrm(shape, scale):
        return jax.random.normal(next(ks), shape, f32) * scale
    def gain(shape):
        return 1.0 + 0.05 * jax.random.normal(next(ks), shape, f32)
    d = D_MODEL
    return {
        "x": jax.random.normal(next(ks), (BATCH, SEQ, d), f32),
        "even_norm_mix": gain((N_EVEN, d)),
        "even_w_in": nrm((N_EVEN, d, EVEN_IN), d ** -0.5),
        "even_gate_up": nrm((N_EVEN, B_GATE_RANK, B_QK_WIDTH), B_GATE_RANK ** -0.5),
        "even_gate_bias": nrm((N_EVEN, B_QK_WIDTH), 0.1),
        "even_w_s": nrm((N_EVEN, A_GROUPS, CHUNK, CHUNK), CHUNK ** -0.5),
        "even_b_s": 1.0 + 0.1 * jax.random.normal(next(ks), (N_EVEN, A_GROUPS, CHUNK), f32),
        "even_ln_g": gain((N_EVEN, A_WIDTH)),
        "even_ln_b": nrm((N_EVEN, A_WIDTH), 0.02),
        "even_head_g": gain((N_EVEN, B_HEADS, B_VAL_DIM)),
        "even_w_o": nrm((N_EVEN, EVEN_MIX_WIDTH, d), EVEN_MIX_WIDTH ** -0.5),
        "even_norm_ffn": gain((N_EVEN, d)),
        "even_ffn_w1": nrm((N_EVEN, d, D_FF_DENSE), d ** -0.5),
        "even_ffn_w3": nrm((N_EVEN, d, D_FF_DENSE), d ** -0.5),
        "even_ffn_w2": nrm((N_EVEN, D_FF_DENSE, d), D_FF_DENSE ** -0.5),
        "odd_norm_mix": gain((N_ODD, d)),
        "odd_w_in": nrm((N_ODD, d, ODD_IN), d ** -0.5),
        "odd_forget_bias": 2.0 + 0.5 * jax.random.normal(next(ks), (N_ODD, C_HEADS), f32),
        "odd_q_g": gain((N_ODD, C_HEAD_DIM)),
        "odd_k_g": gain((N_ODD, C_HEAD_DIM)),
        "odd_w_o": nrm((N_ODD, C_WIDTH, d), C_WIDTH ** -0.5),
        "odd_norm_ffn": gain((N_ODD, d)),
        "odd_router": nrm((N_ODD, d, N_EXPERTS), d ** -0.5),
        "odd_exp_w1": nrm((N_ODD, N_EXPERTS, d, D_FF_EXPERT), d ** -0.5),
        "odd_exp_w3": nrm((N_ODD, N_EXPERTS, d, D_FF_EXPERT), d ** -0.5),
        "odd_exp_w2": nrm((N_ODD, N_EXPERTS, D_FF_EXPERT, d), D_FF_EXPERT ** -0.5),
        "final_norm": gain((d,)),
    }


def reference(x, even_norm_mix, even_w_in, even_gate_up, even_gate_bias, even_w_s, even_b_s,
              even_ln_g, even_ln_b, even_head_g, even_w_o, even_norm_ffn, even_ffn_w1,
              even_ffn_w3, even_ffn_w2, odd_norm_mix, odd_w_in, odd_forget_bias, odd_q_g,
              odd_k_g, odd_w_o, odd_norm_ffn, odd_router, odd_exp_w1, odd_exp_w3,
              odd_exp_w2, final_norm):
    for i in range(DEPTH):
        j = i // 2
        if i % 2 == 0:
            h = rmsnorm(x, even_norm_mix[j])
            x = x + mixer_even(h, even_w_in[j], even_gate_up[j], even_gate_bias[j], even_w_s[j],
                               even_b_s[j], even_ln_g[j], even_ln_b[j], even_head_g[j], even_w_o[j])
            h = rmsnorm(x, even_norm_ffn[j])
            x = x + swiglu(h, even_ffn_w1[j], even_ffn_w3[j], even_ffn_w2[j])
        else:
            h = rmsnorm(x, odd_norm_mix[j])
            x = x + mixer_odd(h, odd_w_in[j], odd_forget_bias[j], odd_q_g[j], odd_k_g[j], odd_w_o[j])
            h = rmsnorm(x, odd_norm_ffn[j])
            x = x + moe_swiglu(h, odd_router[j], odd_exp_w1[j], odd_exp_w3[j], odd_exp_w2[j])
    return rmsnorm(x, final_norm)
```

```python
import functools
import math

import jax
import jax.numpy as jnp
from jax import lax
from jax.experimental import pallas as pl
from jax.experimental.pallas import tpu as pltpu

F32 = jnp.float32
BF16 = jnp.bfloat16
HIGHEST = lax.Precision.HIGHEST

EPS = 1e-6
D_MODEL = 1024
CHUNK = 128
SUB = 32
N_SUB = CHUNK // SUB
A_GROUPS = 4
A_WIDTH = 512
B_HEADS = 4
B_KEY_DIM = 64
B_VAL_DIM = 128
B_QK_WIDTH = 256
B_V_WIDTH = 512
B_GATE_RANK = 16
B_GATE_NORMALIZER = 16.0
C_HEADS = 16
C_HEAD_DIM = 64
C_WIDTH = 1024
D_FF_DENSE = 2816
N_EXPERTS = 8
D_FF_EXPERT = 3584
LANES = 128
MAX_DECAY_EXP = 60.0

E_U, E_V, E_Q, E_K, E_VB, E_OG, E_G, E_END = 0, 512, 1024, 1280, 1536, 2048, 2560, 2688
O_Q, O_K, O_V, O_OG, O_F, O_END = 0, 1024, 2048, 3072, 4096, 4224

VMEM_LIMIT = 56 * 1024 * 1024


def _rms(x, g):
    ms = jnp.mean(x * x, axis=-1, keepdims=True)
    return x * lax.rsqrt(ms + EPS) * g


def _gelu_tanh(x):
    c = math.sqrt(2.0 / math.pi)
    return x * (0.5 * (1.0 + jnp.tanh(c * (x + 0.044715 * (x * x * x)))))


def _sigmoid(x):
    return 1.0 / (1.0 + jnp.exp(-x))


def _log_sigmoid(x):
    return jnp.minimum(x, 0.0) - jnp.log(1.0 + jnp.exp(-jnp.abs(x)))


def _dot(a, b):
    return jnp.dot(a, b, preferred_element_type=F32)


def _dot_nt(a, b):
    return lax.dot_general(a, b, (((1,), (1,)), ((), ())), preferred_element_type=F32)


def _split3(x):
    hi = x.astype(BF16)
    r1 = x - hi.astype(F32)
    mid = r1.astype(BF16)
    lo = (r1 - mid.astype(F32)).astype(BF16)
    return hi, mid, lo


def _cumsum_rows(tril_b, x):
    hi, mid, lo = _split3(x)
    return _dot(tril_b, hi) + _dot(tril_b, mid) + _dot(tril_b, lo)


def _const_spec(shape):
    nd = len(shape)
    return pl.BlockSpec(shape, lambda *_: (0,) * nd)


def _even_mixer_kernel(x_ref, nrm_ref, win_ref, gup_ref, gb_ref, ws_ref, bs_ref, lng_ref, lnb_ref,
                       hg_ref, wo_ref, o_ref, z_ref, mix_ref, st_ref, *, tiles_per_batch, n_chunks):
    i = pl.program_id(0)

    @pl.when(i % tiles_per_batch == 0)
    def _():
        st_ref[...] = jnp.zeros_like(st_ref)

    h = _rms(x_ref[...], nrm_ref[...]).astype(BF16)
    z_ref[...] = _dot(h, win_ref[...])

    row = lax.broadcasted_iota(jnp.int32, (CHUNK, CHUNK), 0)
    col = lax.broadcasted_iota(jnp.int32, (CHUNK, CHUNK), 1)
    tril_b = (col <= row).astype(BF16)
    sub_row = row & (SUB - 1)
    head_lane = lax.broadcasted_iota(jnp.int32, (1, B_QK_WIDTH), 1) // B_KEY_DIM
    bd_mask = (lax.broadcasted_iota(jnp.int32, (B_V_WIDTH, B_QK_WIDTH), 0) // B_VAL_DIM
               == lax.broadcasted_iota(jnp.int32, (B_V_WIDTH, B_QK_WIDTH), 1) // B_KEY_DIM)

    def chunk_body(c, carry):
        rows = pl.ds(pl.multiple_of(c * CHUNK, CHUNK), CHUNK)

        u = _gelu_tanh(z_ref[rows, E_U:E_V])
        v = _gelu_tanh(z_ref[rows, E_V:E_Q])
        mu = jnp.mean(v, axis=-1, keepdims=True)
        vc = v - mu
        var = jnp.mean(vc * vc, axis=-1, keepdims=True)
        vln = (vc * lax.rsqrt(var + EPS) * lng_ref[...] + lnb_ref[...]).astype(BF16)
        for g in range(A_GROUPS):
            sl = slice(g * LANES, (g + 1) * LANES)
            mixed = _dot(ws_ref[g], vln[:, sl]) + bs_ref[g]
            mix_ref[rows, sl] = (u[:, sl] * mixed).astype(BF16)

        q = z_ref[rows, E_Q:E_K] * (B_KEY_DIM ** -0.5)
        k = z_ref[rows, E_K:E_VB]
        vb = z_ref[rows, E_VB:E_OG]
        og = z_ref[rows, E_OG:E_G]
        glr = z_ref[rows, E_G:E_END].astype(BF16)
        logit = _dot(glr, gup_ref[...]) + gb_ref[...]
        log_a = _log_sigmoid(logit) * (1.0 / B_GATE_NORMALIZER)
        g_cum = _cumsum_rows(tril_b, log_a)
        g_last = g_cum[CHUNK - 1:CHUNK, :]
        st = st_ref[...]
        o = _dot_nt((q * jnp.exp(g_cum)).astype(BF16), st.astype(BF16))

        p_rows = [[None] * N_SUB for _ in range(B_HEADS)]
        for s in range(N_SUB):
            gs = g_cum[s * SUB:(s + 1) * SUB, :]
            if s == 0:
                qt = q[0:SUB, :] * jnp.exp(gs)
                kt = k * jnp.exp(jnp.minimum(-g_cum, MAX_DECAY_EXP))
            else:
                ref_g = g_cum[s * SUB - 1:s * SUB, :]
                qt = q[s * SUB:(s + 1) * SUB, :] * jnp.exp(gs - ref_g)
                kt = k * jnp.exp(jnp.minimum(ref_g - g_cum, MAX_DECAY_EXP))
            qs = jnp.concatenate([jnp.where(head_lane == hh, qt, 0.0) for hh in range(B_HEADS)],
                                 axis=0).astype(BF16)
            sc = _dot_nt(qs, kt.astype(BF16))
            sc = jnp.where(col <= (s * SUB + sub_row), sc, 0.0)
            for hh in range(B_HEADS):
                p_rows[hh][s] = sc[hh * SUB:(hh + 1) * SUB, :]

        vb_b = vb.astype(BF16)
        for hh in range(B_HEADS):
            sl = slice(hh * B_VAL_DIM, (hh + 1) * B_VAL_DIM)
            ph = jnp.concatenate(p_rows[hh], axis=0).astype(BF16)
            oh = o[:, sl] + _dot(ph, vb_b[:, sl])
            on = _rms(oh, hg_ref[hh])
            ogh = og[:, sl]
            mix_ref[rows, A_WIDTH + hh * B_VAL_DIM:A_WIDTH + (hh + 1) * B_VAL_DIM] = (
                on * (ogh * _sigmoid(ogh))).astype(BF16)

        k_dec = (k * jnp.exp(g_last - g_cum)).astype(BF16)
        upd = _dot(vb.T.astype(BF16), k_dec)
        st_ref[...] = jnp.exp(g_last) * st + jnp.where(bd_mask, upd, 0.0)
        return carry

    lax.fori_loop(0, n_chunks, chunk_body, 0)
    o_ref[...] = x_ref[...] + _dot(mix_ref[...], wo_ref[...])


def _even_mixer(x, nrm, win, gup, gb, ws, bs, lng, lnb, hg, wo, *, seq, tm):
    t = x.shape[0]
    kern = functools.partial(_even_mixer_kernel, tiles_per_batch=seq // tm, n_chunks=tm // CHUNK)
    return pl.pallas_call(
        kern,
        out_shape=jax.ShapeDtypeStruct((t, D_MODEL), F32),
        grid=(t // tm,),
        in_specs=[
            pl.BlockSpec((tm, D_MODEL), lambda i: (i, 0)),
            _const_spec(nrm.shape), _const_spec(win.shape), _const_spec(gup.shape), _const_spec(gb.shape),
            _const_spec(ws.shape), _const_spec(bs.shape), _const_spec(lng.shape), _const_spec(lnb.shape),
            _const_spec(hg.shape), _const_spec(wo.shape),
        ],
        out_specs=pl.BlockSpec((tm, D_MODEL), lambda i: (i, 0)),
        scratch_shapes=[
            pltpu.VMEM((tm, E_END), F32),
            pltpu.VMEM((tm, D_MODEL), BF16),
            pltpu.VMEM((B_V_WIDTH, B_QK_WIDTH), F32),
        ],
        compiler_params=pltpu.CompilerParams(dimension_semantics=("arbitrary",), vmem_limit_bytes=VMEM_LIMIT),
        name="even_mixer",
    )(x, nrm, win, gup, gb, ws, bs, lng, lnb, hg, wo)


def _dense_ffn_kernel(x_ref, nrm_ref, w1_ref, w3_ref, w2_ref, o_ref, acc_ref):
    kf = pl.program_id(1)
    h = _rms(x_ref[...], nrm_ref[...]).astype(BF16)
    a = _dot(h, w1_ref[...])
    b = _dot(h, w3_ref[...])
    part = _dot((a * _sigmoid(a) * b).astype(BF16), w2_ref[...])

    @pl.when(kf == 0)
    def _():
        acc_ref[...] = x_ref[...] + part

    @pl.when(kf > 0)
    def _():
        acc_ref[...] += part

    @pl.when(kf == pl.num_programs(1) - 1)
    def _():
        o_ref[...] = acc_ref[...]


def _dense_ffn(x, nrm, w1, w3, w2, *, tm, tf):
    t = x.shape[0]
    dff = w1.shape[1]
    return pl.pallas_call(
        _dense_ffn_kernel,
        out_shape=jax.ShapeDtypeStruct((t, D_MODEL), F32),
        grid=(t // tm, dff // tf),
        in_specs=[
            pl.BlockSpec((tm, D_MODEL), lambda i, k: (i, 0)),
            pl.BlockSpec((1, D_MODEL), lambda i, k: (0, 0)),
            pl.BlockSpec((D_MODEL, tf), lambda i, k: (0, k)),
            pl.BlockSpec((D_MODEL, tf), lambda i, k: (0, k)),
            pl.BlockSpec((tf, D_MODEL), lambda i, k: (k, 0)),
        ],
        out_specs=pl.BlockSpec((tm, D_MODEL), lambda i, k: (i, 0)),
        scratch_shapes=[pltpu.VMEM((tm, D_MODEL), F32)],
        compiler_params=pltpu.CompilerParams(dimension_semantics=("arbitrary", "arbitrary"),
                                             vmem_limit_bytes=VMEM_LIMIT),
        name="dense_ffn",
    )(x, nrm, w1, w3, w2)


def _head_rms(x, gain):
    lo = lax.broadcasted_iota(jnp.int32, (1, LANES), 1) < C_HEAD_DIM
    outs = []
    for t in range(C_WIDTH // LANES):
        xt = x[:, t * LANES:(t + 1) * LANES]
        sq = xt * xt
        s_lo = jnp.sum(jnp.where(lo, sq, 0.0), axis=-1, keepdims=True)
        s_hi = jnp.sum(jnp.where(lo, 0.0, sq), axis=-1, keepdims=True)
        inv = jnp.where(lo, lax.rsqrt(s_lo * (1.0 / C_HEAD_DIM) + EPS), lax.rsqrt(s_hi * (1.0 / C_HEAD_DIM) + EPS))
        outs.append(xt * inv)
    return jnp.concatenate(outs, axis=-1) * gain


def _odd_inproj_kernel(x_ref, nrm_ref, w_ref, fb_ref, qg_ref, kg_ref,
                       q_ref, k_ref, v_ref, gate_ref, c_ref, z_ref, carry_ref, *, tiles_per_batch, n_chunks):
    i = pl.program_id(0)

    @pl.when(i % tiles_per_batch == 0)
    def _():
        carry_ref[...] = jnp.zeros_like(carry_ref)

    h = _rms(x_ref[...], nrm_ref[...]).astype(BF16)
    z_ref[...] = _dot(h, w_ref[...])
    q_ref[...] = (_head_rms(z_ref[:, O_Q:O_K], qg_ref[...]) * (C_HEAD_DIM ** -0.5)).astype(BF16)
    k_ref[...] = _head_rms(z_ref[:, O_K:O_V], kg_ref[...]).astype(BF16)
    v_ref[...] = z_ref[:, O_V:O_OG].astype(BF16)
    gate_ref[...] = _sigmoid(z_ref[:, O_OG:O_F]).astype(BF16)

    row = lax.broadcasted_iota(jnp.int32, (CHUNK, CHUNK), 0)
    col = lax.broadcasted_iota(jnp.int32, (CHUNK, CHUNK), 1)
    tril_b = (col <= row).astype(BF16)
    carry = carry_ref[...]
    for c in range(n_chunks):
        rows = slice(c * CHUNK, (c + 1) * CHUNK)
        log_f = _log_sigmoid(z_ref[rows, O_F:O_END] + fb_ref[...])
        cs = _cumsum_rows(tril_b, log_f) + carry
        c_ref[rows, :] = cs
        carry = cs[CHUNK - 1:CHUNK, :]
    carry_ref[...] = carry


def _odd_inproj(x, nrm, w, fb, qg, kg, *, seq, tm):
    t = x.shape[0]
    kern = functools.partial(_odd_inproj_kernel, tiles_per_batch=seq // tm, n_chunks=tm // CHUNK)
    row_spec = pl.BlockSpec((tm, C_WIDTH), lambda i: (i, 0))
    return pl.pallas_call(
        kern,
        out_shape=[jax.ShapeDtypeStruct((t, C_WIDTH), BF16)] * 4 + [jax.ShapeDtypeStruct((t, LANES), F32)],
        grid=(t // tm,),
        in_specs=[row_spec, _const_spec(nrm.shape), _const_spec(w.shape), _const_spec(fb.shape),
                  _const_spec(qg.shape), _const_spec(kg.shape)],
        out_specs=[row_spec] * 4 + [pl.BlockSpec((tm, LANES), lambda i: (i, 0))],
        scratch_shapes=[pltpu.VMEM((tm, O_END), F32), pltpu.VMEM((1, LANES), F32)],
        compiler_params=pltpu.CompilerParams(dimension_semantics=("arbitrary",), vmem_limit_bytes=VMEM_LIMIT),
        name="odd_inproj",
    )(x, nrm, w, fb, qg, kg)


NEG_BIG = -1e30


def _fox_kernel(q_ref, k_ref, v_ref, gate_ref, cq_ref, ck_ref, o_ref, *, tq):
    qi = pl.program_id(2)
    q = q_ref[...]
    lo = lax.broadcasted_iota(jnp.int32, (1, LANES), 1) < C_HEAD_DIM
    row = lax.broadcasted_iota(jnp.int32, (tq, tq), 0)
    col = lax.broadcasted_iota(jnp.int32, (tq, tq), 1)
    causal = col <= row
    zero = jnp.zeros_like(q)

    outs = []
    for hh in range(2):
        qm = jnp.where(lo, q, zero) if hh == 0 else jnp.where(lo, zero, q)
        cq = cq_ref[:, hh:hh + 1]

        def block(j, carry, masked):
            m, l, acc = carry
            ks = pl.ds(pl.multiple_of(j * tq, tq), tq)
            s = _dot_nt(qm, k_ref[ks, :]) + (cq - ck_ref[hh:hh + 1, ks])
            if masked:
                s = jnp.where(causal, s, NEG_BIG)
            m_new = jnp.maximum(m, jnp.max(s, axis=-1, keepdims=True))
            p = jnp.exp(s - m_new)
            alpha = jnp.exp(m - m_new)
            l_new = alpha * l + jnp.sum(p, axis=-1, keepdims=True)
            acc_new = alpha * acc + _dot(p.astype(BF16), v_ref[ks, :])
            return m_new, l_new, acc_new

        init = (jnp.full((tq, 1), NEG_BIG, F32), jnp.zeros((tq, 1), F32), jnp.zeros((tq, LANES), F32))
        carry = lax.fori_loop(0, qi, functools.partial(block, masked=False), init)
        m, l, acc = block(qi, carry, True)
        outs.append(acc / l)

    o = jnp.where(lo, outs[0], outs[1])
    o_ref[...] = (o * gate_ref[...].astype(F32)).astype(BF16)


def _fox_attention(q, k, v, gate, cq, ck, *, batch, seq, tq):
    t = q.shape[0]
    n_pairs = C_HEADS // 2
    nq = seq // tq
    kern = functools.partial(_fox_kernel, tq=tq)
    return pl.pallas_call(
        kern,
        out_shape=jax.ShapeDtypeStruct((t, C_WIDTH), BF16),
        grid=(batch, n_pairs, nq),
        in_specs=[
            pl.BlockSpec((tq, LANES), lambda b, p, i: (b * nq + i, p)),
            pl.BlockSpec((seq, LANES), lambda b, p, i: (b, p)),
            pl.BlockSpec((seq, LANES), lambda b, p, i: (b, p)),
            pl.BlockSpec((tq, LANES), lambda b, p, i: (b * nq + i, p)),
            pl.BlockSpec((None, None, tq, 2), lambda b, p, i: (b, p, i, 0)),
            pl.BlockSpec((None, None, 2, seq), lambda b, p, i: (b, p, 0, 0)),
        ],
        out_specs=pl.BlockSpec((tq, LANES), lambda b, p, i: (b * nq + i, p)),
        compiler_params=pltpu.CompilerParams(dimension_semantics=("arbitrary", "arbitrary", "arbitrary"),
                                             vmem_limit_bytes=VMEM_LIMIT),
        name="fox_attn",
    )(q, k, v, gate, cq, ck)


def _odd_out_kernel(x_ref, a_ref, wo_ref, nrm_ref, r_ref, x3_ref, h_ref, route_ref):
    x3 = x_ref[...] + _dot(a_ref[...], wo_ref[...])
    x3_ref[...] = x3
    h = _rms(x3, nrm_ref[...])
    h_ref[...] = h.astype(BF16)

    h_hi = h.astype(BF16)
    h_lo = (h - h_hi.astype(F32)).astype(BF16)
    r = r_ref[...]
    r_hi = r.astype(BF16)
    r_lo = (r - r_hi.astype(F32)).astype(BF16)
    logits = _dot(h_hi, r_hi) + (_dot(h_lo, r_hi) + _dot(h_hi, r_lo))

    lane = lax.broadcasted_iota(jnp.int32, logits.shape, 1).astype(F32)
    neg_inf = jnp.float32(-jnp.inf)
    lg = jnp.where(lane < N_EXPERTS, logits, neg_inf)
    m1 = jnp.max(lg, axis=-1, keepdims=True)
    i1 = jnp.min(jnp.where(lg == m1, lane, float(LANES)), axis=-1, keepdims=True)
    lg2 = jnp.where(lane == i1, neg_inf, lg)
    m2 = jnp.max(lg2, axis=-1, keepdims=True)
    i2 = jnp.min(jnp.where(lg2 == m2, lane, float(LANES)), axis=-1, keepdims=True)
    e2 = jnp.exp(m2 - m1)
    g1 = 1.0 / (1.0 + e2)
    g2 = e2 / (1.0 + e2)
    route_ref[...] = jnp.where(lane == 0, i1, jnp.where(lane == 1, i2, jnp.where(lane == 2, g1,
                               jnp.where(lane == 3, g2, 0.0))))


def _odd_out(x, a, wo, nrm, router, *, tm):
    t = x.shape[0]
    row_spec = pl.BlockSpec((tm, D_MODEL), lambda i: (i, 0))
    return pl.pallas_call(
        _odd_out_kernel,
        out_shape=[jax.ShapeDtypeStruct((t, D_MODEL), F32), jax.ShapeDtypeStruct((t, D_MODEL), BF16),
                   jax.ShapeDtypeStruct((t, LANES), F32)],
        grid=(t // tm,),
        in_specs=[row_spec, row_spec, _const_spec(wo.shape), _const_spec(nrm.shape), _const_spec(router.shape)],
        out_specs=[row_spec, row_spec, pl.BlockSpec((tm, LANES), lambda i: (i, 0))],
        compiler_params=pltpu.CompilerParams(dimension_semantics=("arbitrary",), vmem_limit_bytes=VMEM_LIMIT),
        name="odd_out",
    )(x, a, wo, nrm, router)


def _moe_ffn_kernel(te_ref, nu_ref, x_ref, w1_ref, w3_ref, w2_ref, o_ref, acc_ref):
    i = pl.program_id(0)
    kf = pl.program_id(1)
    last = pl.num_programs(1) - 1
    used = i < nu_ref[0]

    @pl.when(used)
    def _():
        x = x_ref[...]
        a = _dot(x, w1_ref[...])
        b = _dot(x, w3_ref[...])
        part = _dot((a * _sigmoid(a) * b).astype(BF16), w2_ref[...])

        @pl.when(kf == 0)
        def _():
            acc_ref[...] = part

        @pl.when(kf > 0)
        def _():
            acc_ref[...] += part

        @pl.when(kf == last)
        def _():
            o_ref[...] = acc_ref[...]

    @pl.when(jnp.logical_and(jnp.logical_not(used), kf == last))
    def _():
        o_ref[...] = jnp.zeros_like(o_ref)


def _moe_ffn(tile_expert, n_used, xs, w1, w3, w2, *, tm, tf):
    n_rows = xs.shape[0]
    n_tiles = n_rows // tm
    nk = D_FF_EXPERT // tf

    def kchunk(i, k, nu):
        return jnp.where(i < nu[0], k, nk - 1)

    grid_spec = pltpu.PrefetchScalarGridSpec(
        num_scalar_prefetch=2,
        grid=(n_tiles, nk),
        in_specs=[
            pl.BlockSpec((tm, D_MODEL), lambda i, k, te, nu: (jnp.minimum(i, nu[0] - 1), 0)),
            pl.BlockSpec((None, D_MODEL, tf), lambda i, k, te, nu: (te[i], 0, kchunk(i, k, nu))),
            pl.BlockSpec((None, D_MODEL, tf), lambda i, k, te, nu: (te[i], 0, kchunk(i, k, nu))),
            pl.BlockSpec((None, tf, D_MODEL), lambda i, k, te, nu: (te[i], kchunk(i, k, nu), 0)),
        ],
        out_specs=pl.BlockSpec((tm, D_MODEL), lambda i, k, te, nu: (i, 0)),
        scratch_shapes=[pltpu.VMEM((tm, D_MODEL), F32)],
    )
    return pl.pallas_call(
        _moe_ffn_kernel,
        out_shape=jax.ShapeDtypeStruct((n_rows, D_MODEL), F32),
        grid_spec=grid_spec,
        compiler_params=pltpu.CompilerParams(dimension_semantics=("arbitrary", "arbitrary"),
                                             vmem_limit_bytes=VMEM_LIMIT),
        name="moe_ffn",
    )(tile_expert, n_used, xs, w1, w3, w2)


def _moe_combine_kernel(x_ref, y1_ref, y2_ref, route_ref, nrm_ref, o_ref):
    g1 = route_ref[:, 2:3]
    g2 = route_ref[:, 3:4]
    x = x_ref[...] + (g1 * y1_ref[...] + g2 * y2_ref[...])
    o_ref[...] = _rms(x, nrm_ref[...])


def _moe_combine(x, y1, y2, route, nrm, *, tm):
    t = x.shape[0]
    row_spec = pl.BlockSpec((tm, D_MODEL), lambda i: (i, 0))
    return pl.pallas_call(
        _moe_combine_kernel,
        out_shape=jax.ShapeDtypeStruct((t, D_MODEL), F32),
        grid=(t // tm,),
        in_specs=[row_spec, row_spec, row_spec, pl.BlockSpec((tm, LANES), lambda i: (i, 0)),
                  _const_spec(nrm.shape)],
        out_specs=row_spec,
        compiler_params=pltpu.CompilerParams(dimension_semantics=("arbitrary",), vmem_limit_bytes=VMEM_LIMIT),
        name="moe_combine",
    )(x, y1, y2, route, nrm)


def _routing_tables(idx1, idx2, *, tm, n_tiles):
    t = idx1.shape[0]
    e_flat = jnp.concatenate([idx1, idx2])
    onehot = (e_flat[:, None] == jnp.arange(N_EXPERTS, dtype=jnp.int32)[None, :]).astype(jnp.int32)
    csum = jnp.cumsum(onehot, axis=0)
    rank = jnp.sum((csum - onehot) * onehot, axis=1)
    counts = csum[-1]
    tiles_e = (counts + tm - 1) // tm
    tile_end = jnp.cumsum(tiles_e)
    tile_start = tile_end - tiles_e
    pos = jnp.sum(onehot * tile_start[None, :], axis=1) * tm + rank
    n_used = tile_end[-1]
    tile_ids = jnp.arange(n_tiles, dtype=jnp.int32)
    te = jnp.sum((tile_ids[:, None] >= tile_end[None, :]).astype(jnp.int32), axis=1)
    te_last = jnp.sum((n_used - 1 >= tile_end).astype(jnp.int32))
    tile_expert = jnp.where(tile_ids < n_used, te, te_last).astype(jnp.int32)
    tok = jnp.concatenate([jnp.arange(t, dtype=jnp.int32)] * 2)
    row_tok = jnp.zeros((n_tiles * tm,), jnp.int32).at[pos].set(tok)
    return pos[:t], pos[t:], row_tok, tile_expert, n_used.reshape(1).astype(jnp.int32)


def _pad_cols(w, n):
    return jnp.pad(w, ((0, 0), (0, n - w.shape[1])))


def kernel(x, even_norm_mix, even_w_in, even_gate_up, even_gate_bias, even_w_s, even_b_s, even_ln_g, even_ln_b,
           even_head_g, even_w_o, even_norm_ffn, even_ffn_w1, even_ffn_w3, even_ffn_w2, odd_norm_mix, odd_w_in,
           odd_forget_bias, odd_q_g, odd_k_g, odd_w_o, odd_norm_ffn, odd_router, odd_exp_w1, odd_exp_w3,
           odd_exp_w2, final_norm):
    batch, seq, d = x.shape
    t = batch * seq
    xt = x.reshape(t, d)
    tm = min(512, seq)

    w_in = even_w_in[0]
    u_w, v_w, q_w, k_w, g_w, vb_w, og_w = jnp.split(w_in, [512, 1024, 1280, 1536, 1552, 2064], axis=1)
    win_e = jnp.concatenate([u_w, v_w, q_w, k_w, vb_w, og_w, _pad_cols(g_w, LANES)], axis=1).astype(BF16)
    gup = jnp.pad(even_gate_up[0], ((0, LANES - B_GATE_RANK), (0, 0))).astype(BF16)
    gb = even_gate_bias[0].reshape(1, B_QK_WIDTH)
    tril = jnp.tril(jnp.ones((CHUNK, CHUNK), dtype=bool))
    ws = jnp.where(tril[None], even_w_s[0], 0.0).astype(BF16)
    bs = jnp.broadcast_to(even_b_s[0][:, :, None], (A_GROUPS, CHUNK, LANES))
    lng = even_ln_g[0].reshape(1, A_WIDTH)
    lnb = even_ln_b[0].reshape(1, A_WIDTH)
    hg = even_head_g[0].reshape(B_HEADS, 1, B_VAL_DIM)
    x1 = _even_mixer(xt, even_norm_mix[0].reshape(1, d), win_e, gup, gb, ws, bs, lng, lnb, hg,
                     even_w_o[0].astype(BF16), seq=seq, tm=tm)
    x2 = _dense_ffn(x1, even_norm_ffn[0].reshape(1, d), even_ffn_w1[0].astype(BF16), even_ffn_w3[0].astype(BF16),
                    even_ffn_w2[0].astype(BF16), tm=tm, tf=D_FF_DENSE // 2)

    q_w, k_w, v_w, og_w, f_w = jnp.split(odd_w_in[0], [1024, 2048, 3072, 4096], axis=1)
    win_o = jnp.concatenate([q_w, k_w, v_w, og_w, _pad_cols(f_w, LANES)], axis=1).astype(BF16)
    fb = jnp.pad(odd_forget_bias[0], (0, LANES - C_HEADS)).reshape(1, LANES)
    qg = jnp.tile(odd_q_g[0], C_HEADS).reshape(1, C_WIDTH)
    kg = jnp.tile(odd_k_g[0], C_HEADS).reshape(1, C_WIDTH)
    q, k, v, gate, c = _odd_inproj(x2, odd_norm_mix[0].reshape(1, d), win_o, fb, qg, kg, seq=seq, tm=tm)
    c_heads = c[:, :C_HEADS].reshape(batch, seq, C_HEADS // 2, 2)
    cq = jnp.transpose(c_heads, (0, 2, 1, 3))
    ck = jnp.transpose(c_heads, (0, 2, 3, 1))
    attn = _fox_attention(q, k, v, gate, cq, ck, batch=batch, seq=seq, tq=min(256, seq))

    router = _pad_cols(odd_router[0], LANES)
    x3, h2, route = _odd_out(x2, attn, odd_w_o[0].astype(BF16), odd_norm_ffn[0].reshape(1, d), router, tm=tm)

    tm_moe = 512
    n_tiles = (2 * t) // tm_moe + N_EXPERTS
    idx1 = route[:, 0].astype(jnp.int32)
    idx2 = route[:, 1].astype(jnp.int32)
    pos1, pos2, row_tok, tile_expert, n_used = _routing_tables(idx1, idx2, tm=tm_moe, n_tiles=n_tiles)
    xs = jnp.take(h2, row_tok, axis=0)
    y = _moe_ffn(tile_expert, n_used, xs, odd_exp_w1[0].astype(BF16), odd_exp_w3[0].astype(BF16),
                 odd_exp_w2[0].astype(BF16), tm=tm_moe, tf=512)
    y1 = jnp.take(y, pos1, axis=0)
    y2 = jnp.take(y, pos2, axis=0)
    out = _moe_combine(x3, y1, y2, route, final_norm.reshape(1, d), tm=tm)
    return out.reshape(batch, seq, d)
```

```python
import functools
import math

import jax
import jax.numpy as jnp
from jax import lax
from jax.experimental import pallas as pl
from jax.experimental.pallas import tpu as pltpu

F32 = jnp.float32
BF16 = jnp.bfloat16
HIGHEST = lax.Precision.HIGHEST

EPS = 1e-6
D_MODEL = 1024
CHUNK = 128
SUB = 32
N_SUB = CHUNK // SUB
A_GROUPS = 4
A_WIDTH = 512
B_HEADS = 4
B_KEY_DIM = 64
B_VAL_DIM = 128
B_QK_WIDTH = 256
B_V_WIDTH = 512
B_GATE_RANK = 16
B_GATE_NORMALIZER = 16.0
C_HEADS = 16
C_HEAD_DIM = 64
C_WIDTH = 1024
D_FF_DENSE = 2816
N_EXPERTS = 8
D_FF_EXPERT = 3584
LANES = 128
MAX_DECAY_EXP = 60.0
LOG2E = math.log2(math.e)
ROW_TILE = 8

E_U, E_V, E_Q, E_K, E_VB, E_OG, E_G, E_END = 0, 512, 1024, 1280, 1536, 2048, 2560, 2688
O_Q, O_K, O_V, O_OG, O_F, O_END = 0, 1024, 2048, 3072, 4096, 4224

VMEM_LIMIT = 56 * 1024 * 1024


def _rms(x, g):
    ms = jnp.mean(x * x, axis=-1, keepdims=True)
    return x * lax.rsqrt(ms + EPS) * g


def _gelu_tanh(x):
    c = math.sqrt(2.0 / math.pi)
    return x * (0.5 * (1.0 + jnp.tanh(c * (x + 0.044715 * (x * x * x)))))


def _sigmoid(x):
    return 1.0 / (1.0 + jnp.exp(-x))


def _log_sigmoid(x):
    return jnp.minimum(x, 0.0) - jnp.log(1.0 + jnp.exp(-jnp.abs(x)))


def _dot(a, b):
    return jnp.dot(a, b, preferred_element_type=F32)


def _dot_nt(a, b):
    return lax.dot_general(a, b, (((1,), (1,)), ((), ())), preferred_element_type=F32)


def _split3(x):
    hi = x.astype(BF16)
    r1 = x - hi.astype(F32)
    mid = r1.astype(BF16)
    lo = (r1 - mid.astype(F32)).astype(BF16)
    return hi, mid, lo


def _cumsum_rows(tril_b, x):
    hi, mid, lo = _split3(x)
    return _dot(tril_b, hi) + _dot(tril_b, mid) + _dot(tril_b, lo)


def _const_spec(shape):
    nd = len(shape)
    return pl.BlockSpec(shape, lambda *_: (0,) * nd)


def _even_mixer_kernel(x_ref, nrm_ref, win_ref, gup_ref, gb_ref, ws_ref, bs_ref, lng_ref, lnb_ref,
                       hg_ref, wo_ref, o_ref, z_ref, mix_ref, st_ref, *, tiles_per_batch, n_chunks):
    i = pl.program_id(0)

    @pl.when(i % tiles_per_batch == 0)
    def _():
        st_ref[...] = jnp.zeros_like(st_ref)

    h = _rms(x_ref[...], nrm_ref[...]).astype(BF16)
    z_ref[...] = _dot(h, win_ref[...])

    row = lax.broadcasted_iota(jnp.int32, (CHUNK, CHUNK), 0)
    col = lax.broadcasted_iota(jnp.int32, (CHUNK, CHUNK), 1)
    tril_b = (col <= row).astype(BF16)
    sub_row = row & (SUB - 1)
    head_lane = lax.broadcasted_iota(jnp.int32, (1, B_QK_WIDTH), 1) // B_KEY_DIM
    bd_mask = (lax.broadcasted_iota(jnp.int32, (B_V_WIDTH, B_QK_WIDTH), 0) // B_VAL_DIM
               == lax.broadcasted_iota(jnp.int32, (B_V_WIDTH, B_QK_WIDTH), 1) // B_KEY_DIM)

    def chunk_body(c, carry):
        rows = pl.ds(pl.multiple_of(c * CHUNK, CHUNK), CHUNK)

        u = _gelu_tanh(z_ref[rows, E_U:E_V])
        v = _gelu_tanh(z_ref[rows, E_V:E_Q])
        mu = jnp.mean(v, axis=-1, keepdims=True)
        vc = v - mu
        var = jnp.mean(vc * vc, axis=-1, keepdims=True)
        vln = (vc * lax.rsqrt(var + EPS) * lng_ref[...] + lnb_ref[...]).astype(BF16)
        for g in range(A_GROUPS):
            sl = slice(g * LANES, (g + 1) * LANES)
            mixed = _dot(ws_ref[g], vln[:, sl]) + bs_ref[g]
            mix_ref[rows, sl] = (u[:, sl] * mixed).astype(BF16)

        q = z_ref[rows, E_Q:E_K] * (B_KEY_DIM ** -0.5)
        k = z_ref[rows, E_K:E_VB]
        vb = z_ref[rows, E_VB:E_OG]
        og = z_ref[rows, E_OG:E_G]
        glr = z_ref[rows, E_G:E_END].astype(BF16)
        logit = _dot(glr, gup_ref[...]) + gb_ref[...]
        log_a = _log_sigmoid(logit) * (1.0 / B_GATE_NORMALIZER)
        g_cum = _cumsum_rows(tril_b, log_a)
        g_last = g_cum[CHUNK - 1:CHUNK, :]
        st = st_ref[...]
        o = _dot_nt((q * jnp.exp(g_cum)).astype(BF16), st.astype(BF16))

        p_rows = [[None] * N_SUB for _ in range(B_HEADS)]
        for s in range(N_SUB):
            gs = g_cum[s * SUB:(s + 1) * SUB, :]
            if s == 0:
                qt = q[0:SUB, :] * jnp.exp(gs)
                kt = k * jnp.exp(jnp.minimum(-g_cum, MAX_DECAY_EXP))
            else:
                ref_g = g_cum[s * SUB - 1:s * SUB, :]
                qt = q[s * SUB:(s + 1) * SUB, :] * jnp.exp(gs - ref_g)
                kt = k * jnp.exp(jnp.minimum(ref_g - g_cum, MAX_DECAY_EXP))
            qs = jnp.concatenate([jnp.where(head_lane == hh, qt, 0.0) for hh in range(B_HEADS)],
                                 axis=0).astype(BF16)
            sc = _dot_nt(qs, kt.astype(BF16))
            sc = jnp.where(col <= (s * SUB + sub_row), sc, 0.0)
            for hh in range(B_HEADS):
                p_rows[hh][s] = sc[hh * SUB:(hh + 1) * SUB, :]

        vb_b = vb.astype(BF16)
        for hh in range(B_HEADS):
            sl = slice(hh * B_VAL_DIM, (hh + 1) * B_VAL_DIM)
            ph = jnp.concatenate(p_rows[hh], axis=0).astype(BF16)
            oh = o[:, sl] + _dot(ph, vb_b[:, sl])
            on = _rms(oh, hg_ref[hh])
            ogh = og[:, sl]
            mix_ref[rows, A_WIDTH + hh * B_VAL_DIM:A_WIDTH + (hh + 1) * B_VAL_DIM] = (
                on * (ogh * _sigmoid(ogh))).astype(BF16)

        k_dec = (k * jnp.exp(g_last - g_cum)).astype(BF16)
        upd = _dot(vb.T.astype(BF16), k_dec)
        st_ref[...] = jnp.exp(g_last) * st + jnp.where(bd_mask, upd, 0.0)
        return carry

    lax.fori_loop(0, n_chunks, chunk_body, 0)
    o_ref[...] = x_ref[...] + _dot(mix_ref[...], wo_ref[...])


def _even_mixer(x, nrm, win, gup, gb, ws, bs, lng, lnb, hg, wo, *, seq, tm):
    t = x.shape[0]
    kern = functools.partial(_even_mixer_kernel, tiles_per_batch=seq // tm, n_chunks=tm // CHUNK)
    return pl.pallas_call(
        kern,
        out_shape=jax.ShapeDtypeStruct((t, D_MODEL), F32),
        grid=(t // tm,),
        in_specs=[
            pl.BlockSpec((tm, D_MODEL), lambda i: (i, 0)),
            _const_spec(nrm.shape), _const_spec(win.shape), _const_spec(gup.shape), _const_spec(gb.shape),
            _const_spec(ws.shape), _const_spec(bs.shape), _const_spec(lng.shape), _const_spec(lnb.shape),
            _const_spec(hg.shape), _const_spec(wo.shape),
        ],
        out_specs=pl.BlockSpec((tm, D_MODEL), lambda i: (i, 0)),
        scratch_shapes=[
            pltpu.VMEM((tm, E_END), F32),
            pltpu.VMEM((tm, D_MODEL), BF16),
            pltpu.VMEM((B_V_WIDTH, B_QK_WIDTH), F32),
        ],
        compiler_params=pltpu.CompilerParams(dimension_semantics=("arbitrary",), vmem_limit_bytes=VMEM_LIMIT),
        name="even_mixer",
    )(x, nrm, win, gup, gb, ws, bs, lng, lnb, hg, wo)


def _dense_ffn_kernel(x_ref, nrm_ref, w1_ref, w3_ref, w2_ref, o_ref, acc_ref):
    kf = pl.program_id(1)
    h = _rms(x_ref[...], nrm_ref[...]).astype(BF16)
    a = _dot(h, w1_ref[...])
    b = _dot(h, w3_ref[...])
    part = _dot((a * _sigmoid(a) * b).astype(BF16), w2_ref[...])

    @pl.when(kf == 0)
    def _():
        acc_ref[...] = x_ref[...] + part

    @pl.when(kf > 0)
    def _():
        acc_ref[...] += part

    @pl.when(kf == pl.num_programs(1) - 1)
    def _():
        o_ref[...] = acc_ref[...]


def _dense_ffn(x, nrm, w1, w3, w2, *, tm, tf):
    t = x.shape[0]
    dff = w1.shape[1]
    return pl.pallas_call(
        _dense_ffn_kernel,
        out_shape=jax.ShapeDtypeStruct((t, D_MODEL), F32),
        grid=(t // tm, dff // tf),
        in_specs=[
            pl.BlockSpec((tm, D_MODEL), lambda i, k: (i, 0)),
            pl.BlockSpec((1, D_MODEL), lambda i, k: (0, 0)),
            pl.BlockSpec((D_MODEL, tf), lambda i, k: (0, k)),
            pl.BlockSpec((D_MODEL, tf), lambda i, k: (0, k)),
            pl.BlockSpec((tf, D_MODEL), lambda i, k: (k, 0)),
        ],
        out_specs=pl.BlockSpec((tm, D_MODEL), lambda i, k: (i, 0)),
        scratch_shapes=[pltpu.VMEM((tm, D_MODEL), F32)],
        compiler_params=pltpu.CompilerParams(dimension_semantics=("arbitrary", "arbitrary"),
                                             vmem_limit_bytes=VMEM_LIMIT),
        name="dense_ffn",
    )(x, nrm, w1, w3, w2)


def _head_rms(x, gain):
    lo = lax.broadcasted_iota(jnp.int32, (1, LANES), 1) < C_HEAD_DIM
    outs = []
    for t in range(C_WIDTH // LANES):
        xt = x[:, t * LANES:(t + 1) * LANES]
        sq = xt * xt
        s_lo = jnp.sum(jnp.where(lo, sq, 0.0), axis=-1, keepdims=True)
        s_hi = jnp.sum(jnp.where(lo, 0.0, sq), axis=-1, keepdims=True)
        inv = jnp.where(lo, lax.rsqrt(s_lo * (1.0 / C_HEAD_DIM) + EPS), lax.rsqrt(s_hi * (1.0 / C_HEAD_DIM) + EPS))
        outs.append(xt * inv)
    return jnp.concatenate(outs, axis=-1) * gain


def _odd_inproj_kernel(x_ref, nrm_ref, w_ref, fb_ref, qg_ref, kg_ref,
                       q_ref, k_ref, v_ref, gate_ref, c_ref, z_ref, carry_ref, *, tiles_per_batch, n_chunks):
    i = pl.program_id(0)

    @pl.when(i % tiles_per_batch == 0)
    def _():
        carry_ref[...] = jnp.zeros_like(carry_ref)

    h = _rms(x_ref[...], nrm_ref[...]).astype(BF16)
    z_ref[...] = _dot(h, w_ref[...])
    q_ref[...] = (_head_rms(z_ref[:, O_Q:O_K], qg_ref[...]) * (C_HEAD_DIM ** -0.5 * LOG2E)).astype(BF16)
    k_ref[...] = _head_rms(z_ref[:, O_K:O_V], kg_ref[...]).astype(BF16)
    v_ref[...] = z_ref[:, O_V:O_OG].astype(BF16)
    gate_ref[...] = _sigmoid(z_ref[:, O_OG:O_F]).astype(BF16)

    row = lax.broadcasted_iota(jnp.int32, (CHUNK, CHUNK), 0)
    col = lax.broadcasted_iota(jnp.int32, (CHUNK, CHUNK), 1)
    tril_b = (col <= row).astype(BF16)
    carry = carry_ref[...]
    for c in range(n_chunks):
        rows = slice(c * CHUNK, (c + 1) * CHUNK)
        log_f = _log_sigmoid(z_ref[rows, O_F:O_END] + fb_ref[...])
        cs = _cumsum_rows(tril_b, log_f) + carry
        c_ref[rows, :] = cs
        carry = cs[CHUNK - 1:CHUNK, :]
    carry_ref[...] = carry


def _odd_inproj(x, nrm, w, fb, qg, kg, *, seq, tm):
    t = x.shape[0]
    kern = functools.partial(_odd_inproj_kernel, tiles_per_batch=seq // tm, n_chunks=tm // CHUNK)
    row_spec = pl.BlockSpec((tm, C_WIDTH), lambda i: (i, 0))
    return pl.pallas_call(
        kern,
        out_shape=[jax.ShapeDtypeStruct((t, C_WIDTH), BF16)] * 4 + [jax.ShapeDtypeStruct((t, LANES), F32)],
        grid=(t // tm,),
        in_specs=[row_spec, _const_spec(nrm.shape), _const_spec(w.shape), _const_spec(fb.shape),
                  _const_spec(qg.shape), _const_spec(kg.shape)],
        out_specs=[row_spec] * 4 + [pl.BlockSpec((tm, LANES), lambda i: (i, 0))],
        scratch_shapes=[pltpu.VMEM((tm, O_END), F32), pltpu.VMEM((1, LANES), F32)],
        compiler_params=pltpu.CompilerParams(dimension_semantics=("arbitrary",), vmem_limit_bytes=VMEM_LIMIT),
        name="odd_inproj",
    )(x, nrm, w, fb, qg, kg)


NEG_BIG = -1e30


def _fox_kernel(q_ref, k_ref, vt_ref, gate_ref, ckb_ref, o_ref, *, tq, tk):
    qi = pl.program_id(2)
    ng = tq // tk
    lo = lax.broadcasted_iota(jnp.int32, (1, LANES), 1) < C_HEAD_DIM
    key_i = lax.broadcasted_iota(jnp.int32, (tk, tk), 0)
    qry_i = lax.broadcasted_iota(jnp.int32, (tk, tk), 1)
    causal = key_i <= qry_i
    chains = [(hh, r) for r in range(ng) for hh in range(2)]
    qms = []
    for hh, r in chains:
        qr = q_ref[r * tk:(r + 1) * tk, :]
        zero = jnp.zeros_like(qr)
        qms.append(jnp.where(lo, qr, zero) if hh == 0 else jnp.where(lo, zero, qr))

    def step(j, carry, modes):
        ks = pl.ds(pl.multiple_of(j * tk, tk), tk)
        kb = k_ref[ks, :]
        vtb = vt_ref[j]
        new = list(carry)
        live = [idx for idx, (hh, r) in enumerate(chains) if modes[r] != "skip"]
        scores = {idx: _dot_nt(kb, qms[idx]) for idx in live}
        probs = {}
        for idx in live:
            hh, r = chains[idx]
            m, l, _ = carry[3 * idx:3 * idx + 3]
            cb = ckb_ref[hh, ks, :]
            s = scores[idx] - jnp.concatenate([cb] * (tk // LANES), axis=1)
            if modes[r] == "diag":
                s = jnp.where(causal, s, NEG_BIG)
            m_new = jnp.maximum(m, jnp.max(s, axis=0, keepdims=True))
            p = jnp.exp2(s - m_new)
            alpha = jnp.exp2(m - m_new)
            new[3 * idx] = m_new
            new[3 * idx + 1] = alpha * l + jnp.sum(p, axis=0, keepdims=True)
            probs[idx] = (alpha, p.astype(BF16))
        for idx in live:
            hh, r = chains[idx]
            alpha, p = probs[idx]
            new[3 * idx + 2] = alpha * carry[3 * idx + 2] + _dot(vtb[hh * C_HEAD_DIM:(hh + 1) * C_HEAD_DIM, :], p)
        return tuple(new)

    init = []
    for _ in chains:
        init += [jnp.full((1, tk), NEG_BIG, F32), jnp.zeros((1, tk), F32), jnp.zeros((C_HEAD_DIM, tk), F32)]
    carry = lax.fori_loop(0, qi * ng, functools.partial(step, modes=("full",) * ng), tuple(init))
    for g in range(ng):
        modes = tuple("skip" if r < g else ("diag" if r == g else "full") for r in range(ng))
        carry = step(qi * ng + g, carry, modes)

    for r in range(ng):
        parts = []
        for hh in range(2):
            idx = chains.index((hh, r))
            parts.append(carry[3 * idx + 2] / carry[3 * idx + 1])
        o = jnp.concatenate(parts, axis=0).T
        rows = slice(r * tk, (r + 1) * tk)
        o_ref[rows, :] = (o * gate_ref[rows, :].astype(F32)).astype(BF16)


def _fox_attention(q, k, vt, gate, ckb, *, batch, seq, tq, tk):
    t = q.shape[0]
    n_pairs = C_HEADS // 2
    nq = seq // tq
    kern = functools.partial(_fox_kernel, tq=tq, tk=tk)
    return pl.pallas_call(
        kern,
        out_shape=jax.ShapeDtypeStruct((t, C_WIDTH), BF16),
        grid=(batch, n_pairs, nq),
        in_specs=[
            pl.BlockSpec((tq, LANES), lambda b, p, i: (b * nq + i, p)),
            pl.BlockSpec((seq, LANES), lambda b, p, i: (b, p)),
            pl.BlockSpec((None, None, seq // tk, LANES, tk), lambda b, p, i: (b, p, 0, 0, 0)),
            pl.BlockSpec((tq, LANES), lambda b, p, i: (b * nq + i, p)),
            pl.BlockSpec((None, None, 2, seq, LANES), lambda b, p, i: (b, p, 0, 0, 0)),
        ],
        out_specs=pl.BlockSpec((tq, LANES), lambda b, p, i: (b * nq + i, p)),
        compiler_params=pltpu.CompilerParams(dimension_semantics=("arbitrary", "arbitrary", "arbitrary"),
                                             vmem_limit_bytes=VMEM_LIMIT),
        name="fox_attn",
    )(q, k, vt, gate, ckb)


def _store_token_tiles(dst_ref, val, n_rows):
    for s in range(ROW_TILE):
        dst_ref[pl.ds(s, n_rows, stride=ROW_TILE), :] = val[:, s * LANES:(s + 1) * LANES]


def _load_token_tiles(src_ref, n_rows):
    return jnp.concatenate([src_ref[pl.ds(s, n_rows, stride=ROW_TILE), :] for s in range(ROW_TILE)], axis=1)


def _odd_out_kernel(x_ref, a_ref, wo_ref, nrm_ref, r_ref, x3_ref, h_ref, route_ref):
    x3 = x_ref[...] + _dot(a_ref[...], wo_ref[...])
    x3_ref[...] = x3
    h = _rms(x3, nrm_ref[...])
    _store_token_tiles(h_ref, h, h.shape[0])

    h_hi = h.astype(BF16)
    h_lo = (h - h_hi.astype(F32)).astype(BF16)
    r = r_ref[...]
    r_hi = r.astype(BF16)
    r_lo = (r - r_hi.astype(F32)).astype(BF16)
    logits = _dot(h_hi, r_hi) + (_dot(h_lo, r_hi) + _dot(h_hi, r_lo))

    lane = lax.broadcasted_iota(jnp.int32, logits.shape, 1).astype(F32)
    neg_inf = jnp.float32(-jnp.inf)
    lg = jnp.where(lane < N_EXPERTS, logits, neg_inf)
    m1 = jnp.max(lg, axis=-1, keepdims=True)
    i1 = jnp.min(jnp.where(lg == m1, lane, float(LANES)), axis=-1, keepdims=True)
    lg2 = jnp.where(lane == i1, neg_inf, lg)
    m2 = jnp.max(lg2, axis=-1, keepdims=True)
    i2 = jnp.min(jnp.where(lg2 == m2, lane, float(LANES)), axis=-1, keepdims=True)
    e2 = jnp.exp(m2 - m1)
    g1 = 1.0 / (1.0 + e2)
    g2 = e2 / (1.0 + e2)
    route_ref[...] = jnp.where(lane == 0, i1, jnp.where(lane == 1, i2, jnp.where(lane == 2, g1,
                               jnp.where(lane == 3, g2, 0.0))))


def _odd_out(x, a, wo, nrm, router, *, tm):
    t = x.shape[0]
    row_spec = pl.BlockSpec((tm, D_MODEL), lambda i: (i, 0))
    return pl.pallas_call(
        _odd_out_kernel,
        out_shape=[jax.ShapeDtypeStruct((t, D_MODEL), F32), jax.ShapeDtypeStruct((t * ROW_TILE, LANES), F32),
                   jax.ShapeDtypeStruct((t, LANES), F32)],
        grid=(t // tm,),
        in_specs=[row_spec, row_spec, _const_spec(wo.shape), _const_spec(nrm.shape), _const_spec(router.shape)],
        out_specs=[row_spec, pl.BlockSpec((tm * ROW_TILE, LANES), lambda i: (i, 0)),
                   pl.BlockSpec((tm, LANES), lambda i: (i, 0))],
        compiler_params=pltpu.CompilerParams(dimension_semantics=("arbitrary",), vmem_limit_bytes=VMEM_LIMIT),
        name="odd_out",
    )(x, a, wo, nrm, router)


def _row_gather_copy(src_hbm, src_row, dst_ref, dst_row, sem):
    return pltpu.make_async_copy(
        src_hbm.at[pl.ds(pl.multiple_of(src_row * ROW_TILE, ROW_TILE), ROW_TILE), :],
        dst_ref.at[pl.ds(pl.multiple_of(dst_row * ROW_TILE, ROW_TILE), ROW_TILE), :],
        sem)


def _gather_rows(src_hbm, idx_ref, idx_row, dst_ref, sem, n_rows):
    def issue(r, c):
        _row_gather_copy(src_hbm, idx_ref[idx_row, r], dst_ref, r, sem).start()
        return c

    lax.fori_loop(0, n_rows, issue, 0)

    def drain(r, c):
        _row_gather_copy(src_hbm, 0, dst_ref, r, sem).wait()
        return c

    lax.fori_loop(0, n_rows, drain, 0)


def _moe_ffn_kernel(te_ref, nu_ref, tok_ref, h_hbm, w1_ref, w3_ref, w2_ref, o_ref, xs_ref, buf_ref, acc_ref, sem,
                    *, tm):
    i = pl.program_id(0)
    kf = pl.program_id(1)
    last = pl.num_programs(1) - 1
    used = i < nu_ref[0]

    @pl.when(jnp.logical_and(used, kf == 0))
    def _():
        _gather_rows(h_hbm, tok_ref, 0, buf_ref, sem.at[0], tm)
        xs_ref[...] = _load_token_tiles(buf_ref, tm).astype(BF16)

    @pl.when(used)
    def _():
        x = xs_ref[...]
        a = _dot(x, w1_ref[...])
        b = _dot(x, w3_ref[...])
        part = _dot((a * _sigmoid(a) * b).astype(BF16), w2_ref[...])

        @pl.when(kf == 0)
        def _():
            acc_ref[...] = part

        @pl.when(jnp.logical_and(kf > 0, kf < last))
        def _():
            acc_ref[...] += part

        @pl.when(kf == last)
        def _():
            _store_token_tiles(o_ref, acc_ref[...] + part, tm)

    @pl.when(jnp.logical_and(jnp.logical_not(used), kf == last))
    def _():
        o_ref[...] = jnp.zeros_like(o_ref)


def _moe_ffn(tile_expert, n_used, row_tok, h_tiles, w1, w3, w2, *, tm, tf):
    n_tiles = row_tok.shape[0]
    nk = D_FF_EXPERT // tf
    assert nk >= 2

    def kchunk(i, k, nu):
        return jnp.where(i < nu[0], k, nk - 1)

    grid_spec = pltpu.PrefetchScalarGridSpec(
        num_scalar_prefetch=2,
        grid=(n_tiles, nk),
        in_specs=[
            pl.BlockSpec((None, 1, tm), lambda i, k, te, nu: (i, 0, 0), memory_space=pltpu.SMEM),
            pl.BlockSpec(memory_space=pl.ANY),
            pl.BlockSpec((None, D_MODEL, tf), lambda i, k, te, nu: (te[i], 0, kchunk(i, k, nu))),
            pl.BlockSpec((None, D_MODEL, tf), lambda i, k, te, nu: (te[i], 0, kchunk(i, k, nu))),
            pl.BlockSpec((None, tf, D_MODEL), lambda i, k, te, nu: (te[i], kchunk(i, k, nu), 0)),
        ],
        out_specs=pl.BlockSpec((tm * ROW_TILE, LANES), lambda i, k, te, nu: (i, 0)),
        scratch_shapes=[pltpu.VMEM((tm, D_MODEL), BF16), pltpu.VMEM((tm * ROW_TILE, LANES), F32),
                        pltpu.VMEM((tm, D_MODEL), F32), pltpu.SemaphoreType.DMA((1,))],
    )
    return pl.pallas_call(
        functools.partial(_moe_ffn_kernel, tm=tm),
        out_shape=jax.ShapeDtypeStruct((n_tiles * tm * ROW_TILE, LANES), F32),
        grid_spec=grid_spec,
        compiler_params=pltpu.CompilerParams(dimension_semantics=("arbitrary", "arbitrary"),
                                             vmem_limit_bytes=VMEM_LIMIT),
        name="moe_ffn",
    )(tile_expert, n_used, row_tok, h_tiles, w1, w3, w2)


def _moe_combine_kernel(pos_ref, x_ref, route_ref, nrm_ref, y_hbm, o_ref, buf1_ref, buf2_ref, sem, *, tm):
    def issue(r, c):
        _row_gather_copy(y_hbm, pos_ref[0, r], buf1_ref, r, sem.at[0]).start()
        _row_gather_copy(y_hbm, pos_ref[1, r], buf2_ref, r, sem.at[1]).start()
        return c

    lax.fori_loop(0, tm, issue, 0)

    def drain(r, c):
        _row_gather_copy(y_hbm, 0, buf1_ref, r, sem.at[0]).wait()
        _row_gather_copy(y_hbm, 0, buf2_ref, r, sem.at[1]).wait()
        return c

    lax.fori_loop(0, tm, drain, 0)

    g1 = route_ref[:, 2:3]
    g2 = route_ref[:, 3:4]
    x = x_ref[...] + (g1 * _load_token_tiles(buf1_ref, tm) + g2 * _load_token_tiles(buf2_ref, tm))
    o_ref[...] = _rms(x, nrm_ref[...])


def _moe_combine(pos, x, route, nrm, y_tiles, *, tm):
    t = x.shape[0]
    row_spec = pl.BlockSpec((tm, D_MODEL), lambda i: (i, 0))
    return pl.pallas_call(
        functools.partial(_moe_combine_kernel, tm=tm),
        out_shape=jax.ShapeDtypeStruct((t, D_MODEL), F32),
        grid=(t // tm,),
        in_specs=[pl.BlockSpec((None, 2, tm), lambda i: (i, 0, 0), memory_space=pltpu.SMEM),
                  row_spec, pl.BlockSpec((tm, LANES), lambda i: (i, 0)), _const_spec(nrm.shape),
                  pl.BlockSpec(memory_space=pl.ANY)],
        out_specs=row_spec,
        scratch_shapes=[pltpu.VMEM((tm * ROW_TILE, LANES), F32), pltpu.VMEM((tm * ROW_TILE, LANES), F32),
                        pltpu.SemaphoreType.DMA((2,))],
        compiler_params=pltpu.CompilerParams(dimension_semantics=("arbitrary",), vmem_limit_bytes=VMEM_LIMIT),
        name="moe_combine",
    )(pos, x, route, nrm, y_tiles)


def _routing_tables(idx1, idx2, *, tm, n_tiles):
    t = idx1.shape[0]
    e_flat = jnp.concatenate([idx1, idx2])
    onehot = (e_flat[:, None] == jnp.arange(N_EXPERTS, dtype=jnp.int32)[None, :]).astype(jnp.int32)
    csum = jnp.cumsum(onehot, axis=0)
    rank = jnp.sum((csum - onehot) * onehot, axis=1)
    counts = csum[-1]
    tiles_e = (counts + tm - 1) // tm
    tile_end = jnp.cumsum(tiles_e)
    tile_start = tile_end - tiles_e
    pos = jnp.sum(onehot * tile_start[None, :], axis=1) * tm + rank
    n_used = tile_end[-1]
    tile_ids = jnp.arange(n_tiles, dtype=jnp.int32)
    te = jnp.sum((tile_ids[:, None] >= tile_end[None, :]).astype(jnp.int32), axis=1)
    te_last = jnp.sum((n_used - 1 >= tile_end).astype(jnp.int32))
    tile_expert = jnp.where(tile_ids < n_used, te, te_last).astype(jnp.int32)
    tok = jnp.concatenate([jnp.arange(t, dtype=jnp.int32)] * 2)
    row_tok = jnp.zeros((n_tiles * tm,), jnp.int32).at[pos].set(tok)
    return pos[:t], pos[t:], row_tok, tile_expert, n_used.reshape(1).astype(jnp.int32)


def _pad_cols(w, n):
    return jnp.pad(w, ((0, 0), (0, n - w.shape[1])))


def kernel(x, even_norm_mix, even_w_in, even_gate_up, even_gate_bias, even_w_s, even_b_s, even_ln_g, even_ln_b,
           even_head_g, even_w_o, even_norm_ffn, even_ffn_w1, even_ffn_w3, even_ffn_w2, odd_norm_mix, odd_w_in,
           odd_forget_bias, odd_q_g, odd_k_g, odd_w_o, odd_norm_ffn, odd_router, odd_exp_w1, odd_exp_w3,
           odd_exp_w2, final_norm):
    batch, seq, d = x.shape
    t = batch * seq
    xt = x.reshape(t, d)
    tm = min(512, seq)

    w_in = even_w_in[0]
    u_w, v_w, q_w, k_w, g_w, vb_w, og_w = jnp.split(w_in, [512, 1024, 1280, 1536, 1552, 2064], axis=1)
    win_e = jnp.concatenate([u_w, v_w, q_w, k_w, vb_w, og_w, _pad_cols(g_w, LANES)], axis=1).astype(BF16)
    gup = jnp.pad(even_gate_up[0], ((0, LANES - B_GATE_RANK), (0, 0))).astype(BF16)
    gb = even_gate_bias[0].reshape(1, B_QK_WIDTH)
    tril = jnp.tril(jnp.ones((CHUNK, CHUNK), dtype=bool))
    ws = jnp.where(tril[None], even_w_s[0], 0.0).astype(BF16)
    bs = jnp.broadcast_to(even_b_s[0][:, :, None], (A_GROUPS, CHUNK, LANES))
    lng = even_ln_g[0].reshape(1, A_WIDTH)
    lnb = even_ln_b[0].reshape(1, A_WIDTH)
    hg = even_head_g[0].reshape(B_HEADS, 1, B_VAL_DIM)
    x1 = _even_mixer(xt, even_norm_mix[0].reshape(1, d), win_e, gup, gb, ws, bs, lng, lnb, hg,
                     even_w_o[0].astype(BF16), seq=seq, tm=tm)
    x2 = _dense_ffn(x1, even_norm_ffn[0].reshape(1, d), even_ffn_w1[0].astype(BF16), even_ffn_w3[0].astype(BF16),
                    even_ffn_w2[0].astype(BF16), tm=tm, tf=D_FF_DENSE // 2)

    q_w, k_w, v_w, og_w, f_w = jnp.split(odd_w_in[0], [1024, 2048, 3072, 4096], axis=1)
    win_o = jnp.concatenate([q_w, k_w, v_w, og_w, _pad_cols(f_w, LANES)], axis=1).astype(BF16)
    fb = jnp.pad(odd_forget_bias[0], (0, LANES - C_HEADS)).reshape(1, LANES)
    qg = jnp.tile(odd_q_g[0], C_HEADS).reshape(1, C_WIDTH)
    kg = jnp.tile(odd_k_g[0], C_HEADS).reshape(1, C_WIDTH)
    q, k, v, gate, c = _odd_inproj(x2, odd_norm_mix[0].reshape(1, d), win_o, fb, qg, kg, seq=seq, tm=tm)
    tk = min(256, seq)
    tq = min(512, seq)
    n_pairs = C_HEADS // 2
    ck = (c[:, :C_HEADS] * LOG2E).reshape(batch, seq, n_pairs, 2)
    ckb = jnp.broadcast_to(jnp.transpose(ck, (0, 2, 3, 1))[..., None], (batch, n_pairs, 2, seq, LANES))
    vt = jnp.transpose(v.reshape(batch, seq // tk, tk, n_pairs, LANES), (0, 3, 1, 4, 2))
    attn = _fox_attention(q, k, vt, gate, ckb, batch=batch, seq=seq, tq=tq, tk=tk)

    router = _pad_cols(odd_router[0], LANES)
    x3, h_tiles, route = _odd_out(x2, attn, odd_w_o[0].astype(BF16), odd_norm_ffn[0].reshape(1, d), router, tm=tm)

    tm_moe = 512
    tm_comb = min(256, seq)
    n_tiles = (2 * t) // tm_moe + N_EXPERTS
    idx1 = route[:, 0].astype(jnp.int32)
    idx2 = route[:, 1].astype(jnp.int32)
    pos1, pos2, row_tok, tile_expert, n_used = _routing_tables(idx1, idx2, tm=tm_moe, n_tiles=n_tiles)
    y_tiles = _moe_ffn(tile_expert, n_used, row_tok.reshape(n_tiles, 1, tm_moe), h_tiles,
                       odd_exp_w1[0].astype(BF16), odd_exp_w3[0].astype(BF16), odd_exp_w2[0].astype(BF16),
                       tm=tm_moe, tf=512)
    pos = jnp.stack([pos1.reshape(t // tm_comb, tm_comb), pos2.reshape(t // tm_comb, tm_comb)], axis=1)
    out = _moe_combine(pos, x3, route, final_norm.reshape(1, d), y_tiles, tm=tm_comb)
    return out.reshape(batch, seq, d)
```

```python
import functools
import math

import jax
import jax.numpy as jnp
from jax import lax
from jax.experimental import pallas as pl
from jax.experimental.pallas import tpu as pltpu

F32 = jnp.float32
BF16 = jnp.bfloat16
HIGHEST = lax.Precision.HIGHEST

EPS = 1e-6
D_MODEL = 1024
CHUNK = 128
SUB = 32
N_SUB = CHUNK // SUB
A_GROUPS = 4
A_WIDTH = 512
B_HEADS = 4
B_KEY_DIM = 64
B_VAL_DIM = 128
B_QK_WIDTH = 256
B_V_WIDTH = 512
B_GATE_RANK = 16
B_GATE_NORMALIZER = 16.0
C_HEADS = 16
C_HEAD_DIM = 64
C_WIDTH = 1024
D_FF_DENSE = 2816
N_EXPERTS = 8
D_FF_EXPERT = 3584
LANES = 128
MAX_DECAY_EXP = 60.0
LOG2E = math.log2(math.e)
ROW_TILE = 8

E_U, E_V, E_Q, E_K, E_VB, E_OG, E_G, E_END = 0, 512, 1024, 1280, 1536, 2048, 2560, 2688
O_Q, O_K, O_V, O_OG, O_F, O_END = 0, 1024, 2048, 3072, 4096, 4224

VMEM_LIMIT = 56 * 1024 * 1024


def _rms(x, g):
    ms = jnp.mean(x * x, axis=-1, keepdims=True)
    return x * lax.rsqrt(ms + EPS) * g


def _gelu_tanh(x):
    c = math.sqrt(2.0 / math.pi)
    return x * (0.5 * (1.0 + jnp.tanh(c * (x + 0.044715 * (x * x * x)))))


def _sigmoid(x):
    return 1.0 / (1.0 + jnp.exp(-x))


def _log_sigmoid(x):
    return jnp.minimum(x, 0.0) - jnp.log(1.0 + jnp.exp(-jnp.abs(x)))


def _dot(a, b):
    return jnp.dot(a, b, preferred_element_type=F32)


def _dot_nt(a, b):
    return lax.dot_general(a, b, (((1,), (1,)), ((), ())), preferred_element_type=F32)


def _split3(x):
    hi = x.astype(BF16)
    r1 = x - hi.astype(F32)
    mid = r1.astype(BF16)
    lo = (r1 - mid.astype(F32)).astype(BF16)
    return hi, mid, lo


def _cumsum_rows(tril_b, x):
    hi, mid, lo = _split3(x)
    return _dot(tril_b, hi) + _dot(tril_b, mid) + _dot(tril_b, lo)


def _const_spec(shape):
    nd = len(shape)
    return pl.BlockSpec(shape, lambda *_: (0,) * nd)


def _even_mixer_kernel(x_ref, nrm_ref, win_ref, gup_ref, gb_ref, ws_ref, bs_ref, lng_ref, lnb_ref,
                       hg_ref, wo_ref, o_ref, z_ref, mix_ref, st_ref, *, tiles_per_batch, n_chunks):
    i = pl.program_id(0)

    @pl.when(i % tiles_per_batch == 0)
    def _():
        st_ref[...] = jnp.zeros_like(st_ref)

    h = _rms(x_ref[...], nrm_ref[...]).astype(BF16)
    z_ref[...] = _dot(h, win_ref[...])

    row = lax.broadcasted_iota(jnp.int32, (CHUNK, CHUNK), 0)
    col = lax.broadcasted_iota(jnp.int32, (CHUNK, CHUNK), 1)
    tril_b = (col <= row).astype(BF16)
    sub_row = row & (SUB - 1)
    head_lane = lax.broadcasted_iota(jnp.int32, (1, B_QK_WIDTH), 1) // B_KEY_DIM
    bd_mask = (lax.broadcasted_iota(jnp.int32, (B_V_WIDTH, B_QK_WIDTH), 0) // B_VAL_DIM
               == lax.broadcasted_iota(jnp.int32, (B_V_WIDTH, B_QK_WIDTH), 1) // B_KEY_DIM)

    def chunk_body(c, carry):
        rows = pl.ds(pl.multiple_of(c * CHUNK, CHUNK), CHUNK)

        u = _gelu_tanh(z_ref[rows, E_U:E_V])
        v = _gelu_tanh(z_ref[rows, E_V:E_Q])
        mu = jnp.mean(v, axis=-1, keepdims=True)
        vc = v - mu
        var = jnp.mean(vc * vc, axis=-1, keepdims=True)
        vln = (vc * lax.rsqrt(var + EPS) * lng_ref[...] + lnb_ref[...]).astype(BF16)
        for g in range(A_GROUPS):
            sl = slice(g * LANES, (g + 1) * LANES)
            mixed = _dot(ws_ref[g], vln[:, sl]) + bs_ref[g]
            mix_ref[rows, sl] = (u[:, sl] * mixed).astype(BF16)

        q = z_ref[rows, E_Q:E_K] * (B_KEY_DIM ** -0.5)
        k = z_ref[rows, E_K:E_VB]
        vb = z_ref[rows, E_VB:E_OG]
        og = z_ref[rows, E_OG:E_G]
        glr = z_ref[rows, E_G:E_END].astype(BF16)
        logit = _dot(glr, gup_ref[...]) + gb_ref[...]
        log_a = _log_sigmoid(logit) * (1.0 / B_GATE_NORMALIZER)
        g_cum = _cumsum_rows(tril_b, log_a)
        g_last = g_cum[CHUNK - 1:CHUNK, :]
        st = st_ref[...]
        o = _dot_nt((q * jnp.exp(g_cum)).astype(BF16), st.astype(BF16))

        p_rows = [[None] * N_SUB for _ in range(B_HEADS)]
        for s in range(N_SUB):
            gs = g_cum[s * SUB:(s + 1) * SUB, :]
            if s == 0:
                qt = q[0:SUB, :] * jnp.exp(gs)
                kt = k * jnp.exp(jnp.minimum(-g_cum, MAX_DECAY_EXP))
            else:
                ref_g = g_cum[s * SUB - 1:s * SUB, :]
                qt = q[s * SUB:(s + 1) * SUB, :] * jnp.exp(gs - ref_g)
                kt = k * jnp.exp(jnp.minimum(ref_g - g_cum, MAX_DECAY_EXP))
            qs = jnp.concatenate([jnp.where(head_lane == hh, qt, 0.0) for hh in range(B_HEADS)],
                                 axis=0).astype(BF16)
            sc = _dot_nt(qs, kt.astype(BF16))
            sc = jnp.where(col <= (s * SUB + sub_row), sc, 0.0)
            for hh in range(B_HEADS):
                p_rows[hh][s] = sc[hh * SUB:(hh + 1) * SUB, :]

        vb_b = vb.astype(BF16)
        for hh in range(B_HEADS):
            sl = slice(hh * B_VAL_DIM, (hh + 1) * B_VAL_DIM)
            ph = jnp.concatenate(p_rows[hh], axis=0).astype(BF16)
            oh = o[:, sl] + _dot(ph, vb_b[:, sl])
            on = _rms(oh, hg_ref[hh])
            ogh = og[:, sl]
            mix_ref[rows, A_WIDTH + hh * B_VAL_DIM:A_WIDTH + (hh + 1) * B_VAL_DIM] = (
                on * (ogh * _sigmoid(ogh))).astype(BF16)

        k_dec = (k * jnp.exp(g_last - g_cum)).astype(BF16)
        upd = _dot(vb.T.astype(BF16), k_dec)
        st_ref[...] = jnp.exp(g_last) * st + jnp.where(bd_mask, upd, 0.0)
        return carry

    lax.fori_loop(0, n_chunks, chunk_body, 0)
    o_ref[...] = x_ref[...] + _dot(mix_ref[...], wo_ref[...])


def _even_mixer(x, nrm, win, gup, gb, ws, bs, lng, lnb, hg, wo, *, seq, tm):
    t = x.shape[0]
    kern = functools.partial(_even_mixer_kernel, tiles_per_batch=seq // tm, n_chunks=tm // CHUNK)
    return pl.pallas_call(
        kern,
        out_shape=jax.ShapeDtypeStruct((t, D_MODEL), F32),
        grid=(t // tm,),
        in_specs=[
            pl.BlockSpec((tm, D_MODEL), lambda i: (i, 0)),
            _const_spec(nrm.shape), _const_spec(win.shape), _const_spec(gup.shape), _const_spec(gb.shape),
            _const_spec(ws.shape), _const_spec(bs.shape), _const_spec(lng.shape), _const_spec(lnb.shape),
            _const_spec(hg.shape), _const_spec(wo.shape),
        ],
        out_specs=pl.BlockSpec((tm, D_MODEL), lambda i: (i, 0)),
        scratch_shapes=[
            pltpu.VMEM((tm, E_END), F32),
            pltpu.VMEM((tm, D_MODEL), BF16),
            pltpu.VMEM((B_V_WIDTH, B_QK_WIDTH), F32),
        ],
        compiler_params=pltpu.CompilerParams(dimension_semantics=("arbitrary",), vmem_limit_bytes=VMEM_LIMIT),
        name="even_mixer",
    )(x, nrm, win, gup, gb, ws, bs, lng, lnb, hg, wo)


def _dense_ffn_kernel(x_ref, nrm_ref, w1_ref, w3_ref, w2_ref, o_ref, acc_ref):
    kf = pl.program_id(1)
    h = _rms(x_ref[...], nrm_ref[...]).astype(BF16)
    a = _dot(h, w1_ref[...])
    b = _dot(h, w3_ref[...])
    part = _dot((a * _sigmoid(a) * b).astype(BF16), w2_ref[...])

    @pl.when(kf == 0)
    def _():
        acc_ref[...] = x_ref[...] + part

    @pl.when(kf > 0)
    def _():
        acc_ref[...] += part

    @pl.when(kf == pl.num_programs(1) - 1)
    def _():
        o_ref[...] = acc_ref[...]


def _dense_ffn(x, nrm, w1, w3, w2, *, tm, tf):
    t = x.shape[0]
    dff = w1.shape[1]
    return pl.pallas_call(
        _dense_ffn_kernel,
        out_shape=jax.ShapeDtypeStruct((t, D_MODEL), F32),
        grid=(t // tm, dff // tf),
        in_specs=[
            pl.BlockSpec((tm, D_MODEL), lambda i, k: (i, 0)),
            pl.BlockSpec((1, D_MODEL), lambda i, k: (0, 0)),
            pl.BlockSpec((D_MODEL, tf), lambda i, k: (0, k)),
            pl.BlockSpec((D_MODEL, tf), lambda i, k: (0, k)),
            pl.BlockSpec((tf, D_MODEL), lambda i, k: (k, 0)),
        ],
        out_specs=pl.BlockSpec((tm, D_MODEL), lambda i, k: (i, 0)),
        scratch_shapes=[pltpu.VMEM((tm, D_MODEL), F32)],
        compiler_params=pltpu.CompilerParams(dimension_semantics=("arbitrary", "arbitrary"),
                                             vmem_limit_bytes=VMEM_LIMIT),
        name="dense_ffn",
    )(x, nrm, w1, w3, w2)


def _head_rms(x, gain):
    lo = lax.broadcasted_iota(jnp.int32, (1, LANES), 1) < C_HEAD_DIM
    outs = []
    for t in range(C_WIDTH // LANES):
        xt = x[:, t * LANES:(t + 1) * LANES]
        sq = xt * xt
        s_lo = jnp.sum(jnp.where(lo, sq, 0.0), axis=-1, keepdims=True)
        s_hi = jnp.sum(jnp.where(lo, 0.0, sq), axis=-1, keepdims=True)
        inv = jnp.where(lo, lax.rsqrt(s_lo * (1.0 / C_HEAD_DIM) + EPS), lax.rsqrt(s_hi * (1.0 / C_HEAD_DIM) + EPS))
        outs.append(xt * inv)
    return jnp.concatenate(outs, axis=-1) * gain


def _odd_inproj_kernel(x_ref, nrm_ref, w_ref, fb_ref, qg_ref, kg_ref,
                       q_ref, k_ref, v_ref, gate_ref, c_ref, z_ref, carry_ref, *, tiles_per_batch, n_chunks):
    i = pl.program_id(0)

    @pl.when(i % tiles_per_batch == 0)
    def _():
        carry_ref[...] = jnp.zeros_like(carry_ref)

    h = _rms(x_ref[...], nrm_ref[...]).astype(BF16)
    z_ref[...] = _dot(h, w_ref[...])
    q_ref[...] = (_head_rms(z_ref[:, O_Q:O_K], qg_ref[...]) * (C_HEAD_DIM ** -0.5 * LOG2E)).astype(BF16)
    k_ref[...] = _head_rms(z_ref[:, O_K:O_V], kg_ref[...]).astype(BF16)
    v_ref[...] = z_ref[:, O_V:O_OG].astype(BF16)
    gate_ref[...] = _sigmoid(z_ref[:, O_OG:O_F]).astype(BF16)

    row = lax.broadcasted_iota(jnp.int32, (CHUNK, CHUNK), 0)
    col = lax.broadcasted_iota(jnp.int32, (CHUNK, CHUNK), 1)
    tril_b = (col <= row).astype(BF16)
    carry = carry_ref[...]
    for c in range(n_chunks):
        rows = slice(c * CHUNK, (c + 1) * CHUNK)
        log_f = _log_sigmoid(z_ref[rows, O_F:O_END] + fb_ref[...])
        cs = _cumsum_rows(tril_b, log_f) + carry
        c_ref[rows, :] = cs
        carry = cs[CHUNK - 1:CHUNK, :]
    carry_ref[...] = carry


def _odd_inproj(x, nrm, w, fb, qg, kg, *, seq, tm):
    t = x.shape[0]
    kern = functools.partial(_odd_inproj_kernel, tiles_per_batch=seq // tm, n_chunks=tm // CHUNK)
    row_spec = pl.BlockSpec((tm, C_WIDTH), lambda i: (i, 0))
    return pl.pallas_call(
        kern,
        out_shape=[jax.ShapeDtypeStruct((t, C_WIDTH), BF16)] * 4 + [jax.ShapeDtypeStruct((t, LANES), F32)],
        grid=(t // tm,),
        in_specs=[row_spec, _const_spec(nrm.shape), _const_spec(w.shape), _const_spec(fb.shape),
                  _const_spec(qg.shape), _const_spec(kg.shape)],
        out_specs=[row_spec] * 4 + [pl.BlockSpec((tm, LANES), lambda i: (i, 0))],
        scratch_shapes=[pltpu.VMEM((tm, O_END), F32), pltpu.VMEM((1, LANES), F32)],
        compiler_params=pltpu.CompilerParams(dimension_semantics=("arbitrary",), vmem_limit_bytes=VMEM_LIMIT),
        name="odd_inproj",
    )(x, nrm, w, fb, qg, kg)


NEG_BIG = -1e30


def _fox_kernel(q_ref, k_ref, vt_ref, gate_ref, ckb_ref, o_ref, s_ref, *, tq, tk):
    qi = pl.program_id(2)
    ng = tq // tk
    lo = lax.broadcasted_iota(jnp.int32, (1, LANES), 1) < C_HEAD_DIM
    key_i = lax.broadcasted_iota(jnp.int32, (tk, tk), 0)
    qry_i = lax.broadcasted_iota(jnp.int32, (tk, tk), 1)
    causal = key_i <= qry_i
    chains = [(hh, r) for r in range(ng) for hh in range(2)]
    qms = []
    for hh, r in chains:
        qr = q_ref[r * tk:(r + 1) * tk, :]
        zero = jnp.zeros_like(qr)
        qms.append(jnp.where(lo, qr, zero) if hh == 0 else jnp.where(lo, zero, qr))

    def key_rows(j):
        return pl.ds(pl.multiple_of(j * tk, tk), tk)

    def scores_to_scratch(j, slot, live, modes):
        ks = key_rows(j)
        kb = k_ref[ks, :]
        raw = [_dot_nt(kb, qms[idx]) for idx in live]
        maxes = []
        for s_raw, idx in zip(raw, live):
            hh, r = chains[idx]
            cb = ckb_ref[hh, ks, :]
            s = s_raw - jnp.concatenate([cb] * (tk // LANES), axis=1)
            if modes[r] == "diag":
                s = jnp.where(causal, s, NEG_BIG)
            s_ref[slot, idx] = s
            maxes.append(jnp.max(s, axis=0, keepdims=True))
        return maxes

    def softmax_pv(j, slot, maxes, live, state):
        vtb = vt_ref[j]
        new = list(state)
        probs = []
        for bm, idx in zip(maxes, live):
            m, l, _ = state[3 * idx:3 * idx + 3]
            m_new = jnp.maximum(m, bm)
            p = jnp.exp2(s_ref[slot, idx] - m_new)
            alpha = jnp.exp2(m - m_new)
            new[3 * idx] = m_new
            new[3 * idx + 1] = alpha * l + jnp.sum(p, axis=0, keepdims=True)
            probs.append((alpha, p.astype(BF16)))
        for (alpha, p), idx in zip(probs, live):
            hh, r = chains[idx]
            new[3 * idx + 2] = alpha * state[3 * idx + 2] + _dot(vtb[hh * C_HEAD_DIM:(hh + 1) * C_HEAD_DIM, :], p)
        return new

    assert ng == 2
    all_chains = list(range(len(chains)))
    late_chains = [idx for idx, (hh, r) in enumerate(chains) if r == 1]
    n_state = 3 * len(chains)
    state = []
    for _ in chains:
        state += [jnp.full((1, tk), NEG_BIG, F32), jnp.zeros((1, tk), F32), jnp.zeros((C_HEAD_DIM, tk), F32)]
    n_full = qi * ng
    full_modes = ("full", "full")
    first_diag = ("diag", "full")
    last_diag = ("skip", "diag")

    def two_blocks(j, carry, next_modes):
        st, mx0 = list(carry[:n_state]), carry[n_state:]
        mx1 = scores_to_scratch(j + 1, 1, all_chains, full_modes)
        st = softmax_pv(j, 0, mx0, all_chains, st)
        mx0 = scores_to_scratch(j + 2, 0, all_chains, next_modes)
        st = softmax_pv(j + 1, 1, mx1, all_chains, st)
        return tuple(st) + tuple(mx0)

    def with_full_blocks(_):
        first = scores_to_scratch(0, 0, all_chains, full_modes)
        carry = lax.fori_loop(0, qi - 1, lambda i, c: two_blocks(2 * i, c, full_modes), tuple(state) + tuple(first))
        return two_blocks(n_full - 2, carry, first_diag)

    def no_full_blocks(_):
        return tuple(state) + tuple(scores_to_scratch(0, 0, all_chains, first_diag))

    carry = lax.cond(qi > 0, with_full_blocks, no_full_blocks, 0)
    state, mx0 = list(carry[:n_state]), carry[n_state:]
    mx1 = scores_to_scratch(n_full + 1, 1, late_chains, last_diag)
    state = softmax_pv(n_full, 0, mx0, all_chains, state)
    carry = softmax_pv(n_full + 1, 1, mx1, late_chains, state)

    for r in range(ng):
        parts = []
        for hh in range(2):
            idx = chains.index((hh, r))
            parts.append(carry[3 * idx + 2] / carry[3 * idx + 1])
        o = jnp.concatenate(parts, axis=0).T
        rows = slice(r * tk, (r + 1) * tk)
        o_ref[rows, :] = (o * gate_ref[rows, :].astype(F32)).astype(BF16)


def _fox_attention(q, k, vt, gate, ckb, *, batch, seq, tq, tk):
    t = q.shape[0]
    n_pairs = C_HEADS // 2
    nq = seq // tq
    kern = functools.partial(_fox_kernel, tq=tq, tk=tk)
    return pl.pallas_call(
        kern,
        out_shape=jax.ShapeDtypeStruct((t, C_WIDTH), BF16),
        grid=(batch, n_pairs, nq),
        in_specs=[
            pl.BlockSpec((tq, LANES), lambda b, p, i: (b * nq + i, p)),
            pl.BlockSpec((seq, LANES), lambda b, p, i: (b, p)),
            pl.BlockSpec((None, None, seq // tk, LANES, tk), lambda b, p, i: (b, p, 0, 0, 0)),
            pl.BlockSpec((tq, LANES), lambda b, p, i: (b * nq + i, p)),
            pl.BlockSpec((None, None, 2, seq, LANES), lambda b, p, i: (b, p, 0, 0, 0)),
        ],
        out_specs=pl.BlockSpec((tq, LANES), lambda b, p, i: (b * nq + i, p)),
        scratch_shapes=[pltpu.VMEM((2, 2 * (tq // tk), tk, tk), F32)],
        compiler_params=pltpu.CompilerParams(dimension_semantics=("arbitrary", "arbitrary", "arbitrary"),
                                             vmem_limit_bytes=VMEM_LIMIT),
        name="fox_attn",
    )(q, k, vt, gate, ckb)


def _store_token_tiles(dst_ref, val, n_rows):
    for s in range(ROW_TILE):
        dst_ref[pl.ds(s, n_rows, stride=ROW_TILE), :] = val[:, s * LANES:(s + 1) * LANES]


def _load_token_tiles(src_ref, n_rows):
    return jnp.concatenate([src_ref[pl.ds(s, n_rows, stride=ROW_TILE), :] for s in range(ROW_TILE)], axis=1)


def _odd_out_kernel(x_ref, a_ref, wo_ref, nrm_ref, r_ref, x3_ref, h_ref, route_ref):
    x3 = x_ref[...] + _dot(a_ref[...], wo_ref[...])
    x3_ref[...] = x3
    h = _rms(x3, nrm_ref[...])
    _store_token_tiles(h_ref, h, h.shape[0])

    h_hi = h.astype(BF16)
    h_lo = (h - h_hi.astype(F32)).astype(BF16)
    r = r_ref[...]
    r_hi = r.astype(BF16)
    r_lo = (r - r_hi.astype(F32)).astype(BF16)
    logits = _dot(h_hi, r_hi) + (_dot(h_lo, r_hi) + _dot(h_hi, r_lo))

    lane = lax.broadcasted_iota(jnp.int32, logits.shape, 1).astype(F32)
    neg_inf = jnp.float32(-jnp.inf)
    lg = jnp.where(lane < N_EXPERTS, logits, neg_inf)
    m1 = jnp.max(lg, axis=-1, keepdims=True)
    i1 = jnp.min(jnp.where(lg == m1, lane, float(LANES)), axis=-1, keepdims=True)
    lg2 = jnp.where(lane == i1, neg_inf, lg)
    m2 = jnp.max(lg2, axis=-1, keepdims=True)
    i2 = jnp.min(jnp.where(lg2 == m2, lane, float(LANES)), axis=-1, keepdims=True)
    e2 = jnp.exp(m2 - m1)
    g1 = 1.0 / (1.0 + e2)
    g2 = e2 / (1.0 + e2)
    route_ref[...] = jnp.where(lane == 0, i1, jnp.where(lane == 1, i2, jnp.where(lane == 2, g1,
                               jnp.where(lane == 3, g2, 0.0))))


def _odd_out(x, a, wo, nrm, router, *, tm):
    t = x.shape[0]
    row_spec = pl.BlockSpec((tm, D_MODEL), lambda i: (i, 0))
    return pl.pallas_call(
        _odd_out_kernel,
        out_shape=[jax.ShapeDtypeStruct((t, D_MODEL), F32), jax.ShapeDtypeStruct((t * ROW_TILE, LANES), F32),
                   jax.ShapeDtypeStruct((t, LANES), F32)],
        grid=(t // tm,),
        in_specs=[row_spec, row_spec, _const_spec(wo.shape), _const_spec(nrm.shape), _const_spec(router.shape)],
        out_specs=[row_spec, pl.BlockSpec((tm * ROW_TILE, LANES), lambda i: (i, 0)),
                   pl.BlockSpec((tm, LANES), lambda i: (i, 0))],
        compiler_params=pltpu.CompilerParams(dimension_semantics=("arbitrary",), vmem_limit_bytes=VMEM_LIMIT),
        name="odd_out",
    )(x, a, wo, nrm, router)


def _row_gather_copy(src_hbm, src_row, dst_ref, dst_row, sem):
    return pltpu.make_async_copy(
        src_hbm.at[pl.ds(pl.multiple_of(src_row * ROW_TILE, ROW_TILE), ROW_TILE), :],
        dst_ref.at[pl.ds(pl.multiple_of(dst_row * ROW_TILE, ROW_TILE), ROW_TILE), :],
        sem)


GATHER_UNROLL = 8


def _start_row_gathers(src_hbm, idx_ref, idx_row, first, count, dst_ref, sem, *, inline):
    if inline:
        for u in range(count):
            _row_gather_copy(src_hbm, idx_ref[idx_row, first + u], dst_ref, first + u, sem).start()
        return

    def issue(r, c):
        _row_gather_copy(src_hbm, idx_ref[idx_row, r], dst_ref, r, sem).start()
        return c

    lax.fori_loop(first, first + count, issue, 0, unroll=GATHER_UNROLL)


def _wait_row_gathers(src_hbm, dst_ref, sem):
    pltpu.make_async_copy(src_hbm.at[pl.ds(0, dst_ref.shape[0]), :], dst_ref, sem).wait()


def _moe_ffn_kernel(te_ref, nu_ref, tok_ref, tok_next_ref, h_hbm, w1_ref, w3_ref, w2_ref, o_ref, xs_ref, buf_ref,
                    acc_ref, sem, *, tm, nk):
    i = pl.program_id(0)
    kf = pl.program_id(1)
    last = nk - 1
    n_used = nu_ref[0]
    used = i < n_used
    slot = i % 2

    @pl.when(jnp.logical_and(i == 0, kf == 0))
    def _():
        _start_row_gathers(h_hbm, tok_ref, 0, 0, tm, buf_ref.at[0], sem.at[0], inline=False)

    @pl.when(jnp.logical_and(i <= n_used, kf == 0))
    def _():
        _wait_row_gathers(h_hbm, buf_ref.at[slot], sem.at[slot])
        xs_ref[...] = _load_token_tiles(buf_ref.at[slot], tm).astype(BF16)

    @pl.when(used)
    def _():
        rows_per_step = tm // nk
        _start_row_gathers(h_hbm, tok_next_ref, 0, kf * rows_per_step, rows_per_step, buf_ref.at[1 - slot],
                           sem.at[1 - slot], inline=True)
        x = xs_ref[...]
        a = _dot(x, w1_ref[...])
        b = _dot(x, w3_ref[...])
        part = _dot((a * _sigmoid(a) * b).astype(BF16), w2_ref[...])

        @pl.when(kf == 0)
        def _():
            acc_ref[...] = part

        @pl.when(jnp.logical_and(kf > 0, kf < last))
        def _():
            acc_ref[...] += part

        @pl.when(kf == last)
        def _():
            _store_token_tiles(o_ref, acc_ref[...] + part, tm)

    @pl.when(jnp.logical_and(jnp.logical_not(used), kf == last))
    def _():
        o_ref[...] = jnp.zeros_like(o_ref)


def _moe_ffn(tile_expert, n_used, row_tok, h_tiles, w1, w3, w2, *, tm, tf):
    n_tiles = row_tok.shape[0] - 1
    nk = D_FF_EXPERT // tf
    assert nk >= 2 and tm % nk == 0

    def kchunk(i, k, nu):
        return jnp.where(i < nu[0], k, nk - 1)

    grid_spec = pltpu.PrefetchScalarGridSpec(
        num_scalar_prefetch=2,
        grid=(n_tiles, nk),
        in_specs=[
            pl.BlockSpec((None, 1, tm), lambda i, k, te, nu: (i, 0, 0), memory_space=pltpu.SMEM),
            pl.BlockSpec((None, 1, tm), lambda i, k, te, nu: (i + 1, 0, 0), memory_space=pltpu.SMEM),
            pl.BlockSpec(memory_space=pl.ANY),
            pl.BlockSpec((None, D_MODEL, tf), lambda i, k, te, nu: (te[i], 0, kchunk(i, k, nu))),
            pl.BlockSpec((None, D_MODEL, tf), lambda i, k, te, nu: (te[i], 0, kchunk(i, k, nu))),
            pl.BlockSpec((None, tf, D_MODEL), lambda i, k, te, nu: (te[i], kchunk(i, k, nu), 0)),
        ],
        out_specs=pl.BlockSpec((tm * ROW_TILE, LANES), lambda i, k, te, nu: (i, 0)),
        scratch_shapes=[pltpu.VMEM((tm, D_MODEL), BF16), pltpu.VMEM((2, tm * ROW_TILE, LANES), F32),
                        pltpu.VMEM((tm, D_MODEL), F32), pltpu.SemaphoreType.DMA((2,))],
    )
    return pl.pallas_call(
        functools.partial(_moe_ffn_kernel, tm=tm, nk=nk),
        out_shape=jax.ShapeDtypeStruct((n_tiles * tm * ROW_TILE, LANES), F32),
        grid_spec=grid_spec,
        compiler_params=pltpu.CompilerParams(dimension_semantics=("arbitrary", "arbitrary"),
                                             vmem_limit_bytes=VMEM_LIMIT),
        name="moe_ffn",
    )(tile_expert, n_used, row_tok, row_tok, h_tiles, w1, w3, w2)


def _moe_combine_kernel(pos_ref, pos_next_ref, x_ref, route_ref, nrm_ref, y_hbm, o_ref, buf_ref, sem, *, tm):
    i = pl.program_id(0)
    slot = i % 2

    def start_tile(idx_ref, s):
        for k in range(2):
            _start_row_gathers(y_hbm, idx_ref, k, 0, tm, buf_ref.at[s, k], sem.at[s, k], inline=False)

    @pl.when(i == 0)
    def _():
        start_tile(pos_ref, 0)

    @pl.when(i + 1 < pl.num_programs(0))
    def _():
        start_tile(pos_next_ref, 1 - slot)

    for k in range(2):
        _wait_row_gathers(y_hbm, buf_ref.at[slot, k], sem.at[slot, k])
    g1 = route_ref[:, 2:3]
    g2 = route_ref[:, 3:4]
    x = x_ref[...] + (g1 * _load_token_tiles(buf_ref.at[slot, 0], tm) + g2 * _load_token_tiles(buf_ref.at[slot, 1], tm))
    o_ref[...] = _rms(x, nrm_ref[...])


def _moe_combine(pos, x, route, nrm, y_tiles, *, tm):
    t = x.shape[0]
    n = t // tm
    row_spec = pl.BlockSpec((tm, D_MODEL), lambda i: (i, 0))
    return pl.pallas_call(
        functools.partial(_moe_combine_kernel, tm=tm),
        out_shape=jax.ShapeDtypeStruct((t, D_MODEL), F32),
        grid=(n,),
        in_specs=[pl.BlockSpec((None, 2, tm), lambda i: (i, 0, 0), memory_space=pltpu.SMEM),
                  pl.BlockSpec((None, 2, tm), lambda i: (jnp.minimum(i + 1, n - 1), 0, 0), memory_space=pltpu.SMEM),
                  row_spec, pl.BlockSpec((tm, LANES), lambda i: (i, 0)), _const_spec(nrm.shape),
                  pl.BlockSpec(memory_space=pl.ANY)],
        out_specs=row_spec,
        scratch_shapes=[pltpu.VMEM((2, 2, tm * ROW_TILE, LANES), F32), pltpu.SemaphoreType.DMA((2, 2))],
        compiler_params=pltpu.CompilerParams(dimension_semantics=("arbitrary",), vmem_limit_bytes=VMEM_LIMIT),
        name="moe_combine",
    )(pos, pos, x, route, nrm, y_tiles)


def _routing_tables(idx1, idx2, *, tm, n_tiles):
    t = idx1.shape[0]
    e_flat = jnp.concatenate([idx1, idx2])
    onehot = (e_flat[:, None] == jnp.arange(N_EXPERTS, dtype=jnp.int32)[None, :]).astype(jnp.int32)
    csum = jnp.cumsum(onehot, axis=0)
    rank = jnp.sum((csum - onehot) * onehot, axis=1)
    counts = csum[-1]
    tiles_e = (counts + tm - 1) // tm
    tile_end = jnp.cumsum(tiles_e)
    tile_start = tile_end - tiles_e
    pos = jnp.sum(onehot * tile_start[None, :], axis=1) * tm + rank
    n_used = tile_end[-1]
    tile_ids = jnp.arange(n_tiles, dtype=jnp.int32)
    te = jnp.sum((tile_ids[:, None] >= tile_end[None, :]).astype(jnp.int32), axis=1)
    te_last = jnp.sum((n_used - 1 >= tile_end).astype(jnp.int32))
    tile_expert = jnp.where(tile_ids < n_used, te, te_last).astype(jnp.int32)
    tok = jnp.concatenate([jnp.arange(t, dtype=jnp.int32)] * 2)
    row_tok = jnp.zeros((n_tiles * tm,), jnp.int32).at[pos].set(tok)
    return pos[:t], pos[t:], row_tok, tile_expert, n_used.reshape(1).astype(jnp.int32)


def _pad_cols(w, n):
    return jnp.pad(w, ((0, 0), (0, n - w.shape[1])))


def kernel(x, even_norm_mix, even_w_in, even_gate_up, even_gate_bias, even_w_s, even_b_s, even_ln_g, even_ln_b,
           even_head_g, even_w_o, even_norm_ffn, even_ffn_w1, even_ffn_w3, even_ffn_w2, odd_norm_mix, odd_w_in,
           odd_forget_bias, odd_q_g, odd_k_g, odd_w_o, odd_norm_ffn, odd_router, odd_exp_w1, odd_exp_w3,
           odd_exp_w2, final_norm):
    batch, seq, d = x.shape
    t = batch * seq
    xt = x.reshape(t, d)
    tm = min(512, seq)

    w_in = even_w_in[0]
    u_w, v_w, q_w, k_w, g_w, vb_w, og_w = jnp.split(w_in, [512, 1024, 1280, 1536, 1552, 2064], axis=1)
    win_e = jnp.concatenate([u_w, v_w, q_w, k_w, vb_w, og_w, _pad_cols(g_w, LANES)], axis=1).astype(BF16)
    gup = jnp.pad(even_gate_up[0], ((0, LANES - B_GATE_RANK), (0, 0))).astype(BF16)
    gb = even_gate_bias[0].reshape(1, B_QK_WIDTH)
    tril = jnp.tril(jnp.ones((CHUNK, CHUNK), dtype=bool))
    ws = jnp.where(tril[None], even_w_s[0], 0.0).astype(BF16)
    bs = jnp.broadcast_to(even_b_s[0][:, :, None], (A_GROUPS, CHUNK, LANES))
    lng = even_ln_g[0].reshape(1, A_WIDTH)
    lnb = even_ln_b[0].reshape(1, A_WIDTH)
    hg = even_head_g[0].reshape(B_HEADS, 1, B_VAL_DIM)
    x1 = _even_mixer(xt, even_norm_mix[0].reshape(1, d), win_e, gup, gb, ws, bs, lng, lnb, hg,
                     even_w_o[0].astype(BF16), seq=seq, tm=tm)
    x2 = _dense_ffn(x1, even_norm_ffn[0].reshape(1, d), even_ffn_w1[0].astype(BF16), even_ffn_w3[0].astype(BF16),
                    even_ffn_w2[0].astype(BF16), tm=tm, tf=D_FF_DENSE // 2)

    q_w, k_w, v_w, og_w, f_w = jnp.split(odd_w_in[0], [1024, 2048, 3072, 4096], axis=1)
    win_o = jnp.concatenate([q_w, k_w, v_w, og_w, _pad_cols(f_w, LANES)], axis=1).astype(BF16)
    fb = jnp.pad(odd_forget_bias[0], (0, LANES - C_HEADS)).reshape(1, LANES)
    qg = jnp.tile(odd_q_g[0], C_HEADS).reshape(1, C_WIDTH)
    kg = jnp.tile(odd_k_g[0], C_HEADS).reshape(1, C_WIDTH)
    q, k, v, gate, c = _odd_inproj(x2, odd_norm_mix[0].reshape(1, d), win_o, fb, qg, kg, seq=seq, tm=tm)
    tk = min(256, seq)
    tq = min(512, seq)
    n_pairs = C_HEADS // 2
    ck = (c[:, :C_HEADS] * LOG2E).reshape(batch, seq, n_pairs, 2)
    ckb = jnp.broadcast_to(jnp.transpose(ck, (0, 2, 3, 1))[..., None], (batch, n_pairs, 2, seq, LANES))
    vt = jnp.transpose(v.reshape(batch, seq // tk, tk, n_pairs, LANES), (0, 3, 1, 4, 2))
    attn = _fox_attention(q, k, vt, gate, ckb, batch=batch, seq=seq, tq=tq, tk=tk)

    router = _pad_cols(odd_router[0], LANES)
    x3, h_tiles, route = _odd_out(x2, attn, odd_w_o[0].astype(BF16), odd_norm_ffn[0].reshape(1, d), router, tm=tm)

    tm_moe = 512
    tm_comb = min(256, seq)
    n_tiles = (2 * t) // tm_moe + N_EXPERTS + 1
    idx1 = route[:, 0].astype(jnp.int32)
    idx2 = route[:, 1].astype(jnp.int32)
    pos1, pos2, row_tok, tile_expert, n_used = _routing_tables(idx1, idx2, tm=tm_moe, n_tiles=n_tiles + 1)
    y_tiles = _moe_ffn(tile_expert[:n_tiles], n_used, row_tok.reshape(n_tiles + 1, 1, tm_moe), h_tiles,
                       odd_exp_w1[0].astype(BF16), odd_exp_w3[0].astype(BF16), odd_exp_w2[0].astype(BF16),
                       tm=tm_moe, tf=896)
    pos = jnp.stack([pos1.reshape(t // tm_comb, tm_comb), pos2.reshape(t // tm_comb, tm_comb)], axis=1)
    out = _moe_combine(pos, x3, route, final_norm.reshape(1, d), y_tiles, tm=tm_comb)
    return out.reshape(batch, seq, d)
```

```python
import functools
import math

import jax
import jax.numpy as jnp
from jax import lax
from jax.experimental import pallas as pl
from jax.experimental.pallas import tpu as pltpu

F32 = jnp.float32
BF16 = jnp.bfloat16
HIGHEST = lax.Precision.HIGHEST

EPS = 1e-6
D_MODEL = 1024
CHUNK = 128
SUB = 32
N_SUB = CHUNK // SUB
A_GROUPS = 4
A_WIDTH = 512
B_HEADS = 4
B_KEY_DIM = 64
B_VAL_DIM = 128
B_QK_WIDTH = 256
B_V_WIDTH = 512
B_GATE_RANK = 16
B_GATE_NORMALIZER = 16.0
C_HEADS = 16
C_HEAD_DIM = 64
C_WIDTH = 1024
D_FF_DENSE = 2816
N_EXPERTS = 8
D_FF_EXPERT = 3584
LANES = 128
MAX_DECAY_EXP = 60.0
LOG2E = math.log2(math.e)
ROW_TILE = 8

E_U, E_V, E_Q, E_K, E_VB, E_OG, E_G, E_END = 0, 512, 1024, 1280, 1536, 2048, 2560, 2688
O_Q, O_K, O_V, O_OG, O_F, O_END = 0, 1024, 2048, 3072, 4096, 4224

VMEM_LIMIT = 56 * 1024 * 1024


def _rms(x, g):
    ms = jnp.mean(x * x, axis=-1, keepdims=True)
    return x * lax.rsqrt(ms + EPS) * g


def _gelu_tanh(x):
    c = math.sqrt(2.0 / math.pi)
    return x * (0.5 * (1.0 + jnp.tanh(c * (x + 0.044715 * (x * x * x)))))


def _sigmoid(x):
    return 1.0 / (1.0 + jnp.exp(-x))


def _log_sigmoid(x):
    return jnp.minimum(x, 0.0) - jnp.log(1.0 + jnp.exp(-jnp.abs(x)))


def _dot(a, b):
    return jnp.dot(a, b, preferred_element_type=F32)


def _dot_nt(a, b):
    return lax.dot_general(a, b, (((1,), (1,)), ((), ())), preferred_element_type=F32)


def _split3(x):
    hi = x.astype(BF16)
    r1 = x - hi.astype(F32)
    mid = r1.astype(BF16)
    lo = (r1 - mid.astype(F32)).astype(BF16)
    return hi, mid, lo


def _cumsum_rows(tril_b, x):
    hi, mid, lo = _split3(x)
    return _dot(tril_b, hi) + _dot(tril_b, mid) + _dot(tril_b, lo)


def _const_spec(shape):
    nd = len(shape)
    return pl.BlockSpec(shape, lambda *_: (0,) * nd)


def _even_mixer_kernel(x_ref, nrm_ref, win_ref, gup_ref, gb_ref, ws_ref, bs_ref, lng_ref, lnb_ref,
                       hg_ref, wo_ref, o_ref, z_ref, mix_ref, st_ref, *, tiles_per_batch, n_chunks):
    i = pl.program_id(0)

    @pl.when(i % tiles_per_batch == 0)
    def _():
        st_ref[...] = jnp.zeros_like(st_ref)

    h = _rms(x_ref[...], nrm_ref[...]).astype(BF16)
    z_ref[...] = _dot(h, win_ref[...])

    row = lax.broadcasted_iota(jnp.int32, (CHUNK, CHUNK), 0)
    col = lax.broadcasted_iota(jnp.int32, (CHUNK, CHUNK), 1)
    tril_b = (col <= row).astype(BF16)
    sub_row = row & (SUB - 1)
    head_lane = lax.broadcasted_iota(jnp.int32, (1, B_QK_WIDTH), 1) // B_KEY_DIM
    bd_mask = (lax.broadcasted_iota(jnp.int32, (B_V_WIDTH, B_QK_WIDTH), 0) // B_VAL_DIM
               == lax.broadcasted_iota(jnp.int32, (B_V_WIDTH, B_QK_WIDTH), 1) // B_KEY_DIM)

    def chunk_body(c, carry):
        rows = pl.ds(pl.multiple_of(c * CHUNK, CHUNK), CHUNK)

        u = _gelu_tanh(z_ref[rows, E_U:E_V])
        v = _gelu_tanh(z_ref[rows, E_V:E_Q])
        mu = jnp.mean(v, axis=-1, keepdims=True)
        vc = v - mu
        var = jnp.mean(vc * vc, axis=-1, keepdims=True)
        vln = (vc * lax.rsqrt(var + EPS) * lng_ref[...] + lnb_ref[...]).astype(BF16)
        for g in range(A_GROUPS):
            sl = slice(g * LANES, (g + 1) * LANES)
            mixed = _dot(ws_ref[g], vln[:, sl]) + bs_ref[g]
            mix_ref[rows, sl] = (u[:, sl] * mixed).astype(BF16)

        q = z_ref[rows, E_Q:E_K] * (B_KEY_DIM ** -0.5)
        k = z_ref[rows, E_K:E_VB]
        vb = z_ref[rows, E_VB:E_OG]
        og = z_ref[rows, E_OG:E_G]
        glr = z_ref[rows, E_G:E_END].astype(BF16)
        logit = _dot(glr, gup_ref[...]) + gb_ref[...]
        log_a = _log_sigmoid(logit) * (1.0 / B_GATE_NORMALIZER)
        g_cum = _cumsum_rows(tril_b, log_a)
        g_last = g_cum[CHUNK - 1:CHUNK, :]
        st = st_ref[...]
        o = _dot_nt((q * jnp.exp(g_cum)).astype(BF16), st.astype(BF16))

        p_rows = [[None] * N_SUB for _ in range(B_HEADS)]
        for s in range(N_SUB):
            gs = g_cum[s * SUB:(s + 1) * SUB, :]
            if s == 0:
                qt = q[0:SUB, :] * jnp.exp(gs)
                kt = k * jnp.exp(jnp.minimum(-g_cum, MAX_DECAY_EXP))
            else:
                ref_g = g_cum[s * SUB - 1:s * SUB, :]
                qt = q[s * SUB:(s + 1) * SUB, :] * jnp.exp(gs - ref_g)
                kt = k * jnp.exp(jnp.minimum(ref_g - g_cum, MAX_DECAY_EXP))
            qs = jnp.concatenate([jnp.where(head_lane == hh, qt, 0.0) for hh in range(B_HEADS)],
                                 axis=0).astype(BF16)
            sc = _dot_nt(qs, kt.astype(BF16))
            sc = jnp.where(col <= (s * SUB + sub_row), sc, 0.0)
            for hh in range(B_HEADS):
                p_rows[hh][s] = sc[hh * SUB:(hh + 1) * SUB, :]

        vb_b = vb.astype(BF16)
        for hh in range(B_HEADS):
            sl = slice(hh * B_VAL_DIM, (hh + 1) * B_VAL_DIM)
            ph = jnp.concatenate(p_rows[hh], axis=0).astype(BF16)
            oh = o[:, sl] + _dot(ph, vb_b[:, sl])
            on = _rms(oh, hg_ref[hh])
            ogh = og[:, sl]
            mix_ref[rows, A_WIDTH + hh * B_VAL_DIM:A_WIDTH + (hh + 1) * B_VAL_DIM] = (
                on * (ogh * _sigmoid(ogh))).astype(BF16)

        k_dec = (k * jnp.exp(g_last - g_cum)).astype(BF16)
        upd = _dot(vb.T.astype(BF16), k_dec)
        st_ref[...] = jnp.exp(g_last) * st + jnp.where(bd_mask, upd, 0.0)
        return carry

    lax.fori_loop(0, n_chunks, chunk_body, 0)
    o_ref[...] = x_ref[...] + _dot(mix_ref[...], wo_ref[...])


def _even_mixer(x, nrm, win, gup, gb, ws, bs, lng, lnb, hg, wo, *, seq, tm):
    t = x.shape[0]
    kern = functools.partial(_even_mixer_kernel, tiles_per_batch=seq // tm, n_chunks=tm // CHUNK)
    return pl.pallas_call(
        kern,
        out_shape=jax.ShapeDtypeStruct((t, D_MODEL), F32),
        grid=(t // tm,),
        in_specs=[
            pl.BlockSpec((tm, D_MODEL), lambda i: (i, 0)),
            _const_spec(nrm.shape), _const_spec(win.shape), _const_spec(gup.shape), _const_spec(gb.shape),
            _const_spec(ws.shape), _const_spec(bs.shape), _const_spec(lng.shape), _const_spec(lnb.shape),
            _const_spec(hg.shape), _const_spec(wo.shape),
        ],
        out_specs=pl.BlockSpec((tm, D_MODEL), lambda i: (i, 0)),
        scratch_shapes=[
            pltpu.VMEM((tm, E_END), F32),
            pltpu.VMEM((tm, D_MODEL), BF16),
            pltpu.VMEM((B_V_WIDTH, B_QK_WIDTH), F32),
        ],
        compiler_params=pltpu.CompilerParams(dimension_semantics=("arbitrary",), vmem_limit_bytes=VMEM_LIMIT),
        name="even_mixer",
    )(x, nrm, win, gup, gb, ws, bs, lng, lnb, hg, wo)


def _dense_ffn_kernel(x_ref, nrm_ref, w1_ref, w3_ref, w2_ref, o_ref):
    x = x_ref[...]
    h = _rms(x, nrm_ref[...]).astype(BF16)
    a = _dot(h, w1_ref[...])
    b = _dot(h, w3_ref[...])
    o_ref[...] = x + _dot((a * _sigmoid(a) * b).astype(BF16), w2_ref[...])


def _resident_spec(shape):
    nd = len(shape)
    return pl.BlockSpec(shape, lambda *_: (0,) * nd, pipeline_mode=pl.Buffered(1))


def _dense_ffn(x, nrm, w1, w3, w2, *, tm):
    t = x.shape[0]
    row_spec = pl.BlockSpec((tm, D_MODEL), lambda i: (i, 0))
    return pl.pallas_call(
        _dense_ffn_kernel,
        out_shape=jax.ShapeDtypeStruct((t, D_MODEL), F32),
        grid=(t // tm,),
        in_specs=[row_spec, _const_spec(nrm.shape), _resident_spec(w1.shape), _resident_spec(w3.shape),
                  _resident_spec(w2.shape)],
        out_specs=row_spec,
        compiler_params=pltpu.CompilerParams(dimension_semantics=("arbitrary",), vmem_limit_bytes=VMEM_LIMIT),
        name="dense_ffn",
    )(x, nrm, w1, w3, w2)


def _head_rms(x, gain):
    lo = lax.broadcasted_iota(jnp.int32, (1, LANES), 1) < C_HEAD_DIM
    outs = []
    for t in range(C_WIDTH // LANES):
        xt = x[:, t * LANES:(t + 1) * LANES]
        sq = xt * xt
        s_lo = jnp.sum(jnp.where(lo, sq, 0.0), axis=-1, keepdims=True)
        s_hi = jnp.sum(jnp.where(lo, 0.0, sq), axis=-1, keepdims=True)
        inv = jnp.where(lo, lax.rsqrt(s_lo * (1.0 / C_HEAD_DIM) + EPS), lax.rsqrt(s_hi * (1.0 / C_HEAD_DIM) + EPS))
        outs.append(xt * inv)
    return jnp.concatenate(outs, axis=-1) * gain


N_PAIRS = C_HEADS // 2
K_EXT = 2 * LANES
BIAS_PARTS = 3


def _bias_placement():
    src = jnp.arange(BIAS_PARTS * LANES)
    part, head = src // LANES, src % LANES
    dst = (head // 2) * LANES + BIAS_PARTS * (head % 2) + part
    hit = (dst[:, None] == jnp.arange(N_PAIRS * LANES)[None, :]) & (head < C_HEADS)[:, None]
    return hit.astype(BF16)


def _odd_inproj_kernel(x_ref, nrm_ref, w_ref, fb_ref, qg_ref, kg_ref, place_ref,
                       q_ref, k_ref, v_ref, gate_ref, z_ref, c_ref, carry_ref, *, tiles_per_batch, n_chunks):
    i = pl.program_id(0)

    @pl.when(i % tiles_per_batch == 0)
    def _():
        carry_ref[...] = jnp.zeros_like(carry_ref)

    h = _rms(x_ref[...], nrm_ref[...]).astype(BF16)
    z_ref[...] = _dot(h, w_ref[...])
    q_ref[...] = (_head_rms(z_ref[:, O_Q:O_K], qg_ref[...]) * (C_HEAD_DIM ** -0.5 * LOG2E)).astype(BF16)
    kn = _head_rms(z_ref[:, O_K:O_V], kg_ref[...]).astype(BF16)
    v_ref[...] = z_ref[:, O_V:O_OG].astype(BF16)
    gate_ref[...] = _sigmoid(z_ref[:, O_OG:O_F]).astype(BF16)

    row = lax.broadcasted_iota(jnp.int32, (CHUNK, CHUNK), 0)
    col = lax.broadcasted_iota(jnp.int32, (CHUNK, CHUNK), 1)
    tril_b = (col <= row).astype(BF16)
    carry = carry_ref[...]
    for c in range(n_chunks):
        rows = slice(c * CHUNK, (c + 1) * CHUNK)
        log_f = _log_sigmoid(z_ref[rows, O_F:O_END] + fb_ref[...])
        cs = _cumsum_rows(tril_b, log_f) + carry
        c_ref[rows, :] = cs
        carry = cs[CHUNK - 1:CHUNK, :]
    carry_ref[...] = carry

    bias = _dot(jnp.concatenate(_split3(c_ref[...] * LOG2E), axis=1), place_ref[...]).astype(BF16)
    for p in range(N_PAIRS):
        k_ref[:, p * K_EXT:p * K_EXT + LANES] = kn[:, p * LANES:(p + 1) * LANES]
        k_ref[:, p * K_EXT + LANES:(p + 1) * K_EXT] = bias[:, p * LANES:(p + 1) * LANES]


def _odd_inproj(x, nrm, w, fb, qg, kg, *, seq, tm):
    t = x.shape[0]
    kern = functools.partial(_odd_inproj_kernel, tiles_per_batch=seq // tm, n_chunks=tm // CHUNK)
    row_spec = pl.BlockSpec((tm, C_WIDTH), lambda i: (i, 0))
    kext_spec = pl.BlockSpec((tm, N_PAIRS * K_EXT), lambda i: (i, 0))
    place = _bias_placement()
    wide = jax.ShapeDtypeStruct((t, C_WIDTH), BF16)
    return pl.pallas_call(
        kern,
        out_shape=[wide, jax.ShapeDtypeStruct((t, N_PAIRS * K_EXT), BF16), wide, wide],
        grid=(t // tm,),
        in_specs=[row_spec, _const_spec(nrm.shape), _const_spec(w.shape), _const_spec(fb.shape),
                  _const_spec(qg.shape), _const_spec(kg.shape), _const_spec(place.shape)],
        out_specs=[row_spec, kext_spec, row_spec, row_spec],
        scratch_shapes=[pltpu.VMEM((tm, O_END), F32), pltpu.VMEM((tm, LANES), F32), pltpu.VMEM((1, LANES), F32)],
        compiler_params=pltpu.CompilerParams(dimension_semantics=("arbitrary",), vmem_limit_bytes=VMEM_LIMIT),
        name="odd_inproj",
    )(x, nrm, w, fb, qg, kg, place)


NEG_BIG = -1e30


V_ROWS = 80


def _fox_kernel(q_ref, k_ref, vt_ref, gate_ref, o_ref, s_ref, *, tq, tk):
    qi = pl.program_id(2)
    ng = tq // tk
    lane = lax.broadcasted_iota(jnp.int32, (1, LANES), 1)
    lo = lane < C_HEAD_DIM
    key_i = lax.broadcasted_iota(jnp.int32, (tk, tk), 0)
    qry_i = lax.broadcasted_iota(jnp.int32, (tk, tk), 1)
    causal = key_i <= qry_i
    chains = [(hh, r) for r in range(ng) for hh in range(2)]
    qms = []
    for hh, r in chains:
        qr = q_ref[r * tk:(r + 1) * tk, :]
        zero = jnp.zeros_like(qr)
        qh = jnp.where(lo, qr, zero) if hh == 0 else jnp.where(lo, zero, qr)
        pick = (lane >= BIAS_PARTS * hh) & (lane < BIAS_PARTS * (hh + 1))
        minus_one = jnp.broadcast_to(jnp.where(pick, -1.0, 0.0).astype(BF16), qr.shape)
        qms.append(jnp.concatenate([qh, minus_one], axis=1))

    def key_rows(j):
        return pl.ds(pl.multiple_of(j * tk, tk), tk)

    def scores_to_scratch(j, slot, live, modes):
        kb = k_ref[key_rows(j), :]
        raw = [_dot_nt(kb, qms[idx]) for idx in live]
        maxes = []
        for s, idx in zip(raw, live):
            hh, r = chains[idx]
            if modes[r] == "diag":
                s = jnp.where(causal, s, NEG_BIG)
            s_ref[slot, idx] = s
            maxes.append(jnp.max(s, axis=0, keepdims=True))
        return maxes

    def softmax_pv(j, slot, maxes, live, state):
        vtb = vt_ref[j]
        new = list(state)
        probs = []
        for bm, idx in zip(maxes, live):
            m = state[2 * idx]
            m_new = jnp.maximum(m, bm)
            p = jnp.exp2(s_ref[slot, idx] - m_new)
            new[2 * idx] = m_new
            probs.append((jnp.exp2(m - m_new), p.astype(BF16)))
        for (alpha, p), idx in zip(probs, live):
            hh, r = chains[idx]
            new[2 * idx + 1] = alpha * state[2 * idx + 1] + _dot(vtb[hh], p)
        return new

    assert ng == 2
    all_chains = list(range(len(chains)))
    late_chains = [idx for idx, (hh, r) in enumerate(chains) if r == 1]
    n_state = 2 * len(chains)
    state = []
    for _ in chains:
        state += [jnp.full((1, tk), NEG_BIG, F32), jnp.zeros((V_ROWS, tk), F32)]
    n_full = qi * ng
    full_modes = ("full", "full")
    first_diag = ("diag", "full")
    last_diag = ("skip", "diag")

    def two_blocks(j, carry, next_modes):
        st, mx0 = list(carry[:n_state]), carry[n_state:]
        mx1 = scores_to_scratch(j + 1, 1, all_chains, full_modes)
        st = softmax_pv(j, 0, mx0, all_chains, st)
        mx0 = scores_to_scratch(j + 2, 0, all_chains, next_modes)
        st = softmax_pv(j + 1, 1, mx1, all_chains, st)
        return tuple(st) + tuple(mx0)

    def with_full_blocks(_):
        first = scores_to_scratch(0, 0, all_chains, full_modes)
        carry = lax.fori_loop(0, qi - 1, lambda i, c: two_blocks(2 * i, c, full_modes), tuple(state) + tuple(first))
        return two_blocks(n_full - 2, carry, first_diag)

    def no_full_blocks(_):
        return tuple(state) + tuple(scores_to_scratch(0, 0, all_chains, first_diag))

    carry = lax.cond(qi > 0, with_full_blocks, no_full_blocks, 0)
    state, mx0 = list(carry[:n_state]), carry[n_state:]
    mx1 = scores_to_scratch(n_full + 1, 1, late_chains, last_diag)
    state = softmax_pv(n_full, 0, mx0, all_chains, state)
    state = softmax_pv(n_full + 1, 1, mx1, late_chains, state)

    for r in range(ng):
        parts = []
        for hh in range(2):
            acc = state[2 * chains.index((hh, r)) + 1]
            parts.append(acc[:C_HEAD_DIM, :] / acc[C_HEAD_DIM:C_HEAD_DIM + 1, :])
        o = jnp.concatenate(parts, axis=0).T
        rows = slice(r * tk, (r + 1) * tk)
        o_ref[rows, :] = (o * gate_ref[rows, :].astype(F32)).astype(BF16)


def _fox_attention(q, k_ext, vt, gate, *, batch, seq, tq, tk):
    t = q.shape[0]
    nq = seq // tq
    kern = functools.partial(_fox_kernel, tq=tq, tk=tk)
    return pl.pallas_call(
        kern,
        out_shape=jax.ShapeDtypeStruct((t, C_WIDTH), BF16),
        grid=(batch, N_PAIRS, nq),
        in_specs=[
            pl.BlockSpec((tq, LANES), lambda b, p, i: (b * nq + i, p)),
            pl.BlockSpec((seq, K_EXT), lambda b, p, i: (b, p)),
            pl.BlockSpec((None, None, seq // tk, 2, V_ROWS, tk), lambda b, p, i: (b, p, 0, 0, 0, 0)),
            pl.BlockSpec((tq, LANES), lambda b, p, i: (b * nq + i, p)),
        ],
        out_specs=pl.BlockSpec((tq, LANES), lambda b, p, i: (b * nq + i, p)),
        scratch_shapes=[pltpu.VMEM((2, 2 * (tq // tk), tk, tk), F32)],
        compiler_params=pltpu.CompilerParams(dimension_semantics=("arbitrary", "arbitrary", "arbitrary"),
                                             vmem_limit_bytes=VMEM_LIMIT),
        name="fox_attn",
    )(q, k_ext, vt, gate)


def _store_token_tiles(dst_ref, val, n_rows):
    for s in range(ROW_TILE):
        dst_ref[pl.ds(s, n_rows, stride=ROW_TILE), :] = val[:, s * LANES:(s + 1) * LANES]


def _load_token_tiles(src_ref, n_rows):
    return jnp.concatenate([src_ref[pl.ds(s, n_rows, stride=ROW_TILE), :] for s in range(ROW_TILE)], axis=1)


def _odd_out_kernel(x_ref, a_ref, wo_ref, nrm_ref, r_ref, x3_ref, h_ref, route_ref):
    x3 = x_ref[...] + _dot(a_ref[...], wo_ref[...])
    x3_ref[...] = x3
    h = _rms(x3, nrm_ref[...])
    _store_token_tiles(h_ref, h, h.shape[0])

    h_hi = h.astype(BF16)
    h_lo = (h - h_hi.astype(F32)).astype(BF16)
    r = r_ref[...]
    r_hi = r.astype(BF16)
    r_lo = (r - r_hi.astype(F32)).astype(BF16)
    logits = _dot(h_hi, r_hi) + (_dot(h_lo, r_hi) + _dot(h_hi, r_lo))

    lane = lax.broadcasted_iota(jnp.int32, logits.shape, 1).astype(F32)
    neg_inf = jnp.float32(-jnp.inf)
    lg = jnp.where(lane < N_EXPERTS, logits, neg_inf)
    m1 = jnp.max(lg, axis=-1, keepdims=True)
    i1 = jnp.min(jnp.where(lg == m1, lane, float(LANES)), axis=-1, keepdims=True)
    lg2 = jnp.where(lane == i1, neg_inf, lg)
    m2 = jnp.max(lg2, axis=-1, keepdims=True)
    i2 = jnp.min(jnp.where(lg2 == m2, lane, float(LANES)), axis=-1, keepdims=True)
    e2 = jnp.exp(m2 - m1)
    g1 = 1.0 / (1.0 + e2)
    g2 = e2 / (1.0 + e2)
    route_ref[...] = jnp.where(lane == 0, i1, jnp.where(lane == 1, i2, jnp.where(lane == 2, g1,
                               jnp.where(lane == 3, g2, 0.0))))


def _odd_out(x, a, wo, nrm, router, *, tm):
    t = x.shape[0]
    row_spec = pl.BlockSpec((tm, D_MODEL), lambda i: (i, 0))
    return pl.pallas_call(
        _odd_out_kernel,
        out_shape=[jax.ShapeDtypeStruct((t, D_MODEL), F32), jax.ShapeDtypeStruct((t * ROW_TILE, LANES), F32),
                   jax.ShapeDtypeStruct((t, LANES), F32)],
        grid=(t // tm,),
        in_specs=[row_spec, row_spec, _const_spec(wo.shape), _const_spec(nrm.shape), _const_spec(router.shape)],
        out_specs=[row_spec, pl.BlockSpec((tm * ROW_TILE, LANES), lambda i: (i, 0)),
                   pl.BlockSpec((tm, LANES), lambda i: (i, 0))],
        compiler_params=pltpu.CompilerParams(dimension_semantics=("arbitrary",), vmem_limit_bytes=VMEM_LIMIT),
        name="odd_out",
    )(x, a, wo, nrm, router)


def _row_gather_copy(src_hbm, src_row, dst_ref, dst_row, sem):
    return pltpu.make_async_copy(
        src_hbm.at[pl.ds(pl.multiple_of(src_row * ROW_TILE, ROW_TILE), ROW_TILE), :],
        dst_ref.at[pl.ds(pl.multiple_of(dst_row * ROW_TILE, ROW_TILE), ROW_TILE), :],
        sem)


GATHER_UNROLL = 8


def _start_row_gathers(src_hbm, idx_ref, idx_row, first, count, dst_ref, sem, *, inline):
    if inline:
        for u in range(count):
            _row_gather_copy(src_hbm, idx_ref[idx_row, first + u], dst_ref, first + u, sem).start()
        return

    def issue(r, c):
        _row_gather_copy(src_hbm, idx_ref[idx_row, r], dst_ref, r, sem).start()
        return c

    lax.fori_loop(first, first + count, issue, 0, unroll=GATHER_UNROLL)


def _wait_row_gathers(src_hbm, dst_ref, sem):
    pltpu.make_async_copy(src_hbm.at[pl.ds(0, dst_ref.shape[0]), :], dst_ref, sem).wait()


def _moe_ffn_kernel(te_ref, nu_ref, tok_ref, tok_next_ref, h_hbm, w1_ref, w3_ref, w2_ref, o_ref, xs_ref, buf_ref,
                    acc_ref, sem, *, tm, nk):
    i = pl.program_id(0)
    kf = pl.program_id(1)
    last = nk - 1
    n_used = nu_ref[0]
    used = i < n_used
    slot = i % 2

    @pl.when(jnp.logical_and(i == 0, kf == 0))
    def _():
        _start_row_gathers(h_hbm, tok_ref, 0, 0, tm, buf_ref.at[0], sem.at[0], inline=False)

    @pl.when(jnp.logical_and(i <= n_used, kf == 0))
    def _():
        _wait_row_gathers(h_hbm, buf_ref.at[slot], sem.at[slot])
        xs_ref[...] = _load_token_tiles(buf_ref.at[slot], tm).astype(BF16)

    @pl.when(used)
    def _():
        rows_per_step = tm // nk
        _start_row_gathers(h_hbm, tok_next_ref, 0, kf * rows_per_step, rows_per_step, buf_ref.at[1 - slot],
                           sem.at[1 - slot], inline=True)
        x = xs_ref[...]
        a = _dot(x, w1_ref[...])
        b = _dot(x, w3_ref[...])
        part = _dot((a * _sigmoid(a) * b).astype(BF16), w2_ref[...])

        @pl.when(kf == 0)
        def _():
            acc_ref[...] = part

        @pl.when(jnp.logical_and(kf > 0, kf < last))
        def _():
            acc_ref[...] += part

        @pl.when(kf == last)
        def _():
            _store_token_tiles(o_ref, acc_ref[...] + part, tm)

    @pl.when(jnp.logical_and(jnp.logical_not(used), kf == last))
    def _():
        o_ref[...] = jnp.zeros_like(o_ref)


def _moe_ffn(tile_expert, n_used, row_tok, h_tiles, w1, w3, w2, *, tm, tf):
    n_tiles = row_tok.shape[0] - 1
    nk = D_FF_EXPERT // tf
    assert nk >= 2 and tm % nk == 0

    def kchunk(i, k, nu):
        return jnp.where(i < nu[0], k, nk - 1)

    grid_spec = pltpu.PrefetchScalarGridSpec(
        num_scalar_prefetch=2,
        grid=(n_tiles, nk),
        in_specs=[
            pl.BlockSpec((None, 1, tm), lambda i, k, te, nu: (i, 0, 0), memory_space=pltpu.SMEM),
            pl.BlockSpec((None, 1, tm), lambda i, k, te, nu: (i + 1, 0, 0), memory_space=pltpu.SMEM),
            pl.BlockSpec(memory_space=pl.ANY),
            pl.BlockSpec((None, D_MODEL, tf), lambda i, k, te, nu: (te[i], 0, kchunk(i, k, nu))),
            pl.BlockSpec((None, D_MODEL, tf), lambda i, k, te, nu: (te[i], 0, kchunk(i, k, nu))),
            pl.BlockSpec((None, tf, D_MODEL), lambda i, k, te, nu: (te[i], kchunk(i, k, nu), 0)),
        ],
        out_specs=pl.BlockSpec((tm * ROW_TILE, LANES), lambda i, k, te, nu: (i, 0)),
        scratch_shapes=[pltpu.VMEM((tm, D_MODEL), BF16), pltpu.VMEM((2, tm * ROW_TILE, LANES), F32),
                        pltpu.VMEM((tm, D_MODEL), F32), pltpu.SemaphoreType.DMA((2,))],
    )
    return pl.pallas_call(
        functools.partial(_moe_ffn_kernel, tm=tm, nk=nk),
        out_shape=jax.ShapeDtypeStruct((n_tiles * tm * ROW_TILE, LANES), F32),
        grid_spec=grid_spec,
        compiler_params=pltpu.CompilerParams(dimension_semantics=("arbitrary", "arbitrary"),
                                             vmem_limit_bytes=VMEM_LIMIT),
        name="moe_ffn",
    )(tile_expert, n_used, row_tok, row_tok, h_tiles, w1, w3, w2)


def _moe_combine_kernel(pos_ref, pos_next_ref, x_ref, route_ref, nrm_ref, y_hbm, o_ref, buf_ref, sem, *, tm):
    i = pl.program_id(0)
    slot = i % 2

    def start_tile(idx_ref, s):
        for k in range(2):
            _start_row_gathers(y_hbm, idx_ref, k, 0, tm, buf_ref.at[s, k], sem.at[s, k], inline=False)

    @pl.when(i == 0)
    def _():
        start_tile(pos_ref, 0)

    @pl.when(i + 1 < pl.num_programs(0))
    def _():
        start_tile(pos_next_ref, 1 - slot)

    for k in range(2):
        _wait_row_gathers(y_hbm, buf_ref.at[slot, k], sem.at[slot, k])
    g1 = route_ref[:, 2:3]
    g2 = route_ref[:, 3:4]
    x = x_ref[...] + (g1 * _load_token_tiles(buf_ref.at[slot, 0], tm) + g2 * _load_token_tiles(buf_ref.at[slot, 1], tm))
    o_ref[...] = _rms(x, nrm_ref[...])


def _moe_combine(pos, x, route, nrm, y_tiles, *, tm):
    t = x.shape[0]
    n = t // tm
    row_spec = pl.BlockSpec((tm, D_MODEL), lambda i: (i, 0))
    return pl.pallas_call(
        functools.partial(_moe_combine_kernel, tm=tm),
        out_shape=jax.ShapeDtypeStruct((t, D_MODEL), F32),
        grid=(n,),
        in_specs=[pl.BlockSpec((None, 2, tm), lambda i: (i, 0, 0), memory_space=pltpu.SMEM),
                  pl.BlockSpec((None, 2, tm), lambda i: (jnp.minimum(i + 1, n - 1), 0, 0), memory_space=pltpu.SMEM),
                  row_spec, pl.BlockSpec((tm, LANES), lambda i: (i, 0)), _const_spec(nrm.shape),
                  pl.BlockSpec(memory_space=pl.ANY)],
        out_specs=row_spec,
        scratch_shapes=[pltpu.VMEM((2, 2, tm * ROW_TILE, LANES), F32), pltpu.SemaphoreType.DMA((2, 2))],
        compiler_params=pltpu.CompilerParams(dimension_semantics=("arbitrary",), vmem_limit_bytes=VMEM_LIMIT),
        name="moe_combine",
    )(pos, pos, x, route, nrm, y_tiles)


def _routing_tables(idx1, idx2, *, tm, n_tiles):
    t = idx1.shape[0]
    e_flat = jnp.concatenate([idx1, idx2])
    onehot = (e_flat[:, None] == jnp.arange(N_EXPERTS, dtype=jnp.int32)[None, :]).astype(jnp.int32)
    csum = jnp.cumsum(onehot, axis=0)
    rank = jnp.sum((csum - onehot) * onehot, axis=1)
    counts = csum[-1]
    tiles_e = (counts + tm - 1) // tm
    tile_end = jnp.cumsum(tiles_e)
    tile_start = tile_end - tiles_e
    pos = jnp.sum(onehot * tile_start[None, :], axis=1) * tm + rank
    n_used = tile_end[-1]
    tile_ids = jnp.arange(n_tiles, dtype=jnp.int32)
    te = jnp.sum((tile_ids[:, None] >= tile_end[None, :]).astype(jnp.int32), axis=1)
    te_last = jnp.sum((n_used - 1 >= tile_end).astype(jnp.int32))
    tile_expert = jnp.where(tile_ids < n_used, te, te_last).astype(jnp.int32)
    tok = jnp.concatenate([jnp.arange(t, dtype=jnp.int32)] * 2)
    row_tok = jnp.zeros((n_tiles * tm,), jnp.int32).at[pos].set(tok, unique_indices=True)
    return pos[:t], pos[t:], row_tok, tile_expert, n_used.reshape(1).astype(jnp.int32)


def _pad_cols(w, n):
    return jnp.pad(w, ((0, 0), (0, n - w.shape[1])))


def kernel(x, even_norm_mix, even_w_in, even_gate_up, even_gate_bias, even_w_s, even_b_s, even_ln_g, even_ln_b,
           even_head_g, even_w_o, even_norm_ffn, even_ffn_w1, even_ffn_w3, even_ffn_w2, odd_norm_mix, odd_w_in,
           odd_forget_bias, odd_q_g, odd_k_g, odd_w_o, odd_norm_ffn, odd_router, odd_exp_w1, odd_exp_w3,
           odd_exp_w2, final_norm):
    batch, seq, d = x.shape
    t = batch * seq
    xt = x.reshape(t, d)
    tm = min(512, seq)

    w_in = even_w_in[0]
    u_w, v_w, q_w, k_w, g_w, vb_w, og_w = jnp.split(w_in, [512, 1024, 1280, 1536, 1552, 2064], axis=1)
    win_e = jnp.concatenate([u_w, v_w, q_w, k_w, vb_w, og_w, _pad_cols(g_w, LANES)], axis=1).astype(BF16)
    gup = jnp.pad(even_gate_up[0], ((0, LANES - B_GATE_RANK), (0, 0))).astype(BF16)
    gb = even_gate_bias[0].reshape(1, B_QK_WIDTH)
    tril = jnp.tril(jnp.ones((CHUNK, CHUNK), dtype=bool))
    ws = jnp.where(tril[None], even_w_s[0], 0.0).astype(BF16)
    bs = jnp.broadcast_to(even_b_s[0][:, :, None], (A_GROUPS, CHUNK, LANES))
    lng = even_ln_g[0].reshape(1, A_WIDTH)
    lnb = even_ln_b[0].reshape(1, A_WIDTH)
    hg = even_head_g[0].reshape(B_HEADS, 1, B_VAL_DIM)
    x1 = _even_mixer(xt, even_norm_mix[0].reshape(1, d), win_e, gup, gb, ws, bs, lng, lnb, hg,
                     even_w_o[0].astype(BF16), seq=seq, tm=tm)
    x2 = _dense_ffn(x1, even_norm_ffn[0].reshape(1, d), even_ffn_w1[0].astype(BF16), even_ffn_w3[0].astype(BF16),
                    even_ffn_w2[0].astype(BF16), tm=tm)

    q_w, k_w, v_w, og_w, f_w = jnp.split(odd_w_in[0], [1024, 2048, 3072, 4096], axis=1)
    win_o = jnp.concatenate([q_w, k_w, v_w, og_w, _pad_cols(f_w, LANES)], axis=1).astype(BF16)
    fb = jnp.pad(odd_forget_bias[0], (0, LANES - C_HEADS)).reshape(1, LANES)
    qg = jnp.tile(odd_q_g[0], C_HEADS).reshape(1, C_WIDTH)
    kg = jnp.tile(odd_k_g[0], C_HEADS).reshape(1, C_WIDTH)
    q, k_ext, v, gate = _odd_inproj(x2, odd_norm_mix[0].reshape(1, d), win_o, fb, qg, kg, seq=seq, tm=tm)
    tk = min(256, seq)
    tq = min(512, seq)
    vt = jnp.transpose(v.reshape(batch, seq // tk, tk, N_PAIRS, 2, C_HEAD_DIM), (0, 3, 1, 4, 5, 2))
    extra = jnp.zeros(vt.shape[:4] + (V_ROWS - C_HEAD_DIM, tk), BF16).at[..., 0, :].set(1.0)
    vt = jnp.concatenate([vt, extra], axis=4)
    attn = _fox_attention(q, k_ext, vt, gate, batch=batch, seq=seq, tq=tq, tk=tk)

    router = _pad_cols(odd_router[0], LANES)
    x3, h_tiles, route = _odd_out(x2, attn, odd_w_o[0].astype(BF16), odd_norm_ffn[0].reshape(1, d), router, tm=tm)

    tm_moe = 512
    tm_comb = min(256, seq)
    n_tiles = (2 * t) // tm_moe + N_EXPERTS + 1
    idx1 = route[:, 0].astype(jnp.int32)
    idx2 = route[:, 1].astype(jnp.int32)
    pos1, pos2, row_tok, tile_expert, n_used = _routing_tables(idx1, idx2, tm=tm_moe, n_tiles=n_tiles + 1)
    y_tiles = _moe_ffn(tile_expert[:n_tiles], n_used, row_tok.reshape(n_tiles + 1, 1, tm_moe), h_tiles,
                       odd_exp_w1[0].astype(BF16), odd_exp_w3[0].astype(BF16), odd_exp_w2[0].astype(BF16),
                       tm=tm_moe, tf=D_FF_EXPERT // 2)
    pos = jnp.stack([pos1.reshape(t // tm_comb, tm_comb), pos2.reshape(t // tm_comb, tm_comb)], axis=1)
    out = _moe_combine(pos, x3, route, final_norm.reshape(1, d), y_tiles, tm=tm_comb)
    return out.reshape(batch, seq, d)
```

```python
import functools
import math

import jax
import jax.numpy as jnp
from jax import lax
from jax.experimental import pallas as pl
from jax.experimental.pallas import tpu as pltpu

F32 = jnp.float32
BF16 = jnp.bfloat16
HIGHEST = lax.Precision.HIGHEST

EPS = 1e-6
D_MODEL = 1024
CHUNK = 128
SUB = 32
N_SUB = CHUNK // SUB
A_GROUPS = 4
A_WIDTH = 512
B_HEADS = 4
B_KEY_DIM = 64
B_VAL_DIM = 128
B_QK_WIDTH = 256
B_V_WIDTH = 512
B_GATE_RANK = 16
B_GATE_NORMALIZER = 16.0
C_HEADS = 16
C_HEAD_DIM = 64
C_WIDTH = 1024
D_FF_DENSE = 2816
N_EXPERTS = 8
D_FF_EXPERT = 3584
LANES = 128
MAX_DECAY_EXP = 60.0
LOG2E = math.log2(math.e)
ROW_TILE = 8

E_U, E_V, E_Q, E_K, E_VB, E_OG, E_G, E_END = 0, 512, 1024, 1280, 1536, 2048, 2560, 2688
O_Q, O_K, O_V, O_OG, O_F, O_END = 0, 1024, 2048, 3072, 4096, 4224

VMEM_LIMIT = 56 * 1024 * 1024


def _rms(x, g):
    ms = jnp.mean(x * x, axis=-1, keepdims=True)
    return x * lax.rsqrt(ms + EPS) * g


def _gelu_tanh(x):
    c = math.sqrt(2.0 / math.pi)
    return x * (0.5 * (1.0 + jnp.tanh(c * (x + 0.044715 * (x * x * x)))))


def _sigmoid(x):
    return 1.0 / (1.0 + jnp.exp(-x))


def _log_sigmoid(x):
    return jnp.minimum(x, 0.0) - jnp.log(1.0 + jnp.exp(-jnp.abs(x)))


def _dot(a, b):
    return jnp.dot(a, b, preferred_element_type=F32)


def _dot_nt(a, b):
    return lax.dot_general(a, b, (((1,), (1,)), ((), ())), preferred_element_type=F32)


def _split3(x):
    hi = x.astype(BF16)
    r1 = x - hi.astype(F32)
    mid = r1.astype(BF16)
    lo = (r1 - mid.astype(F32)).astype(BF16)
    return hi, mid, lo


def _cumsum_rows(tril_b, x):
    hi, mid, lo = _split3(x)
    return _dot(tril_b, hi) + _dot(tril_b, mid) + _dot(tril_b, lo)


def _const_spec(shape):
    nd = len(shape)
    return pl.BlockSpec(shape, lambda *_: (0,) * nd)


def _even_mixer_kernel(x_ref, nrm_ref, win_ref, gup_ref, gb_ref, ws_ref, bs_ref, lng_ref, lnb_ref,
                       hg_ref, wo_ref, o_ref, z_ref, mix_ref, st_ref, *, tiles_per_batch, n_chunks):
    i = pl.program_id(0)

    @pl.when(i % tiles_per_batch == 0)
    def _():
        st_ref[...] = jnp.zeros_like(st_ref)

    h = _rms(x_ref[...], nrm_ref[...]).astype(BF16)
    z_ref[...] = _dot(h, win_ref[...])

    row = lax.broadcasted_iota(jnp.int32, (CHUNK, CHUNK), 0)
    col = lax.broadcasted_iota(jnp.int32, (CHUNK, CHUNK), 1)
    tril_b = (col <= row).astype(BF16)
    sub_row = row & (SUB - 1)
    head_lane = lax.broadcasted_iota(jnp.int32, (1, B_QK_WIDTH), 1) // B_KEY_DIM
    bd_mask = (lax.broadcasted_iota(jnp.int32, (B_V_WIDTH, B_QK_WIDTH), 0) // B_VAL_DIM
               == lax.broadcasted_iota(jnp.int32, (B_V_WIDTH, B_QK_WIDTH), 1) // B_KEY_DIM)

    def chunk_body(c, carry):
        rows = pl.ds(pl.multiple_of(c * CHUNK, CHUNK), CHUNK)

        u = _gelu_tanh(z_ref[rows, E_U:E_V])
        v = _gelu_tanh(z_ref[rows, E_V:E_Q])
        mu = jnp.mean(v, axis=-1, keepdims=True)
        vc = v - mu
        var = jnp.mean(vc * vc, axis=-1, keepdims=True)
        vln = (vc * lax.rsqrt(var + EPS) * lng_ref[...] + lnb_ref[...]).astype(BF16)
        for g in range(A_GROUPS):
            sl = slice(g * LANES, (g + 1) * LANES)
            mixed = _dot(ws_ref[g], vln[:, sl]) + bs_ref[g]
            mix_ref[rows, sl] = (u[:, sl] * mixed).astype(BF16)

        q = z_ref[rows, E_Q:E_K] * (B_KEY_DIM ** -0.5)
        k = z_ref[rows, E_K:E_VB]
        vb = z_ref[rows, E_VB:E_OG]
        og = z_ref[rows, E_OG:E_G]
        glr = z_ref[rows, E_G:E_END].astype(BF16)
        logit = _dot(glr, gup_ref[...]) + gb_ref[...]
        log_a = _log_sigmoid(logit) * (1.0 / B_GATE_NORMALIZER)
        g_cum = _cumsum_rows(tril_b, log_a)
        g_last = g_cum[CHUNK - 1:CHUNK, :]
        st = st_ref[...]
        o = _dot_nt((q * jnp.exp(g_cum)).astype(BF16), st.astype(BF16))

        p_rows = [[None] * N_SUB for _ in range(B_HEADS)]
        for s in range(N_SUB):
            gs = g_cum[s * SUB:(s + 1) * SUB, :]
            if s == 0:
                qt = q[0:SUB, :] * jnp.exp(gs)
                kt = k * jnp.exp(jnp.minimum(-g_cum, MAX_DECAY_EXP))
            else:
                ref_g = g_cum[s * SUB - 1:s * SUB, :]
                qt = q[s * SUB:(s + 1) * SUB, :] * jnp.exp(gs - ref_g)
                kt = k * jnp.exp(jnp.minimum(ref_g - g_cum, MAX_DECAY_EXP))
            qs = jnp.concatenate([jnp.where(head_lane == hh, qt, 0.0) for hh in range(B_HEADS)],
                                 axis=0).astype(BF16)
            sc = _dot_nt(qs, kt.astype(BF16))
            sc = jnp.where(col <= (s * SUB + sub_row), sc, 0.0)
            for hh in range(B_HEADS):
                p_rows[hh][s] = sc[hh * SUB:(hh + 1) * SUB, :]

        vb_b = vb.astype(BF16)
        for hh in range(B_HEADS):
            sl = slice(hh * B_VAL_DIM, (hh + 1) * B_VAL_DIM)
            ph = jnp.concatenate(p_rows[hh], axis=0).astype(BF16)
            oh = o[:, sl] + _dot(ph, vb_b[:, sl])
            on = _rms(oh, hg_ref[hh])
            ogh = og[:, sl]
            mix_ref[rows, A_WIDTH + hh * B_VAL_DIM:A_WIDTH + (hh + 1) * B_VAL_DIM] = (
                on * (ogh * _sigmoid(ogh))).astype(BF16)

        k_dec = (k * jnp.exp(g_last - g_cum)).astype(BF16)
        upd = _dot(vb.T.astype(BF16), k_dec)
        st_ref[...] = jnp.exp(g_last) * st + jnp.where(bd_mask, upd, 0.0)
        return carry

    lax.fori_loop(0, n_chunks, chunk_body, 0)
    o_ref[...] = x_ref[...] + _dot(mix_ref[...], wo_ref[...])


def _even_mixer(x, nrm, win, gup, gb, ws, bs, lng, lnb, hg, wo, *, seq, tm):
    t = x.shape[0]
    kern = functools.partial(_even_mixer_kernel, tiles_per_batch=seq // tm, n_chunks=tm // CHUNK)
    return pl.pallas_call(
        kern,
        out_shape=jax.ShapeDtypeStruct((t, D_MODEL), F32),
        grid=(t // tm,),
        in_specs=[
            pl.BlockSpec((tm, D_MODEL), lambda i: (i, 0)),
            _const_spec(nrm.shape), _const_spec(win.shape), _const_spec(gup.shape), _const_spec(gb.shape),
            _const_spec(ws.shape), _const_spec(bs.shape), _const_spec(lng.shape), _const_spec(lnb.shape),
            _const_spec(hg.shape), _const_spec(wo.shape),
        ],
        out_specs=pl.BlockSpec((tm, D_MODEL), lambda i: (i, 0)),
        scratch_shapes=[
            pltpu.VMEM((tm, E_END), F32),
            pltpu.VMEM((tm, D_MODEL), BF16),
            pltpu.VMEM((B_V_WIDTH, B_QK_WIDTH), F32),
        ],
        compiler_params=pltpu.CompilerParams(dimension_semantics=("arbitrary",), vmem_limit_bytes=VMEM_LIMIT),
        name="even_mixer",
    )(x, nrm, win, gup, gb, ws, bs, lng, lnb, hg, wo)


def _dense_ffn_kernel(x_ref, nrm_ref, w1_ref, w3_ref, w2_ref, o_ref):
    x = x_ref[...]
    h = _rms(x, nrm_ref[...]).astype(BF16)
    a = _dot(h, w1_ref[...])
    b = _dot(h, w3_ref[...])
    o_ref[...] = x + _dot((a * _sigmoid(a) * b).astype(BF16), w2_ref[...])


def _resident_spec(shape):
    nd = len(shape)
    return pl.BlockSpec(shape, lambda *_: (0,) * nd, pipeline_mode=pl.Buffered(1))


def _dense_ffn(x, nrm, w1, w3, w2, *, tm):
    t = x.shape[0]
    row_spec = pl.BlockSpec((tm, D_MODEL), lambda i: (i, 0))
    return pl.pallas_call(
        _dense_ffn_kernel,
        out_shape=jax.ShapeDtypeStruct((t, D_MODEL), F32),
        grid=(t // tm,),
        in_specs=[row_spec, _const_spec(nrm.shape), _resident_spec(w1.shape), _resident_spec(w3.shape),
                  _resident_spec(w2.shape)],
        out_specs=row_spec,
        compiler_params=pltpu.CompilerParams(dimension_semantics=("arbitrary",), vmem_limit_bytes=VMEM_LIMIT),
        name="dense_ffn",
    )(x, nrm, w1, w3, w2)


def _head_rms(x, gain):
    lo = lax.broadcasted_iota(jnp.int32, (1, LANES), 1) < C_HEAD_DIM
    outs = []
    for t in range(C_WIDTH // LANES):
        xt = x[:, t * LANES:(t + 1) * LANES]
        sq = xt * xt
        s_lo = jnp.sum(jnp.where(lo, sq, 0.0), axis=-1, keepdims=True)
        s_hi = jnp.sum(jnp.where(lo, 0.0, sq), axis=-1, keepdims=True)
        inv = jnp.where(lo, lax.rsqrt(s_lo * (1.0 / C_HEAD_DIM) + EPS), lax.rsqrt(s_hi * (1.0 / C_HEAD_DIM) + EPS))
        outs.append(xt * inv)
    return jnp.concatenate(outs, axis=-1) * gain


N_PAIRS = C_HEADS // 2
K_EXT = 2 * LANES
BIAS_PARTS = 3


def _bias_placement():
    src = jnp.arange(BIAS_PARTS * LANES)
    part, head = src // LANES, src % LANES
    dst = (head // 2) * LANES + BIAS_PARTS * (head % 2) + part
    hit = (dst[:, None] == jnp.arange(N_PAIRS * LANES)[None, :]) & (head < C_HEADS)[:, None]
    return hit.astype(BF16)


def _odd_inproj_kernel(x_ref, nrm_ref, w_ref, fb_ref, qg_ref, kg_ref, place_ref,
                       q_ref, k_ref, v_ref, gate_ref, z_ref, c_ref, carry_ref, *, tiles_per_batch, n_chunks):
    i = pl.program_id(0)

    @pl.when(i % tiles_per_batch == 0)
    def _():
        carry_ref[...] = jnp.zeros_like(carry_ref)

    h = _rms(x_ref[...], nrm_ref[...]).astype(BF16)
    z_ref[...] = _dot(h, w_ref[...])
    q_ref[...] = (_head_rms(z_ref[:, O_Q:O_K], qg_ref[...]) * (C_HEAD_DIM ** -0.5 * LOG2E)).astype(BF16)
    kn = _head_rms(z_ref[:, O_K:O_V], kg_ref[...]).astype(BF16)
    v_ref[...] = z_ref[:, O_V:O_OG].astype(BF16)
    gate_ref[...] = _sigmoid(z_ref[:, O_OG:O_F]).astype(BF16)

    row = lax.broadcasted_iota(jnp.int32, (CHUNK, CHUNK), 0)
    col = lax.broadcasted_iota(jnp.int32, (CHUNK, CHUNK), 1)
    tril_b = (col <= row).astype(BF16)
    carry = carry_ref[...]
    for c in range(n_chunks):
        rows = slice(c * CHUNK, (c + 1) * CHUNK)
        log_f = _log_sigmoid(z_ref[rows, O_F:O_END] + fb_ref[...])
        cs = _cumsum_rows(tril_b, log_f) + carry
        c_ref[rows, :] = cs
        carry = cs[CHUNK - 1:CHUNK, :]
    carry_ref[...] = carry

    bias = _dot(jnp.concatenate(_split3(c_ref[...] * LOG2E), axis=1), place_ref[...]).astype(BF16)
    for p in range(N_PAIRS):
        k_ref[:, p * K_EXT:p * K_EXT + LANES] = kn[:, p * LANES:(p + 1) * LANES]
        k_ref[:, p * K_EXT + LANES:(p + 1) * K_EXT] = bias[:, p * LANES:(p + 1) * LANES]


def _odd_inproj(x, nrm, w, fb, qg, kg, *, seq, tm):
    t = x.shape[0]
    kern = functools.partial(_odd_inproj_kernel, tiles_per_batch=seq // tm, n_chunks=tm // CHUNK)
    row_spec = pl.BlockSpec((tm, C_WIDTH), lambda i: (i, 0))
    kext_spec = pl.BlockSpec((tm, N_PAIRS * K_EXT), lambda i: (i, 0))
    place = _bias_placement()
    wide = jax.ShapeDtypeStruct((t, C_WIDTH), BF16)
    return pl.pallas_call(
        kern,
        out_shape=[wide, jax.ShapeDtypeStruct((t, N_PAIRS * K_EXT), BF16), wide, wide],
        grid=(t // tm,),
        in_specs=[row_spec, _const_spec(nrm.shape), _const_spec(w.shape), _const_spec(fb.shape),
                  _const_spec(qg.shape), _const_spec(kg.shape), _const_spec(place.shape)],
        out_specs=[row_spec, kext_spec, row_spec, row_spec],
        scratch_shapes=[pltpu.VMEM((tm, O_END), F32), pltpu.VMEM((tm, LANES), F32), pltpu.VMEM((1, LANES), F32)],
        compiler_params=pltpu.CompilerParams(dimension_semantics=("arbitrary",), vmem_limit_bytes=VMEM_LIMIT),
        name="odd_inproj",
    )(x, nrm, w, fb, qg, kg, place)


NEG_BIG = -1e30


V_ROWS = 80


def _fox_kernel(q_ref, k_ref, vt_ref, gate_ref, o_ref, s_ref, *, tq, tk):
    qi = pl.program_id(2)
    ng = tq // tk
    lane = lax.broadcasted_iota(jnp.int32, (1, LANES), 1)
    lo = lane < C_HEAD_DIM
    key_i = lax.broadcasted_iota(jnp.int32, (tk, tk), 0)
    qry_i = lax.broadcasted_iota(jnp.int32, (tk, tk), 1)
    causal = key_i <= qry_i
    chains = [(hh, r) for r in range(ng) for hh in range(2)]
    qms = []
    for hh, r in chains:
        qr = q_ref[r * tk:(r + 1) * tk, :]
        zero = jnp.zeros_like(qr)
        qh = jnp.where(lo, qr, zero) if hh == 0 else jnp.where(lo, zero, qr)
        pick = (lane >= BIAS_PARTS * hh) & (lane < BIAS_PARTS * (hh + 1))
        minus_one = jnp.broadcast_to(jnp.where(pick, -1.0, 0.0).astype(BF16), qr.shape)
        qms.append(jnp.concatenate([qh, minus_one], axis=1))

    def key_rows(j):
        return pl.ds(pl.multiple_of(j * tk, tk), tk)

    def scores_to_scratch(j, slot, live, modes):
        kb = k_ref[key_rows(j), :]
        raw = [_dot_nt(kb, qms[idx]) for idx in live]
        maxes = []
        for s, idx in zip(raw, live):
            hh, r = chains[idx]
            if modes[r] == "diag":
                s = jnp.where(causal, s, NEG_BIG)
            s_ref[slot, idx] = s
            maxes.append(jnp.max(s, axis=0, keepdims=True))
        return maxes

    ones_rows = (lax.broadcasted_iota(jnp.int32, (V_ROWS - C_HEAD_DIM, tk), 0) == 0).astype(BF16)

    def softmax_pv(j, slot, maxes, live, state):
        vt_pair = vt_ref[j]
        vtb = [jnp.concatenate([vt_pair[hh * C_HEAD_DIM:(hh + 1) * C_HEAD_DIM, :], ones_rows], axis=0)
               for hh in range(2)]
        new = list(state)
        probs = []
        for bm, idx in zip(maxes, live):
            m = state[2 * idx]
            m_new = jnp.maximum(m, bm)
            p = jnp.exp2(s_ref[slot, idx] - m_new)
            new[2 * idx] = m_new
            probs.append((jnp.exp2(m - m_new), p.astype(BF16)))
        for (alpha, p), idx in zip(probs, live):
            hh, r = chains[idx]
            new[2 * idx + 1] = alpha * state[2 * idx + 1] + _dot(vtb[hh], p)
        return new

    assert ng == 2
    all_chains = list(range(len(chains)))
    late_chains = [idx for idx, (hh, r) in enumerate(chains) if r == 1]
    n_state = 2 * len(chains)
    state = []
    for _ in chains:
        state += [jnp.full((1, tk), NEG_BIG, F32), jnp.zeros((V_ROWS, tk), F32)]
    n_full = qi * ng
    full_modes = ("full", "full")
    first_diag = ("diag", "full")
    last_diag = ("skip", "diag")

    def two_blocks(j, carry, next_modes):
        st, mx0 = list(carry[:n_state]), carry[n_state:]
        mx1 = scores_to_scratch(j + 1, 1, all_chains, full_modes)
        st = softmax_pv(j, 0, mx0, all_chains, st)
        mx0 = scores_to_scratch(j + 2, 0, all_chains, next_modes)
        st = softmax_pv(j + 1, 1, mx1, all_chains, st)
        return tuple(st) + tuple(mx0)

    def with_full_blocks(_):
        first = scores_to_scratch(0, 0, all_chains, full_modes)
        carry = lax.fori_loop(0, qi - 1, lambda i, c: two_blocks(2 * i, c, full_modes), tuple(state) + tuple(first))
        return two_blocks(n_full - 2, carry, first_diag)

    def no_full_blocks(_):
        return tuple(state) + tuple(scores_to_scratch(0, 0, all_chains, first_diag))

    carry = lax.cond(qi > 0, with_full_blocks, no_full_blocks, 0)
    state, mx0 = list(carry[:n_state]), carry[n_state:]
    mx1 = scores_to_scratch(n_full + 1, 1, late_chains, last_diag)
    state = softmax_pv(n_full, 0, mx0, all_chains, state)
    state = softmax_pv(n_full + 1, 1, mx1, late_chains, state)

    for r in range(ng):
        parts = []
        for hh in range(2):
            acc = state[2 * chains.index((hh, r)) + 1]
            parts.append(acc[:C_HEAD_DIM, :] / acc[C_HEAD_DIM:C_HEAD_DIM + 1, :])
        o = jnp.concatenate(parts, axis=0).T
        rows = slice(r * tk, (r + 1) * tk)
        o_ref[rows, :] = (o * gate_ref[rows, :].astype(F32)).astype(BF16)


def _fox_attention(q, k_ext, vt, gate, *, batch, seq, tq, tk):
    t = q.shape[0]
    nq = seq // tq
    kern = functools.partial(_fox_kernel, tq=tq, tk=tk)
    return pl.pallas_call(
        kern,
        out_shape=jax.ShapeDtypeStruct((t, C_WIDTH), BF16),
        grid=(batch, N_PAIRS, nq),
        in_specs=[
            pl.BlockSpec((tq, LANES), lambda b, p, i: (b * nq + i, p)),
            pl.BlockSpec((seq, K_EXT), lambda b, p, i: (b, p)),
            pl.BlockSpec((None, None, seq // tk, LANES, tk), lambda b, p, i: (b, p, 0, 0, 0)),
            pl.BlockSpec((tq, LANES), lambda b, p, i: (b * nq + i, p)),
        ],
        out_specs=pl.BlockSpec((tq, LANES), lambda b, p, i: (b * nq + i, p)),
        scratch_shapes=[pltpu.VMEM((2, 2 * (tq // tk), tk, tk), F32)],
        compiler_params=pltpu.CompilerParams(dimension_semantics=("arbitrary", "arbitrary", "arbitrary"),
                                             vmem_limit_bytes=VMEM_LIMIT),
        name="fox_attn",
    )(q, k_ext, vt, gate)


def _store_token_tiles(dst_ref, val, n_rows):
    for s in range(ROW_TILE):
        dst_ref[pl.ds(s, n_rows, stride=ROW_TILE), :] = val[:, s * LANES:(s + 1) * LANES]


def _load_token_tiles(src_ref, n_rows):
    return jnp.concatenate([src_ref[pl.ds(s, n_rows, stride=ROW_TILE), :] for s in range(ROW_TILE)], axis=1)


def _odd_out_kernel(x_ref, a_ref, wo_ref, nrm_ref, r_ref, x3_ref, h_ref, route_ref):
    x3 = x_ref[...] + _dot(a_ref[...], wo_ref[...])
    x3_ref[...] = x3
    h = _rms(x3, nrm_ref[...])
    _store_token_tiles(h_ref, h, h.shape[0])

    h_hi = h.astype(BF16)
    h_lo = (h - h_hi.astype(F32)).astype(BF16)
    r = r_ref[...]
    r_hi = r.astype(BF16)
    r_lo = (r - r_hi.astype(F32)).astype(BF16)
    logits = _dot(h_hi, r_hi) + (_dot(h_lo, r_hi) + _dot(h_hi, r_lo))

    lane = lax.broadcasted_iota(jnp.int32, logits.shape, 1).astype(F32)
    neg_inf = jnp.float32(-jnp.inf)
    lg = jnp.where(lane < N_EXPERTS, logits, neg_inf)
    m1 = jnp.max(lg, axis=-1, keepdims=True)
    i1 = jnp.min(jnp.where(lg == m1, lane, float(LANES)), axis=-1, keepdims=True)
    lg2 = jnp.where(lane == i1, neg_inf, lg)
    m2 = jnp.max(lg2, axis=-1, keepdims=True)
    i2 = jnp.min(jnp.where(lg2 == m2, lane, float(LANES)), axis=-1, keepdims=True)
    e2 = jnp.exp(m2 - m1)
    g1 = 1.0 / (1.0 + e2)
    g2 = e2 / (1.0 + e2)
    route_ref[...] = jnp.where(lane == 0, i1, jnp.where(lane == 1, i2, jnp.where(lane == 2, g1,
                               jnp.where(lane == 3, g2, 0.0))))


def _odd_out(x, a, wo, nrm, router, *, tm):
    t = x.shape[0]
    row_spec = pl.BlockSpec((tm, D_MODEL), lambda i: (i, 0))
    return pl.pallas_call(
        _odd_out_kernel,
        out_shape=[jax.ShapeDtypeStruct((t, D_MODEL), F32), jax.ShapeDtypeStruct((t * ROW_TILE, LANES), F32),
                   jax.ShapeDtypeStruct((t, LANES), F32)],
        grid=(t // tm,),
        in_specs=[row_spec, row_spec, _const_spec(wo.shape), _const_spec(nrm.shape), _const_spec(router.shape)],
        out_specs=[row_spec, pl.BlockSpec((tm * ROW_TILE, LANES), lambda i: (i, 0)),
                   pl.BlockSpec((tm, LANES), lambda i: (i, 0))],
        compiler_params=pltpu.CompilerParams(dimension_semantics=("arbitrary",), vmem_limit_bytes=VMEM_LIMIT),
        name="odd_out",
    )(x, a, wo, nrm, router)


def _row_gather_copy(src_hbm, src_row, dst_ref, dst_row, sem):
    return pltpu.make_async_copy(
        src_hbm.at[pl.ds(pl.multiple_of(src_row * ROW_TILE, ROW_TILE), ROW_TILE), :],
        dst_ref.at[pl.ds(pl.multiple_of(dst_row * ROW_TILE, ROW_TILE), ROW_TILE), :],
        sem)


GATHER_UNROLL = 8


def _start_row_gathers(src_hbm, idx_ref, idx_row, first, count, dst_ref, sem, *, inline):
    if inline:
        for u in range(count):
            _row_gather_copy(src_hbm, idx_ref[idx_row, first + u], dst_ref, first + u, sem).start()
        return

    def issue(r, c):
        _row_gather_copy(src_hbm, idx_ref[idx_row, r], dst_ref, r, sem).start()
        return c

    lax.fori_loop(first, first + count, issue, 0, unroll=GATHER_UNROLL)


def _wait_row_gathers(src_hbm, dst_ref, sem):
    pltpu.make_async_copy(src_hbm.at[pl.ds(0, dst_ref.shape[0]), :], dst_ref, sem).wait()


def _moe_ffn_kernel(te_ref, nu_ref, tok_ref, tok_next_ref, h_hbm, w1_ref, w3_ref, w2_ref, o_ref, xs_ref, buf_ref,
                    sem, *, tm, n_chunks):
    i = pl.program_id(0)
    n_used = nu_ref[0]
    used = i < n_used
    slot = i % 2

    @pl.when(i == 0)
    def _():
        _start_row_gathers(h_hbm, tok_ref, 0, 0, tm, buf_ref.at[0], sem.at[0], inline=False)

    @pl.when(i <= n_used)
    def _():
        _wait_row_gathers(h_hbm, buf_ref.at[slot], sem.at[slot])
        xs_ref[...] = _load_token_tiles(buf_ref.at[slot], tm).astype(BF16)

    @pl.when(used)
    def _():
        n_groups = 3 * n_chunks
        per_group = tm // n_groups
        starts = [(g * per_group, per_group if g + 1 < n_groups else tm - g * per_group) for g in range(n_groups)]

        def prefetch(g):
            first, count = starts[g]
            _start_row_gathers(h_hbm, tok_next_ref, 0, first, count, buf_ref.at[1 - slot], sem.at[1 - slot],
                               inline=True)

        x = xs_ref[...]
        tf = D_FF_EXPERT // n_chunks
        y = None
        for c in range(n_chunks):
            cols = slice(c * tf, (c + 1) * tf)
            prefetch(3 * c)
            a = _dot(x, w1_ref[:, cols])
            prefetch(3 * c + 1)
            b = _dot(x, w3_ref[:, cols])
            prefetch(3 * c + 2)
            part = _dot((a * _sigmoid(a) * b).astype(BF16), w2_ref[cols, :])
            y = part if y is None else y + part
        _store_token_tiles(o_ref, y, tm)

    @pl.when(jnp.logical_not(used))
    def _():
        o_ref[...] = jnp.zeros_like(o_ref)


def _moe_ffn(tile_expert, n_used, row_tok, h_tiles, w1, w3, w2, *, tm, n_chunks):
    n_tiles = row_tok.shape[0] - 1

    def expert_spec(shape):
        return pl.BlockSpec((None,) + shape, lambda i, te, nu: (te[i], 0, 0), pipeline_mode=pl.Buffered(1))

    grid_spec = pltpu.PrefetchScalarGridSpec(
        num_scalar_prefetch=2,
        grid=(n_tiles,),
        in_specs=[
            pl.BlockSpec((None, 1, tm), lambda i, te, nu: (i, 0, 0), memory_space=pltpu.SMEM),
            pl.BlockSpec((None, 1, tm), lambda i, te, nu: (i + 1, 0, 0), memory_space=pltpu.SMEM),
            pl.BlockSpec(memory_space=pl.ANY),
            expert_spec((D_MODEL, D_FF_EXPERT)), expert_spec((D_MODEL, D_FF_EXPERT)),
            expert_spec((D_FF_EXPERT, D_MODEL)),
        ],
        out_specs=pl.BlockSpec((tm * ROW_TILE, LANES), lambda i, te, nu: (i, 0)),
        scratch_shapes=[pltpu.VMEM((tm, D_MODEL), BF16), pltpu.VMEM((2, tm * ROW_TILE, LANES), F32),
                        pltpu.SemaphoreType.DMA((2,))],
    )
    return pl.pallas_call(
        functools.partial(_moe_ffn_kernel, tm=tm, n_chunks=n_chunks),
        out_shape=jax.ShapeDtypeStruct((n_tiles * tm * ROW_TILE, LANES), F32),
        grid_spec=grid_spec,
        compiler_params=pltpu.CompilerParams(dimension_semantics=("arbitrary",), vmem_limit_bytes=VMEM_LIMIT),
        name="moe_ffn",
    )(tile_expert, n_used, row_tok, row_tok, h_tiles, w1, w3, w2)


def _moe_combine_kernel(pos_ref, pos_next_ref, x_ref, route_ref, nrm_ref, y_hbm, o_ref, buf_ref, sem, *, tm):
    i = pl.program_id(0)
    slot = i % 2

    def start_tile(idx_ref, s):
        for k in range(2):
            _start_row_gathers(y_hbm, idx_ref, k, 0, tm, buf_ref.at[s, k], sem.at[s, k], inline=False)

    @pl.when(i == 0)
    def _():
        start_tile(pos_ref, 0)

    @pl.when(i + 1 < pl.num_programs(0))
    def _():
        start_tile(pos_next_ref, 1 - slot)

    for k in range(2):
        _wait_row_gathers(y_hbm, buf_ref.at[slot, k], sem.at[slot, k])
    g1 = route_ref[:, 2:3]
    g2 = route_ref[:, 3:4]
    x = x_ref[...] + (g1 * _load_token_tiles(buf_ref.at[slot, 0], tm) + g2 * _load_token_tiles(buf_ref.at[slot, 1], tm))
    o_ref[...] = _rms(x, nrm_ref[...])


def _moe_combine(pos, x, route, nrm, y_tiles, *, tm):
    t = x.shape[0]
    n = t // tm
    row_spec = pl.BlockSpec((tm, D_MODEL), lambda i: (i, 0))
    return pl.pallas_call(
        functools.partial(_moe_combine_kernel, tm=tm),
        out_shape=jax.ShapeDtypeStruct((t, D_MODEL), F32),
        grid=(n,),
        in_specs=[pl.BlockSpec((None, 2, tm), lambda i: (i, 0, 0), memory_space=pltpu.SMEM),
                  pl.BlockSpec((None, 2, tm), lambda i: (jnp.minimum(i + 1, n - 1), 0, 0), memory_space=pltpu.SMEM),
                  row_spec, pl.BlockSpec((tm, LANES), lambda i: (i, 0)), _const_spec(nrm.shape),
                  pl.BlockSpec(memory_space=pl.ANY)],
        out_specs=row_spec,
        scratch_shapes=[pltpu.VMEM((2, 2, tm * ROW_TILE, LANES), F32), pltpu.SemaphoreType.DMA((2, 2))],
        compiler_params=pltpu.CompilerParams(dimension_semantics=("arbitrary",), vmem_limit_bytes=VMEM_LIMIT),
        name="moe_combine",
    )(pos, pos, x, route, nrm, y_tiles)


def _routing_tables(idx1, idx2, *, tm, n_tiles):
    t = idx1.shape[0]
    e_flat = jnp.concatenate([idx1, idx2])
    onehot = (e_flat[:, None] == jnp.arange(N_EXPERTS, dtype=jnp.int32)[None, :]).astype(jnp.int32)
    csum = jnp.cumsum(onehot, axis=0)
    rank = jnp.sum((csum - onehot) * onehot, axis=1)
    counts = csum[-1]
    tiles_e = (counts + tm - 1) // tm
    tile_end = jnp.cumsum(tiles_e)
    tile_start = tile_end - tiles_e
    pos = jnp.sum(onehot * tile_start[None, :], axis=1) * tm + rank
    n_used = tile_end[-1]
    tile_ids = jnp.arange(n_tiles, dtype=jnp.int32)
    te = jnp.sum((tile_ids[:, None] >= tile_end[None, :]).astype(jnp.int32), axis=1)
    te_last = jnp.sum((n_used - 1 >= tile_end).astype(jnp.int32))
    tile_expert = jnp.where(tile_ids < n_used, te, te_last).astype(jnp.int32)
    tok = jnp.concatenate([jnp.arange(t, dtype=jnp.int32)] * 2)
    row_tok = jnp.zeros((n_tiles * tm,), jnp.int32).at[pos].set(tok, unique_indices=True)
    return pos[:t], pos[t:], row_tok, tile_expert, n_used.reshape(1).astype(jnp.int32)


def _pad_cols(w, n):
    return jnp.pad(w, ((0, 0), (0, n - w.shape[1])))


def kernel(x, even_norm_mix, even_w_in, even_gate_up, even_gate_bias, even_w_s, even_b_s, even_ln_g, even_ln_b,
           even_head_g, even_w_o, even_norm_ffn, even_ffn_w1, even_ffn_w3, even_ffn_w2, odd_norm_mix, odd_w_in,
           odd_forget_bias, odd_q_g, odd_k_g, odd_w_o, odd_norm_ffn, odd_router, odd_exp_w1, odd_exp_w3,
           odd_exp_w2, final_norm):
    batch, seq, d = x.shape
    t = batch * seq
    xt = x.reshape(t, d)
    tm = min(512, seq)

    w_in = even_w_in[0]
    u_w, v_w, q_w, k_w, g_w, vb_w, og_w = jnp.split(w_in, [512, 1024, 1280, 1536, 1552, 2064], axis=1)
    win_e = jnp.concatenate([u_w, v_w, q_w, k_w, vb_w, og_w, _pad_cols(g_w, LANES)], axis=1).astype(BF16)
    gup = jnp.pad(even_gate_up[0], ((0, LANES - B_GATE_RANK), (0, 0))).astype(BF16)
    gb = even_gate_bias[0].reshape(1, B_QK_WIDTH)
    tril = jnp.tril(jnp.ones((CHUNK, CHUNK), dtype=bool))
    ws = jnp.where(tril[None], even_w_s[0], 0.0).astype(BF16)
    bs = jnp.broadcast_to(even_b_s[0][:, :, None], (A_GROUPS, CHUNK, LANES))
    lng = even_ln_g[0].reshape(1, A_WIDTH)
    lnb = even_ln_b[0].reshape(1, A_WIDTH)
    hg = even_head_g[0].reshape(B_HEADS, 1, B_VAL_DIM)
    x1 = _even_mixer(xt, even_norm_mix[0].reshape(1, d), win_e, gup, gb, ws, bs, lng, lnb, hg,
                     even_w_o[0].astype(BF16), seq=seq, tm=tm)
    x2 = _dense_ffn(x1, even_norm_ffn[0].reshape(1, d), even_ffn_w1[0].astype(BF16), even_ffn_w3[0].astype(BF16),
                    even_ffn_w2[0].astype(BF16), tm=tm)

    q_w, k_w, v_w, og_w, f_w = jnp.split(odd_w_in[0], [1024, 2048, 3072, 4096], axis=1)
    win_o = jnp.concatenate([q_w, k_w, v_w, og_w, _pad_cols(f_w, LANES)], axis=1).astype(BF16)
    fb = jnp.pad(odd_forget_bias[0], (0, LANES - C_HEADS)).reshape(1, LANES)
    qg = jnp.tile(odd_q_g[0], C_HEADS).reshape(1, C_WIDTH)
    kg = jnp.tile(odd_k_g[0], C_HEADS).reshape(1, C_WIDTH)
    q, k_ext, v, gate = _odd_inproj(x2, odd_norm_mix[0].reshape(1, d), win_o, fb, qg, kg, seq=seq, tm=tm)
    tk = min(256, seq)
    tq = min(512, seq)
    vt = jnp.transpose(v.reshape(batch, seq // tk, tk, N_PAIRS, LANES), (0, 3, 1, 4, 2))
    attn = _fox_attention(q, k_ext, vt, gate, batch=batch, seq=seq, tq=tq, tk=tk)

    router = _pad_cols(odd_router[0], LANES)
    x3, h_tiles, route = _odd_out(x2, attn, odd_w_o[0].astype(BF16), odd_norm_ffn[0].reshape(1, d), router, tm=tm)

    tm_moe = 512
    tm_comb = min(256, seq)
    n_tiles = (2 * t) // tm_moe + N_EXPERTS + 1
    idx1 = route[:, 0].astype(jnp.int32)
    idx2 = route[:, 1].astype(jnp.int32)
    pos1, pos2, row_tok, tile_expert, n_used = _routing_tables(idx1, idx2, tm=tm_moe, n_tiles=n_tiles + 1)
    y_tiles = _moe_ffn(tile_expert[:n_tiles], n_used, row_tok.reshape(n_tiles + 1, 1, tm_moe), h_tiles,
                       odd_exp_w1[0].astype(BF16), odd_exp_w3[0].astype(BF16), odd_exp_w2[0].astype(BF16),
                       tm=tm_moe, n_chunks=2)
    pos = jnp.stack([pos1.reshape(t // tm_comb, tm_comb), pos2.reshape(t // tm_comb, tm_comb)], axis=1)
    out = _moe_combine(pos, x3, route, final_norm.reshape(1, d), y_tiles, tm=tm_comb)
    return out.reshape(batch, seq, d)
```

```python
import functools
import math

import jax
import jax.numpy as jnp
from jax import lax
from jax.experimental import pallas as pl
from jax.experimental.pallas import tpu as pltpu

F32 = jnp.float32
BF16 = jnp.bfloat16
HIGHEST = lax.Precision.HIGHEST

EPS = 1e-6
D_MODEL = 1024
CHUNK = 128
SUB = 32
N_SUB = CHUNK // SUB
A_GROUPS = 4
A_WIDTH = 512
B_HEADS = 4
B_KEY_DIM = 64
B_VAL_DIM = 128
B_QK_WIDTH = 256
B_V_WIDTH = 512
B_GATE_RANK = 16
B_GATE_NORMALIZER = 16.0
C_HEADS = 16
C_HEAD_DIM = 64
C_WIDTH = 1024
D_FF_DENSE = 2816
N_EXPERTS = 8
D_FF_EXPERT = 3584
LANES = 128
MAX_DECAY_EXP = 60.0
LOG2E = math.log2(math.e)
ROW_TILE = 8

E_U, E_V, E_Q, E_K, E_VB, E_OG, E_G, E_END = 0, 512, 1024, 1280, 1536, 2048, 2560, 2688
O_Q, O_K, O_V, O_OG, O_F, O_END = 0, 1024, 2048, 3072, 4096, 4224

VMEM_LIMIT = 56 * 1024 * 1024


def _rms(x, g):
    ms = jnp.mean(x * x, axis=-1, keepdims=True)
    return x * lax.rsqrt(ms + EPS) * g


def _gelu_tanh(x):
    c = math.sqrt(2.0 / math.pi)
    return x * (0.5 * (1.0 + jnp.tanh(c * (x + 0.044715 * (x * x * x)))))


def _sigmoid(x):
    return 1.0 / (1.0 + jnp.exp(-x))


def _log_sigmoid(x):
    return jnp.minimum(x, 0.0) - jnp.log(1.0 + jnp.exp(-jnp.abs(x)))


def _dot(a, b):
    return jnp.dot(a, b, preferred_element_type=F32)


def _dot_nt(a, b):
    return lax.dot_general(a, b, (((1,), (1,)), ((), ())), preferred_element_type=F32)


def _split3(x):
    hi = x.astype(BF16)
    r1 = x - hi.astype(F32)
    mid = r1.astype(BF16)
    lo = (r1 - mid.astype(F32)).astype(BF16)
    return hi, mid, lo


def _cumsum_rows(tril_b, x):
    hi, mid, lo = _split3(x)
    return _dot(tril_b, hi) + _dot(tril_b, mid) + _dot(tril_b, lo)


def _const_spec(shape):
    nd = len(shape)
    return pl.BlockSpec(shape, lambda *_: (0,) * nd)


def _even_mixer_kernel(x_ref, nrm_ref, win_ref, gup_ref, gb_ref, ws_ref, bs_ref, lng_ref, lnb_ref,
                       hg_ref, wo_ref, o_ref, z_ref, mix_ref, st_ref, *, tiles_per_batch, n_chunks):
    i = pl.program_id(0)

    @pl.when(i % tiles_per_batch == 0)
    def _():
        st_ref[...] = jnp.zeros_like(st_ref)

    h = _rms(x_ref[...], nrm_ref[...]).astype(BF16)
    z_ref[...] = _dot(h, win_ref[...])

    row = lax.broadcasted_iota(jnp.int32, (CHUNK, CHUNK), 0)
    col = lax.broadcasted_iota(jnp.int32, (CHUNK, CHUNK), 1)
    tril_b = (col <= row).astype(BF16)
    sub_row = row & (SUB - 1)
    head_lane = lax.broadcasted_iota(jnp.int32, (1, B_QK_WIDTH), 1) // B_KEY_DIM
    bd_mask = (lax.broadcasted_iota(jnp.int32, (B_V_WIDTH, B_QK_WIDTH), 0) // B_VAL_DIM
               == lax.broadcasted_iota(jnp.int32, (B_V_WIDTH, B_QK_WIDTH), 1) // B_KEY_DIM)

    def chunk_body(c, carry):
        rows = pl.ds(pl.multiple_of(c * CHUNK, CHUNK), CHUNK)

        u = _gelu_tanh(z_ref[rows, E_U:E_V])
        v = _gelu_tanh(z_ref[rows, E_V:E_Q])
        mu = jnp.mean(v, axis=-1, keepdims=True)
        vc = v - mu
        var = jnp.mean(vc * vc, axis=-1, keepdims=True)
        vln = (vc * lax.rsqrt(var + EPS) * lng_ref[...] + lnb_ref[...]).astype(BF16)
        for g in range(A_GROUPS):
            sl = slice(g * LANES, (g + 1) * LANES)
            mixed = _dot(ws_ref[g], vln[:, sl]) + bs_ref[g]
            mix_ref[rows, sl] = (u[:, sl] * mixed).astype(BF16)

        q = z_ref[rows, E_Q:E_K] * (B_KEY_DIM ** -0.5)
        k = z_ref[rows, E_K:E_VB]
        vb = z_ref[rows, E_VB:E_OG]
        og = z_ref[rows, E_OG:E_G]
        glr = z_ref[rows, E_G:E_END].astype(BF16)
        logit = _dot(glr, gup_ref[...]) + gb_ref[...]
        log_a = _log_sigmoid(logit) * (1.0 / B_GATE_NORMALIZER)
        g_cum = _cumsum_rows(tril_b, log_a)
        g_last = g_cum[CHUNK - 1:CHUNK, :]
        st = st_ref[...]
        o = _dot_nt((q * jnp.exp(g_cum)).astype(BF16), st.astype(BF16))

        p_rows = [[None] * N_SUB for _ in range(B_HEADS)]
        for s in range(N_SUB):
            gs = g_cum[s * SUB:(s + 1) * SUB, :]
            if s == 0:
                qt = q[0:SUB, :] * jnp.exp(gs)
                kt = k * jnp.exp(jnp.minimum(-g_cum, MAX_DECAY_EXP))
            else:
                ref_g = g_cum[s * SUB - 1:s * SUB, :]
                qt = q[s * SUB:(s + 1) * SUB, :] * jnp.exp(gs - ref_g)
                kt = k * jnp.exp(jnp.minimum(ref_g - g_cum, MAX_DECAY_EXP))
            qs = jnp.concatenate([jnp.where(head_lane == hh, qt, 0.0) for hh in range(B_HEADS)],
                                 axis=0).astype(BF16)
            sc = _dot_nt(qs, kt.astype(BF16))
            sc = jnp.where(col <= (s * SUB + sub_row), sc, 0.0)
            for hh in range(B_HEADS):
                p_rows[hh][s] = sc[hh * SUB:(hh + 1) * SUB, :]

        vb_b = vb.astype(BF16)
        for hh in range(B_HEADS):
            sl = slice(hh * B_VAL_DIM, (hh + 1) * B_VAL_DIM)
            ph = jnp.concatenate(p_rows[hh], axis=0).astype(BF16)
            oh = o[:, sl] + _dot(ph, vb_b[:, sl])
            on = _rms(oh, hg_ref[hh])
            ogh = og[:, sl]
            mix_ref[rows, A_WIDTH + hh * B_VAL_DIM:A_WIDTH + (hh + 1) * B_VAL_DIM] = (
                on * (ogh * _sigmoid(ogh))).astype(BF16)

        k_dec = (k * jnp.exp(g_last - g_cum)).astype(BF16)
        upd = _dot(vb.T.astype(BF16), k_dec)
        st_ref[...] = jnp.exp(g_last) * st + jnp.where(bd_mask, upd, 0.0)
        return carry

    lax.fori_loop(0, n_chunks, chunk_body, 0)
    o_ref[...] = x_ref[...] + _dot(mix_ref[...], wo_ref[...])


def _even_mixer(x, nrm, win, gup, gb, ws, bs, lng, lnb, hg, wo, *, seq, tm):
    t = x.shape[0]
    kern = functools.partial(_even_mixer_kernel, tiles_per_batch=seq // tm, n_chunks=tm // CHUNK)
    return pl.pallas_call(
        kern,
        out_shape=jax.ShapeDtypeStruct((t, D_MODEL), F32),
        grid=(t // tm,),
        in_specs=[
            pl.BlockSpec((tm, D_MODEL), lambda i: (i, 0)),
            _const_spec(nrm.shape), _const_spec(win.shape), _const_spec(gup.shape), _const_spec(gb.shape),
            _const_spec(ws.shape), _const_spec(bs.shape), _const_spec(lng.shape), _const_spec(lnb.shape),
            _const_spec(hg.shape), _const_spec(wo.shape),
        ],
        out_specs=pl.BlockSpec((tm, D_MODEL), lambda i: (i, 0)),
        scratch_shapes=[
            pltpu.VMEM((tm, E_END), F32),
            pltpu.VMEM((tm, D_MODEL), BF16),
            pltpu.VMEM((B_V_WIDTH, B_QK_WIDTH), F32),
        ],
        compiler_params=pltpu.CompilerParams(dimension_semantics=("arbitrary",), vmem_limit_bytes=VMEM_LIMIT),
        name="even_mixer",
    )(x, nrm, win, gup, gb, ws, bs, lng, lnb, hg, wo)


def _dense_ffn_kernel(x_ref, nrm_ref, w1_ref, w3_ref, w2_ref, o_ref):
    x = x_ref[...]
    h = _rms(x, nrm_ref[...]).astype(BF16)
    a = _dot(h, w1_ref[...])
    b = _dot(h, w3_ref[...])
    o_ref[...] = x + _dot((a * _sigmoid(a) * b).astype(BF16), w2_ref[...])


def _resident_spec(shape):
    nd = len(shape)
    return pl.BlockSpec(shape, lambda *_: (0,) * nd, pipeline_mode=pl.Buffered(1))


def _dense_ffn(x, nrm, w1, w3, w2, *, tm):
    t = x.shape[0]
    row_spec = pl.BlockSpec((tm, D_MODEL), lambda i: (i, 0))
    return pl.pallas_call(
        _dense_ffn_kernel,
        out_shape=jax.ShapeDtypeStruct((t, D_MODEL), F32),
        grid=(t // tm,),
        in_specs=[row_spec, _const_spec(nrm.shape), _resident_spec(w1.shape), _resident_spec(w3.shape),
                  _resident_spec(w2.shape)],
        out_specs=row_spec,
        compiler_params=pltpu.CompilerParams(dimension_semantics=("arbitrary",), vmem_limit_bytes=VMEM_LIMIT),
        name="dense_ffn",
    )(x, nrm, w1, w3, w2)


def _head_rms(x, gain):
    lo = lax.broadcasted_iota(jnp.int32, (1, LANES), 1) < C_HEAD_DIM
    outs = []
    for t in range(C_WIDTH // LANES):
        xt = x[:, t * LANES:(t + 1) * LANES]
        sq = xt * xt
        s_lo = jnp.sum(jnp.where(lo, sq, 0.0), axis=-1, keepdims=True)
        s_hi = jnp.sum(jnp.where(lo, 0.0, sq), axis=-1, keepdims=True)
        inv = jnp.where(lo, lax.rsqrt(s_lo * (1.0 / C_HEAD_DIM) + EPS), lax.rsqrt(s_hi * (1.0 / C_HEAD_DIM) + EPS))
        outs.append(xt * inv)
    return jnp.concatenate(outs, axis=-1) * gain


N_PAIRS = C_HEADS // 2
K_EXT = 2 * LANES
BIAS_PARTS = 3


def _bias_placement():
    src = jnp.arange(BIAS_PARTS * LANES)
    part, head = src // LANES, src % LANES
    dst = (head // 2) * LANES + BIAS_PARTS * (head % 2) + part
    hit = (dst[:, None] == jnp.arange(N_PAIRS * LANES)[None, :]) & (head < C_HEADS)[:, None]
    return hit.astype(BF16)


def _odd_inproj_kernel(x_ref, nrm_ref, w_ref, fb_ref, qg_ref, kg_ref, place_ref,
                       q_ref, k_ref, v_ref, gate_ref, z_ref, c_ref, carry_ref, *, tiles_per_batch, n_chunks):
    i = pl.program_id(0)

    @pl.when(i % tiles_per_batch == 0)
    def _():
        carry_ref[...] = jnp.zeros_like(carry_ref)

    h = _rms(x_ref[...], nrm_ref[...]).astype(BF16)
    z_ref[...] = _dot(h, w_ref[...])
    q_ref[...] = (_head_rms(z_ref[:, O_Q:O_K], qg_ref[...]) * (C_HEAD_DIM ** -0.5 * LOG2E)).astype(BF16)
    kn = _head_rms(z_ref[:, O_K:O_V], kg_ref[...]).astype(BF16)
    v_ref[...] = z_ref[:, O_V:O_OG].astype(BF16)
    gate_ref[...] = _sigmoid(z_ref[:, O_OG:O_F]).astype(BF16)

    row = lax.broadcasted_iota(jnp.int32, (CHUNK, CHUNK), 0)
    col = lax.broadcasted_iota(jnp.int32, (CHUNK, CHUNK), 1)
    tril_b = (col <= row).astype(BF16)
    carry = carry_ref[...]
    for c in range(n_chunks):
        rows = slice(c * CHUNK, (c + 1) * CHUNK)
        log_f = _log_sigmoid(z_ref[rows, O_F:O_END] + fb_ref[...])
        cs = _cumsum_rows(tril_b, log_f) + carry
        c_ref[rows, :] = cs
        carry = cs[CHUNK - 1:CHUNK, :]
    carry_ref[...] = carry

    bias = _dot(jnp.concatenate(_split3(c_ref[...] * LOG2E), axis=1), place_ref[...]).astype(BF16)
    for p in range(N_PAIRS):
        k_ref[:, p * K_EXT:p * K_EXT + LANES] = kn[:, p * LANES:(p + 1) * LANES]
        k_ref[:, p * K_EXT + LANES:(p + 1) * K_EXT] = bias[:, p * LANES:(p + 1) * LANES]


def _odd_inproj(x, nrm, w, fb, qg, kg, *, seq, tm):
    t = x.shape[0]
    kern = functools.partial(_odd_inproj_kernel, tiles_per_batch=seq // tm, n_chunks=tm // CHUNK)
    row_spec = pl.BlockSpec((tm, C_WIDTH), lambda i: (i, 0))
    kext_spec = pl.BlockSpec((tm, N_PAIRS * K_EXT), lambda i: (i, 0))
    place = _bias_placement()
    wide = jax.ShapeDtypeStruct((t, C_WIDTH), BF16)
    return pl.pallas_call(
        kern,
        out_shape=[wide, jax.ShapeDtypeStruct((t, N_PAIRS * K_EXT), BF16), wide, wide],
        grid=(t // tm,),
        in_specs=[row_spec, _const_spec(nrm.shape), _const_spec(w.shape), _const_spec(fb.shape),
                  _const_spec(qg.shape), _const_spec(kg.shape), _const_spec(place.shape)],
        out_specs=[row_spec, kext_spec, row_spec, row_spec],
        scratch_shapes=[pltpu.VMEM((tm, O_END), F32), pltpu.VMEM((tm, LANES), F32), pltpu.VMEM((1, LANES), F32)],
        compiler_params=pltpu.CompilerParams(dimension_semantics=("arbitrary",), vmem_limit_bytes=VMEM_LIMIT),
        name="odd_inproj",
    )(x, nrm, w, fb, qg, kg, place)


NEG_BIG = -1e30


V_ROWS = 80


def _fox_kernel(q_ref, k_ref, vt_ref, gate_ref, o_ref, s_ref, *, tq, tk):
    qi = pl.program_id(2)
    ng = tq // tk
    key_i = lax.broadcasted_iota(jnp.int32, (tk, tk), 0)
    qry_i = lax.broadcasted_iota(jnp.int32, (tk, tk), 1)
    causal = key_i <= qry_i
    chains = [(hh, r) for r in range(ng) for hh in range(2)]
    feat = lax.broadcasted_iota(jnp.int32, (LANES, 1), 0)
    qms = []
    for hh, r in chains:
        q_t = q_ref[r * tk:(r + 1) * tk, :].astype(F32).T
        own = (feat < C_HEAD_DIM) if hh == 0 else (feat >= C_HEAD_DIM)
        qh = jnp.where(own, q_t, 0.0).astype(BF16)
        pick = (feat >= BIAS_PARTS * hh) & (feat < BIAS_PARTS * (hh + 1))
        minus_one = jnp.broadcast_to(jnp.where(pick, -1.0, 0.0).astype(BF16), qh.shape)
        qms.append(jnp.concatenate([qh, minus_one], axis=0))

    def key_rows(j):
        return pl.ds(pl.multiple_of(j * tk, tk), tk)

    def scores_to_scratch(j, slot, live, modes):
        kb = k_ref[key_rows(j), :]
        raw = [_dot(kb, qms[idx]) for idx in live]
        maxes = []
        for s, idx in zip(raw, live):
            hh, r = chains[idx]
            if modes[r] == "diag":
                s = jnp.where(causal, s, NEG_BIG)
            s_ref[slot, idx] = s
            maxes.append(jnp.max(s, axis=0, keepdims=True))
        return maxes

    ones_rows = (lax.broadcasted_iota(jnp.int32, (V_ROWS - C_HEAD_DIM, tk), 0) == 0).astype(BF16)

    def softmax_pv(j, slot, maxes, live, state):
        vt_pair = vt_ref[j]
        vtb = [jnp.concatenate([vt_pair[hh * C_HEAD_DIM:(hh + 1) * C_HEAD_DIM, :], ones_rows], axis=0)
               for hh in range(2)]
        new = list(state)
        probs = []
        for bm, idx in zip(maxes, live):
            m = state[2 * idx]
            m_new = jnp.maximum(m, bm)
            p = jnp.exp2(s_ref[slot, idx] - m_new)
            new[2 * idx] = m_new
            probs.append((jnp.exp2(m - m_new), p.astype(BF16)))
        for (alpha, p), idx in zip(probs, live):
            hh, r = chains[idx]
            new[2 * idx + 1] = alpha * state[2 * idx + 1] + _dot(vtb[hh], p)
        return new

    assert ng % 2 == 0
    all_chains = list(range(len(chains)))
    n_state = 2 * len(chains)
    state = []
    for _ in chains:
        state += [jnp.full((1, tk), NEG_BIG, F32), jnp.zeros((V_ROWS, tk), F32)]
    n_full = qi * ng
    full_modes = ("full",) * ng

    def diag_modes(g):
        return tuple("skip" if r < g else ("diag" if r == g else "full") for r in range(ng))

    def live_chains(g):
        return [idx for idx, (hh, r) in enumerate(chains) if r >= g]

    def two_blocks(j, carry, next_modes):
        st, mx0 = list(carry[:n_state]), carry[n_state:]
        mx1 = scores_to_scratch(j + 1, 1, all_chains, full_modes)
        st = softmax_pv(j, 0, mx0, all_chains, st)
        mx0 = scores_to_scratch(j + 2, 0, all_chains, next_modes)
        st = softmax_pv(j + 1, 1, mx1, all_chains, st)
        return tuple(st) + tuple(mx0)

    def with_full_blocks(_):
        first = scores_to_scratch(0, 0, all_chains, full_modes)
        carry = lax.fori_loop(0, n_full // 2 - 1, lambda i, c: two_blocks(2 * i, c, full_modes),
                              tuple(state) + tuple(first))
        return two_blocks(n_full - 2, carry, diag_modes(0))

    def no_full_blocks(_):
        return tuple(state) + tuple(scores_to_scratch(0, 0, all_chains, diag_modes(0)))

    carry = lax.cond(qi > 0, with_full_blocks, no_full_blocks, 0)
    state, mx = list(carry[:n_state]), carry[n_state:]
    for g in range(ng):
        if g + 1 < ng:
            mx_next = scores_to_scratch(n_full + g + 1, (g + 1) % 2, live_chains(g + 1), diag_modes(g + 1))
        state = softmax_pv(n_full + g, g % 2, mx, live_chains(g), state)
        if g + 1 < ng:
            mx = mx_next

    for r in range(ng):
        parts = []
        for hh in range(2):
            acc = state[2 * chains.index((hh, r)) + 1]
            parts.append(acc[:C_HEAD_DIM, :] / acc[C_HEAD_DIM:C_HEAD_DIM + 1, :])
        o = jnp.concatenate(parts, axis=0).T
        rows = slice(r * tk, (r + 1) * tk)
        o_ref[rows, :] = (o * gate_ref[rows, :].astype(F32)).astype(BF16)


def _fox_attention(q, k_ext, vt, gate, *, batch, seq, tq, tk):
    t = q.shape[0]
    nq = seq // tq
    kern = functools.partial(_fox_kernel, tq=tq, tk=tk)
    return pl.pallas_call(
        kern,
        out_shape=jax.ShapeDtypeStruct((t, C_WIDTH), BF16),
        grid=(batch, N_PAIRS, nq),
        in_specs=[
            pl.BlockSpec((tq, LANES), lambda b, p, i: (b * nq + i, p)),
            pl.BlockSpec((seq, K_EXT), lambda b, p, i: (b, p)),
            pl.BlockSpec((None, None, seq // tk, LANES, tk), lambda b, p, i: (b, p, 0, 0, 0)),
            pl.BlockSpec((tq, LANES), lambda b, p, i: (b * nq + i, p)),
        ],
        out_specs=pl.BlockSpec((tq, LANES), lambda b, p, i: (b * nq + i, p)),
        scratch_shapes=[pltpu.VMEM((2, 2 * (tq // tk), tk, tk), F32)],
        compiler_params=pltpu.CompilerParams(dimension_semantics=("arbitrary", "arbitrary", "arbitrary"),
                                             vmem_limit_bytes=VMEM_LIMIT),
        name="fox_attn",
    )(q, k_ext, vt, gate)


def _store_token_tiles(dst_ref, val, n_rows):
    for s in range(ROW_TILE):
        dst_ref[pl.ds(s, n_rows, stride=ROW_TILE), :] = val[:, s * LANES:(s + 1) * LANES]


def _load_token_tiles(src_ref, n_rows):
    return jnp.concatenate([src_ref[pl.ds(s, n_rows, stride=ROW_TILE), :] for s in range(ROW_TILE)], axis=1)


def _odd_out_kernel(x_ref, a_ref, wo_ref, nrm_ref, r_ref, x3_ref, h_ref, route_ref):
    x3 = x_ref[...] + _dot(a_ref[...], wo_ref[...])
    x3_ref[...] = x3
    h = _rms(x3, nrm_ref[...])
    _store_token_tiles(h_ref, h, h.shape[0])

    h_hi = h.astype(BF16)
    h_lo = (h - h_hi.astype(F32)).astype(BF16)
    r = r_ref[...]
    r_hi = r.astype(BF16)
    r_lo = (r - r_hi.astype(F32)).astype(BF16)
    logits = _dot(h_hi, r_hi) + (_dot(h_lo, r_hi) + _dot(h_hi, r_lo))

    lane = lax.broadcasted_iota(jnp.int32, logits.shape, 1).astype(F32)
    neg_inf = jnp.float32(-jnp.inf)
    lg = jnp.where(lane < N_EXPERTS, logits, neg_inf)
    m1 = jnp.max(lg, axis=-1, keepdims=True)
    i1 = jnp.min(jnp.where(lg == m1, lane, float(LANES)), axis=-1, keepdims=True)
    lg2 = jnp.where(lane == i1, neg_inf, lg)
    m2 = jnp.max(lg2, axis=-1, keepdims=True)
    i2 = jnp.min(jnp.where(lg2 == m2, lane, float(LANES)), axis=-1, keepdims=True)
    e2 = jnp.exp(m2 - m1)
    g1 = 1.0 / (1.0 + e2)
    g2 = e2 / (1.0 + e2)
    route_ref[...] = jnp.where(lane == 0, i1, jnp.where(lane == 1, i2, jnp.where(lane == 2, g1,
                               jnp.where(lane == 3, g2, 0.0))))


def _odd_out(x, a, wo, nrm, router, *, tm):
    t = x.shape[0]
    row_spec = pl.BlockSpec((tm, D_MODEL), lambda i: (i, 0))
    return pl.pallas_call(
        _odd_out_kernel,
        out_shape=[jax.ShapeDtypeStruct((t, D_MODEL), F32), jax.ShapeDtypeStruct((t * ROW_TILE, LANES), F32),
                   jax.ShapeDtypeStruct((t, LANES), F32)],
        grid=(t // tm,),
        in_specs=[row_spec, row_spec, _const_spec(wo.shape), _const_spec(nrm.shape), _const_spec(router.shape)],
        out_specs=[row_spec, pl.BlockSpec((tm * ROW_TILE, LANES), lambda i: (i, 0)),
                   pl.BlockSpec((tm, LANES), lambda i: (i, 0))],
        compiler_params=pltpu.CompilerParams(dimension_semantics=("arbitrary",), vmem_limit_bytes=VMEM_LIMIT),
        name="odd_out",
    )(x, a, wo, nrm, router)


def _row_gather_copy(src_hbm, src_row, dst_ref, dst_row, sem):
    return pltpu.make_async_copy(
        src_hbm.at[pl.ds(pl.multiple_of(src_row * ROW_TILE, ROW_TILE), ROW_TILE), :],
        dst_ref.at[pl.ds(pl.multiple_of(dst_row * ROW_TILE, ROW_TILE), ROW_TILE), :],
        sem)


GATHER_UNROLL = 8


def _start_row_gathers(src_hbm, idx_ref, idx_row, first, count, dst_ref, sem, *, inline):
    if inline:
        for u in range(count):
            _row_gather_copy(src_hbm, idx_ref[idx_row, first + u], dst_ref, first + u, sem).start()
        return

    def issue(r, c):
        _row_gather_copy(src_hbm, idx_ref[idx_row, r], dst_ref, r, sem).start()
        return c

    lax.fori_loop(first, first + count, issue, 0, unroll=GATHER_UNROLL)


def _wait_row_gathers(src_hbm, dst_ref, sem):
    pltpu.make_async_copy(src_hbm.at[pl.ds(0, dst_ref.shape[0]), :], dst_ref, sem).wait()


def _moe_ffn_kernel(te_ref, nu_ref, tok_ref, tok_next_ref, h_hbm, w1_ref, w3_ref, w2_ref, o_ref, xs_ref, buf_ref,
                    sem, *, tm, n_chunks):
    i = pl.program_id(0)
    n_used = nu_ref[0]
    used = i < n_used
    slot = i % 2

    @pl.when(i == 0)
    def _():
        _start_row_gathers(h_hbm, tok_ref, 0, 0, tm, buf_ref.at[0], sem.at[0], inline=False)

    @pl.when(i <= n_used)
    def _():
        _wait_row_gathers(h_hbm, buf_ref.at[slot], sem.at[slot])
        xs_ref[...] = _load_token_tiles(buf_ref.at[slot], tm).astype(BF16)

    @pl.when(used)
    def _():
        n_groups = 3 * n_chunks
        per_group = tm // n_groups
        starts = [(g * per_group, per_group if g + 1 < n_groups else tm - g * per_group) for g in range(n_groups)]

        def prefetch(g):
            first, count = starts[g]
            _start_row_gathers(h_hbm, tok_next_ref, 0, first, count, buf_ref.at[1 - slot], sem.at[1 - slot],
                               inline=True)

        x = xs_ref[...]
        tf = D_FF_EXPERT // n_chunks
        y = None
        for c in range(n_chunks):
            cols = slice(c * tf, (c + 1) * tf)
            prefetch(3 * c)
            a = _dot(x, w1_ref[:, cols])
            prefetch(3 * c + 1)
            b = _dot(x, w3_ref[:, cols])
            prefetch(3 * c + 2)
            part = _dot((a * _sigmoid(a) * b).astype(BF16), w2_ref[cols, :])
            y = part if y is None else y + part
        _store_token_tiles(o_ref, y, tm)

    @pl.when(jnp.logical_not(used))
    def _():
        o_ref[...] = jnp.zeros_like(o_ref)


def _moe_ffn(tile_expert, n_used, row_tok, h_tiles, w1, w3, w2, *, tm, n_chunks):
    n_tiles = row_tok.shape[0] - 1

    def expert_spec(shape):
        return pl.BlockSpec((None,) + shape, lambda i, te, nu: (te[i], 0, 0), pipeline_mode=pl.Buffered(1))

    grid_spec = pltpu.PrefetchScalarGridSpec(
        num_scalar_prefetch=2,
        grid=(n_tiles,),
        in_specs=[
            pl.BlockSpec((None, 1, tm), lambda i, te, nu: (i, 0, 0), memory_space=pltpu.SMEM),
            pl.BlockSpec((None, 1, tm), lambda i, te, nu: (i + 1, 0, 0), memory_space=pltpu.SMEM),
            pl.BlockSpec(memory_space=pl.ANY),
            expert_spec((D_MODEL, D_FF_EXPERT)), expert_spec((D_MODEL, D_FF_EXPERT)),
            expert_spec((D_FF_EXPERT, D_MODEL)),
        ],
        out_specs=pl.BlockSpec((tm * ROW_TILE, LANES), lambda i, te, nu: (i, 0)),
        scratch_shapes=[pltpu.VMEM((tm, D_MODEL), BF16), pltpu.VMEM((2, tm * ROW_TILE, LANES), F32),
                        pltpu.SemaphoreType.DMA((2,))],
    )
    return pl.pallas_call(
        functools.partial(_moe_ffn_kernel, tm=tm, n_chunks=n_chunks),
        out_shape=jax.ShapeDtypeStruct((n_tiles * tm * ROW_TILE, LANES), F32),
        grid_spec=grid_spec,
        compiler_params=pltpu.CompilerParams(dimension_semantics=("arbitrary",), vmem_limit_bytes=VMEM_LIMIT),
        name="moe_ffn",
    )(tile_expert, n_used, row_tok, row_tok, h_tiles, w1, w3, w2)


def _moe_combine_kernel(pos_ref, pos_next_ref, x_ref, route_ref, nrm_ref, y_hbm, o_ref, buf_ref, sem, *, tm):
    i = pl.program_id(0)
    slot = i % 2

    def start_tile(idx_ref, s):
        for k in range(2):
            _start_row_gathers(y_hbm, idx_ref, k, 0, tm, buf_ref.at[s, k], sem.at[s, k], inline=False)

    @pl.when(i == 0)
    def _():
        start_tile(pos_ref, 0)

    @pl.when(i + 1 < pl.num_programs(0))
    def _():
        start_tile(pos_next_ref, 1 - slot)

    for k in range(2):
        _wait_row_gathers(y_hbm, buf_ref.at[slot, k], sem.at[slot, k])
    g1 = route_ref[:, 2:3]
    g2 = route_ref[:, 3:4]
    x = x_ref[...] + (g1 * _load_token_tiles(buf_ref.at[slot, 0], tm) + g2 * _load_token_tiles(buf_ref.at[slot, 1], tm))
    o_ref[...] = _rms(x, nrm_ref[...])


def _moe_combine(pos, x, route, nrm, y_tiles, *, tm):
    t = x.shape[0]
    n = t // tm
    row_spec = pl.BlockSpec((tm, D_MODEL), lambda i: (i, 0))
    return pl.pallas_call(
        functools.partial(_moe_combine_kernel, tm=tm),
        out_shape=jax.ShapeDtypeStruct((t, D_MODEL), F32),
        grid=(n,),
        in_specs=[pl.BlockSpec((None, 2, tm), lambda i: (i, 0, 0), memory_space=pltpu.SMEM),
                  pl.BlockSpec((None, 2, tm), lambda i: (jnp.minimum(i + 1, n - 1), 0, 0), memory_space=pltpu.SMEM),
                  row_spec, pl.BlockSpec((tm, LANES), lambda i: (i, 0)), _const_spec(nrm.shape),
                  pl.BlockSpec(memory_space=pl.ANY)],
        out_specs=row_spec,
        scratch_shapes=[pltpu.VMEM((2, 2, tm * ROW_TILE, LANES), F32), pltpu.SemaphoreType.DMA((2, 2))],
        compiler_params=pltpu.CompilerParams(dimension_semantics=("arbitrary",), vmem_limit_bytes=VMEM_LIMIT),
        name="moe_combine",
    )(pos, pos, x, route, nrm, y_tiles)


def _routing_tables(idx1, idx2, *, tm, n_tiles):
    t = idx1.shape[0]
    e_flat = jnp.concatenate([idx1, idx2])
    onehot = (e_flat[:, None] == jnp.arange(N_EXPERTS, dtype=jnp.int32)[None, :]).astype(jnp.int32)
    csum = jnp.cumsum(onehot, axis=0)
    rank = jnp.sum((csum - onehot) * onehot, axis=1)
    counts = csum[-1]
    tiles_e = (counts + tm - 1) // tm
    tile_end = jnp.cumsum(tiles_e)
    tile_start = tile_end - tiles_e
    pos = jnp.sum(onehot * tile_start[None, :], axis=1) * tm + rank
    n_used = tile_end[-1]
    tile_ids = jnp.arange(n_tiles, dtype=jnp.int32)
    te = jnp.sum((tile_ids[:, None] >= tile_end[None, :]).astype(jnp.int32), axis=1)
    te_last = jnp.sum((n_used - 1 >= tile_end).astype(jnp.int32))
    tile_expert = jnp.where(tile_ids < n_used, te, te_last).astype(jnp.int32)
    tok = jnp.concatenate([jnp.arange(t, dtype=jnp.int32)] * 2)
    row_tok = jnp.zeros((n_tiles * tm,), jnp.int32).at[pos].set(tok, unique_indices=True)
    return pos[:t], pos[t:], row_tok, tile_expert, n_used.reshape(1).astype(jnp.int32)


def _pad_cols(w, n):
    return jnp.pad(w, ((0, 0), (0, n - w.shape[1])))


def kernel(x, even_norm_mix, even_w_in, even_gate_up, even_gate_bias, even_w_s, even_b_s, even_ln_g, even_ln_b,
           even_head_g, even_w_o, even_norm_ffn, even_ffn_w1, even_ffn_w3, even_ffn_w2, odd_norm_mix, odd_w_in,
           odd_forget_bias, odd_q_g, odd_k_g, odd_w_o, odd_norm_ffn, odd_router, odd_exp_w1, odd_exp_w3,
           odd_exp_w2, final_norm):
    batch, seq, d = x.shape
    t = batch * seq
    xt = x.reshape(t, d)
    tm = min(512, seq)

    w_in = even_w_in[0]
    u_w, v_w, q_w, k_w, g_w, vb_w, og_w = jnp.split(w_in, [512, 1024, 1280, 1536, 1552, 2064], axis=1)
    win_e = jnp.concatenate([u_w, v_w, q_w, k_w, vb_w, og_w, _pad_cols(g_w, LANES)], axis=1).astype(BF16)
    gup = jnp.pad(even_gate_up[0], ((0, LANES - B_GATE_RANK), (0, 0))).astype(BF16)
    gb = even_gate_bias[0].reshape(1, B_QK_WIDTH)
    tril = jnp.tril(jnp.ones((CHUNK, CHUNK), dtype=bool))
    ws = jnp.where(tril[None], even_w_s[0], 0.0).astype(BF16)
    bs = jnp.broadcast_to(even_b_s[0][:, :, None], (A_GROUPS, CHUNK, LANES))
    lng = even_ln_g[0].reshape(1, A_WIDTH)
    lnb = even_ln_b[0].reshape(1, A_WIDTH)
    hg = even_head_g[0].reshape(B_HEADS, 1, B_VAL_DIM)
    x1 = _even_mixer(xt, even_norm_mix[0].reshape(1, d), win_e, gup, gb, ws, bs, lng, lnb, hg,
                     even_w_o[0].astype(BF16), seq=seq, tm=tm)
    x2 = _dense_ffn(x1, even_norm_ffn[0].reshape(1, d), even_ffn_w1[0].astype(BF16), even_ffn_w3[0].astype(BF16),
                    even_ffn_w2[0].astype(BF16), tm=tm)

    q_w, k_w, v_w, og_w, f_w = jnp.split(odd_w_in[0], [1024, 2048, 3072, 4096], axis=1)
    win_o = jnp.concatenate([q_w, k_w, v_w, og_w, _pad_cols(f_w, LANES)], axis=1).astype(BF16)
    fb = jnp.pad(odd_forget_bias[0], (0, LANES - C_HEADS)).reshape(1, LANES)
    qg = jnp.tile(odd_q_g[0], C_HEADS).reshape(1, C_WIDTH)
    kg = jnp.tile(odd_k_g[0], C_HEADS).reshape(1, C_WIDTH)
    q, k_ext, v, gate = _odd_inproj(x2, odd_norm_mix[0].reshape(1, d), win_o, fb, qg, kg, seq=seq, tm=tm)
    tk = min(256, seq)
    tq = min(1024, seq)
    vt = jnp.transpose(v.reshape(batch, seq // tk, tk, N_PAIRS, LANES), (0, 3, 1, 4, 2))
    attn = _fox_attention(q, k_ext, vt, gate, batch=batch, seq=seq, tq=tq, tk=tk)

    router = _pad_cols(odd_router[0], LANES)
    x3, h_tiles, route = _odd_out(x2, attn, odd_w_o[0].astype(BF16), odd_norm_ffn[0].reshape(1, d), router, tm=tm)

    tm_moe = 512
    tm_comb = min(256, seq)
    n_tiles = (2 * t) // tm_moe + N_EXPERTS + 1
    idx1 = route[:, 0].astype(jnp.int32)
    idx2 = route[:, 1].astype(jnp.int32)
    pos1, pos2, row_tok, tile_expert, n_used = _routing_tables(idx1, idx2, tm=tm_moe, n_tiles=n_tiles + 1)
    y_tiles = _moe_ffn(tile_expert[:n_tiles], n_used, row_tok.reshape(n_tiles + 1, 1, tm_moe), h_tiles,
                       odd_exp_w1[0].astype(BF16), odd_exp_w3[0].astype(BF16), odd_exp_w2[0].astype(BF16),
                       tm=tm_moe, n_chunks=2)
    pos = jnp.stack([pos1.reshape(t // tm_comb, tm_comb), pos2.reshape(t // tm_comb, tm_comb)], axis=1)
    out = _moe_combine(pos, x3, route, final_norm.reshape(1, d), y_tiles, tm=tm_comb)
    return out.reshape(batch, seq, d)
```

```python
import functools
import math

import jax
import jax.numpy as jnp
from jax import lax
from jax.experimental import pallas as pl
from jax.experimental.pallas import tpu as pltpu

F32 = jnp.float32
BF16 = jnp.bfloat16
HIGHEST = lax.Precision.HIGHEST

EPS = 1e-6
D_MODEL = 1024
CHUNK = 128
SUB = 32
N_SUB = CHUNK // SUB
A_GROUPS = 4
A_WIDTH = 512
B_HEADS = 4
B_KEY_DIM = 64
B_VAL_DIM = 128
B_QK_WIDTH = 256
B_V_WIDTH = 512
B_GATE_RANK = 16
B_GATE_NORMALIZER = 16.0
C_HEADS = 16
C_HEAD_DIM = 64
C_WIDTH = 1024
D_FF_DENSE = 2816
N_EXPERTS = 8
D_FF_EXPERT = 3584
LANES = 128
MAX_DECAY_EXP = 60.0
LOG2E = math.log2(math.e)
ROW_TILE = 8

E_U, E_V, E_Q, E_K, E_VB, E_OG, E_G, E_END = 0, 512, 1024, 1280, 1536, 2048, 2560, 2688
O_Q, O_K, O_V, O_OG, O_F, O_END = 0, 1024, 2048, 3072, 4096, 4224

VMEM_LIMIT = 56 * 1024 * 1024


def _rms(x, g):
    ms = jnp.mean(x * x, axis=-1, keepdims=True)
    return x * lax.rsqrt(ms + EPS) * g


def _gelu_tanh(x):
    c = math.sqrt(2.0 / math.pi)
    return x * (0.5 * (1.0 + jnp.tanh(c * (x + 0.044715 * (x * x * x)))))


def _sigmoid(x):
    return 1.0 / (1.0 + jnp.exp(-x))


def _log_sigmoid(x):
    return jnp.minimum(x, 0.0) - jnp.log(1.0 + jnp.exp(-jnp.abs(x)))


def _dot(a, b):
    return jnp.dot(a, b, preferred_element_type=F32)


def _dot_nt(a, b):
    return lax.dot_general(a, b, (((1,), (1,)), ((), ())), preferred_element_type=F32)


def _split3(x):
    hi = x.astype(BF16)
    r1 = x - hi.astype(F32)
    mid = r1.astype(BF16)
    lo = (r1 - mid.astype(F32)).astype(BF16)
    return hi, mid, lo


def _cumsum_rows(tril_b, x):
    hi, mid, lo = _split3(x)
    return _dot(tril_b, hi) + _dot(tril_b, mid) + _dot(tril_b, lo)


def _const_spec(shape):
    nd = len(shape)
    return pl.BlockSpec(shape, lambda *_: (0,) * nd)


def _even_mixer_kernel(x_ref, nrm_ref, win_ref, gup_ref, gb_ref, ws_ref, bs_ref, lng_ref, lnb_ref,
                       hg_ref, wo_ref, o_ref, z_ref, mix_ref, st_ref, *, tiles_per_batch, n_chunks):
    i = pl.program_id(0)

    @pl.when(i % tiles_per_batch == 0)
    def _():
        st_ref[...] = jnp.zeros_like(st_ref)

    h = _rms(x_ref[...], nrm_ref[...]).astype(BF16)
    z_ref[...] = _dot(h, win_ref[...])

    row = lax.broadcasted_iota(jnp.int32, (CHUNK, CHUNK), 0)
    col = lax.broadcasted_iota(jnp.int32, (CHUNK, CHUNK), 1)
    tril_b = (col <= row).astype(BF16)
    sub_row = row & (SUB - 1)
    head_lane = lax.broadcasted_iota(jnp.int32, (1, B_QK_WIDTH), 1) // B_KEY_DIM
    bd_mask = (lax.broadcasted_iota(jnp.int32, (B_V_WIDTH, B_QK_WIDTH), 0) // B_VAL_DIM
               == lax.broadcasted_iota(jnp.int32, (B_V_WIDTH, B_QK_WIDTH), 1) // B_KEY_DIM)

    def chunk_body(c, carry):
        rows = pl.ds(pl.multiple_of(c * CHUNK, CHUNK), CHUNK)

        u = _gelu_tanh(z_ref[rows, E_U:E_V])
        v = _gelu_tanh(z_ref[rows, E_V:E_Q])
        mu = jnp.mean(v, axis=-1, keepdims=True)
        vc = v - mu
        var = jnp.mean(vc * vc, axis=-1, keepdims=True)
        vln = (vc * lax.rsqrt(var + EPS) * lng_ref[...] + lnb_ref[...]).astype(BF16)
        for g in range(A_GROUPS):
            sl = slice(g * LANES, (g + 1) * LANES)
            mixed = _dot(ws_ref[g], vln[:, sl]) + bs_ref[g]
            mix_ref[rows, sl] = (u[:, sl] * mixed).astype(BF16)

        q = z_ref[rows, E_Q:E_K] * (B_KEY_DIM ** -0.5)
        k = z_ref[rows, E_K:E_VB]
        vb = z_ref[rows, E_VB:E_OG]
        og = z_ref[rows, E_OG:E_G]
        glr = z_ref[rows, E_G:E_END].astype(BF16)
        logit = _dot(glr, gup_ref[...]) + gb_ref[...]
        log_a = _log_sigmoid(logit) * (1.0 / B_GATE_NORMALIZER)
        g_cum = _cumsum_rows(tril_b, log_a)
        g_last = g_cum[CHUNK - 1:CHUNK, :]
        st = st_ref[...]
        o = _dot_nt((q * jnp.exp(g_cum)).astype(BF16), st.astype(BF16))

        p_rows = [[None] * N_SUB for _ in range(B_HEADS)]
        for s in range(N_SUB):
            gs = g_cum[s * SUB:(s + 1) * SUB, :]
            if s == 0:
                qt = q[0:SUB, :] * jnp.exp(gs)
                kt = k * jnp.exp(jnp.minimum(-g_cum, MAX_DECAY_EXP))
            else:
                ref_g = g_cum[s * SUB - 1:s * SUB, :]
                qt = q[s * SUB:(s + 1) * SUB, :] * jnp.exp(gs - ref_g)
                kt = k * jnp.exp(jnp.minimum(ref_g - g_cum, MAX_DECAY_EXP))
            qs = jnp.concatenate([jnp.where(head_lane == hh, qt, 0.0) for hh in range(B_HEADS)],
                                 axis=0).astype(BF16)
            sc = _dot_nt(qs, kt.astype(BF16))
            sc = jnp.where(col <= (s * SUB + sub_row), sc, 0.0)
            for hh in range(B_HEADS):
                p_rows[hh][s] = sc[hh * SUB:(hh + 1) * SUB, :]

        vb_b = vb.astype(BF16)
        for hh in range(B_HEADS):
            sl = slice(hh * B_VAL_DIM, (hh + 1) * B_VAL_DIM)
            ph = jnp.concatenate(p_rows[hh], axis=0).astype(BF16)
            oh = o[:, sl] + _dot(ph, vb_b[:, sl])
            on = _rms(oh, hg_ref[hh])
            ogh = og[:, sl]
            mix_ref[rows, A_WIDTH + hh * B_VAL_DIM:A_WIDTH + (hh + 1) * B_VAL_DIM] = (
                on * (ogh * _sigmoid(ogh))).astype(BF16)

        k_dec = (k * jnp.exp(g_last - g_cum)).astype(BF16)
        upd = _dot(vb.T.astype(BF16), k_dec)
        st_ref[...] = jnp.exp(g_last) * st + jnp.where(bd_mask, upd, 0.0)
        return carry

    lax.fori_loop(0, n_chunks, chunk_body, 0, unroll=True)
    o_ref[...] = x_ref[...] + _dot(mix_ref[...], wo_ref[...])


def _even_mixer(x, nrm, win, gup, gb, ws, bs, lng, lnb, hg, wo, *, seq, tm):
    t = x.shape[0]
    kern = functools.partial(_even_mixer_kernel, tiles_per_batch=seq // tm, n_chunks=tm // CHUNK)
    return pl.pallas_call(
        kern,
        out_shape=jax.ShapeDtypeStruct((t, D_MODEL), F32),
        grid=(t // tm,),
        in_specs=[
            pl.BlockSpec((tm, D_MODEL), lambda i: (i, 0)),
            _const_spec(nrm.shape), _const_spec(win.shape), _const_spec(gup.shape), _const_spec(gb.shape),
            _const_spec(ws.shape), _const_spec(bs.shape), _const_spec(lng.shape), _const_spec(lnb.shape),
            _const_spec(hg.shape), _const_spec(wo.shape),
        ],
        out_specs=pl.BlockSpec((tm, D_MODEL), lambda i: (i, 0)),
        scratch_shapes=[
            pltpu.VMEM((tm, E_END), F32),
            pltpu.VMEM((tm, D_MODEL), BF16),
            pltpu.VMEM((B_V_WIDTH, B_QK_WIDTH), F32),
        ],
        compiler_params=pltpu.CompilerParams(dimension_semantics=("arbitrary",), vmem_limit_bytes=VMEM_LIMIT),
        name="even_mixer",
    )(x, nrm, win, gup, gb, ws, bs, lng, lnb, hg, wo)


def _dense_ffn_kernel(x_ref, nrm_ref, w1_ref, w3_ref, w2_ref, o_ref):
    x = x_ref[...]
    h = _rms(x, nrm_ref[...]).astype(BF16)
    a = _dot(h, w1_ref[...])
    b = _dot(h, w3_ref[...])
    o_ref[...] = x + _dot((a * _sigmoid(a) * b).astype(BF16), w2_ref[...])


def _resident_spec(shape):
    nd = len(shape)
    return pl.BlockSpec(shape, lambda *_: (0,) * nd, pipeline_mode=pl.Buffered(1))


def _dense_ffn(x, nrm, w1, w3, w2, *, tm):
    t = x.shape[0]
    row_spec = pl.BlockSpec((tm, D_MODEL), lambda i: (i, 0))
    return pl.pallas_call(
        _dense_ffn_kernel,
        out_shape=jax.ShapeDtypeStruct((t, D_MODEL), F32),
        grid=(t // tm,),
        in_specs=[row_spec, _const_spec(nrm.shape), _resident_spec(w1.shape), _resident_spec(w3.shape),
                  _resident_spec(w2.shape)],
        out_specs=row_spec,
        compiler_params=pltpu.CompilerParams(dimension_semantics=("arbitrary",), vmem_limit_bytes=VMEM_LIMIT),
        name="dense_ffn",
    )(x, nrm, w1, w3, w2)


def _head_rms(x, gain):
    lo = lax.broadcasted_iota(jnp.int32, (1, LANES), 1) < C_HEAD_DIM
    outs = []
    for t in range(C_WIDTH // LANES):
        xt = x[:, t * LANES:(t + 1) * LANES]
        sq = xt * xt
        s_lo = jnp.sum(jnp.where(lo, sq, 0.0), axis=-1, keepdims=True)
        s_hi = jnp.sum(jnp.where(lo, 0.0, sq), axis=-1, keepdims=True)
        inv = jnp.where(lo, lax.rsqrt(s_lo * (1.0 / C_HEAD_DIM) + EPS), lax.rsqrt(s_hi * (1.0 / C_HEAD_DIM) + EPS))
        outs.append(xt * inv)
    return jnp.concatenate(outs, axis=-1) * gain


N_PAIRS = C_HEADS // 2
K_EXT = 2 * LANES
BIAS_PARTS = 3


def _bias_placement():
    src = jnp.arange(BIAS_PARTS * LANES)
    part, head = src // LANES, src % LANES
    dst = (head // 2) * LANES + BIAS_PARTS * (head % 2) + part
    hit = (dst[:, None] == jnp.arange(N_PAIRS * LANES)[None, :]) & (head < C_HEADS)[:, None]
    return hit.astype(BF16)


def _odd_inproj_kernel(x_ref, nrm_ref, w_ref, fb_ref, qg_ref, kg_ref, place_ref,
                       q_ref, k_ref, v_ref, gate_ref, z_ref, c_ref, carry_ref, *, tiles_per_batch, n_chunks):
    i = pl.program_id(0)

    @pl.when(i % tiles_per_batch == 0)
    def _():
        carry_ref[...] = jnp.zeros_like(carry_ref)

    h = _rms(x_ref[...], nrm_ref[...]).astype(BF16)
    z_ref[...] = _dot(h, w_ref[...])
    q_ref[...] = (_head_rms(z_ref[:, O_Q:O_K], qg_ref[...]) * (C_HEAD_DIM ** -0.5 * LOG2E)).astype(BF16)
    kn = _head_rms(z_ref[:, O_K:O_V], kg_ref[...]).astype(BF16)
    v_ref[...] = z_ref[:, O_V:O_OG].astype(BF16)
    gate_ref[...] = _sigmoid(z_ref[:, O_OG:O_F]).astype(BF16)

    row = lax.broadcasted_iota(jnp.int32, (CHUNK, CHUNK), 0)
    col = lax.broadcasted_iota(jnp.int32, (CHUNK, CHUNK), 1)
    tril_b = (col <= row).astype(BF16)
    carry = carry_ref[...]
    for c in range(n_chunks):
        rows = slice(c * CHUNK, (c + 1) * CHUNK)
        log_f = _log_sigmoid(z_ref[rows, O_F:O_END] + fb_ref[...])
        cs = _cumsum_rows(tril_b, log_f) + carry
        c_ref[rows, :] = cs
        carry = cs[CHUNK - 1:CHUNK, :]
    carry_ref[...] = carry

    bias = _dot(jnp.concatenate(_split3(c_ref[...] * LOG2E), axis=1), place_ref[...]).astype(BF16)
    for p in range(N_PAIRS):
        k_ref[:, p * K_EXT:p * K_EXT + LANES] = kn[:, p * LANES:(p + 1) * LANES]
        k_ref[:, p * K_EXT + LANES:(p + 1) * K_EXT] = bias[:, p * LANES:(p + 1) * LANES]


def _odd_inproj(x, nrm, w, fb, qg, kg, *, seq, tm):
    t = x.shape[0]
    kern = functools.partial(_odd_inproj_kernel, tiles_per_batch=seq // tm, n_chunks=tm // CHUNK)
    row_spec = pl.BlockSpec((tm, C_WIDTH), lambda i: (i, 0))
    kext_spec = pl.BlockSpec((tm, N_PAIRS * K_EXT), lambda i: (i, 0))
    place = _bias_placement()
    wide = jax.ShapeDtypeStruct((t, C_WIDTH), BF16)
    return pl.pallas_call(
        kern,
        out_shape=[wide, jax.ShapeDtypeStruct((t, N_PAIRS * K_EXT), BF16), wide, wide],
        grid=(t // tm,),
        in_specs=[row_spec, _const_spec(nrm.shape), _const_spec(w.shape), _const_spec(fb.shape),
                  _const_spec(qg.shape), _const_spec(kg.shape), _const_spec(place.shape)],
        out_specs=[row_spec, kext_spec, row_spec, row_spec],
        scratch_shapes=[pltpu.VMEM((tm, O_END), F32), pltpu.VMEM((tm, LANES), F32), pltpu.VMEM((1, LANES), F32)],
        compiler_params=pltpu.CompilerParams(dimension_semantics=("arbitrary",), vmem_limit_bytes=VMEM_LIMIT),
        name="odd_inproj",
    )(x, nrm, w, fb, qg, kg, place)


NEG_BIG = -1e30


V_ROWS = 80


def _fox_kernel(q_ref, k_ref, vt_ref, gate_ref, o_ref, s_ref, *, tq, tk):
    qi = pl.program_id(2)
    ng = tq // tk
    key_i = lax.broadcasted_iota(jnp.int32, (tk, tk), 0)
    qry_i = lax.broadcasted_iota(jnp.int32, (tk, tk), 1)
    causal = key_i <= qry_i
    chains = [(hh, r) for r in range(ng) for hh in range(2)]
    feat = lax.broadcasted_iota(jnp.int32, (LANES, 1), 0)
    qms = []
    for hh, r in chains:
        q_t = q_ref[r * tk:(r + 1) * tk, :].astype(F32).T
        own = (feat < C_HEAD_DIM) if hh == 0 else (feat >= C_HEAD_DIM)
        qh = jnp.where(own, q_t, 0.0).astype(BF16)
        pick = (feat >= BIAS_PARTS * hh) & (feat < BIAS_PARTS * (hh + 1))
        minus_one = jnp.broadcast_to(jnp.where(pick, -1.0, 0.0).astype(BF16), qh.shape)
        qms.append(jnp.concatenate([qh, minus_one], axis=0))

    def key_rows(j):
        return pl.ds(pl.multiple_of(j * tk, tk), tk)

    def scores_to_scratch(j, slot, live, modes):
        kb = k_ref[key_rows(j), :]
        raw = [_dot(kb, qms[idx]) for idx in live]
        maxes = []
        for s, idx in zip(raw, live):
            hh, r = chains[idx]
            if modes[r] == "diag":
                s = jnp.where(causal, s, NEG_BIG)
            s_ref[slot, idx] = s
            maxes.append(jnp.max(s, axis=0, keepdims=True))
        return maxes

    ones_rows = (lax.broadcasted_iota(jnp.int32, (V_ROWS - C_HEAD_DIM, tk), 0) == 0).astype(BF16)

    def softmax_pv(j, slot, maxes, live, state):
        vt_pair = vt_ref[j]
        vtb = [jnp.concatenate([vt_pair[hh * C_HEAD_DIM:(hh + 1) * C_HEAD_DIM, :], ones_rows], axis=0)
               for hh in range(2)]
        new = list(state)
        probs = []
        for bm, idx in zip(maxes, live):
            m = state[2 * idx]
            m_new = jnp.maximum(m, bm)
            p = jnp.exp2(s_ref[slot, idx] - m_new)
            new[2 * idx] = m_new
            probs.append((jnp.exp2(m - m_new), p.astype(BF16)))
        for (alpha, p), idx in zip(probs, live):
            hh, r = chains[idx]
            new[2 * idx + 1] = alpha * state[2 * idx + 1] + _dot(vtb[hh], p)
        return new

    assert ng % 2 == 0
    all_chains = list(range(len(chains)))
    n_state = 2 * len(chains)
    state = []
    for _ in chains:
        state += [jnp.full((1, tk), NEG_BIG, F32), jnp.zeros((V_ROWS, tk), F32)]
    n_full = qi * ng
    full_modes = ("full",) * ng

    def diag_modes(g):
        return tuple("skip" if r < g else ("diag" if r == g else "full") for r in range(ng))

    def live_chains(g):
        return [idx for idx, (hh, r) in enumerate(chains) if r >= g]

    def two_blocks(j, carry, next_modes):
        st, mx0 = list(carry[:n_state]), carry[n_state:]
        mx1 = scores_to_scratch(j + 1, 1, all_chains, full_modes)
        st = softmax_pv(j, 0, mx0, all_chains, st)
        mx0 = scores_to_scratch(j + 2, 0, all_chains, next_modes)
        st = softmax_pv(j + 1, 1, mx1, all_chains, st)
        return tuple(st) + tuple(mx0)

    def with_full_blocks(_):
        first = scores_to_scratch(0, 0, all_chains, full_modes)
        carry = lax.fori_loop(0, n_full // 2 - 1, lambda i, c: two_blocks(2 * i, c, full_modes),
                              tuple(state) + tuple(first))
        return two_blocks(n_full - 2, carry, diag_modes(0))

    def no_full_blocks(_):
        return tuple(state) + tuple(scores_to_scratch(0, 0, all_chains, diag_modes(0)))

    carry = lax.cond(qi > 0, with_full_blocks, no_full_blocks, 0)
    state, mx = list(carry[:n_state]), carry[n_state:]
    for g in range(ng):
        if g + 1 < ng:
            mx_next = scores_to_scratch(n_full + g + 1, (g + 1) % 2, live_chains(g + 1), diag_modes(g + 1))
        state = softmax_pv(n_full + g, g % 2, mx, live_chains(g), state)
        if g + 1 < ng:
            mx = mx_next

    for r in range(ng):
        parts = []
        for hh in range(2):
            acc = state[2 * chains.index((hh, r)) + 1]
            parts.append(acc[:C_HEAD_DIM, :] / acc[C_HEAD_DIM:C_HEAD_DIM + 1, :])
        o = jnp.concatenate(parts, axis=0).T
        rows = slice(r * tk, (r + 1) * tk)
        o_ref[rows, :] = (o * gate_ref[rows, :].astype(F32)).astype(BF16)


def _fox_attention(q, k_ext, vt, gate, *, batch, seq, tq, tk):
    t = q.shape[0]
    nq = seq // tq
    kern = functools.partial(_fox_kernel, tq=tq, tk=tk)
    return pl.pallas_call(
        kern,
        out_shape=jax.ShapeDtypeStruct((t, C_WIDTH), BF16),
        grid=(batch, N_PAIRS, nq),
        in_specs=[
            pl.BlockSpec((tq, LANES), lambda b, p, i: (b * nq + i, p)),
            pl.BlockSpec((seq, K_EXT), lambda b, p, i: (b, p)),
            pl.BlockSpec((None, None, seq // tk, LANES, tk), lambda b, p, i: (b, p, 0, 0, 0)),
            pl.BlockSpec((tq, LANES), lambda b, p, i: (b * nq + i, p)),
        ],
        out_specs=pl.BlockSpec((tq, LANES), lambda b, p, i: (b * nq + i, p)),
        scratch_shapes=[pltpu.VMEM((2, 2 * (tq // tk), tk, tk), F32)],
        compiler_params=pltpu.CompilerParams(dimension_semantics=("arbitrary", "arbitrary", "arbitrary"),
                                             vmem_limit_bytes=VMEM_LIMIT),
        name="fox_attn",
    )(q, k_ext, vt, gate)


def _store_token_tiles(dst_ref, val, n_rows):
    for s in range(ROW_TILE):
        dst_ref[pl.ds(s, n_rows, stride=ROW_TILE), :] = val[:, s * LANES:(s + 1) * LANES]


def _load_token_tiles(src_ref, n_rows):
    return jnp.concatenate([src_ref[pl.ds(s, n_rows, stride=ROW_TILE), :] for s in range(ROW_TILE)], axis=1)


def _odd_out_kernel(x_ref, a_ref, wo_ref, nrm_ref, r_ref, x3_ref, h_ref, route_ref):
    x3 = x_ref[...] + _dot(a_ref[...], wo_ref[...])
    x3_ref[...] = x3
    h = _rms(x3, nrm_ref[...])
    _store_token_tiles(h_ref, h, h.shape[0])

    h_hi = h.astype(BF16)
    h_lo = (h - h_hi.astype(F32)).astype(BF16)
    r = r_ref[...]
    r_hi = r.astype(BF16)
    r_lo = (r - r_hi.astype(F32)).astype(BF16)
    logits = _dot(h_hi, r_hi) + (_dot(h_lo, r_hi) + _dot(h_hi, r_lo))

    lane = lax.broadcasted_iota(jnp.int32, logits.shape, 1).astype(F32)
    neg_inf = jnp.float32(-jnp.inf)
    lg = jnp.where(lane < N_EXPERTS, logits, neg_inf)
    m1 = jnp.max(lg, axis=-1, keepdims=True)
    i1 = jnp.min(jnp.where(lg == m1, lane, float(LANES)), axis=-1, keepdims=True)
    lg2 = jnp.where(lane == i1, neg_inf, lg)
    m2 = jnp.max(lg2, axis=-1, keepdims=True)
    i2 = jnp.min(jnp.where(lg2 == m2, lane, float(LANES)), axis=-1, keepdims=True)
    e2 = jnp.exp(m2 - m1)
    g1 = 1.0 / (1.0 + e2)
    g2 = e2 / (1.0 + e2)
    route_ref[...] = jnp.where(lane == 0, i1, jnp.where(lane == 1, i2, jnp.where(lane == 2, g1,
                               jnp.where(lane == 3, g2, 0.0))))


def _odd_out(x, a, wo, nrm, router, *, tm):
    t = x.shape[0]
    row_spec = pl.BlockSpec((tm, D_MODEL), lambda i: (i, 0))
    return pl.pallas_call(
        _odd_out_kernel,
        out_shape=[jax.ShapeDtypeStruct((t, D_MODEL), F32), jax.ShapeDtypeStruct((t * ROW_TILE, LANES), F32),
                   jax.ShapeDtypeStruct((t, LANES), F32)],
        grid=(t // tm,),
        in_specs=[row_spec, row_spec, _const_spec(wo.shape), _const_spec(nrm.shape), _const_spec(router.shape)],
        out_specs=[row_spec, pl.BlockSpec((tm * ROW_TILE, LANES), lambda i: (i, 0)),
                   pl.BlockSpec((tm, LANES), lambda i: (i, 0))],
        compiler_params=pltpu.CompilerParams(dimension_semantics=("arbitrary",), vmem_limit_bytes=VMEM_LIMIT),
        name="odd_out",
    )(x, a, wo, nrm, router)


def _row_gather_copy(src_hbm, src_row, dst_ref, dst_row, sem):
    return pltpu.make_async_copy(
        src_hbm.at[pl.ds(pl.multiple_of(src_row * ROW_TILE, ROW_TILE), ROW_TILE), :],
        dst_ref.at[pl.ds(pl.multiple_of(dst_row * ROW_TILE, ROW_TILE), ROW_TILE), :],
        sem)


GATHER_UNROLL = 8


def _start_row_gathers(src_hbm, idx_ref, idx_row, first, count, dst_ref, sem, *, inline):
    if inline:
        for u in range(count):
            _row_gather_copy(src_hbm, idx_ref[idx_row, first + u], dst_ref, first + u, sem).start(priority=u % 2)
        return

    def issue(r2, c):
        for u in range(2):
            r = 2 * r2 + u
            _row_gather_copy(src_hbm, idx_ref[idx_row, r], dst_ref, r, sem).start(priority=u)
        return c

    assert first % 2 == 0 and count % 2 == 0
    lax.fori_loop(first // 2, (first + count) // 2, issue, 0, unroll=GATHER_UNROLL // 2)


def _wait_row_gathers(src_hbm, dst_ref, sem):
    pltpu.make_async_copy(src_hbm.at[pl.ds(0, dst_ref.shape[0]), :], dst_ref, sem).wait()


def _moe_ffn_kernel(te_ref, nu_ref, tok_ref, tok_next_ref, h_hbm, w1_ref, w3_ref, w2_ref, o_ref, xs_ref, buf_ref,
                    sem, *, tm, n_chunks):
    i = pl.program_id(0)
    n_used = nu_ref[0]
    used = i < n_used
    slot = i % 2

    @pl.when(i == 0)
    def _():
        _start_row_gathers(h_hbm, tok_ref, 0, 0, tm, buf_ref.at[0], sem.at[0], inline=False)

    @pl.when(i <= n_used)
    def _():
        _wait_row_gathers(h_hbm, buf_ref.at[slot], sem.at[slot])
        xs_ref[...] = _load_token_tiles(buf_ref.at[slot], tm).astype(BF16)

    @pl.when(used)
    def _():
        n_groups = 3 * n_chunks
        per_group = tm // n_groups
        starts = [(g * per_group, per_group if g + 1 < n_groups else tm - g * per_group) for g in range(n_groups)]

        def prefetch(g):
            first, count = starts[g]
            _start_row_gathers(h_hbm, tok_next_ref, 0, first, count, buf_ref.at[1 - slot], sem.at[1 - slot],
                               inline=True)

        x = xs_ref[...]
        tf = D_FF_EXPERT // n_chunks
        y = None
        for c in range(n_chunks):
            cols = slice(c * tf, (c + 1) * tf)
            prefetch(3 * c)
            a = _dot(x, w1_ref[:, cols])
            prefetch(3 * c + 1)
            b = _dot(x, w3_ref[:, cols])
            prefetch(3 * c + 2)
            part = _dot((a * _sigmoid(a) * b).astype(BF16), w2_ref[cols, :])
            y = part if y is None else y + part
        _store_token_tiles(o_ref, y, tm)

    @pl.when(jnp.logical_not(used))
    def _():
        o_ref[...] = jnp.zeros_like(o_ref)


def _moe_ffn(tile_expert, n_used, row_tok, h_tiles, w1, w3, w2, *, tm, n_chunks):
    n_tiles = row_tok.shape[0] - 1

    def expert_spec(shape):
        return pl.BlockSpec((None,) + shape, lambda i, te, nu: (te[i], 0, 0), pipeline_mode=pl.Buffered(1))

    grid_spec = pltpu.PrefetchScalarGridSpec(
        num_scalar_prefetch=2,
        grid=(n_tiles,),
        in_specs=[
            pl.BlockSpec((None, 1, tm), lambda i, te, nu: (i, 0, 0), memory_space=pltpu.SMEM),
            pl.BlockSpec((None, 1, tm), lambda i, te, nu: (i + 1, 0, 0), memory_space=pltpu.SMEM),
            pl.BlockSpec(memory_space=pl.ANY),
            expert_spec((D_MODEL, D_FF_EXPERT)), expert_spec((D_MODEL, D_FF_EXPERT)),
            expert_spec((D_FF_EXPERT, D_MODEL)),
        ],
        out_specs=pl.BlockSpec((tm * ROW_TILE, LANES), lambda i, te, nu: (i, 0)),
        scratch_shapes=[pltpu.VMEM((tm, D_MODEL), BF16), pltpu.VMEM((2, tm * ROW_TILE, LANES), F32),
                        pltpu.SemaphoreType.DMA((2,))],
    )
    return pl.pallas_call(
        functools.partial(_moe_ffn_kernel, tm=tm, n_chunks=n_chunks),
        out_shape=jax.ShapeDtypeStruct((n_tiles * tm * ROW_TILE, LANES), F32),
        grid_spec=grid_spec,
        compiler_params=pltpu.CompilerParams(dimension_semantics=("arbitrary",), vmem_limit_bytes=VMEM_LIMIT),
        name="moe_ffn",
    )(tile_expert, n_used, row_tok, row_tok, h_tiles, w1, w3, w2)


def _moe_combine_kernel(pos_ref, pos_next_ref, x_ref, route_ref, nrm_ref, y_hbm, o_ref, buf_ref, sem, *, tm):
    i = pl.program_id(0)
    slot = i % 2

    def start_tile(idx_ref, s):
        for k in range(2):
            _start_row_gathers(y_hbm, idx_ref, k, 0, tm, buf_ref.at[s, k], sem.at[s, k], inline=False)

    @pl.when(i == 0)
    def _():
        start_tile(pos_ref, 0)

    @pl.when(i + 1 < pl.num_programs(0))
    def _():
        start_tile(pos_next_ref, 1 - slot)

    for k in range(2):
        _wait_row_gathers(y_hbm, buf_ref.at[slot, k], sem.at[slot, k])
    g1 = route_ref[:, 2:3]
    g2 = route_ref[:, 3:4]
    x = x_ref[...] + (g1 * _load_token_tiles(buf_ref.at[slot, 0], tm) + g2 * _load_token_tiles(buf_ref.at[slot, 1], tm))
    o_ref[...] = _rms(x, nrm_ref[...])


def _moe_combine(pos, x, route, nrm, y_tiles, *, tm):
    t = x.shape[0]
    n = t // tm
    row_spec = pl.BlockSpec((tm, D_MODEL), lambda i: (i, 0))
    return pl.pallas_call(
        functools.partial(_moe_combine_kernel, tm=tm),
        out_shape=jax.ShapeDtypeStruct((t, D_MODEL), F32),
        grid=(n,),
        in_specs=[pl.BlockSpec((None, 2, tm), lambda i: (i, 0, 0), memory_space=pltpu.SMEM),
                  pl.BlockSpec((None, 2, tm), lambda i: (jnp.minimum(i + 1, n - 1), 0, 0), memory_space=pltpu.SMEM),
                  row_spec, pl.BlockSpec((tm, LANES), lambda i: (i, 0)), _const_spec(nrm.shape),
                  pl.BlockSpec(memory_space=pl.ANY)],
        out_specs=row_spec,
        scratch_shapes=[pltpu.VMEM((2, 2, tm * ROW_TILE, LANES), F32), pltpu.SemaphoreType.DMA((2, 2))],
        compiler_params=pltpu.CompilerParams(dimension_semantics=("arbitrary",), vmem_limit_bytes=VMEM_LIMIT),
        name="moe_combine",
    )(pos, pos, x, route, nrm, y_tiles)


def _routing_tables(idx1, idx2, *, tm, n_tiles):
    t = idx1.shape[0]
    e_flat = jnp.concatenate([idx1, idx2])
    onehot = (e_flat[:, None] == jnp.arange(N_EXPERTS, dtype=jnp.int32)[None, :]).astype(jnp.int32)
    csum = jnp.cumsum(onehot, axis=0)
    rank = jnp.sum((csum - onehot) * onehot, axis=1)
    counts = csum[-1]
    tiles_e = (counts + tm - 1) // tm
    tile_end = jnp.cumsum(tiles_e)
    tile_start = tile_end - tiles_e
    pos = jnp.sum(onehot * tile_start[None, :], axis=1) * tm + rank
    n_used = tile_end[-1]
    tile_ids = jnp.arange(n_tiles, dtype=jnp.int32)
    te = jnp.sum((tile_ids[:, None] >= tile_end[None, :]).astype(jnp.int32), axis=1)
    te_last = jnp.sum((n_used - 1 >= tile_end).astype(jnp.int32))
    tile_expert = jnp.where(tile_ids < n_used, te, te_last).astype(jnp.int32)
    tok = jnp.concatenate([jnp.arange(t, dtype=jnp.int32)] * 2)
    row_tok = jnp.zeros((n_tiles * tm,), jnp.int32).at[pos].set(tok, unique_indices=True)
    return pos[:t], pos[t:], row_tok, tile_expert, n_used.reshape(1).astype(jnp.int32)


def _pad_cols(w, n):
    return jnp.pad(w, ((0, 0), (0, n - w.shape[1])))


def kernel(x, even_norm_mix, even_w_in, even_gate_up, even_gate_bias, even_w_s, even_b_s, even_ln_g, even_ln_b,
           even_head_g, even_w_o, even_norm_ffn, even_ffn_w1, even_ffn_w3, even_ffn_w2, odd_norm_mix, odd_w_in,
           odd_forget_bias, odd_q_g, odd_k_g, odd_w_o, odd_norm_ffn, odd_router, odd_exp_w1, odd_exp_w3,
           odd_exp_w2, final_norm):
    batch, seq, d = x.shape
    t = batch * seq
    xt = x.reshape(t, d)
    tm = min(512, seq)

    w_in = even_w_in[0]
    u_w, v_w, q_w, k_w, g_w, vb_w, og_w = jnp.split(w_in, [512, 1024, 1280, 1536, 1552, 2064], axis=1)
    win_e = jnp.concatenate([u_w, v_w, q_w, k_w, vb_w, og_w, _pad_cols(g_w, LANES)], axis=1).astype(BF16)
    gup = jnp.pad(even_gate_up[0], ((0, LANES - B_GATE_RANK), (0, 0))).astype(BF16)
    gb = even_gate_bias[0].reshape(1, B_QK_WIDTH)
    tril = jnp.tril(jnp.ones((CHUNK, CHUNK), dtype=bool))
    ws = jnp.where(tril[None], even_w_s[0], 0.0).astype(BF16)
    bs = jnp.broadcast_to(even_b_s[0][:, :, None], (A_GROUPS, CHUNK, LANES))
    lng = even_ln_g[0].reshape(1, A_WIDTH)
    lnb = even_ln_b[0].reshape(1, A_WIDTH)
    hg = even_head_g[0].reshape(B_HEADS, 1, B_VAL_DIM)
    x1 = _even_mixer(xt, even_norm_mix[0].reshape(1, d), win_e, gup, gb, ws, bs, lng, lnb, hg,
                     even_w_o[0].astype(BF16), seq=seq, tm=tm)
    x2 = _dense_ffn(x1, even_norm_ffn[0].reshape(1, d), even_ffn_w1[0].astype(BF16), even_ffn_w3[0].astype(BF16),
                    even_ffn_w2[0].astype(BF16), tm=tm)

    q_w, k_w, v_w, og_w, f_w = jnp.split(odd_w_in[0], [1024, 2048, 3072, 4096], axis=1)
    win_o = jnp.concatenate([q_w, k_w, v_w, og_w, _pad_cols(f_w, LANES)], axis=1).astype(BF16)
    fb = jnp.pad(odd_forget_bias[0], (0, LANES - C_HEADS)).reshape(1, LANES)
    qg = jnp.tile(odd_q_g[0], C_HEADS).reshape(1, C_WIDTH)
    kg = jnp.tile(odd_k_g[0], C_HEADS).reshape(1, C_WIDTH)
    q, k_ext, v, gate = _odd_inproj(x2, odd_norm_mix[0].reshape(1, d), win_o, fb, qg, kg, seq=seq, tm=tm)
    tk = min(256, seq)
    tq = min(1024, seq)
    vt = jnp.transpose(v.reshape(batch, seq // tk, tk, N_PAIRS, LANES), (0, 3, 1, 4, 2))
    attn = _fox_attention(q, k_ext, vt, gate, batch=batch, seq=seq, tq=tq, tk=tk)

    router = _pad_cols(odd_router[0], LANES)
    x3, h_tiles, route = _odd_out(x2, attn, odd_w_o[0].astype(BF16), odd_norm_ffn[0].reshape(1, d), router, tm=tm)

    tm_moe = 512
    tm_comb = min(256, seq)
    n_tiles = (2 * t) // tm_moe + N_EXPERTS + 1
    idx1 = route[:, 0].astype(jnp.int32)
    idx2 = route[:, 1].astype(jnp.int32)
    pos1, pos2, row_tok, tile_expert, n_used = _routing_tables(idx1, idx2, tm=tm_moe, n_tiles=n_tiles + 1)
    y_tiles = _moe_ffn(tile_expert[:n_tiles], n_used, row_tok.reshape(n_tiles + 1, 1, tm_moe), h_tiles,
                       odd_exp_w1[0].astype(BF16), odd_exp_w3[0].astype(BF16), odd_exp_w2[0].astype(BF16),
                       tm=tm_moe, n_chunks=2)
    pos = jnp.stack([pos1.reshape(t // tm_comb, tm_comb), pos2.reshape(t // tm_comb, tm_comb)], axis=1)
    out = _moe_combine(pos, x3, route, final_norm.reshape(1, d), y_tiles, tm=tm_comb)
    return out.reshape(batch, seq, d)
```

```python
import functools
import math

import jax
import jax.numpy as jnp
from jax import lax
from jax.experimental import pallas as pl
from jax.experimental.pallas import tpu as pltpu

F32 = jnp.float32
BF16 = jnp.bfloat16
HIGHEST = lax.Precision.HIGHEST

EPS = 1e-6
D_MODEL = 1024
CHUNK = 128
SUB = 32
N_SUB = CHUNK // SUB
A_GROUPS = 4
A_WIDTH = 512
B_HEADS = 4
B_KEY_DIM = 64
B_VAL_DIM = 128
B_QK_WIDTH = 256
B_V_WIDTH = 512
B_GATE_RANK = 16
B_GATE_NORMALIZER = 16.0
C_HEADS = 16
C_HEAD_DIM = 64
C_WIDTH = 1024
D_FF_DENSE = 2816
N_EXPERTS = 8
D_FF_EXPERT = 3584
LANES = 128
MAX_DECAY_EXP = 60.0
LOG2E = math.log2(math.e)
ROW_TILE = 8

E_U, E_V, E_Q, E_K, E_VB, E_OG, E_G, E_END = 0, 512, 1024, 1280, 1536, 2048, 2560, 2688
O_Q, O_K, O_V, O_OG, O_F, O_END = 0, 1024, 2048, 3072, 4096, 4224

VMEM_LIMIT = 56 * 1024 * 1024


def _rms(x, g):
    ms = jnp.mean(x * x, axis=-1, keepdims=True)
    return x * lax.rsqrt(ms + EPS) * g


def _gelu_tanh(x):
    c = math.sqrt(2.0 / math.pi)
    return x * (0.5 * (1.0 + jnp.tanh(c * (x + 0.044715 * (x * x * x)))))


def _sigmoid(x):
    return 1.0 / (1.0 + jnp.exp(-x))


def _log_sigmoid(x):
    return jnp.minimum(x, 0.0) - jnp.log(1.0 + jnp.exp(-jnp.abs(x)))


def _dot(a, b):
    return jnp.dot(a, b, preferred_element_type=F32)


def _dot_nt(a, b):
    return lax.dot_general(a, b, (((1,), (1,)), ((), ())), preferred_element_type=F32)


def _split3(x):
    hi = x.astype(BF16)
    r1 = x - hi.astype(F32)
    mid = r1.astype(BF16)
    lo = (r1 - mid.astype(F32)).astype(BF16)
    return hi, mid, lo


def _cumsum_rows(tril_b, x):
    hi, mid, lo = _split3(x)
    return _dot(tril_b, hi) + _dot(tril_b, mid) + _dot(tril_b, lo)


def _const_spec(shape):
    nd = len(shape)
    return pl.BlockSpec(shape, lambda *_: (0,) * nd)


def _side_cast_specs(w2d, n_steps, index_map):
    rows = w2d.shape[0] // n_steps
    assert rows * n_steps == w2d.shape[0] and rows % 16 == 0
    spec = pl.BlockSpec((rows, w2d.shape[1]), index_map)
    return spec, spec, jax.ShapeDtypeStruct(w2d.shape, BF16)


def _even_mixer_kernel(x_ref, nrm_ref, win_ref, gup_ref, gb_ref, ws_ref, bs_ref, lng_ref, lnb_ref,
                       hg_ref, wo_ref, wcast_ref, o_ref, wcast_out_ref, z_ref, mix_ref, st_ref,
                       *, tiles_per_batch, n_chunks):
    i = pl.program_id(0)
    wcast_out_ref[...] = wcast_ref[...].astype(BF16)

    @pl.when(i % tiles_per_batch == 0)
    def _():
        st_ref[...] = jnp.zeros_like(st_ref)

    h = _rms(x_ref[...], nrm_ref[...]).astype(BF16)
    z_ref[...] = _dot(h, win_ref[...])

    row = lax.broadcasted_iota(jnp.int32, (CHUNK, CHUNK), 0)
    col = lax.broadcasted_iota(jnp.int32, (CHUNK, CHUNK), 1)
    tril_b = (col <= row).astype(BF16)
    sub_row = row & (SUB - 1)
    head_lane = lax.broadcasted_iota(jnp.int32, (1, B_QK_WIDTH), 1) // B_KEY_DIM
    bd_mask = (lax.broadcasted_iota(jnp.int32, (B_V_WIDTH, B_QK_WIDTH), 0) // B_VAL_DIM
               == lax.broadcasted_iota(jnp.int32, (B_V_WIDTH, B_QK_WIDTH), 1) // B_KEY_DIM)

    def chunk_body(c, carry):
        rows = pl.ds(pl.multiple_of(c * CHUNK, CHUNK), CHUNK)

        u = _gelu_tanh(z_ref[rows, E_U:E_V])
        v = _gelu_tanh(z_ref[rows, E_V:E_Q])
        mu = jnp.mean(v, axis=-1, keepdims=True)
        vc = v - mu
        var = jnp.mean(vc * vc, axis=-1, keepdims=True)
        vln = (vc * lax.rsqrt(var + EPS) * lng_ref[...] + lnb_ref[...]).astype(BF16)
        for g in range(A_GROUPS):
            sl = slice(g * LANES, (g + 1) * LANES)
            mixed = _dot(ws_ref[g], vln[:, sl]) + bs_ref[g]
            mix_ref[rows, sl] = (u[:, sl] * mixed).astype(BF16)

        q = z_ref[rows, E_Q:E_K] * (B_KEY_DIM ** -0.5)
        k = z_ref[rows, E_K:E_VB]
        vb = z_ref[rows, E_VB:E_OG]
        og = z_ref[rows, E_OG:E_G]
        glr = z_ref[rows, E_G:E_END].astype(BF16)
        logit = _dot(glr, gup_ref[...]) + gb_ref[...]
        log_a = _log_sigmoid(logit) * (1.0 / B_GATE_NORMALIZER)
        g_cum = _cumsum_rows(tril_b, log_a)
        g_last = g_cum[CHUNK - 1:CHUNK, :]
        st = st_ref[...]
        o = _dot_nt((q * jnp.exp(g_cum)).astype(BF16), st.astype(BF16))

        p_rows = [[None] * N_SUB for _ in range(B_HEADS)]
        for s in range(N_SUB):
            gs = g_cum[s * SUB:(s + 1) * SUB, :]
            if s == 0:
                qt = q[0:SUB, :] * jnp.exp(gs)
                kt = k * jnp.exp(jnp.minimum(-g_cum, MAX_DECAY_EXP))
            else:
                ref_g = g_cum[s * SUB - 1:s * SUB, :]
                qt = q[s * SUB:(s + 1) * SUB, :] * jnp.exp(gs - ref_g)
                kt = k * jnp.exp(jnp.minimum(ref_g - g_cum, MAX_DECAY_EXP))
            qs = jnp.concatenate([jnp.where(head_lane == hh, qt, 0.0) for hh in range(B_HEADS)],
                                 axis=0).astype(BF16)
            sc = _dot_nt(qs, kt.astype(BF16))
            sc = jnp.where(col <= (s * SUB + sub_row), sc, 0.0)
            for hh in range(B_HEADS):
                p_rows[hh][s] = sc[hh * SUB:(hh + 1) * SUB, :]

        vb_b = vb.astype(BF16)
        for hh in range(B_HEADS):
            sl = slice(hh * B_VAL_DIM, (hh + 1) * B_VAL_DIM)
            ph = jnp.concatenate(p_rows[hh], axis=0).astype(BF16)
            oh = o[:, sl] + _dot(ph, vb_b[:, sl])
            on = _rms(oh, hg_ref[hh])
            ogh = og[:, sl]
            mix_ref[rows, A_WIDTH + hh * B_VAL_DIM:A_WIDTH + (hh + 1) * B_VAL_DIM] = (
                on * (ogh * _sigmoid(ogh))).astype(BF16)

        k_dec = (k * jnp.exp(g_last - g_cum)).astype(BF16)
        upd = _dot(vb.T.astype(BF16), k_dec)
        st_ref[...] = jnp.exp(g_last) * st + jnp.where(bd_mask, upd, 0.0)
        return carry

    lax.fori_loop(0, n_chunks, chunk_body, 0, unroll=True)
    o_ref[...] = x_ref[...] + _dot(mix_ref[...], wo_ref[...])


def _even_mixer(x, nrm, win, gup, gb, ws, bs, lng, lnb, hg, wo, wcast, *, seq, tm):
    t = x.shape[0]
    kern = functools.partial(_even_mixer_kernel, tiles_per_batch=seq // tm, n_chunks=tm // CHUNK)
    cast_in, cast_out, cast_shape = _side_cast_specs(wcast, t // tm, lambda i: (i, 0))
    return pl.pallas_call(
        kern,
        out_shape=[jax.ShapeDtypeStruct((t, D_MODEL), F32), cast_shape],
        grid=(t // tm,),
        in_specs=[
            pl.BlockSpec((tm, D_MODEL), lambda i: (i, 0)),
            _const_spec(nrm.shape), _resident_spec(win.shape), _const_spec(gup.shape), _const_spec(gb.shape),
            _const_spec(ws.shape), _const_spec(bs.shape), _const_spec(lng.shape), _const_spec(lnb.shape),
            _const_spec(hg.shape), _resident_spec(wo.shape), cast_in,
        ],
        out_specs=[pl.BlockSpec((tm, D_MODEL), lambda i: (i, 0)), cast_out],
        scratch_shapes=[
            pltpu.VMEM((tm, E_END), F32),
            pltpu.VMEM((tm, D_MODEL), BF16),
            pltpu.VMEM((B_V_WIDTH, B_QK_WIDTH), F32),
        ],
        compiler_params=pltpu.CompilerParams(dimension_semantics=("arbitrary",), vmem_limit_bytes=VMEM_LIMIT),
        name="even_mixer",
    )(x, nrm, win, gup, gb, ws, bs, lng, lnb, hg, wo, wcast)


def _dense_ffn_kernel(x_ref, nrm_ref, w1_ref, w3_ref, w2_ref, o_ref):
    x = x_ref[...]
    h = _rms(x, nrm_ref[...]).astype(BF16)
    a = _dot(h, w1_ref[...])
    b = _dot(h, w3_ref[...])
    o_ref[...] = x + _dot((a * _sigmoid(a) * b).astype(BF16), w2_ref[...])


def _resident_spec(shape):
    nd = len(shape)
    return pl.BlockSpec(shape, lambda *_: (0,) * nd, pipeline_mode=pl.Buffered(1))


def _dense_ffn(x, nrm, w1, w3, w2, *, tm):
    t = x.shape[0]
    row_spec = pl.BlockSpec((tm, D_MODEL), lambda i: (i, 0))
    return pl.pallas_call(
        _dense_ffn_kernel,
        out_shape=jax.ShapeDtypeStruct((t, D_MODEL), F32),
        grid=(t // tm,),
        in_specs=[row_spec, _const_spec(nrm.shape), _resident_spec(w1.shape), _resident_spec(w3.shape),
                  _resident_spec(w2.shape)],
        out_specs=row_spec,
        compiler_params=pltpu.CompilerParams(dimension_semantics=("arbitrary",), vmem_limit_bytes=VMEM_LIMIT),
        name="dense_ffn",
    )(x, nrm, w1, w3, w2)


def _head_rms(x, gain):
    lo = lax.broadcasted_iota(jnp.int32, (1, LANES), 1) < C_HEAD_DIM
    outs = []
    for t in range(C_WIDTH // LANES):
        xt = x[:, t * LANES:(t + 1) * LANES]
        sq = xt * xt
        s_lo = jnp.sum(jnp.where(lo, sq, 0.0), axis=-1, keepdims=True)
        s_hi = jnp.sum(jnp.where(lo, 0.0, sq), axis=-1, keepdims=True)
        inv = jnp.where(lo, lax.rsqrt(s_lo * (1.0 / C_HEAD_DIM) + EPS), lax.rsqrt(s_hi * (1.0 / C_HEAD_DIM) + EPS))
        outs.append(xt * inv)
    return jnp.concatenate(outs, axis=-1) * gain


N_PAIRS = C_HEADS // 2
K_EXT = 2 * LANES
BIAS_PARTS = 3


def _bias_placement():
    src = jnp.arange(BIAS_PARTS * LANES)
    part, head = src // LANES, src % LANES
    dst = (head // 2) * LANES + BIAS_PARTS * (head % 2) + part
    hit = (dst[:, None] == jnp.arange(N_PAIRS * LANES)[None, :]) & (head < C_HEADS)[:, None]
    return hit.astype(BF16)


def _odd_inproj_kernel(x_ref, nrm_ref, w_ref, fb_ref, qg_ref, kg_ref, place_ref,
                       q_ref, k_ref, v_ref, gate_ref, z_ref, c_ref, carry_ref, *, tiles_per_batch, n_chunks):
    i = pl.program_id(0)

    @pl.when(i % tiles_per_batch == 0)
    def _():
        carry_ref[...] = jnp.zeros_like(carry_ref)

    h = _rms(x_ref[...], nrm_ref[...]).astype(BF16)
    z_ref[...] = _dot(h, w_ref[...])
    q_ref[...] = (_head_rms(z_ref[:, O_Q:O_K], qg_ref[...]) * (C_HEAD_DIM ** -0.5 * LOG2E)).astype(BF16)
    kn = _head_rms(z_ref[:, O_K:O_V], kg_ref[...]).astype(BF16)
    v_ref[...] = z_ref[:, O_V:O_OG].astype(BF16)
    gate_ref[...] = _sigmoid(z_ref[:, O_OG:O_F]).astype(BF16)

    row = lax.broadcasted_iota(jnp.int32, (CHUNK, CHUNK), 0)
    col = lax.broadcasted_iota(jnp.int32, (CHUNK, CHUNK), 1)
    tril_b = (col <= row).astype(BF16)
    carry = carry_ref[...]
    for c in range(n_chunks):
        rows = slice(c * CHUNK, (c + 1) * CHUNK)
        log_f = _log_sigmoid(z_ref[rows, O_F:O_END] + fb_ref[...])
        cs = _cumsum_rows(tril_b, log_f) + carry
        c_ref[rows, :] = cs
        carry = cs[CHUNK - 1:CHUNK, :]
    carry_ref[...] = carry

    bias = _dot(jnp.concatenate(_split3(c_ref[...] * LOG2E), axis=1), place_ref[...]).astype(BF16)
    for p in range(N_PAIRS):
        k_ref[:, p * K_EXT:p * K_EXT + LANES] = kn[:, p * LANES:(p + 1) * LANES]
        k_ref[:, p * K_EXT + LANES:(p + 1) * K_EXT] = bias[:, p * LANES:(p + 1) * LANES]


def _odd_inproj(x, nrm, w, fb, qg, kg, *, seq, tm):
    t = x.shape[0]
    kern = functools.partial(_odd_inproj_kernel, tiles_per_batch=seq // tm, n_chunks=tm // CHUNK)
    row_spec = pl.BlockSpec((tm, C_WIDTH), lambda i: (i, 0))
    kext_spec = pl.BlockSpec((tm, N_PAIRS * K_EXT), lambda i: (i, 0))
    place = _bias_placement()
    wide = jax.ShapeDtypeStruct((t, C_WIDTH), BF16)
    return pl.pallas_call(
        kern,
        out_shape=[wide, jax.ShapeDtypeStruct((t, N_PAIRS * K_EXT), BF16), wide, wide],
        grid=(t // tm,),
        in_specs=[row_spec, _const_spec(nrm.shape), _const_spec(w.shape), _const_spec(fb.shape),
                  _const_spec(qg.shape), _const_spec(kg.shape), _const_spec(place.shape)],
        out_specs=[row_spec, kext_spec, row_spec, row_spec],
        scratch_shapes=[pltpu.VMEM((tm, O_END), F32), pltpu.VMEM((tm, LANES), F32), pltpu.VMEM((1, LANES), F32)],
        compiler_params=pltpu.CompilerParams(dimension_semantics=("arbitrary",), vmem_limit_bytes=VMEM_LIMIT),
        name="odd_inproj",
    )(x, nrm, w, fb, qg, kg, place)


NEG_BIG = -1e30


V_ROWS = 80


def _fox_kernel(q_ref, k_ref, vt_ref, gate_ref, wcast_ref, o_ref, wcast_out_ref, s_ref, *, tq, tk):
    wcast_out_ref[...] = wcast_ref[...].astype(BF16)
    qi = pl.program_id(2)
    ng = tq // tk
    key_i = lax.broadcasted_iota(jnp.int32, (tk, tk), 0)
    qry_i = lax.broadcasted_iota(jnp.int32, (tk, tk), 1)
    causal = key_i <= qry_i
    chains = [(hh, r) for r in range(ng) for hh in range(2)]
    feat = lax.broadcasted_iota(jnp.int32, (LANES, 1), 0)
    qms = []
    for hh, r in chains:
        q_t = q_ref[r * tk:(r + 1) * tk, :].astype(F32).T
        own = (feat < C_HEAD_DIM) if hh == 0 else (feat >= C_HEAD_DIM)
        qh = jnp.where(own, q_t, 0.0).astype(BF16)
        pick = (feat >= BIAS_PARTS * hh) & (feat < BIAS_PARTS * (hh + 1))
        minus_one = jnp.broadcast_to(jnp.where(pick, -1.0, 0.0).astype(BF16), qh.shape)
        qms.append(jnp.concatenate([qh, minus_one], axis=0))

    def key_rows(j):
        return pl.ds(pl.multiple_of(j * tk, tk), tk)

    def scores_to_scratch(j, slot, live, modes):
        kb = k_ref[key_rows(j), :]
        raw = [_dot(kb, qms[idx]) for idx in live]
        maxes = []
        for s, idx in zip(raw, live):
            hh, r = chains[idx]
            if modes[r] == "diag":
                s = jnp.where(causal, s, NEG_BIG)
            s_ref[slot, idx] = s
            maxes.append(jnp.max(s, axis=0, keepdims=True))
        return maxes

    ones_rows = (lax.broadcasted_iota(jnp.int32, (V_ROWS - C_HEAD_DIM, tk), 0) == 0).astype(BF16)

    def softmax_pv(j, slot, maxes, live, state):
        vt_pair = vt_ref[j]
        vtb = [jnp.concatenate([vt_pair[hh * C_HEAD_DIM:(hh + 1) * C_HEAD_DIM, :], ones_rows], axis=0)
               for hh in range(2)]
        new = list(state)
        probs = []
        for bm, idx in zip(maxes, live):
            m = state[2 * idx]
            m_new = jnp.maximum(m, bm)
            p = jnp.exp2(s_ref[slot, idx] - m_new)
            new[2 * idx] = m_new
            probs.append((jnp.exp2(m - m_new), p.astype(BF16)))
        for (alpha, p), idx in zip(probs, live):
            hh, r = chains[idx]
            new[2 * idx + 1] = alpha * state[2 * idx + 1] + _dot(vtb[hh], p)
        return new

    assert ng % 2 == 0
    all_chains = list(range(len(chains)))
    n_state = 2 * len(chains)
    state = []
    for _ in chains:
        state += [jnp.full((1, tk), NEG_BIG, F32), jnp.zeros((V_ROWS, tk), F32)]
    n_full = qi * ng
    full_modes = ("full",) * ng

    def diag_modes(g):
        return tuple("skip" if r < g else ("diag" if r == g else "full") for r in range(ng))

    def live_chains(g):
        return [idx for idx, (hh, r) in enumerate(chains) if r >= g]

    def two_blocks(j, carry, next_modes):
        st, mx0 = list(carry[:n_state]), carry[n_state:]
        mx1 = scores_to_scratch(j + 1, 1, all_chains, full_modes)
        st = softmax_pv(j, 0, mx0, all_chains, st)
        mx0 = scores_to_scratch(j + 2, 0, all_chains, next_modes)
        st = softmax_pv(j + 1, 1, mx1, all_chains, st)
        return tuple(st) + tuple(mx0)

    def with_full_blocks(_):
        first = scores_to_scratch(0, 0, all_chains, full_modes)
        carry = lax.fori_loop(0, n_full // 2 - 1, lambda i, c: two_blocks(2 * i, c, full_modes),
                              tuple(state) + tuple(first))
        return two_blocks(n_full - 2, carry, diag_modes(0))

    def no_full_blocks(_):
        return tuple(state) + tuple(scores_to_scratch(0, 0, all_chains, diag_modes(0)))

    carry = lax.cond(qi > 0, with_full_blocks, no_full_blocks, 0)
    state, mx = list(carry[:n_state]), carry[n_state:]
    for g in range(ng):
        if g + 1 < ng:
            mx_next = scores_to_scratch(n_full + g + 1, (g + 1) % 2, live_chains(g + 1), diag_modes(g + 1))
        state = softmax_pv(n_full + g, g % 2, mx, live_chains(g), state)
        if g + 1 < ng:
            mx = mx_next

    for r in range(ng):
        parts = []
        for hh in range(2):
            acc = state[2 * chains.index((hh, r)) + 1]
            parts.append(acc[:C_HEAD_DIM, :] / acc[C_HEAD_DIM:C_HEAD_DIM + 1, :])
        o = jnp.concatenate(parts, axis=0).T
        rows = slice(r * tk, (r + 1) * tk)
        o_ref[rows, :] = (o * gate_ref[rows, :].astype(F32)).astype(BF16)


def _fox_attention(q, k_ext, vt, gate, wcast, *, batch, seq, tq, tk):
    t = q.shape[0]
    nq = seq // tq
    kern = functools.partial(_fox_kernel, tq=tq, tk=tk)
    cast_in, cast_out, cast_shape = _side_cast_specs(wcast, batch * N_PAIRS * nq,
                                                     lambda b, p, i: ((b * N_PAIRS + p) * nq + i, 0))
    return pl.pallas_call(
        kern,
        out_shape=[jax.ShapeDtypeStruct((t, C_WIDTH), BF16), cast_shape],
        grid=(batch, N_PAIRS, nq),
        in_specs=[
            pl.BlockSpec((tq, LANES), lambda b, p, i: (b * nq + i, p)),
            pl.BlockSpec((seq, K_EXT), lambda b, p, i: (b, p)),
            pl.BlockSpec((None, None, seq // tk, LANES, tk), lambda b, p, i: (b, p, 0, 0, 0)),
            pl.BlockSpec((tq, LANES), lambda b, p, i: (b * nq + i, p)),
            cast_in,
        ],
        out_specs=[pl.BlockSpec((tq, LANES), lambda b, p, i: (b * nq + i, p)), cast_out],
        scratch_shapes=[pltpu.VMEM((2, 2 * (tq // tk), tk, tk), F32)],
        compiler_params=pltpu.CompilerParams(dimension_semantics=("arbitrary", "arbitrary", "arbitrary"),
                                             vmem_limit_bytes=VMEM_LIMIT),
        name="fox_attn",
    )(q, k_ext, vt, gate, wcast)


def _store_token_tiles(dst_ref, val, n_rows):
    for s in range(ROW_TILE):
        dst_ref[pl.ds(s, n_rows, stride=ROW_TILE), :] = val[:, s * LANES:(s + 1) * LANES]


def _load_token_tiles(src_ref, n_rows):
    return jnp.concatenate([src_ref[pl.ds(s, n_rows, stride=ROW_TILE), :] for s in range(ROW_TILE)], axis=1)


def _odd_out_kernel(x_ref, a_ref, wo_ref, nrm_ref, r_ref, wcast_ref, x3_ref, h_ref, route_ref, wcast_out_ref):
    wcast_out_ref[...] = wcast_ref[...].astype(BF16)
    x3 = x_ref[...] + _dot(a_ref[...], wo_ref[...])
    x3_ref[...] = x3
    h = _rms(x3, nrm_ref[...])
    _store_token_tiles(h_ref, h, h.shape[0])

    h_hi = h.astype(BF16)
    h_lo = (h - h_hi.astype(F32)).astype(BF16)
    r = r_ref[...]
    r_hi = r.astype(BF16)
    r_lo = (r - r_hi.astype(F32)).astype(BF16)
    logits = _dot(h_hi, r_hi) + (_dot(h_lo, r_hi) + _dot(h_hi, r_lo))

    lane = lax.broadcasted_iota(jnp.int32, logits.shape, 1).astype(F32)
    neg_inf = jnp.float32(-jnp.inf)
    lg = jnp.where(lane < N_EXPERTS, logits, neg_inf)
    m1 = jnp.max(lg, axis=-1, keepdims=True)
    i1 = jnp.min(jnp.where(lg == m1, lane, float(LANES)), axis=-1, keepdims=True)
    lg2 = jnp.where(lane == i1, neg_inf, lg)
    m2 = jnp.max(lg2, axis=-1, keepdims=True)
    i2 = jnp.min(jnp.where(lg2 == m2, lane, float(LANES)), axis=-1, keepdims=True)
    e2 = jnp.exp(m2 - m1)
    g1 = 1.0 / (1.0 + e2)
    g2 = e2 / (1.0 + e2)
    route_ref[...] = jnp.where(lane == 0, i1, jnp.where(lane == 1, i2, jnp.where(lane == 2, g1,
                               jnp.where(lane == 3, g2, 0.0))))


def _odd_out(x, a, wo, nrm, router, wcast, *, tm):
    t = x.shape[0]
    row_spec = pl.BlockSpec((tm, D_MODEL), lambda i: (i, 0))
    cast_in, cast_out, cast_shape = _side_cast_specs(wcast, t // tm, lambda i: (i, 0))
    return pl.pallas_call(
        _odd_out_kernel,
        out_shape=[jax.ShapeDtypeStruct((t, D_MODEL), F32), jax.ShapeDtypeStruct((t * ROW_TILE, LANES), F32),
                   jax.ShapeDtypeStruct((t, LANES), F32), cast_shape],
        grid=(t // tm,),
        in_specs=[row_spec, row_spec, _const_spec(wo.shape), _const_spec(nrm.shape), _const_spec(router.shape),
                  cast_in],
        out_specs=[row_spec, pl.BlockSpec((tm * ROW_TILE, LANES), lambda i: (i, 0)),
                   pl.BlockSpec((tm, LANES), lambda i: (i, 0)), cast_out],
        compiler_params=pltpu.CompilerParams(dimension_semantics=("arbitrary",), vmem_limit_bytes=VMEM_LIMIT),
        name="odd_out",
    )(x, a, wo, nrm, router, wcast)


def _row_gather_copy(src_hbm, src_row, dst_ref, dst_row, sem):
    return pltpu.make_async_copy(
        src_hbm.at[pl.ds(pl.multiple_of(src_row * ROW_TILE, ROW_TILE), ROW_TILE), :],
        dst_ref.at[pl.ds(pl.multiple_of(dst_row * ROW_TILE, ROW_TILE), ROW_TILE), :],
        sem)


GATHER_UNROLL = 8


def _start_row_gathers(src_hbm, idx_ref, idx_row, first, count, dst_ref, sem, *, inline):
    if inline:
        for u in range(count):
            _row_gather_copy(src_hbm, idx_ref[idx_row, first + u], dst_ref, first + u, sem).start(priority=u % 2)
        return

    def issue(r2, c):
        for u in range(2):
            r = 2 * r2 + u
            _row_gather_copy(src_hbm, idx_ref[idx_row, r], dst_ref, r, sem).start(priority=u)
        return c

    assert first % 2 == 0 and count % 2 == 0
    lax.fori_loop(first // 2, (first + count) // 2, issue, 0, unroll=GATHER_UNROLL // 2)


def _wait_row_gathers(src_hbm, dst_ref, sem):
    pltpu.make_async_copy(src_hbm.at[pl.ds(0, dst_ref.shape[0]), :], dst_ref, sem).wait()


def _moe_ffn_kernel(te_ref, nu_ref, tok_ref, tok_next_ref, h_hbm, w1_ref, w3_ref, w2_ref, o_ref, xs_ref, buf_ref,
                    sem, *, tm, n_chunks):
    i = pl.program_id(0)
    n_used = nu_ref[0]
    used = i < n_used
    slot = i % 2

    @pl.when(i == 0)
    def _():
        _start_row_gathers(h_hbm, tok_ref, 0, 0, tm, buf_ref.at[0], sem.at[0], inline=False)

    @pl.when(i <= n_used)
    def _():
        _wait_row_gathers(h_hbm, buf_ref.at[slot], sem.at[slot])
        xs_ref[...] = _load_token_tiles(buf_ref.at[slot], tm).astype(BF16)

    @pl.when(used)
    def _():
        n_groups = 3 * n_chunks
        per_group = tm // n_groups
        starts = [(g * per_group, per_group if g + 1 < n_groups else tm - g * per_group) for g in range(n_groups)]

        def prefetch(g):
            first, count = starts[g]
            _start_row_gathers(h_hbm, tok_next_ref, 0, first, count, buf_ref.at[1 - slot], sem.at[1 - slot],
                               inline=True)

        x = xs_ref[...]
        tf = D_FF_EXPERT // n_chunks
        y = None
        for c in range(n_chunks):
            cols = slice(c * tf, (c + 1) * tf)
            prefetch(3 * c)
            a = _dot(x, w1_ref[:, cols])
            prefetch(3 * c + 1)
            b = _dot(x, w3_ref[:, cols])
            prefetch(3 * c + 2)
            part = _dot((a * _sigmoid(a) * b).astype(BF16), w2_ref[cols, :])
            y = part if y is None else y + part
        _store_token_tiles(o_ref, y, tm)

    @pl.when(jnp.logical_not(used))
    def _():
        o_ref[...] = jnp.zeros_like(o_ref)


def _moe_ffn(tile_expert, n_used, row_tok, h_tiles, w1, w3, w2, *, tm, n_chunks):
    n_tiles = row_tok.shape[0] - 1

    def expert_spec(shape):
        return pl.BlockSpec((None,) + shape, lambda i, te, nu: (te[i], 0, 0), pipeline_mode=pl.Buffered(1))

    grid_spec = pltpu.PrefetchScalarGridSpec(
        num_scalar_prefetch=2,
        grid=(n_tiles,),
        in_specs=[
            pl.BlockSpec((None, 1, tm), lambda i, te, nu: (i, 0, 0), memory_space=pltpu.SMEM),
            pl.BlockSpec((None, 1, tm), lambda i, te, nu: (i + 1, 0, 0), memory_space=pltpu.SMEM),
            pl.BlockSpec(memory_space=pl.ANY),
            expert_spec((D_MODEL, D_FF_EXPERT)), expert_spec((D_MODEL, D_FF_EXPERT)),
            expert_spec((D_FF_EXPERT, D_MODEL)),
        ],
        out_specs=pl.BlockSpec((tm * ROW_TILE, LANES), lambda i, te, nu: (i, 0)),
        scratch_shapes=[pltpu.VMEM((tm, D_MODEL), BF16), pltpu.VMEM((2, tm * ROW_TILE, LANES), F32),
                        pltpu.SemaphoreType.DMA((2,))],
    )
    return pl.pallas_call(
        functools.partial(_moe_ffn_kernel, tm=tm, n_chunks=n_chunks),
        out_shape=jax.ShapeDtypeStruct((n_tiles * tm * ROW_TILE, LANES), F32),
        grid_spec=grid_spec,
        compiler_params=pltpu.CompilerParams(dimension_semantics=("arbitrary",), vmem_limit_bytes=VMEM_LIMIT),
        name="moe_ffn",
    )(tile_expert, n_used, row_tok, row_tok, h_tiles, w1, w3, w2)


def _moe_combine_kernel(pos_ref, pos_next_ref, x_ref, route_ref, nrm_ref, y_hbm, o_ref, buf_ref, sem, *, tm):
    i = pl.program_id(0)
    slot = i % 2

    def start_tile(idx_ref, s, inline):
        for k in range(2):
            _start_row_gathers(y_hbm, idx_ref, k, 0, tm, buf_ref.at[s, k], sem.at[s, k], inline=inline)

    @pl.when(i == 0)
    def _():
        start_tile(pos_ref, 0, False)

    @pl.when(i + 1 < pl.num_programs(0))
    def _():
        start_tile(pos_next_ref, 1 - slot, True)

    for k in range(2):
        _wait_row_gathers(y_hbm, buf_ref.at[slot, k], sem.at[slot, k])
    g1 = route_ref[:, 2:3]
    g2 = route_ref[:, 3:4]
    x = x_ref[...] + (g1 * _load_token_tiles(buf_ref.at[slot, 0], tm) + g2 * _load_token_tiles(buf_ref.at[slot, 1], tm))
    o_ref[...] = _rms(x, nrm_ref[...])


def _moe_combine(pos, x, route, nrm, y_tiles, *, tm):
    t = x.shape[0]
    n = t // tm
    row_spec = pl.BlockSpec((tm, D_MODEL), lambda i: (i, 0))
    return pl.pallas_call(
        functools.partial(_moe_combine_kernel, tm=tm),
        out_shape=jax.ShapeDtypeStruct((t, D_MODEL), F32),
        grid=(n,),
        in_specs=[pl.BlockSpec((None, 2, tm), lambda i: (i, 0, 0), memory_space=pltpu.SMEM),
                  pl.BlockSpec((None, 2, tm), lambda i: (jnp.minimum(i + 1, n - 1), 0, 0), memory_space=pltpu.SMEM),
                  row_spec, pl.BlockSpec((tm, LANES), lambda i: (i, 0)), _const_spec(nrm.shape),
                  pl.BlockSpec(memory_space=pl.ANY)],
        out_specs=row_spec,
        scratch_shapes=[pltpu.VMEM((2, 2, tm * ROW_TILE, LANES), F32), pltpu.SemaphoreType.DMA((2, 2))],
        compiler_params=pltpu.CompilerParams(dimension_semantics=("arbitrary",), vmem_limit_bytes=VMEM_LIMIT),
        name="moe_combine",
    )(pos, pos, x, route, nrm, y_tiles)


def _routing_tables(idx1, idx2, *, tm, n_tiles):
    t = idx1.shape[0]
    e_flat = jnp.concatenate([idx1, idx2])
    onehot = (e_flat[:, None] == jnp.arange(N_EXPERTS, dtype=jnp.int32)[None, :]).astype(jnp.int32)
    csum = jnp.cumsum(onehot, axis=0)
    rank = jnp.sum((csum - onehot) * onehot, axis=1)
    counts = csum[-1]
    tiles_e = (counts + tm - 1) // tm
    tile_end = jnp.cumsum(tiles_e)
    tile_start = tile_end - tiles_e
    pos = jnp.sum(onehot * tile_start[None, :], axis=1) * tm + rank
    n_used = tile_end[-1]
    tile_ids = jnp.arange(n_tiles, dtype=jnp.int32)
    te = jnp.sum((tile_ids[:, None] >= tile_end[None, :]).astype(jnp.int32), axis=1)
    te_last = jnp.sum((n_used - 1 >= tile_end).astype(jnp.int32))
    tile_expert = jnp.where(tile_ids < n_used, te, te_last).astype(jnp.int32)
    tok = jnp.concatenate([jnp.arange(t, dtype=jnp.int32)] * 2)
    row_tok = jnp.zeros((n_tiles * tm,), jnp.int32).at[pos].set(tok, unique_indices=True)
    return pos[:t], pos[t:], row_tok, tile_expert, n_used.reshape(1).astype(jnp.int32)


def _pad_cols(w, n):
    return jnp.pad(w, ((0, 0), (0, n - w.shape[1])))


def kernel(x, even_norm_mix, even_w_in, even_gate_up, even_gate_bias, even_w_s, even_b_s, even_ln_g, even_ln_b,
           even_head_g, even_w_o, even_norm_ffn, even_ffn_w1, even_ffn_w3, even_ffn_w2, odd_norm_mix, odd_w_in,
           odd_forget_bias, odd_q_g, odd_k_g, odd_w_o, odd_norm_ffn, odd_router, odd_exp_w1, odd_exp_w3,
           odd_exp_w2, final_norm):
    batch, seq, d = x.shape
    t = batch * seq
    xt = x.reshape(t, d)
    tm = min(512, seq)

    w_in = even_w_in[0]
    u_w, v_w, q_w, k_w, g_w, vb_w, og_w = jnp.split(w_in, [512, 1024, 1280, 1536, 1552, 2064], axis=1)
    win_e = jnp.concatenate([u_w, v_w, q_w, k_w, vb_w, og_w, _pad_cols(g_w, LANES)], axis=1).astype(BF16)
    gup = jnp.pad(even_gate_up[0], ((0, LANES - B_GATE_RANK), (0, 0))).astype(BF16)
    gb = even_gate_bias[0].reshape(1, B_QK_WIDTH)
    tril = jnp.tril(jnp.ones((CHUNK, CHUNK), dtype=bool))
    ws = jnp.where(tril[None], even_w_s[0], 0.0).astype(BF16)
    bs = jnp.broadcast_to(even_b_s[0][:, :, None], (A_GROUPS, CHUNK, LANES))
    lng = even_ln_g[0].reshape(1, A_WIDTH)
    lnb = even_ln_b[0].reshape(1, A_WIDTH)
    hg = even_head_g[0].reshape(B_HEADS, 1, B_VAL_DIM)
    ew1 = odd_exp_w1[0].reshape(N_EXPERTS * D_MODEL, D_FF_EXPERT)
    ew3 = odd_exp_w3[0].reshape(N_EXPERTS * D_MODEL, D_FF_EXPERT)
    ew2 = odd_exp_w2[0].reshape(N_EXPERTS * D_FF_EXPERT, D_MODEL)
    x1, ew1_b = _even_mixer(xt, even_norm_mix[0].reshape(1, d), win_e, gup, gb, ws, bs, lng, lnb, hg,
                            even_w_o[0].astype(BF16), ew1, seq=seq, tm=tm)
    x2 = _dense_ffn(x1, even_norm_ffn[0].reshape(1, d), even_ffn_w1[0].astype(BF16), even_ffn_w3[0].astype(BF16),
                    even_ffn_w2[0].astype(BF16), tm=tm)

    q_w, k_w, v_w, og_w, f_w = jnp.split(odd_w_in[0], [1024, 2048, 3072, 4096], axis=1)
    win_o = jnp.concatenate([q_w, k_w, v_w, og_w, _pad_cols(f_w, LANES)], axis=1).astype(BF16)
    fb = jnp.pad(odd_forget_bias[0], (0, LANES - C_HEADS)).reshape(1, LANES)
    qg = jnp.tile(odd_q_g[0], C_HEADS).reshape(1, C_WIDTH)
    kg = jnp.tile(odd_k_g[0], C_HEADS).reshape(1, C_WIDTH)
    q, k_ext, v, gate = _odd_inproj(x2, odd_norm_mix[0].reshape(1, d), win_o, fb, qg, kg, seq=seq, tm=tm)
    tk = min(256, seq)
    tq = min(1024, seq)
    vt = jnp.transpose(v.reshape(batch, seq // tk, tk, N_PAIRS, LANES), (0, 3, 1, 4, 2))
    attn, ew2_b = _fox_attention(q, k_ext, vt, gate, ew2, batch=batch, seq=seq, tq=tq, tk=tk)

    router = _pad_cols(odd_router[0], LANES)
    x3, h_tiles, route, ew3_b = _odd_out(x2, attn, odd_w_o[0].astype(BF16), odd_norm_ffn[0].reshape(1, d), router,
                                         ew3, tm=tm)

    tm_moe = 512
    tm_comb = min(256, seq)
    n_tiles = (2 * t) // tm_moe + N_EXPERTS + 1
    idx1 = route[:, 0].astype(jnp.int32)
    idx2 = route[:, 1].astype(jnp.int32)
    pos1, pos2, row_tok, tile_expert, n_used = _routing_tables(idx1, idx2, tm=tm_moe, n_tiles=n_tiles + 1)
    y_tiles = _moe_ffn(tile_expert[:n_tiles], n_used, row_tok.reshape(n_tiles + 1, 1, tm_moe), h_tiles,
                       ew1_b.reshape(N_EXPERTS, D_MODEL, D_FF_EXPERT), ew3_b.reshape(N_EXPERTS, D_MODEL, D_FF_EXPERT),
                       ew2_b.reshape(N_EXPERTS, D_FF_EXPERT, D_MODEL), tm=tm_moe, n_chunks=2)
    pos = jnp.stack([pos1.reshape(t // tm_comb, tm_comb), pos2.reshape(t // tm_comb, tm_comb)], axis=1)
    out = _moe_combine(pos, x3, route, final_norm.reshape(1, d), y_tiles, tm=tm_comb)
    return out.reshape(batch, seq, d)
```

```python
import functools
import math

import jax
import jax.numpy as jnp
from jax import lax
from jax.experimental import pallas as pl
from jax.experimental.pallas import tpu as pltpu

F32 = jnp.float32
BF16 = jnp.bfloat16
HIGHEST = lax.Precision.HIGHEST

EPS = 1e-6
D_MODEL = 1024
CHUNK = 128
SUB = 32
N_SUB = CHUNK // SUB
A_GROUPS = 4
A_WIDTH = 512
B_HEADS = 4
B_KEY_DIM = 64
B_VAL_DIM = 128
B_QK_WIDTH = 256
B_V_WIDTH = 512
B_GATE_RANK = 16
B_GATE_NORMALIZER = 16.0
C_HEADS = 16
C_HEAD_DIM = 64
C_WIDTH = 1024
D_FF_DENSE = 2816
N_EXPERTS = 8
D_FF_EXPERT = 3584
LANES = 128
MAX_DECAY_EXP = 60.0
LOG2E = math.log2(math.e)
ROW_TILE = 8

E_U, E_V, E_Q, E_K, E_VB, E_OG, E_G, E_END = 0, 512, 1024, 1280, 1536, 2048, 2560, 2688
O_Q, O_K, O_V, O_OG, O_F, O_END = 0, 1024, 2048, 3072, 4096, 4224

VMEM_LIMIT = 56 * 1024 * 1024


def _rms(x, g):
    ms = jnp.mean(x * x, axis=-1, keepdims=True)
    return x * lax.rsqrt(ms + EPS) * g


def _gelu_tanh(x):
    c = math.sqrt(2.0 / math.pi)
    return x * (0.5 * (1.0 + jnp.tanh(c * (x + 0.044715 * (x * x * x)))))


def _sigmoid(x):
    return 1.0 / (1.0 + jnp.exp(-x))


def _log_sigmoid(x):
    return jnp.minimum(x, 0.0) - jnp.log(1.0 + jnp.exp(-jnp.abs(x)))


def _dot(a, b):
    return jnp.dot(a, b, preferred_element_type=F32)


def _dot_nt(a, b):
    return lax.dot_general(a, b, (((1,), (1,)), ((), ())), preferred_element_type=F32)


def _split3(x):
    hi = x.astype(BF16)
    r1 = x - hi.astype(F32)
    mid = r1.astype(BF16)
    lo = (r1 - mid.astype(F32)).astype(BF16)
    return hi, mid, lo


def _cumsum_rows(tril_b, x):
    hi, mid, lo = _split3(x)
    return _dot(tril_b, hi) + _dot(tril_b, mid) + _dot(tril_b, lo)


def _const_spec(shape):
    nd = len(shape)
    return pl.BlockSpec(shape, lambda *_: (0,) * nd)


def _side_cast_specs(w2d, n_steps, index_map):
    rows = w2d.shape[0] // n_steps
    assert rows * n_steps == w2d.shape[0] and rows % 16 == 0
    spec = pl.BlockSpec((rows, w2d.shape[1]), index_map)
    return spec, spec, jax.ShapeDtypeStruct(w2d.shape, BF16)


def _even_mixer_kernel(x_ref, nrm_ref, win_ref, gup_ref, gb_ref, ws_ref, bs_ref, lng_ref, lnb_ref,
                       hg_ref, wo_ref, wcast_ref, o_ref, wcast_out_ref, z_ref, mix_ref, st_ref,
                       *, tiles_per_batch, n_chunks):
    i = pl.program_id(0)
    wcast_out_ref[...] = wcast_ref[...].astype(BF16)

    @pl.when(i % tiles_per_batch == 0)
    def _():
        st_ref[...] = jnp.zeros_like(st_ref)

    h = _rms(x_ref[...], nrm_ref[...]).astype(BF16)
    z_ref[...] = _dot(h, win_ref[...])

    row = lax.broadcasted_iota(jnp.int32, (CHUNK, CHUNK), 0)
    col = lax.broadcasted_iota(jnp.int32, (CHUNK, CHUNK), 1)
    tril_b = (col <= row).astype(BF16)
    sub_row = row & (SUB - 1)
    head_lane = lax.broadcasted_iota(jnp.int32, (1, B_QK_WIDTH), 1) // B_KEY_DIM
    bd_mask = (lax.broadcasted_iota(jnp.int32, (B_V_WIDTH, B_QK_WIDTH), 0) // B_VAL_DIM
               == lax.broadcasted_iota(jnp.int32, (B_V_WIDTH, B_QK_WIDTH), 1) // B_KEY_DIM)

    def chunk_body(c, carry):
        rows = pl.ds(pl.multiple_of(c * CHUNK, CHUNK), CHUNK)

        u = _gelu_tanh(z_ref[rows, E_U:E_V])
        v = _gelu_tanh(z_ref[rows, E_V:E_Q])
        mu = jnp.mean(v, axis=-1, keepdims=True)
        vc = v - mu
        var = jnp.mean(vc * vc, axis=-1, keepdims=True)
        vln = (vc * lax.rsqrt(var + EPS) * lng_ref[...] + lnb_ref[...]).astype(BF16)
        for g in range(A_GROUPS):
            sl = slice(g * LANES, (g + 1) * LANES)
            mixed = _dot(ws_ref[g], vln[:, sl]) + bs_ref[g]
            mix_ref[rows, sl] = (u[:, sl] * mixed).astype(BF16)

        q = z_ref[rows, E_Q:E_K] * (B_KEY_DIM ** -0.5)
        k = z_ref[rows, E_K:E_VB]
        vb = z_ref[rows, E_VB:E_OG]
        og = z_ref[rows, E_OG:E_G]
        glr = z_ref[rows, E_G:E_END].astype(BF16)
        logit = _dot(glr, gup_ref[...]) + gb_ref[...]
        log_a = _log_sigmoid(logit) * (1.0 / B_GATE_NORMALIZER)
        g_cum = _cumsum_rows(tril_b, log_a)
        g_last = g_cum[CHUNK - 1:CHUNK, :]
        st = st_ref[...]
        o = _dot_nt((q * jnp.exp(g_cum)).astype(BF16), st.astype(BF16))

        p_rows = [[None] * N_SUB for _ in range(B_HEADS)]
        for s in range(N_SUB):
            gs = g_cum[s * SUB:(s + 1) * SUB, :]
            if s == 0:
                qt = q[0:SUB, :] * jnp.exp(gs)
                kt = k * jnp.exp(jnp.minimum(-g_cum, MAX_DECAY_EXP))
            else:
                ref_g = g_cum[s * SUB - 1:s * SUB, :]
                qt = q[s * SUB:(s + 1) * SUB, :] * jnp.exp(gs - ref_g)
                kt = k * jnp.exp(jnp.minimum(ref_g - g_cum, MAX_DECAY_EXP))
            qs = jnp.concatenate([jnp.where(head_lane == hh, qt, 0.0) for hh in range(B_HEADS)],
                                 axis=0).astype(BF16)
            sc = _dot_nt(qs, kt.astype(BF16))
            sc = jnp.where(col <= (s * SUB + sub_row), sc, 0.0)
            for hh in range(B_HEADS):
                p_rows[hh][s] = sc[hh * SUB:(hh + 1) * SUB, :]

        vb_b = vb.astype(BF16)
        for hh in range(B_HEADS):
            sl = slice(hh * B_VAL_DIM, (hh + 1) * B_VAL_DIM)
            ph = jnp.concatenate(p_rows[hh], axis=0).astype(BF16)
            oh = o[:, sl] + _dot(ph, vb_b[:, sl])
            on = _rms(oh, hg_ref[hh])
            ogh = og[:, sl]
            mix_ref[rows, A_WIDTH + hh * B_VAL_DIM:A_WIDTH + (hh + 1) * B_VAL_DIM] = (
                on * (ogh * _sigmoid(ogh))).astype(BF16)

        k_dec = (k * jnp.exp(g_last - g_cum)).astype(BF16)
        upd = _dot(vb.T.astype(BF16), k_dec)
        st_ref[...] = jnp.exp(g_last) * st + jnp.where(bd_mask, upd, 0.0)
        return carry

    lax.fori_loop(0, n_chunks, chunk_body, 0, unroll=True)
    o_ref[...] = x_ref[...] + _dot(mix_ref[...], wo_ref[...])


def _even_mixer(x, nrm, win, gup, gb, ws, bs, lng, lnb, hg, wo, wcast, *, seq, tm):
    t = x.shape[0]
    kern = functools.partial(_even_mixer_kernel, tiles_per_batch=seq // tm, n_chunks=tm // CHUNK)
    cast_in, cast_out, cast_shape = _side_cast_specs(wcast, t // tm, lambda i: (i, 0))
    return pl.pallas_call(
        kern,
        out_shape=[jax.ShapeDtypeStruct((t, D_MODEL), F32), cast_shape],
        grid=(t // tm,),
        in_specs=[
            pl.BlockSpec((tm, D_MODEL), lambda i: (i, 0)),
            _const_spec(nrm.shape), _resident_spec(win.shape), _const_spec(gup.shape), _const_spec(gb.shape),
            _const_spec(ws.shape), _const_spec(bs.shape), _const_spec(lng.shape), _const_spec(lnb.shape),
            _const_spec(hg.shape), _resident_spec(wo.shape), cast_in,
        ],
        out_specs=[pl.BlockSpec((tm, D_MODEL), lambda i: (i, 0)), cast_out],
        scratch_shapes=[
            pltpu.VMEM((tm, E_END), F32),
            pltpu.VMEM((tm, D_MODEL), BF16),
            pltpu.VMEM((B_V_WIDTH, B_QK_WIDTH), F32),
        ],
        compiler_params=pltpu.CompilerParams(dimension_semantics=("arbitrary",), vmem_limit_bytes=VMEM_LIMIT),
        name="even_mixer",
    )(x, nrm, win, gup, gb, ws, bs, lng, lnb, hg, wo, wcast)


def _dense_ffn_kernel(x_ref, nrm_ref, w1_ref, w3_ref, w2_ref, o_ref):
    x = x_ref[...]
    h = _rms(x, nrm_ref[...]).astype(BF16)
    a = _dot(h, w1_ref[...])
    b = _dot(h, w3_ref[...])
    o_ref[...] = x + _dot((a * _sigmoid(a) * b).astype(BF16), w2_ref[...])


def _resident_spec(shape):
    nd = len(shape)
    return pl.BlockSpec(shape, lambda *_: (0,) * nd, pipeline_mode=pl.Buffered(1))


def _dense_ffn(x, nrm, w1, w3, w2, *, tm):
    t = x.shape[0]
    row_spec = pl.BlockSpec((tm, D_MODEL), lambda i: (i, 0))
    return pl.pallas_call(
        _dense_ffn_kernel,
        out_shape=jax.ShapeDtypeStruct((t, D_MODEL), F32),
        grid=(t // tm,),
        in_specs=[row_spec, _const_spec(nrm.shape), _resident_spec(w1.shape), _resident_spec(w3.shape),
                  _resident_spec(w2.shape)],
        out_specs=row_spec,
        compiler_params=pltpu.CompilerParams(dimension_semantics=("arbitrary",), vmem_limit_bytes=VMEM_LIMIT),
        name="dense_ffn",
    )(x, nrm, w1, w3, w2)


def _head_rms(x, gain):
    lo = lax.broadcasted_iota(jnp.int32, (1, LANES), 1) < C_HEAD_DIM
    outs = []
    for t in range(C_WIDTH // LANES):
        xt = x[:, t * LANES:(t + 1) * LANES]
        sq = xt * xt
        s_lo = jnp.sum(jnp.where(lo, sq, 0.0), axis=-1, keepdims=True)
        s_hi = jnp.sum(jnp.where(lo, 0.0, sq), axis=-1, keepdims=True)
        inv = jnp.where(lo, lax.rsqrt(s_lo * (1.0 / C_HEAD_DIM) + EPS), lax.rsqrt(s_hi * (1.0 / C_HEAD_DIM) + EPS))
        outs.append(xt * inv)
    return jnp.concatenate(outs, axis=-1) * gain


N_PAIRS = C_HEADS // 2
K_EXT = 2 * LANES
BIAS_PARTS = 3


def _bias_placement():
    src = jnp.arange(BIAS_PARTS * LANES)
    part, head = src // LANES, src % LANES
    dst = (head // 2) * LANES + BIAS_PARTS * (head % 2) + part
    hit = (dst[:, None] == jnp.arange(N_PAIRS * LANES)[None, :]) & (head < C_HEADS)[:, None]
    return hit.astype(BF16)


def _odd_inproj_kernel(x_ref, nrm_ref, w_ref, fb_ref, qg_ref, kg_ref, place_ref,
                       q_ref, k_ref, vt_ref, gate_ref, z_ref, c_ref, carry_ref, *, tiles_per_batch, n_chunks, tk):
    i = pl.program_id(0)

    @pl.when(i % tiles_per_batch == 0)
    def _():
        carry_ref[...] = jnp.zeros_like(carry_ref)

    h = _rms(x_ref[...], nrm_ref[...]).astype(BF16)
    z_ref[...] = _dot(h, w_ref[...])
    q_ref[...] = (_head_rms(z_ref[:, O_Q:O_K], qg_ref[...]) * (C_HEAD_DIM ** -0.5 * LOG2E)).astype(BF16)
    kn = _head_rms(z_ref[:, O_K:O_V], kg_ref[...]).astype(BF16)
    for p in range(N_PAIRS):
        for kb in range(vt_ref.shape[1]):
            blk = z_ref[kb * tk:(kb + 1) * tk, O_V + p * LANES:O_V + (p + 1) * LANES]
            vt_ref[p, kb] = blk.T.astype(BF16)
    gate_ref[...] = _sigmoid(z_ref[:, O_OG:O_F]).astype(BF16)

    row = lax.broadcasted_iota(jnp.int32, (CHUNK, CHUNK), 0)
    col = lax.broadcasted_iota(jnp.int32, (CHUNK, CHUNK), 1)
    tril_b = (col <= row).astype(BF16)
    carry = carry_ref[...]
    for c in range(n_chunks):
        rows = slice(c * CHUNK, (c + 1) * CHUNK)
        log_f = _log_sigmoid(z_ref[rows, O_F:O_END] + fb_ref[...])
        cs = _cumsum_rows(tril_b, log_f) + carry
        c_ref[rows, :] = cs
        carry = cs[CHUNK - 1:CHUNK, :]
    carry_ref[...] = carry

    bias = _dot(jnp.concatenate(_split3(c_ref[...] * LOG2E), axis=1), place_ref[...]).astype(BF16)
    for p in range(N_PAIRS):
        k_ref[:, p * K_EXT:p * K_EXT + LANES] = kn[:, p * LANES:(p + 1) * LANES]
        k_ref[:, p * K_EXT + LANES:(p + 1) * K_EXT] = bias[:, p * LANES:(p + 1) * LANES]


def _odd_inproj(x, nrm, w, fb, qg, kg, *, seq, tm, tk):
    t = x.shape[0]
    tiles_per_batch = seq // tm
    kern = functools.partial(_odd_inproj_kernel, tiles_per_batch=tiles_per_batch, n_chunks=tm // CHUNK, tk=tk)
    row_spec = pl.BlockSpec((tm, C_WIDTH), lambda i: (i, 0))
    kext_spec = pl.BlockSpec((tm, N_PAIRS * K_EXT), lambda i: (i, 0))
    vt_spec = pl.BlockSpec((None, N_PAIRS, tm // tk, LANES, tk),
                           lambda i: (i // tiles_per_batch, 0, i % tiles_per_batch, 0, 0))
    place = _bias_placement()
    wide = jax.ShapeDtypeStruct((t, C_WIDTH), BF16)
    return pl.pallas_call(
        kern,
        out_shape=[wide, jax.ShapeDtypeStruct((t, N_PAIRS * K_EXT), BF16),
                   jax.ShapeDtypeStruct((t // seq, N_PAIRS, seq // tk, LANES, tk), BF16), wide],
        grid=(t // tm,),
        in_specs=[row_spec, _const_spec(nrm.shape), _const_spec(w.shape), _const_spec(fb.shape),
                  _const_spec(qg.shape), _const_spec(kg.shape), _const_spec(place.shape)],
        out_specs=[row_spec, kext_spec, vt_spec, row_spec],
        scratch_shapes=[pltpu.VMEM((tm, O_END), F32), pltpu.VMEM((tm, LANES), F32), pltpu.VMEM((1, LANES), F32)],
        compiler_params=pltpu.CompilerParams(dimension_semantics=("arbitrary",), vmem_limit_bytes=VMEM_LIMIT),
        name="odd_inproj",
    )(x, nrm, w, fb, qg, kg, place)


NEG_BIG = -1e30


V_ROWS = 80


def _fox_kernel(q_ref, k_ref, vt_ref, gate_ref, wcast_ref, o_ref, wcast_out_ref, s_ref, *, tq, tk):
    wcast_out_ref[...] = wcast_ref[...].astype(BF16)
    qi = pl.program_id(2)
    ng = tq // tk
    key_i = lax.broadcasted_iota(jnp.int32, (tk, tk), 0)
    qry_i = lax.broadcasted_iota(jnp.int32, (tk, tk), 1)
    causal = key_i <= qry_i
    chains = [(hh, r) for r in range(ng) for hh in range(2)]
    feat = lax.broadcasted_iota(jnp.int32, (LANES, 1), 0)
    qms = []
    for hh, r in chains:
        q_t = q_ref[r * tk:(r + 1) * tk, :].astype(F32).T
        own = (feat < C_HEAD_DIM) if hh == 0 else (feat >= C_HEAD_DIM)
        qh = jnp.where(own, q_t, 0.0).astype(BF16)
        pick = (feat >= BIAS_PARTS * hh) & (feat < BIAS_PARTS * (hh + 1))
        minus_one = jnp.broadcast_to(jnp.where(pick, -1.0, 0.0).astype(BF16), qh.shape)
        qms.append(jnp.concatenate([qh, minus_one], axis=0))

    def key_rows(j):
        return pl.ds(pl.multiple_of(j * tk, tk), tk)

    def scores_to_scratch(j, slot, live, modes):
        kb = k_ref[key_rows(j), :]
        raw = [_dot(kb, qms[idx]) for idx in live]
        maxes = []
        for s, idx in zip(raw, live):
            hh, r = chains[idx]
            if modes[r] == "diag":
                s = jnp.where(causal, s, NEG_BIG)
            s_ref[slot, idx] = s
            maxes.append(jnp.max(s, axis=0, keepdims=True))
        return maxes

    ones_rows = (lax.broadcasted_iota(jnp.int32, (V_ROWS - C_HEAD_DIM, tk), 0) == 0).astype(BF16)

    def softmax_pv(j, slot, maxes, live, state):
        vt_pair = vt_ref[j]
        vtb = [jnp.concatenate([vt_pair[hh * C_HEAD_DIM:(hh + 1) * C_HEAD_DIM, :], ones_rows], axis=0)
               for hh in range(2)]
        new = list(state)
        probs = []
        for bm, idx in zip(maxes, live):
            m = state[2 * idx]
            m_new = jnp.maximum(m, bm)
            p = jnp.exp2(s_ref[slot, idx] - m_new)
            new[2 * idx] = m_new
            probs.append((jnp.exp2(m - m_new), p.astype(BF16)))
        for (alpha, p), idx in zip(probs, live):
            hh, r = chains[idx]
            new[2 * idx + 1] = alpha * state[2 * idx + 1] + _dot(vtb[hh], p)
        return new

    assert ng % 2 == 0
    all_chains = list(range(len(chains)))
    n_state = 2 * len(chains)
    state = []
    for _ in chains:
        state += [jnp.full((1, tk), NEG_BIG, F32), jnp.zeros((V_ROWS, tk), F32)]
    n_full = qi * ng
    full_modes = ("full",) * ng

    def diag_modes(g):
        return tuple("skip" if r < g else ("diag" if r == g else "full") for r in range(ng))

    def live_chains(g):
        return [idx for idx, (hh, r) in enumerate(chains) if r >= g]

    def two_blocks(j, carry, next_modes):
        st, mx0 = list(carry[:n_state]), carry[n_state:]
        mx1 = scores_to_scratch(j + 1, 1, all_chains, full_modes)
        st = softmax_pv(j, 0, mx0, all_chains, st)
        mx0 = scores_to_scratch(j + 2, 0, all_chains, next_modes)
        st = softmax_pv(j + 1, 1, mx1, all_chains, st)
        return tuple(st) + tuple(mx0)

    def with_full_blocks(_):
        first = scores_to_scratch(0, 0, all_chains, full_modes)
        carry = lax.fori_loop(0, n_full // 2 - 1, lambda i, c: two_blocks(2 * i, c, full_modes),
                              tuple(state) + tuple(first))
        return two_blocks(n_full - 2, carry, diag_modes(0))

    def no_full_blocks(_):
        return tuple(state) + tuple(scores_to_scratch(0, 0, all_chains, diag_modes(0)))

    carry = lax.cond(qi > 0, with_full_blocks, no_full_blocks, 0)
    state, mx = list(carry[:n_state]), carry[n_state:]
    for g in range(ng):
        if g + 1 < ng:
            mx_next = scores_to_scratch(n_full + g + 1, (g + 1) % 2, live_chains(g + 1), diag_modes(g + 1))
        state = softmax_pv(n_full + g, g % 2, mx, live_chains(g), state)
        if g + 1 < ng:
            mx = mx_next

    for r in range(ng):
        parts = []
        for hh in range(2):
            acc = state[2 * chains.index((hh, r)) + 1]
            parts.append(acc[:C_HEAD_DIM, :] / acc[C_HEAD_DIM:C_HEAD_DIM + 1, :])
        o = jnp.concatenate(parts, axis=0).T
        rows = slice(r * tk, (r + 1) * tk)
        o_ref[rows, :] = (o * gate_ref[rows, :].astype(F32)).astype(BF16)


def _fox_attention(q, k_ext, vt, gate, wcast, *, batch, seq, tq, tk):
    t = q.shape[0]
    nq = seq // tq
    kern = functools.partial(_fox_kernel, tq=tq, tk=tk)
    cast_in, cast_out, cast_shape = _side_cast_specs(wcast, batch * N_PAIRS * nq,
                                                     lambda b, p, i: ((b * N_PAIRS + p) * nq + i, 0))
    return pl.pallas_call(
        kern,
        out_shape=[jax.ShapeDtypeStruct((t, C_WIDTH), BF16), cast_shape],
        grid=(batch, N_PAIRS, nq),
        in_specs=[
            pl.BlockSpec((tq, LANES), lambda b, p, i: (b * nq + i, p)),
            pl.BlockSpec((seq, K_EXT), lambda b, p, i: (b, p)),
            pl.BlockSpec((None, None, seq // tk, LANES, tk), lambda b, p, i: (b, p, 0, 0, 0)),
            pl.BlockSpec((tq, LANES), lambda b, p, i: (b * nq + i, p)),
            cast_in,
        ],
        out_specs=[pl.BlockSpec((tq, LANES), lambda b, p, i: (b * nq + i, p)), cast_out],
        scratch_shapes=[pltpu.VMEM((2, 2 * (tq // tk), tk, tk), F32)],
        compiler_params=pltpu.CompilerParams(dimension_semantics=("arbitrary", "arbitrary", "arbitrary"),
                                             vmem_limit_bytes=VMEM_LIMIT),
        name="fox_attn",
    )(q, k_ext, vt, gate, wcast)


def _store_token_tiles(dst_ref, val, n_rows):
    for s in range(ROW_TILE):
        dst_ref[pl.ds(s, n_rows, stride=ROW_TILE), :] = val[:, s * LANES:(s + 1) * LANES]


def _load_token_tiles(src_ref, n_rows):
    return jnp.concatenate([src_ref[pl.ds(s, n_rows, stride=ROW_TILE), :] for s in range(ROW_TILE)], axis=1)


def _odd_out_kernel(x_ref, a_ref, wo_ref, nrm_ref, r_ref, wcast_ref, x3_ref, h_ref, route_ref, wcast_out_ref):
    wcast_out_ref[...] = wcast_ref[...].astype(BF16)
    x3 = x_ref[...] + _dot(a_ref[...], wo_ref[...])
    x3_ref[...] = x3
    h = _rms(x3, nrm_ref[...])
    _store_token_tiles(h_ref, h, h.shape[0])

    h_hi = h.astype(BF16)
    h_lo = (h - h_hi.astype(F32)).astype(BF16)
    r = r_ref[...]
    r_hi = r.astype(BF16)
    r_lo = (r - r_hi.astype(F32)).astype(BF16)
    logits = _dot(h_hi, r_hi) + (_dot(h_lo, r_hi) + _dot(h_hi, r_lo))

    lane = lax.broadcasted_iota(jnp.int32, logits.shape, 1).astype(F32)
    neg_inf = jnp.float32(-jnp.inf)
    lg = jnp.where(lane < N_EXPERTS, logits, neg_inf)
    m1 = jnp.max(lg, axis=-1, keepdims=True)
    i1 = jnp.min(jnp.where(lg == m1, lane, float(LANES)), axis=-1, keepdims=True)
    lg2 = jnp.where(lane == i1, neg_inf, lg)
    m2 = jnp.max(lg2, axis=-1, keepdims=True)
    i2 = jnp.min(jnp.where(lg2 == m2, lane, float(LANES)), axis=-1, keepdims=True)
    e2 = jnp.exp(m2 - m1)
    g1 = 1.0 / (1.0 + e2)
    g2 = e2 / (1.0 + e2)
    route_ref[...] = jnp.where(lane == 0, i1, jnp.where(lane == 1, i2, jnp.where(lane == 2, g1,
                               jnp.where(lane == 3, g2, 0.0))))


def _odd_out(x, a, wo, nrm, router, wcast, *, tm):
    t = x.shape[0]
    row_spec = pl.BlockSpec((tm, D_MODEL), lambda i: (i, 0))
    cast_in, cast_out, cast_shape = _side_cast_specs(wcast, t // tm, lambda i: (i, 0))
    return pl.pallas_call(
        _odd_out_kernel,
        out_shape=[jax.ShapeDtypeStruct((t, D_MODEL), F32), jax.ShapeDtypeStruct((t * ROW_TILE, LANES), F32),
                   jax.ShapeDtypeStruct((t, LANES), F32), cast_shape],
        grid=(t // tm,),
        in_specs=[row_spec, row_spec, _const_spec(wo.shape), _const_spec(nrm.shape), _const_spec(router.shape),
                  cast_in],
        out_specs=[row_spec, pl.BlockSpec((tm * ROW_TILE, LANES), lambda i: (i, 0)),
                   pl.BlockSpec((tm, LANES), lambda i: (i, 0)), cast_out],
        compiler_params=pltpu.CompilerParams(dimension_semantics=("arbitrary",), vmem_limit_bytes=VMEM_LIMIT),
        name="odd_out",
    )(x, a, wo, nrm, router, wcast)


def _row_gather_copy(src_hbm, src_row, dst_ref, dst_row, sem):
    return pltpu.make_async_copy(
        src_hbm.at[pl.ds(pl.multiple_of(src_row * ROW_TILE, ROW_TILE), ROW_TILE), :],
        dst_ref.at[pl.ds(pl.multiple_of(dst_row * ROW_TILE, ROW_TILE), ROW_TILE), :],
        sem)


GATHER_UNROLL = 8


def _start_row_gathers(src_hbm, idx_ref, idx_row, first, count, dst_ref, sem, *, inline):
    if inline:
        for u in range(count):
            _row_gather_copy(src_hbm, idx_ref[idx_row, first + u], dst_ref, first + u, sem).start(priority=u % 2)
        return

    def issue(r2, c):
        for u in range(2):
            r = 2 * r2 + u
            _row_gather_copy(src_hbm, idx_ref[idx_row, r], dst_ref, r, sem).start(priority=u)
        return c

    assert first % 2 == 0 and count % 2 == 0
    lax.fori_loop(first // 2, (first + count) // 2, issue, 0, unroll=GATHER_UNROLL // 2)


def _wait_row_gathers(src_hbm, dst_ref, sem):
    pltpu.make_async_copy(src_hbm.at[pl.ds(0, dst_ref.shape[0]), :], dst_ref, sem).wait()


def _moe_ffn_kernel(te_ref, nu_ref, tok_ref, tok_next_ref, h_hbm, w1_ref, w3_ref, w2_ref, o_ref, xs_ref, buf_ref,
                    sem, *, tm, n_chunks):
    i = pl.program_id(0)
    n_used = nu_ref[0]
    used = i < n_used
    slot = i % 2

    @pl.when(i == 0)
    def _():
        _start_row_gathers(h_hbm, tok_ref, 0, 0, tm, buf_ref.at[0], sem.at[0], inline=False)

    @pl.when(i <= n_used)
    def _():
        _wait_row_gathers(h_hbm, buf_ref.at[slot], sem.at[slot])
        xs_ref[...] = _load_token_tiles(buf_ref.at[slot], tm).astype(BF16)

    @pl.when(used)
    def _():
        n_groups = 3 * n_chunks
        per_group = tm // n_groups
        starts = [(g * per_group, per_group if g + 1 < n_groups else tm - g * per_group) for g in range(n_groups)]

        def prefetch(g):
            first, count = starts[g]
            _start_row_gathers(h_hbm, tok_next_ref, 0, first, count, buf_ref.at[1 - slot], sem.at[1 - slot],
                               inline=True)

        x = xs_ref[...]
        tf = D_FF_EXPERT // n_chunks
        y = None
        for c in range(n_chunks):
            cols = slice(c * tf, (c + 1) * tf)
            prefetch(3 * c)
            a = _dot(x, w1_ref[:, cols])
            prefetch(3 * c + 1)
            b = _dot(x, w3_ref[:, cols])
            prefetch(3 * c + 2)
            part = _dot((a * _sigmoid(a) * b).astype(BF16), w2_ref[cols, :])
            y = part if y is None else y + part
        _store_token_tiles(o_ref, y, tm)

    @pl.when(jnp.logical_not(used))
    def _():
        o_ref[...] = jnp.zeros_like(o_ref)


def _moe_ffn(tile_expert, n_used, row_tok, h_tiles, w1, w3, w2, *, tm, n_chunks):
    n_tiles = row_tok.shape[0] - 1

    def expert_spec(shape):
        return pl.BlockSpec((None,) + shape, lambda i, te, nu: (te[i], 0, 0), pipeline_mode=pl.Buffered(1))

    grid_spec = pltpu.PrefetchScalarGridSpec(
        num_scalar_prefetch=2,
        grid=(n_tiles,),
        in_specs=[
            pl.BlockSpec((None, 1, tm), lambda i, te, nu: (i, 0, 0), memory_space=pltpu.SMEM),
            pl.BlockSpec((None, 1, tm), lambda i, te, nu: (i + 1, 0, 0), memory_space=pltpu.SMEM),
            pl.BlockSpec(memory_space=pl.ANY),
            pl.BlockSpec((None, D_MODEL, D_FF_EXPERT), lambda i, te, nu: (te[i], 0, 0)),
            expert_spec((D_MODEL, D_FF_EXPERT)), expert_spec((D_FF_EXPERT, D_MODEL)),
        ],
        out_specs=pl.BlockSpec((tm * ROW_TILE, LANES), lambda i, te, nu: (i, 0)),
        scratch_shapes=[pltpu.VMEM((tm, D_MODEL), BF16), pltpu.VMEM((2, tm * ROW_TILE, LANES), F32),
                        pltpu.SemaphoreType.DMA((2,))],
    )
    return pl.pallas_call(
        functools.partial(_moe_ffn_kernel, tm=tm, n_chunks=n_chunks),
        out_shape=jax.ShapeDtypeStruct((n_tiles * tm * ROW_TILE, LANES), F32),
        grid_spec=grid_spec,
        compiler_params=pltpu.CompilerParams(dimension_semantics=("arbitrary",), vmem_limit_bytes=VMEM_LIMIT),
        name="moe_ffn",
    )(tile_expert, n_used, row_tok, row_tok, h_tiles, w1, w3, w2)


def _moe_combine_kernel(pos_ref, pos_next_ref, x_ref, route_ref, nrm_ref, y_hbm, o_ref, buf_ref, sem, *, tm):
    i = pl.program_id(0)
    slot = i % 2

    @pl.when(i == 0)
    def _():
        for k in range(2):
            _start_row_gathers(y_hbm, pos_ref, k, 0, tm, buf_ref.at[0, k], sem.at[0, k], inline=False)

    def combine(prefetch):
        for k in range(2):
            _wait_row_gathers(y_hbm, buf_ref.at[slot, k], sem.at[slot, k])
        g1 = route_ref[:, 2:3]
        g2 = route_ref[:, 3:4]
        per_piece = tm // ROW_TILE
        pieces = []
        for s in range(ROW_TILE):
            if prefetch:
                for k in range(2):
                    _start_row_gathers(y_hbm, pos_next_ref, k, s * per_piece, per_piece, buf_ref.at[1 - slot, k],
                                       sem.at[1 - slot, k], inline=True)
            y1 = buf_ref[slot, 0, pl.ds(s, tm, stride=ROW_TILE), :]
            y2 = buf_ref[slot, 1, pl.ds(s, tm, stride=ROW_TILE), :]
            pieces.append(x_ref[:, s * LANES:(s + 1) * LANES] + (g1 * y1 + g2 * y2))
        o_ref[...] = _rms(jnp.concatenate(pieces, axis=1), nrm_ref[...])

    has_next = i + 1 < pl.num_programs(0)
    pl.when(has_next)(lambda: combine(True))
    pl.when(jnp.logical_not(has_next))(lambda: combine(False))


def _moe_combine(pos, x, route, nrm, y_tiles, *, tm):
    t = x.shape[0]
    n = t // tm
    row_spec = pl.BlockSpec((tm, D_MODEL), lambda i: (i, 0))
    return pl.pallas_call(
        functools.partial(_moe_combine_kernel, tm=tm),
        out_shape=jax.ShapeDtypeStruct((t, D_MODEL), F32),
        grid=(n,),
        in_specs=[pl.BlockSpec((None, 2, tm), lambda i: (i, 0, 0), memory_space=pltpu.SMEM),
                  pl.BlockSpec((None, 2, tm), lambda i: (jnp.minimum(i + 1, n - 1), 0, 0), memory_space=pltpu.SMEM),
                  row_spec, pl.BlockSpec((tm, LANES), lambda i: (i, 0)), _const_spec(nrm.shape),
                  pl.BlockSpec(memory_space=pl.ANY)],
        out_specs=row_spec,
        scratch_shapes=[pltpu.VMEM((2, 2, tm * ROW_TILE, LANES), F32), pltpu.SemaphoreType.DMA((2, 2))],
        compiler_params=pltpu.CompilerParams(dimension_semantics=("arbitrary",), vmem_limit_bytes=VMEM_LIMIT),
        name="moe_combine",
    )(pos, pos, x, route, nrm, y_tiles)


def _routing_tables(idx1, idx2, *, tm, n_tiles):
    t = idx1.shape[0]
    e_flat = jnp.concatenate([idx1, idx2])
    onehot = (e_flat[:, None] == jnp.arange(N_EXPERTS, dtype=jnp.int32)[None, :]).astype(jnp.int32)
    csum = jnp.cumsum(onehot, axis=0)
    rank = jnp.sum((csum - onehot) * onehot, axis=1)
    counts = csum[-1]
    tiles_e = (counts + tm - 1) // tm
    tile_end = jnp.cumsum(tiles_e)
    tile_start = tile_end - tiles_e
    pos = jnp.sum(onehot * tile_start[None, :], axis=1) * tm + rank
    n_used = tile_end[-1]
    tile_ids = jnp.arange(n_tiles, dtype=jnp.int32)
    te = jnp.sum((tile_ids[:, None] >= tile_end[None, :]).astype(jnp.int32), axis=1)
    te_last = jnp.sum((n_used - 1 >= tile_end).astype(jnp.int32))
    tile_expert = jnp.where(tile_ids < n_used, te, te_last).astype(jnp.int32)
    tok = jnp.concatenate([jnp.arange(t, dtype=jnp.int32)] * 2)
    row_tok = jnp.zeros((n_tiles * tm,), jnp.int32).at[pos].set(tok, unique_indices=True)
    return pos[:t], pos[t:], row_tok, tile_expert, n_used.reshape(1).astype(jnp.int32)


def _pad_cols(w, n):
    return jnp.pad(w, ((0, 0), (0, n - w.shape[1])))


def kernel(x, even_norm_mix, even_w_in, even_gate_up, even_gate_bias, even_w_s, even_b_s, even_ln_g, even_ln_b,
           even_head_g, even_w_o, even_norm_ffn, even_ffn_w1, even_ffn_w3, even_ffn_w2, odd_norm_mix, odd_w_in,
           odd_forget_bias, odd_q_g, odd_k_g, odd_w_o, odd_norm_ffn, odd_router, odd_exp_w1, odd_exp_w3,
           odd_exp_w2, final_norm):
    batch, seq, d = x.shape
    t = batch * seq
    xt = x.reshape(t, d)
    tm = min(512, seq)

    w_in = even_w_in[0]
    u_w, v_w, q_w, k_w, g_w, vb_w, og_w = jnp.split(w_in, [512, 1024, 1280, 1536, 1552, 2064], axis=1)
    win_e = jnp.concatenate([u_w, v_w, q_w, k_w, vb_w, og_w, _pad_cols(g_w, LANES)], axis=1).astype(BF16)
    gup = jnp.pad(even_gate_up[0], ((0, LANES - B_GATE_RANK), (0, 0))).astype(BF16)
    gb = even_gate_bias[0].reshape(1, B_QK_WIDTH)
    tril = jnp.tril(jnp.ones((CHUNK, CHUNK), dtype=bool))
    ws = jnp.where(tril[None], even_w_s[0], 0.0).astype(BF16)
    bs = jnp.broadcast_to(even_b_s[0][:, :, None], (A_GROUPS, CHUNK, LANES))
    lng = even_ln_g[0].reshape(1, A_WIDTH)
    lnb = even_ln_b[0].reshape(1, A_WIDTH)
    hg = even_head_g[0].reshape(B_HEADS, 1, B_VAL_DIM)
    ew1 = odd_exp_w1[0].reshape(N_EXPERTS * D_MODEL, D_FF_EXPERT)
    ew3 = odd_exp_w3[0].reshape(N_EXPERTS * D_MODEL, D_FF_EXPERT)
    ew2 = odd_exp_w2[0].reshape(N_EXPERTS * D_FF_EXPERT, D_MODEL)
    x1, ew1_b = _even_mixer(xt, even_norm_mix[0].reshape(1, d), win_e, gup, gb, ws, bs, lng, lnb, hg,
                            even_w_o[0].astype(BF16), ew1, seq=seq, tm=tm)
    x2 = _dense_ffn(x1, even_norm_ffn[0].reshape(1, d), even_ffn_w1[0].astype(BF16), even_ffn_w3[0].astype(BF16),
                    even_ffn_w2[0].astype(BF16), tm=tm)

    q_w, k_w, v_w, og_w, f_w = jnp.split(odd_w_in[0], [1024, 2048, 3072, 4096], axis=1)
    win_o = jnp.concatenate([q_w, k_w, v_w, og_w, _pad_cols(f_w, LANES)], axis=1).astype(BF16)
    fb = jnp.pad(odd_forget_bias[0], (0, LANES - C_HEADS)).reshape(1, LANES)
    qg = jnp.tile(odd_q_g[0], C_HEADS).reshape(1, C_WIDTH)
    kg = jnp.tile(odd_k_g[0], C_HEADS).reshape(1, C_WIDTH)
    tk = min(256, seq)
    tq = min(1024, seq)
    q, k_ext, vt, gate = _odd_inproj(x2, odd_norm_mix[0].reshape(1, d), win_o, fb, qg, kg, seq=seq, tm=tm, tk=tk)
    attn, ew2_b = _fox_attention(q, k_ext, vt, gate, ew2, batch=batch, seq=seq, tq=tq, tk=tk)

    router = _pad_cols(odd_router[0], LANES)
    x3, h_tiles, route, ew3_b = _odd_out(x2, attn, odd_w_o[0].astype(BF16), odd_norm_ffn[0].reshape(1, d), router,
                                         ew3, tm=tm)

    tm_moe = 512
    tm_comb = min(256, seq)
    n_tiles = (2 * t) // tm_moe + N_EXPERTS + 1
    idx1 = route[:, 0].astype(jnp.int32)
    idx2 = route[:, 1].astype(jnp.int32)
    pos1, pos2, row_tok, tile_expert, n_used = _routing_tables(idx1, idx2, tm=tm_moe, n_tiles=n_tiles + 1)
    y_tiles = _moe_ffn(tile_expert[:n_tiles], n_used, row_tok.reshape(n_tiles + 1, 1, tm_moe), h_tiles,
                       ew1_b.reshape(N_EXPERTS, D_MODEL, D_FF_EXPERT), ew3_b.reshape(N_EXPERTS, D_MODEL, D_FF_EXPERT),
                       ew2_b.reshape(N_EXPERTS, D_FF_EXPERT, D_MODEL), tm=tm_moe, n_chunks=2)
    pos = jnp.stack([pos1.reshape(t // tm_comb, tm_comb), pos2.reshape(t // tm_comb, tm_comb)], axis=1)
    out = _moe_combine(pos, x3, route, final_norm.reshape(1, d), y_tiles, tm=tm_comb)
    return out.reshape(batch, seq, d)
```

```python
import functools
import math

import jax
import jax.numpy as jnp
from jax import lax
from jax.experimental import pallas as pl
from jax.experimental.pallas import tpu as pltpu

F32 = jnp.float32
BF16 = jnp.bfloat16
HIGHEST = lax.Precision.HIGHEST

EPS = 1e-6
D_MODEL = 1024
CHUNK = 128
SUB = 32
N_SUB = CHUNK // SUB
A_GROUPS = 4
A_WIDTH = 512
B_HEADS = 4
B_KEY_DIM = 64
B_VAL_DIM = 128
B_QK_WIDTH = 256
B_V_WIDTH = 512
B_GATE_RANK = 16
B_GATE_NORMALIZER = 16.0
C_HEADS = 16
C_HEAD_DIM = 64
C_WIDTH = 1024
D_FF_DENSE = 2816
N_EXPERTS = 8
D_FF_EXPERT = 3584
LANES = 128
MAX_DECAY_EXP = 60.0
LOG2E = math.log2(math.e)
ROW_TILE = 8

E_U, E_V, E_Q, E_K, E_VB, E_OG, E_G, E_END = 0, 512, 1024, 1280, 1536, 2048, 2560, 2688
O_Q, O_K, O_V, O_OG, O_F, O_END = 0, 1024, 2048, 3072, 4096, 4224

VMEM_LIMIT = 56 * 1024 * 1024


def _rms(x, g):
    ms = jnp.mean(x * x, axis=-1, keepdims=True)
    return x * lax.rsqrt(ms + EPS) * g


def _gelu_tanh(x):
    c = math.sqrt(2.0 / math.pi)
    return x * (0.5 * (1.0 + jnp.tanh(c * (x + 0.044715 * (x * x * x)))))


def _sigmoid(x):
    return 1.0 / (1.0 + jnp.exp(-x))


def _log_sigmoid(x):
    return jnp.minimum(x, 0.0) - jnp.log(1.0 + jnp.exp(-jnp.abs(x)))


def _dot(a, b):
    return jnp.dot(a, b, preferred_element_type=F32)


def _dot_nt(a, b):
    return lax.dot_general(a, b, (((1,), (1,)), ((), ())), preferred_element_type=F32)


def _split3(x):
    hi = x.astype(BF16)
    r1 = x - hi.astype(F32)
    mid = r1.astype(BF16)
    lo = (r1 - mid.astype(F32)).astype(BF16)
    return hi, mid, lo


def _cumsum_rows(tril_b, x):
    hi, mid, lo = _split3(x)
    return _dot(tril_b, hi) + _dot(tril_b, mid) + _dot(tril_b, lo)


def _const_spec(shape):
    nd = len(shape)
    return pl.BlockSpec(shape, lambda *_: (0,) * nd)


def _side_cast_specs(w2d, n_steps, index_map):
    rows = w2d.shape[0] // n_steps
    assert rows * n_steps == w2d.shape[0] and rows % 16 == 0
    spec = pl.BlockSpec((rows, w2d.shape[1]), index_map)
    return spec, spec, jax.ShapeDtypeStruct(w2d.shape, BF16)


def _side_cast_specs_1d(w2d, n_steps):
    span = 1 if (w2d.shape[0] // n_steps) % 16 == 0 and w2d.shape[0] % n_steps == 0 else 2
    return _side_cast_specs(w2d, n_steps // span, lambda i: (i // span, 0))


def _even_mixer_kernel(x_ref, nrm_ref, win_ref, gup_ref, gb_ref, ws_ref, bs_ref, lng_ref, lnb_ref,
                       hg_ref, wo_ref, *rest, tiles_per_batch, n_chunks, n_casts):
    cast_refs, o_ref, cast_out_refs = rest[:n_casts], rest[n_casts], rest[n_casts + 1:2 * n_casts + 1]
    z_ref, mix_ref, st_ref = rest[2 * n_casts + 1:]
    i = pl.program_id(0)
    for src, dst in zip(cast_refs, cast_out_refs):
        dst[...] = src[...].astype(BF16)

    @pl.when(i % tiles_per_batch == 0)
    def _():
        st_ref[...] = jnp.zeros_like(st_ref)

    h = _rms(x_ref[...], nrm_ref[...]).astype(BF16)
    z_ref[...] = _dot(h, win_ref[...])

    row = lax.broadcasted_iota(jnp.int32, (CHUNK, CHUNK), 0)
    col = lax.broadcasted_iota(jnp.int32, (CHUNK, CHUNK), 1)
    tril_b = (col <= row).astype(BF16)
    sub_row = row & (SUB - 1)
    head_lane = lax.broadcasted_iota(jnp.int32, (1, B_QK_WIDTH), 1) // B_KEY_DIM
    bd_mask = (lax.broadcasted_iota(jnp.int32, (B_V_WIDTH, B_QK_WIDTH), 0) // B_VAL_DIM
               == lax.broadcasted_iota(jnp.int32, (B_V_WIDTH, B_QK_WIDTH), 1) // B_KEY_DIM)

    def chunk_body(c, carry):
        rows = pl.ds(pl.multiple_of(c * CHUNK, CHUNK), CHUNK)

        u = _gelu_tanh(z_ref[rows, E_U:E_V])
        v = _gelu_tanh(z_ref[rows, E_V:E_Q])
        mu = jnp.mean(v, axis=-1, keepdims=True)
        vc = v - mu
        var = jnp.mean(vc * vc, axis=-1, keepdims=True)
        vln = (vc * lax.rsqrt(var + EPS) * lng_ref[...] + lnb_ref[...]).astype(BF16)
        for g in range(A_GROUPS):
            sl = slice(g * LANES, (g + 1) * LANES)
            mixed = _dot(ws_ref[g], vln[:, sl]) + bs_ref[g]
            mix_ref[rows, sl] = (u[:, sl] * mixed).astype(BF16)

        q = z_ref[rows, E_Q:E_K] * (B_KEY_DIM ** -0.5)
        k = z_ref[rows, E_K:E_VB]
        vb = z_ref[rows, E_VB:E_OG]
        og = z_ref[rows, E_OG:E_G]
        glr = z_ref[rows, E_G:E_END].astype(BF16)
        logit = _dot(glr, gup_ref[...]) + gb_ref[...]
        log_a = _log_sigmoid(logit) * (1.0 / B_GATE_NORMALIZER)
        g_cum = _cumsum_rows(tril_b, log_a)
        g_last = g_cum[CHUNK - 1:CHUNK, :]
        st = st_ref[...]
        o = _dot_nt((q * jnp.exp(g_cum)).astype(BF16), st.astype(BF16))

        p_rows = [[None] * N_SUB for _ in range(B_HEADS)]
        for s in range(N_SUB):
            gs = g_cum[s * SUB:(s + 1) * SUB, :]
            if s == 0:
                qt = q[0:SUB, :] * jnp.exp(gs)
                kt = k * jnp.exp(jnp.minimum(-g_cum, MAX_DECAY_EXP))
            else:
                ref_g = g_cum[s * SUB - 1:s * SUB, :]
                qt = q[s * SUB:(s + 1) * SUB, :] * jnp.exp(gs - ref_g)
                kt = k * jnp.exp(jnp.minimum(ref_g - g_cum, MAX_DECAY_EXP))
            qs = jnp.concatenate([jnp.where(head_lane == hh, qt, 0.0) for hh in range(B_HEADS)],
                                 axis=0).astype(BF16)
            sc = _dot_nt(qs, kt.astype(BF16))
            sc = jnp.where(col <= (s * SUB + sub_row), sc, 0.0)
            for hh in range(B_HEADS):
                p_rows[hh][s] = sc[hh * SUB:(hh + 1) * SUB, :]

        vb_b = vb.astype(BF16)
        for hh in range(B_HEADS):
            sl = slice(hh * B_VAL_DIM, (hh + 1) * B_VAL_DIM)
            ph = jnp.concatenate(p_rows[hh], axis=0).astype(BF16)
            oh = o[:, sl] + _dot(ph, vb_b[:, sl])
            on = _rms(oh, hg_ref[hh])
            ogh = og[:, sl]
            mix_ref[rows, A_WIDTH + hh * B_VAL_DIM:A_WIDTH + (hh + 1) * B_VAL_DIM] = (
                on * (ogh * _sigmoid(ogh))).astype(BF16)

        k_dec = (k * jnp.exp(g_last - g_cum)).astype(BF16)
        upd = _dot(vb.T.astype(BF16), k_dec)
        st_ref[...] = jnp.exp(g_last) * st + jnp.where(bd_mask, upd, 0.0)
        return carry

    lax.fori_loop(0, n_chunks, chunk_body, 0, unroll=True)
    o_ref[...] = x_ref[...] + _dot(mix_ref[...], wo_ref[...])


def _even_mixer(x, nrm, win, gup, gb, ws, bs, lng, lnb, hg, wo, wcasts, *, seq, tm):
    t = x.shape[0]
    kern = functools.partial(_even_mixer_kernel, tiles_per_batch=seq // tm, n_chunks=tm // CHUNK,
                             n_casts=len(wcasts))
    casts = [_side_cast_specs_1d(w, t // tm) for w in wcasts]
    return pl.pallas_call(
        kern,
        out_shape=[jax.ShapeDtypeStruct((t, D_MODEL), F32)] + [c[2] for c in casts],
        grid=(t // tm,),
        in_specs=[
            pl.BlockSpec((tm, D_MODEL), lambda i: (i, 0)),
            _const_spec(nrm.shape), _resident_spec(win.shape), _const_spec(gup.shape), _const_spec(gb.shape),
            _const_spec(ws.shape), _const_spec(bs.shape), _const_spec(lng.shape), _const_spec(lnb.shape),
            _const_spec(hg.shape), _resident_spec(wo.shape),
        ] + [c[0] for c in casts],
        out_specs=[pl.BlockSpec((tm, D_MODEL), lambda i: (i, 0))] + [c[1] for c in casts],
        scratch_shapes=[
            pltpu.VMEM((tm, E_END), F32),
            pltpu.VMEM((tm, D_MODEL), BF16),
            pltpu.VMEM((B_V_WIDTH, B_QK_WIDTH), F32),
        ],
        compiler_params=pltpu.CompilerParams(dimension_semantics=("arbitrary",), vmem_limit_bytes=VMEM_LIMIT),
        name="even_mixer",
    )(x, nrm, win, gup, gb, ws, bs, lng, lnb, hg, wo, *wcasts)


def _dense_ffn_kernel(x_ref, nrm_ref, w1_ref, w3_ref, w2_ref, wcast_ref, o_ref, wcast_out_ref):
    wcast_out_ref[...] = wcast_ref[:, :wcast_out_ref.shape[1]].astype(BF16)
    x = x_ref[...]
    h = _rms(x, nrm_ref[...]).astype(BF16)
    a = _dot(h, w1_ref[...])
    b = _dot(h, w3_ref[...])
    o_ref[...] = x + _dot((a * _sigmoid(a) * b).astype(BF16), w2_ref[...])


def _resident_spec(shape):
    nd = len(shape)
    return pl.BlockSpec(shape, lambda *_: (0,) * nd, pipeline_mode=pl.Buffered(1))


def _dense_ffn(x, nrm, w1, w3, w2, wcast, wcast_cols, *, tm):
    t = x.shape[0]
    n_steps = t // tm
    row_spec = pl.BlockSpec((tm, D_MODEL), lambda i: (i, 0))
    rows = wcast.shape[0] // n_steps
    assert rows * n_steps == wcast.shape[0] and rows % 16 == 0
    return pl.pallas_call(
        _dense_ffn_kernel,
        out_shape=[jax.ShapeDtypeStruct((t, D_MODEL), F32), jax.ShapeDtypeStruct((wcast.shape[0], wcast_cols), BF16)],
        grid=(n_steps,),
        in_specs=[row_spec, _const_spec(nrm.shape), _resident_spec(w1.shape), _resident_spec(w3.shape),
                  _resident_spec(w2.shape), pl.BlockSpec((rows, wcast.shape[1]), lambda i: (i, 0))],
        out_specs=[row_spec, pl.BlockSpec((rows, wcast_cols), lambda i: (i, 0))],
        compiler_params=pltpu.CompilerParams(dimension_semantics=("arbitrary",), vmem_limit_bytes=VMEM_LIMIT),
        name="dense_ffn",
    )(x, nrm, w1, w3, w2, wcast)


def _head_rms(x, gain):
    lo = lax.broadcasted_iota(jnp.int32, (1, LANES), 1) < C_HEAD_DIM
    outs = []
    for t in range(C_WIDTH // LANES):
        xt = x[:, t * LANES:(t + 1) * LANES]
        sq = xt * xt
        s_lo = jnp.sum(jnp.where(lo, sq, 0.0), axis=-1, keepdims=True)
        s_hi = jnp.sum(jnp.where(lo, 0.0, sq), axis=-1, keepdims=True)
        inv = jnp.where(lo, lax.rsqrt(s_lo * (1.0 / C_HEAD_DIM) + EPS), lax.rsqrt(s_hi * (1.0 / C_HEAD_DIM) + EPS))
        outs.append(xt * inv)
    return jnp.concatenate(outs, axis=-1) * gain


N_PAIRS = C_HEADS // 2
K_EXT = 2 * LANES
BIAS_PARTS = 3


def _bias_placement():
    src = jnp.arange(BIAS_PARTS * LANES)
    part, head = src // LANES, src % LANES
    dst = (head // 2) * LANES + BIAS_PARTS * (head % 2) + part
    hit = (dst[:, None] == jnp.arange(N_PAIRS * LANES)[None, :]) & (head < C_HEADS)[:, None]
    return hit.astype(BF16)


def _odd_inproj_kernel(x_ref, nrm_ref, w_ref, wf_ref, fb_ref, qg_ref, kg_ref, place_ref,
                       q_ref, k_ref, vt_ref, gate_ref, z_ref, c_ref, carry_ref, *, tiles_per_batch, n_chunks, tk):
    i = pl.program_id(0)

    @pl.when(i % tiles_per_batch == 0)
    def _():
        carry_ref[...] = jnp.zeros_like(carry_ref)

    h = _rms(x_ref[...], nrm_ref[...]).astype(BF16)
    z_ref[:, :O_F] = _dot(h, w_ref[...])
    z_ref[:, O_F:] = _dot(h, wf_ref[...])
    q_ref[...] = (_head_rms(z_ref[:, O_Q:O_K], qg_ref[...]) * (C_HEAD_DIM ** -0.5 * LOG2E)).astype(BF16)
    kn = _head_rms(z_ref[:, O_K:O_V], kg_ref[...]).astype(BF16)
    for p in range(N_PAIRS):
        for kb in range(vt_ref.shape[1]):
            blk = z_ref[kb * tk:(kb + 1) * tk, O_V + p * LANES:O_V + (p + 1) * LANES]
            vt_ref[p, kb] = blk.T.astype(BF16)
    gate_ref[...] = _sigmoid(z_ref[:, O_OG:O_F]).astype(BF16)

    row = lax.broadcasted_iota(jnp.int32, (CHUNK, CHUNK), 0)
    col = lax.broadcasted_iota(jnp.int32, (CHUNK, CHUNK), 1)
    tril_b = (col <= row).astype(BF16)
    carry = carry_ref[...]
    for c in range(n_chunks):
        rows = slice(c * CHUNK, (c + 1) * CHUNK)
        log_f = _log_sigmoid(z_ref[rows, O_F:O_END] + fb_ref[...])
        cs = _cumsum_rows(tril_b, log_f) + carry
        c_ref[rows, :] = cs
        carry = cs[CHUNK - 1:CHUNK, :]
    carry_ref[...] = carry

    bias = _dot(jnp.concatenate(_split3(c_ref[...] * LOG2E), axis=1), place_ref[...]).astype(BF16)
    for p in range(N_PAIRS):
        k_ref[:, p * K_EXT:p * K_EXT + LANES] = kn[:, p * LANES:(p + 1) * LANES]
        k_ref[:, p * K_EXT + LANES:(p + 1) * K_EXT] = bias[:, p * LANES:(p + 1) * LANES]


def _odd_inproj(x, nrm, w, wf, fb, qg, kg, *, seq, tm, tk):
    t = x.shape[0]
    tiles_per_batch = seq // tm
    kern = functools.partial(_odd_inproj_kernel, tiles_per_batch=tiles_per_batch, n_chunks=tm // CHUNK, tk=tk)
    row_spec = pl.BlockSpec((tm, C_WIDTH), lambda i: (i, 0))
    kext_spec = pl.BlockSpec((tm, N_PAIRS * K_EXT), lambda i: (i, 0))
    vt_spec = pl.BlockSpec((None, N_PAIRS, tm // tk, LANES, tk),
                           lambda i: (i // tiles_per_batch, 0, i % tiles_per_batch, 0, 0))
    place = _bias_placement()
    wide = jax.ShapeDtypeStruct((t, C_WIDTH), BF16)
    return pl.pallas_call(
        kern,
        out_shape=[wide, jax.ShapeDtypeStruct((t, N_PAIRS * K_EXT), BF16),
                   jax.ShapeDtypeStruct((t // seq, N_PAIRS, seq // tk, LANES, tk), BF16), wide],
        grid=(t // tm,),
        in_specs=[row_spec, _const_spec(nrm.shape), _resident_spec(w.shape), _const_spec(wf.shape),
                  _const_spec(fb.shape), _const_spec(qg.shape), _const_spec(kg.shape), _const_spec(place.shape)],
        out_specs=[row_spec, kext_spec, vt_spec, row_spec],
        scratch_shapes=[pltpu.VMEM((tm, O_END), F32), pltpu.VMEM((tm, LANES), F32), pltpu.VMEM((1, LANES), F32)],
        compiler_params=pltpu.CompilerParams(dimension_semantics=("arbitrary",), vmem_limit_bytes=VMEM_LIMIT),
        name="odd_inproj",
    )(x, nrm, w, wf, fb, qg, kg, place)


NEG_BIG = -1e30


V_ROWS = 80


def _fox_kernel(q_ref, k_ref, vt_ref, gate_ref, wcast_ref, o_ref, wcast_out_ref, s_ref, *, tq, tk):
    wcast_out_ref[...] = wcast_ref[...].astype(BF16)
    qi = pl.program_id(2)
    ng = tq // tk
    key_i = lax.broadcasted_iota(jnp.int32, (tk, tk), 0)
    qry_i = lax.broadcasted_iota(jnp.int32, (tk, tk), 1)
    causal = key_i <= qry_i
    chains = [(hh, r) for r in range(ng) for hh in range(2)]
    feat = lax.broadcasted_iota(jnp.int32, (LANES, 1), 0)
    qms = []
    for hh, r in chains:
        q_t = q_ref[r * tk:(r + 1) * tk, :].astype(F32).T
        own = (feat < C_HEAD_DIM) if hh == 0 else (feat >= C_HEAD_DIM)
        qh = jnp.where(own, q_t, 0.0).astype(BF16)
        pick = (feat >= BIAS_PARTS * hh) & (feat < BIAS_PARTS * (hh + 1))
        minus_one = jnp.broadcast_to(jnp.where(pick, -1.0, 0.0).astype(BF16), qh.shape)
        qms.append(jnp.concatenate([qh, minus_one], axis=0))

    def key_rows(j):
        return pl.ds(pl.multiple_of(j * tk, tk), tk)

    def scores_to_scratch(j, slot, live, modes):
        kb = k_ref[key_rows(j), :]
        raw = [_dot(kb, qms[idx]) for idx in live]
        maxes = []
        for s, idx in zip(raw, live):
            hh, r = chains[idx]
            if modes[r] == "diag":
                s = jnp.where(causal, s, NEG_BIG)
            s_ref[slot, idx] = s
            maxes.append(jnp.max(s, axis=0, keepdims=True))
        return maxes

    ones_rows = (lax.broadcasted_iota(jnp.int32, (V_ROWS - C_HEAD_DIM, tk), 0) == 0).astype(BF16)

    def softmax_pv(j, slot, maxes, live, state):
        vt_pair = vt_ref[j]
        vtb = [jnp.concatenate([vt_pair[hh * C_HEAD_DIM:(hh + 1) * C_HEAD_DIM, :], ones_rows], axis=0)
               for hh in range(2)]
        new = list(state)
        probs = []
        for bm, idx in zip(maxes, live):
            m = state[2 * idx]
            m_new = jnp.maximum(m, bm)
            p = jnp.exp2(s_ref[slot, idx] - m_new)
            new[2 * idx] = m_new
            probs.append((jnp.exp2(m - m_new), p.astype(BF16)))
        for (alpha, p), idx in zip(probs, live):
            hh, r = chains[idx]
            new[2 * idx + 1] = alpha * state[2 * idx + 1] + _dot(vtb[hh], p)
        return new

    assert ng % 2 == 0
    all_chains = list(range(len(chains)))
    n_state = 2 * len(chains)
    state = []
    for _ in chains:
        state += [jnp.full((1, tk), NEG_BIG, F32), jnp.zeros((V_ROWS, tk), F32)]
    n_full = qi * ng
    full_modes = ("full",) * ng

    def diag_modes(g):
        return tuple("skip" if r < g else ("diag" if r == g else "full") for r in range(ng))

    def live_chains(g):
        return [idx for idx, (hh, r) in enumerate(chains) if r >= g]

    def two_blocks(j, carry, next_modes):
        st, mx0 = list(carry[:n_state]), carry[n_state:]
        mx1 = scores_to_scratch(j + 1, 1, all_chains, full_modes)
        st = softmax_pv(j, 0, mx0, all_chains, st)
        mx0 = scores_to_scratch(j + 2, 0, all_chains, next_modes)
        st = softmax_pv(j + 1, 1, mx1, all_chains, st)
        return tuple(st) + tuple(mx0)

    def with_full_blocks(_):
        first = scores_to_scratch(0, 0, all_chains, full_modes)
        carry = lax.fori_loop(0, n_full // 2 - 1, lambda i, c: two_blocks(2 * i, c, full_modes),
                              tuple(state) + tuple(first))
        return two_blocks(n_full - 2, carry, diag_modes(0))

    def no_full_blocks(_):
        return tuple(state) + tuple(scores_to_scratch(0, 0, all_chains, diag_modes(0)))

    carry = lax.cond(qi > 0, with_full_blocks, no_full_blocks, 0)
    state, mx = list(carry[:n_state]), carry[n_state:]
    for g in range(ng):
        if g + 1 < ng:
            mx_next = scores_to_scratch(n_full + g + 1, (g + 1) % 2, live_chains(g + 1), diag_modes(g + 1))
        state = softmax_pv(n_full + g, g % 2, mx, live_chains(g), state)
        if g + 1 < ng:
            mx = mx_next

    for r in range(ng):
        parts = []
        for hh in range(2):
            acc = state[2 * chains.index((hh, r)) + 1]
            parts.append(acc[:C_HEAD_DIM, :] / acc[C_HEAD_DIM:C_HEAD_DIM + 1, :])
        o = jnp.concatenate(parts, axis=0).T
        rows = slice(r * tk, (r + 1) * tk)
        o_ref[rows, :] = (o * gate_ref[rows, :].astype(F32)).astype(BF16)


def _fox_attention(q, k_ext, vt, gate, wcast, *, batch, seq, tq, tk):
    t = q.shape[0]
    nq = seq // tq
    kern = functools.partial(_fox_kernel, tq=tq, tk=tk)
    cast_in, cast_out, cast_shape = _side_cast_specs(wcast, batch * N_PAIRS * nq,
                                                     lambda b, p, i: ((b * N_PAIRS + p) * nq + i, 0))
    return pl.pallas_call(
        kern,
        out_shape=[jax.ShapeDtypeStruct((t, C_WIDTH), BF16), cast_shape],
        grid=(batch, N_PAIRS, nq),
        in_specs=[
            pl.BlockSpec((tq, LANES), lambda b, p, i: (b * nq + i, p)),
            pl.BlockSpec((seq, K_EXT), lambda b, p, i: (b, p)),
            pl.BlockSpec((None, None, seq // tk, LANES, tk), lambda b, p, i: (b, p, 0, 0, 0)),
            pl.BlockSpec((tq, LANES), lambda b, p, i: (b * nq + i, p)),
            cast_in,
        ],
        out_specs=[pl.BlockSpec((tq, LANES), lambda b, p, i: (b * nq + i, p)), cast_out],
        scratch_shapes=[pltpu.VMEM((2, 2 * (tq // tk), tk, tk), F32)],
        compiler_params=pltpu.CompilerParams(dimension_semantics=("arbitrary", "arbitrary", "arbitrary"),
                                             vmem_limit_bytes=VMEM_LIMIT),
        name="fox_attn",
    )(q, k_ext, vt, gate, wcast)


def _store_token_tiles(dst_ref, val, n_rows):
    for s in range(ROW_TILE):
        dst_ref[pl.ds(s, n_rows, stride=ROW_TILE), :] = val[:, s * LANES:(s + 1) * LANES]


def _load_token_tiles(src_ref, n_rows):
    return jnp.concatenate([src_ref[pl.ds(s, n_rows, stride=ROW_TILE), :] for s in range(ROW_TILE)], axis=1)


def _odd_out_kernel(x_ref, a_ref, wo_ref, nrm_ref, r_ref, wcast_ref, x3_ref, h_ref, route_ref, wcast_out_ref):
    wcast_out_ref[...] = wcast_ref[...].astype(BF16)
    x3 = x_ref[...] + _dot(a_ref[...], wo_ref[...])
    x3_ref[...] = x3
    h = _rms(x3, nrm_ref[...])
    _store_token_tiles(h_ref, h, h.shape[0])

    h_hi = h.astype(BF16)
    h_lo = (h - h_hi.astype(F32)).astype(BF16)
    r = r_ref[...]
    r_hi = r.astype(BF16)
    r_lo = (r - r_hi.astype(F32)).astype(BF16)
    logits = _dot(h_hi, r_hi) + (_dot(h_lo, r_hi) + _dot(h_hi, r_lo))

    lane = lax.broadcasted_iota(jnp.int32, logits.shape, 1).astype(F32)
    neg_inf = jnp.float32(-jnp.inf)
    lg = jnp.where(lane < N_EXPERTS, logits, neg_inf)
    m1 = jnp.max(lg, axis=-1, keepdims=True)
    i1 = jnp.min(jnp.where(lg == m1, lane, float(LANES)), axis=-1, keepdims=True)
    lg2 = jnp.where(lane == i1, neg_inf, lg)
    m2 = jnp.max(lg2, axis=-1, keepdims=True)
    i2 = jnp.min(jnp.where(lg2 == m2, lane, float(LANES)), axis=-1, keepdims=True)
    e2 = jnp.exp(m2 - m1)
    g1 = 1.0 / (1.0 + e2)
    g2 = e2 / (1.0 + e2)
    route_ref[...] = jnp.where(lane == 0, i1, jnp.where(lane == 1, i2, jnp.where(lane == 2, g1,
                               jnp.where(lane == 3, g2, 0.0))))


def _odd_out(x, a, wo, nrm, router, wcast, *, tm):
    t = x.shape[0]
    row_spec = pl.BlockSpec((tm, D_MODEL), lambda i: (i, 0))
    cast_in, cast_out, cast_shape = _side_cast_specs(wcast, t // tm, lambda i: (i, 0))
    return pl.pallas_call(
        _odd_out_kernel,
        out_shape=[jax.ShapeDtypeStruct((t, D_MODEL), F32), jax.ShapeDtypeStruct((t * ROW_TILE, LANES), F32),
                   jax.ShapeDtypeStruct((t, LANES), F32), cast_shape],
        grid=(t // tm,),
        in_specs=[row_spec, row_spec, _const_spec(wo.shape), _const_spec(nrm.shape), _const_spec(router.shape),
                  cast_in],
        out_specs=[row_spec, pl.BlockSpec((tm * ROW_TILE, LANES), lambda i: (i, 0)),
                   pl.BlockSpec((tm, LANES), lambda i: (i, 0)), cast_out],
        compiler_params=pltpu.CompilerParams(dimension_semantics=("arbitrary",), vmem_limit_bytes=VMEM_LIMIT),
        name="odd_out",
    )(x, a, wo, nrm, router, wcast)


def _row_gather_copy(src_hbm, src_row, dst_ref, dst_row, sem):
    return pltpu.make_async_copy(
        src_hbm.at[pl.ds(pl.multiple_of(src_row * ROW_TILE, ROW_TILE), ROW_TILE), :],
        dst_ref.at[pl.ds(pl.multiple_of(dst_row * ROW_TILE, ROW_TILE), ROW_TILE), :],
        sem)


GATHER_UNROLL = 8


def _start_row_gathers(src_hbm, idx_ref, idx_row, first, count, dst_ref, sem, *, inline):
    if inline:
        for u in range(count):
            _row_gather_copy(src_hbm, idx_ref[idx_row, first + u], dst_ref, first + u, sem).start(priority=u % 2)
        return

    def issue(r2, c):
        for u in range(2):
            r = 2 * r2 + u
            _row_gather_copy(src_hbm, idx_ref[idx_row, r], dst_ref, r, sem).start(priority=u)
        return c

    assert first % 2 == 0 and count % 2 == 0
    lax.fori_loop(first // 2, (first + count) // 2, issue, 0, unroll=GATHER_UNROLL // 2)


def _wait_row_gathers(src_hbm, dst_ref, sem):
    pltpu.make_async_copy(src_hbm.at[pl.ds(0, dst_ref.shape[0]), :], dst_ref, sem).wait()


def _moe_ffn_kernel(te_ref, nu_ref, tok_ref, tok_next_ref, h_hbm, w1_ref, w3_ref, w2_ref, o_ref, xs_ref, buf_ref,
                    sem, *, tm, n_chunks):
    i = pl.program_id(0)
    n_used = nu_ref[0]
    used = i < n_used
    slot = i % 2

    @pl.when(i == 0)
    def _():
        _start_row_gathers(h_hbm, tok_ref, 0, 0, tm, buf_ref.at[0], sem.at[0], inline=False)

    @pl.when(i <= n_used)
    def _():
        _wait_row_gathers(h_hbm, buf_ref.at[slot], sem.at[slot])
        xs_ref[...] = _load_token_tiles(buf_ref.at[slot], tm).astype(BF16)

    @pl.when(used)
    def _():
        n_groups = 3 * n_chunks
        per_group = tm // n_groups
        starts = [(g * per_group, per_group if g + 1 < n_groups else tm - g * per_group) for g in range(n_groups)]

        def prefetch(g):
            first, count = starts[g]
            _start_row_gathers(h_hbm, tok_next_ref, 0, first, count, buf_ref.at[1 - slot], sem.at[1 - slot],
                               inline=True)

        x = xs_ref[...]
        tf = D_FF_EXPERT // n_chunks
        y = None
        for c in range(n_chunks):
            cols = slice(c * tf, (c + 1) * tf)
            prefetch(3 * c)
            a = _dot(x, w1_ref[:, cols])
            prefetch(3 * c + 1)
            b = _dot(x, w3_ref[:, cols])
            prefetch(3 * c + 2)
            part = _dot((a * _sigmoid(a) * b).astype(BF16), w2_ref[cols, :])
            y = part if y is None else y + part
        _store_token_tiles(o_ref, y, tm)

    @pl.when(jnp.logical_not(used))
    def _():
        o_ref[...] = jnp.zeros_like(o_ref)


def _moe_ffn(tile_expert, n_used, row_tok, h_tiles, w1, w3, w2, *, tm, n_chunks):
    n_tiles = row_tok.shape[0] - 1

    def expert_spec(shape):
        return pl.BlockSpec((None,) + shape, lambda i, te, nu: (te[i], 0, 0), pipeline_mode=pl.Buffered(1))

    grid_spec = pltpu.PrefetchScalarGridSpec(
        num_scalar_prefetch=2,
        grid=(n_tiles,),
        in_specs=[
            pl.BlockSpec((None, 1, tm), lambda i, te, nu: (i, 0, 0), memory_space=pltpu.SMEM),
            pl.BlockSpec((None, 1, tm), lambda i, te, nu: (i + 1, 0, 0), memory_space=pltpu.SMEM),
            pl.BlockSpec(memory_space=pl.ANY),
            expert_spec((D_MODEL, D_FF_EXPERT)), expert_spec((D_MODEL, D_FF_EXPERT)),
            expert_spec((D_FF_EXPERT, D_MODEL)),
        ],
        out_specs=pl.BlockSpec((tm * ROW_TILE, LANES), lambda i, te, nu: (i, 0)),
        scratch_shapes=[pltpu.VMEM((tm, D_MODEL), BF16), pltpu.VMEM((2, tm * ROW_TILE, LANES), F32),
                        pltpu.SemaphoreType.DMA((2,))],
    )
    return pl.pallas_call(
        functools.partial(_moe_ffn_kernel, tm=tm, n_chunks=n_chunks),
        out_shape=jax.ShapeDtypeStruct((n_tiles * tm * ROW_TILE, LANES), F32),
        grid_spec=grid_spec,
        compiler_params=pltpu.CompilerParams(dimension_semantics=("arbitrary",), vmem_limit_bytes=VMEM_LIMIT),
        name="moe_ffn",
    )(tile_expert, n_used, row_tok, row_tok, h_tiles, w1, w3, w2)


def _moe_combine_kernel(pos_ref, pos_next_ref, x_ref, route_ref, nrm_ref, y_hbm, o_ref, buf_ref, sem, *, tm):
    i = pl.program_id(0)
    slot = i % 2

    def start_tile(idx_ref, s, inline):
        for k in range(2):
            _start_row_gathers(y_hbm, idx_ref, k, 0, tm, buf_ref.at[s, k], sem.at[s, k], inline=inline)

    @pl.when(i == 0)
    def _():
        start_tile(pos_ref, 0, False)

    @pl.when(i + 1 < pl.num_programs(0))
    def _():
        start_tile(pos_next_ref, 1 - slot, True)

    for k in range(2):
        _wait_row_gathers(y_hbm, buf_ref.at[slot, k], sem.at[slot, k])
    g1 = route_ref[:, 2:3]
    g2 = route_ref[:, 3:4]
    x = x_ref[...] + (g1 * _load_token_tiles(buf_ref.at[slot, 0], tm) + g2 * _load_token_tiles(buf_ref.at[slot, 1], tm))
    o_ref[...] = _rms(x, nrm_ref[...])


def _moe_combine(pos, x, route, nrm, y_tiles, *, tm):
    t = x.shape[0]
    n = t // tm
    row_spec = pl.BlockSpec((tm, D_MODEL), lambda i: (i, 0))
    return pl.pallas_call(
        functools.partial(_moe_combine_kernel, tm=tm),
        out_shape=jax.ShapeDtypeStruct((t, D_MODEL), F32),
        grid=(n,),
        in_specs=[pl.BlockSpec((None, 2, tm), lambda i: (i, 0, 0), memory_space=pltpu.SMEM),
                  pl.BlockSpec((None, 2, tm), lambda i: (jnp.minimum(i + 1, n - 1), 0, 0), memory_space=pltpu.SMEM),
                  row_spec, pl.BlockSpec((tm, LANES), lambda i: (i, 0)), _const_spec(nrm.shape),
                  pl.BlockSpec(memory_space=pl.ANY)],
        out_specs=row_spec,
        scratch_shapes=[pltpu.VMEM((2, 2, tm * ROW_TILE, LANES), F32), pltpu.SemaphoreType.DMA((2, 2))],
        compiler_params=pltpu.CompilerParams(dimension_semantics=("arbitrary",), vmem_limit_bytes=VMEM_LIMIT),
        name="moe_combine",
    )(pos, pos, x, route, nrm, y_tiles)


def _routing_tables(idx1, idx2, *, tm, n_tiles):
    t = idx1.shape[0]
    e_flat = jnp.concatenate([idx1, idx2])
    onehot = (e_flat[:, None] == jnp.arange(N_EXPERTS, dtype=jnp.int32)[None, :]).astype(jnp.int32)
    csum = jnp.cumsum(onehot, axis=0)
    rank = jnp.sum((csum - onehot) * onehot, axis=1)
    counts = csum[-1]
    tiles_e = (counts + tm - 1) // tm
    tile_end = jnp.cumsum(tiles_e)
    tile_start = tile_end - tiles_e
    pos = jnp.sum(onehot * tile_start[None, :], axis=1) * tm + rank
    n_used = tile_end[-1]
    tile_ids = jnp.arange(n_tiles, dtype=jnp.int32)
    te = jnp.sum((tile_ids[:, None] >= tile_end[None, :]).astype(jnp.int32), axis=1)
    te_last = jnp.sum((n_used - 1 >= tile_end).astype(jnp.int32))
    tile_expert = jnp.where(tile_ids < n_used, te, te_last).astype(jnp.int32)
    tok = jnp.concatenate([jnp.arange(t, dtype=jnp.int32)] * 2)
    row_tok = jnp.zeros((n_tiles * tm,), jnp.int32).at[pos].set(tok, unique_indices=True)
    return pos[:t], pos[t:], row_tok, tile_expert, n_used.reshape(1).astype(jnp.int32)


def _pad_cols(w, n):
    return jnp.pad(w, ((0, 0), (0, n - w.shape[1])))


def kernel(x, even_norm_mix, even_w_in, even_gate_up, even_gate_bias, even_w_s, even_b_s, even_ln_g, even_ln_b,
           even_head_g, even_w_o, even_norm_ffn, even_ffn_w1, even_ffn_w3, even_ffn_w2, odd_norm_mix, odd_w_in,
           odd_forget_bias, odd_q_g, odd_k_g, odd_w_o, odd_norm_ffn, odd_router, odd_exp_w1, odd_exp_w3,
           odd_exp_w2, final_norm):
    batch, seq, d = x.shape
    t = batch * seq
    xt = x.reshape(t, d)
    tm = min(512, seq)

    w_in = even_w_in[0]
    u_w, v_w, q_w, k_w, g_w, vb_w, og_w = jnp.split(w_in, [512, 1024, 1280, 1536, 1552, 2064], axis=1)
    win_e = jnp.concatenate([u_w, v_w, q_w, k_w, vb_w, og_w, _pad_cols(g_w, LANES)], axis=1).astype(BF16)
    gup = jnp.pad(even_gate_up[0], ((0, LANES - B_GATE_RANK), (0, 0))).astype(BF16)
    gb = even_gate_bias[0].reshape(1, B_QK_WIDTH)
    tril = jnp.tril(jnp.ones((CHUNK, CHUNK), dtype=bool))
    ws = jnp.where(tril[None], even_w_s[0], 0.0).astype(BF16)
    bs = jnp.broadcast_to(even_b_s[0][:, :, None], (A_GROUPS, CHUNK, LANES))
    lng = even_ln_g[0].reshape(1, A_WIDTH)
    lnb = even_ln_b[0].reshape(1, A_WIDTH)
    hg = even_head_g[0].reshape(B_HEADS, 1, B_VAL_DIM)
    ew1 = odd_exp_w1[0].reshape(N_EXPERTS * D_MODEL, D_FF_EXPERT)
    ew3 = odd_exp_w3[0].reshape(N_EXPERTS * D_MODEL, D_FF_EXPERT)
    ew2 = odd_exp_w2[0].reshape(N_EXPERTS * D_FF_EXPERT, D_MODEL)
    x1, ew1_b, fw1_b, fw3_b, fw2_b = _even_mixer(
        xt, even_norm_mix[0].reshape(1, d), win_e, gup, gb, ws, bs, lng, lnb, hg, even_w_o[0].astype(BF16),
        [ew1, even_ffn_w1[0], even_ffn_w3[0], even_ffn_w2[0]], seq=seq, tm=tm)
    x2, win_o = _dense_ffn(x1, even_norm_ffn[0].reshape(1, d), fw1_b, fw3_b, fw2_b, odd_w_in[0], O_F, tm=tm)

    wf_o = _pad_cols(odd_w_in[0][:, O_F:], LANES).astype(BF16)
    fb = jnp.pad(odd_forget_bias[0], (0, LANES - C_HEADS)).reshape(1, LANES)
    qg = jnp.tile(odd_q_g[0], C_HEADS).reshape(1, C_WIDTH)
    kg = jnp.tile(odd_k_g[0], C_HEADS).reshape(1, C_WIDTH)
    tk = min(256, seq)
    tq = min(1024, seq)
    q, k_ext, vt, gate = _odd_inproj(x2, odd_norm_mix[0].reshape(1, d), win_o, wf_o, fb, qg, kg, seq=seq, tm=tm,
                                     tk=tk)
    attn, ew2_b = _fox_attention(q, k_ext, vt, gate, ew2, batch=batch, seq=seq, tq=tq, tk=tk)

    router = _pad_cols(odd_router[0], LANES)
    x3, h_tiles, route, ew3_b = _odd_out(x2, attn, odd_w_o[0].astype(BF16), odd_norm_ffn[0].reshape(1, d), router,
                                         ew3, tm=tm)

    tm_moe = 512
    tm_comb = min(256, seq)
    n_tiles = (2 * t) // tm_moe + N_EXPERTS + 1
    idx1 = route[:, 0].astype(jnp.int32)
    idx2 = route[:, 1].astype(jnp.int32)
    pos1, pos2, row_tok, tile_expert, n_used = _routing_tables(idx1, idx2, tm=tm_moe, n_tiles=n_tiles + 1)
    y_tiles = _moe_ffn(tile_expert[:n_tiles], n_used, row_tok.reshape(n_tiles + 1, 1, tm_moe), h_tiles,
                       ew1_b.reshape(N_EXPERTS, D_MODEL, D_FF_EXPERT), ew3_b.reshape(N_EXPERTS, D_MODEL, D_FF_EXPERT),
                       ew2_b.reshape(N_EXPERTS, D_FF_EXPERT, D_MODEL), tm=tm_moe, n_chunks=2)
    pos = jnp.stack([pos1.reshape(t // tm_comb, tm_comb), pos2.reshape(t // tm_comb, tm_comb)], axis=1)
    out = _moe_combine(pos, x3, route, final_norm.reshape(1, d), y_tiles, tm=tm_comb)
    return out.reshape(batch, seq, d)
```

```python
import functools
import math

import jax
import jax.numpy as jnp
from jax import lax
from jax.experimental import pallas as pl
from jax.experimental.pallas import tpu as pltpu

F32 = jnp.float32
BF16 = jnp.bfloat16
HIGHEST = lax.Precision.HIGHEST

EPS = 1e-6
D_MODEL = 1024
CHUNK = 128
SUB = 32
N_SUB = CHUNK // SUB
A_GROUPS = 4
A_WIDTH = 512
B_HEADS = 4
B_KEY_DIM = 64
B_VAL_DIM = 128
B_QK_WIDTH = 256
B_V_WIDTH = 512
B_GATE_RANK = 16
B_GATE_NORMALIZER = 16.0
C_HEADS = 16
C_HEAD_DIM = 64
C_WIDTH = 1024
D_FF_DENSE = 2816
N_EXPERTS = 8
D_FF_EXPERT = 3584
LANES = 128
MAX_DECAY_EXP = 60.0
LOG2E = math.log2(math.e)
ROW_TILE = 8

E_U, E_V, E_Q, E_K, E_VB, E_OG, E_G, E_END = 0, 512, 1024, 1280, 1536, 2048, 2560, 2688
O_Q, O_K, O_V, O_OG, O_F, O_END = 0, 1024, 2048, 3072, 4096, 4224

VMEM_LIMIT = 56 * 1024 * 1024


def _rms(x, g):
    ms = jnp.mean(x * x, axis=-1, keepdims=True)
    return x * lax.rsqrt(ms + EPS) * g


def _gelu_tanh(x):
    c = math.sqrt(2.0 / math.pi)
    return x * (0.5 * (1.0 + jnp.tanh(c * (x + 0.044715 * (x * x * x)))))


def _sigmoid(x):
    return 1.0 / (1.0 + jnp.exp(-x))


def _log_sigmoid(x):
    return jnp.minimum(x, 0.0) - jnp.log(1.0 + jnp.exp(-jnp.abs(x)))


def _dot(a, b):
    return jnp.dot(a, b, preferred_element_type=F32)


def _dot_nt(a, b):
    return lax.dot_general(a, b, (((1,), (1,)), ((), ())), preferred_element_type=F32)


def _split3(x):
    hi = x.astype(BF16)
    r1 = x - hi.astype(F32)
    mid = r1.astype(BF16)
    lo = (r1 - mid.astype(F32)).astype(BF16)
    return hi, mid, lo


def _cumsum_rows(tril_b, x):
    hi, mid, lo = _split3(x)
    return _dot(tril_b, hi) + _dot(tril_b, mid) + _dot(tril_b, lo)


def _const_spec(shape):
    nd = len(shape)
    return pl.BlockSpec(shape, lambda *_: (0,) * nd)


def _side_cast_specs(w2d, n_steps, index_map):
    rows = w2d.shape[0] // n_steps
    assert rows * n_steps == w2d.shape[0] and rows % 16 == 0
    spec = pl.BlockSpec((rows, w2d.shape[1]), index_map)
    return spec, spec, jax.ShapeDtypeStruct(w2d.shape, BF16)


def _side_cast_specs_1d(w2d, n_steps):
    span = 1 if (w2d.shape[0] // n_steps) % 16 == 0 and w2d.shape[0] % n_steps == 0 else 2
    return _side_cast_specs(w2d, n_steps // span, lambda i: (i // span, 0))


def _even_mixer_kernel(x_ref, nrm_ref, win_ref, gup_ref, gb_ref, ws_ref, bs_ref, lng_ref, lnb_ref,
                       hg_ref, wo_ref, *rest, tiles_per_batch, n_chunks, n_casts):
    cast_refs, o_ref, cast_out_refs = rest[:n_casts], rest[n_casts], rest[n_casts + 1:2 * n_casts + 1]
    z_ref, mix_ref, st_ref = rest[2 * n_casts + 1:]
    i = pl.program_id(0)
    for src, dst in zip(cast_refs, cast_out_refs):
        dst[...] = src[...].astype(BF16)

    @pl.when(i % tiles_per_batch == 0)
    def _():
        st_ref[...] = jnp.zeros_like(st_ref)

    h = _rms(x_ref[...], nrm_ref[...]).astype(BF16)
    z_ref[...] = _dot(h, win_ref[...])

    row = lax.broadcasted_iota(jnp.int32, (CHUNK, CHUNK), 0)
    col = lax.broadcasted_iota(jnp.int32, (CHUNK, CHUNK), 1)
    tril_b = (col <= row).astype(BF16)
    sub_row = row & (SUB - 1)
    head_lane = lax.broadcasted_iota(jnp.int32, (1, B_QK_WIDTH), 1) // B_KEY_DIM
    bd_mask = (lax.broadcasted_iota(jnp.int32, (B_V_WIDTH, B_QK_WIDTH), 0) // B_VAL_DIM
               == lax.broadcasted_iota(jnp.int32, (B_V_WIDTH, B_QK_WIDTH), 1) // B_KEY_DIM)

    def chunk_body(c, carry):
        rows = pl.ds(pl.multiple_of(c * CHUNK, CHUNK), CHUNK)

        u = _gelu_tanh(z_ref[rows, E_U:E_V])
        v = _gelu_tanh(z_ref[rows, E_V:E_Q])
        mu = jnp.mean(v, axis=-1, keepdims=True)
        vc = v - mu
        var = jnp.mean(vc * vc, axis=-1, keepdims=True)
        vln = (vc * lax.rsqrt(var + EPS) * lng_ref[...] + lnb_ref[...]).astype(BF16)
        for g in range(A_GROUPS):
            sl = slice(g * LANES, (g + 1) * LANES)
            mixed = _dot(ws_ref[g], vln[:, sl]) + bs_ref[g]
            mix_ref[rows, sl] = (u[:, sl] * mixed).astype(BF16)

        q = z_ref[rows, E_Q:E_K] * (B_KEY_DIM ** -0.5)
        k = z_ref[rows, E_K:E_VB]
        vb = z_ref[rows, E_VB:E_OG]
        og = z_ref[rows, E_OG:E_G]
        glr = z_ref[rows, E_G:E_END].astype(BF16)
        logit = _dot(glr, gup_ref[...]) + gb_ref[...]
        log_a = _log_sigmoid(logit) * (1.0 / B_GATE_NORMALIZER)
        g_cum = _cumsum_rows(tril_b, log_a)
        g_last = g_cum[CHUNK - 1:CHUNK, :]
        st = st_ref[...]
        o = _dot_nt((q * jnp.exp(g_cum)).astype(BF16), st.astype(BF16))

        p_rows = [[None] * N_SUB for _ in range(B_HEADS)]
        for s in range(N_SUB):
            gs = g_cum[s * SUB:(s + 1) * SUB, :]
            if s == 0:
                qt = q[0:SUB, :] * jnp.exp(gs)
                kt = k * jnp.exp(jnp.minimum(-g_cum, MAX_DECAY_EXP))
            else:
                ref_g = g_cum[s * SUB - 1:s * SUB, :]
                qt = q[s * SUB:(s + 1) * SUB, :] * jnp.exp(gs - ref_g)
                kt = k * jnp.exp(jnp.minimum(ref_g - g_cum, MAX_DECAY_EXP))
            qs = jnp.concatenate([jnp.where(head_lane == hh, qt, 0.0) for hh in range(B_HEADS)],
                                 axis=0).astype(BF16)
            sc = _dot_nt(qs, kt.astype(BF16))
            sc = jnp.where(col <= (s * SUB + sub_row), sc, 0.0)
            for hh in range(B_HEADS):
                p_rows[hh][s] = sc[hh * SUB:(hh + 1) * SUB, :]

        vb_b = vb.astype(BF16)
        for hh in range(B_HEADS):
            sl = slice(hh * B_VAL_DIM, (hh + 1) * B_VAL_DIM)
            ph = jnp.concatenate(p_rows[hh], axis=0).astype(BF16)
            oh = o[:, sl] + _dot(ph, vb_b[:, sl])
            on = _rms(oh, hg_ref[hh])
            ogh = og[:, sl]
            mix_ref[rows, A_WIDTH + hh * B_VAL_DIM:A_WIDTH + (hh + 1) * B_VAL_DIM] = (
                on * (ogh * _sigmoid(ogh))).astype(BF16)

        k_dec = (k * jnp.exp(g_last - g_cum)).astype(BF16)
        upd = _dot(vb.T.astype(BF16), k_dec)
        st_ref[...] = jnp.exp(g_last) * st + jnp.where(bd_mask, upd, 0.0)
        return carry

    lax.fori_loop(0, n_chunks, chunk_body, 0, unroll=True)
    o_ref[...] = x_ref[...] + _dot(mix_ref[...], wo_ref[...])


def _even_mixer(x, nrm, win, gup, gb, ws, bs, lng, lnb, hg, wo, wcasts, *, seq, tm):
    t = x.shape[0]
    kern = functools.partial(_even_mixer_kernel, tiles_per_batch=seq // tm, n_chunks=tm // CHUNK,
                             n_casts=len(wcasts))
    casts = [_side_cast_specs_1d(w, t // tm) for w in wcasts]
    return pl.pallas_call(
        kern,
        out_shape=[jax.ShapeDtypeStruct((t, D_MODEL), F32)] + [c[2] for c in casts],
        grid=(t // tm,),
        in_specs=[
            pl.BlockSpec((tm, D_MODEL), lambda i: (i, 0)),
            _const_spec(nrm.shape), _resident_spec(win.shape), _const_spec(gup.shape), _const_spec(gb.shape),
            _const_spec(ws.shape), _const_spec(bs.shape), _const_spec(lng.shape), _const_spec(lnb.shape),
            _const_spec(hg.shape), _resident_spec(wo.shape),
        ] + [c[0] for c in casts],
        out_specs=[pl.BlockSpec((tm, D_MODEL), lambda i: (i, 0))] + [c[1] for c in casts],
        scratch_shapes=[
            pltpu.VMEM((tm, E_END), F32),
            pltpu.VMEM((tm, D_MODEL), BF16),
            pltpu.VMEM((B_V_WIDTH, B_QK_WIDTH), F32),
        ],
        compiler_params=pltpu.CompilerParams(dimension_semantics=("arbitrary",), vmem_limit_bytes=VMEM_LIMIT),
        name="even_mixer",
    )(x, nrm, win, gup, gb, ws, bs, lng, lnb, hg, wo, *wcasts)


def _dense_ffn_kernel(x_ref, nrm_ref, w1_ref, w3_ref, w2_ref, wcast_ref, o_ref, wcast_out_ref):
    wcast_out_ref[...] = wcast_ref[:, :wcast_out_ref.shape[1]].astype(BF16)
    x = x_ref[...]
    h = _rms(x, nrm_ref[...]).astype(BF16)
    a = _dot(h, w1_ref[...])
    b = _dot(h, w3_ref[...])
    o_ref[...] = x + _dot((a * _sigmoid(a) * b).astype(BF16), w2_ref[...])


def _resident_spec(shape):
    nd = len(shape)
    return pl.BlockSpec(shape, lambda *_: (0,) * nd, pipeline_mode=pl.Buffered(1))


def _dense_ffn(x, nrm, w1, w3, w2, wcast, wcast_cols, *, tm):
    t = x.shape[0]
    n_steps = t // tm
    row_spec = pl.BlockSpec((tm, D_MODEL), lambda i: (i, 0))
    rows = wcast.shape[0] // n_steps
    assert rows * n_steps == wcast.shape[0] and rows % 16 == 0
    return pl.pallas_call(
        _dense_ffn_kernel,
        out_shape=[jax.ShapeDtypeStruct((t, D_MODEL), F32), jax.ShapeDtypeStruct((wcast.shape[0], wcast_cols), BF16)],
        grid=(n_steps,),
        in_specs=[row_spec, _const_spec(nrm.shape), _resident_spec(w1.shape), _resident_spec(w3.shape),
                  _resident_spec(w2.shape), pl.BlockSpec((rows, wcast.shape[1]), lambda i: (i, 0))],
        out_specs=[row_spec, pl.BlockSpec((rows, wcast_cols), lambda i: (i, 0))],
        compiler_params=pltpu.CompilerParams(dimension_semantics=("arbitrary",), vmem_limit_bytes=VMEM_LIMIT),
        name="dense_ffn",
    )(x, nrm, w1, w3, w2, wcast)


def _head_rms(x, gain):
    lo = lax.broadcasted_iota(jnp.int32, (1, LANES), 1) < C_HEAD_DIM
    outs = []
    for t in range(C_WIDTH // LANES):
        xt = x[:, t * LANES:(t + 1) * LANES]
        sq = xt * xt
        s_lo = jnp.sum(jnp.where(lo, sq, 0.0), axis=-1, keepdims=True)
        s_hi = jnp.sum(jnp.where(lo, 0.0, sq), axis=-1, keepdims=True)
        inv = jnp.where(lo, lax.rsqrt(s_lo * (1.0 / C_HEAD_DIM) + EPS), lax.rsqrt(s_hi * (1.0 / C_HEAD_DIM) + EPS))
        outs.append(xt * inv)
    return jnp.concatenate(outs, axis=-1) * gain


N_PAIRS = C_HEADS // 2
K_EXT = 2 * LANES
BIAS_PARTS = 3


def _bias_placement():
    src = jnp.arange(BIAS_PARTS * LANES)
    part, head = src // LANES, src % LANES
    dst = (head // 2) * LANES + BIAS_PARTS * (head % 2) + part
    hit = (dst[:, None] == jnp.arange(N_PAIRS * LANES)[None, :]) & (head < C_HEADS)[:, None]
    return hit.astype(BF16)


def _odd_inproj_kernel(x_ref, nrm_ref, w_ref, wf_ref, fb_ref, qg_ref, kg_ref, place_ref,
                       q_ref, k_ref, vt_ref, gate_ref, z_ref, c_ref, carry_ref, *, tiles_per_batch, n_chunks, tk):
    i = pl.program_id(0)

    @pl.when(i % tiles_per_batch == 0)
    def _():
        carry_ref[...] = jnp.zeros_like(carry_ref)

    h = _rms(x_ref[...], nrm_ref[...]).astype(BF16)
    z_ref[:, :O_F] = _dot(h, w_ref[...])
    z_ref[:, O_F:] = _dot(h, wf_ref[...])
    q_ref[...] = (_head_rms(z_ref[:, O_Q:O_K], qg_ref[...]) * (C_HEAD_DIM ** -0.5 * LOG2E)).astype(BF16)
    kn = _head_rms(z_ref[:, O_K:O_V], kg_ref[...]).astype(BF16)
    for p in range(N_PAIRS):
        for kb in range(vt_ref.shape[1]):
            blk = z_ref[kb * tk:(kb + 1) * tk, O_V + p * LANES:O_V + (p + 1) * LANES]
            vt_ref[p, kb] = blk.T.astype(BF16)
    gate_ref[...] = _sigmoid(z_ref[:, O_OG:O_F]).astype(BF16)

    row = lax.broadcasted_iota(jnp.int32, (CHUNK, CHUNK), 0)
    col = lax.broadcasted_iota(jnp.int32, (CHUNK, CHUNK), 1)
    tril_b = (col <= row).astype(BF16)
    carry = carry_ref[...]
    for c in range(n_chunks):
        rows = slice(c * CHUNK, (c + 1) * CHUNK)
        log_f = _log_sigmoid(z_ref[rows, O_F:O_END] + fb_ref[...])
        cs = _cumsum_rows(tril_b, log_f) + carry
        c_ref[rows, :] = cs
        carry = cs[CHUNK - 1:CHUNK, :]
    carry_ref[...] = carry

    bias = _dot(jnp.concatenate(_split3(c_ref[...] * LOG2E), axis=1), place_ref[...]).astype(BF16)
    for p in range(N_PAIRS):
        k_ref[:, p * K_EXT:p * K_EXT + LANES] = kn[:, p * LANES:(p + 1) * LANES]
        k_ref[:, p * K_EXT + LANES:(p + 1) * K_EXT] = bias[:, p * LANES:(p + 1) * LANES]


def _odd_inproj(x, nrm, w, wf, fb, qg, kg, *, seq, tm, tk):
    t = x.shape[0]
    tiles_per_batch = seq // tm
    kern = functools.partial(_odd_inproj_kernel, tiles_per_batch=tiles_per_batch, n_chunks=tm // CHUNK, tk=tk)
    row_spec = pl.BlockSpec((tm, C_WIDTH), lambda i: (i, 0))
    kext_spec = pl.BlockSpec((tm, N_PAIRS * K_EXT), lambda i: (i, 0))
    vt_spec = pl.BlockSpec((None, N_PAIRS, tm // tk, LANES, tk),
                           lambda i: (i // tiles_per_batch, 0, i % tiles_per_batch, 0, 0))
    place = _bias_placement()
    wide = jax.ShapeDtypeStruct((t, C_WIDTH), BF16)
    return pl.pallas_call(
        kern,
        out_shape=[wide, jax.ShapeDtypeStruct((t, N_PAIRS * K_EXT), BF16),
                   jax.ShapeDtypeStruct((t // seq, N_PAIRS, seq // tk, LANES, tk), BF16), wide],
        grid=(t // tm,),
        in_specs=[row_spec, _const_spec(nrm.shape), _resident_spec(w.shape), _const_spec(wf.shape),
                  _const_spec(fb.shape), _const_spec(qg.shape), _const_spec(kg.shape), _const_spec(place.shape)],
        out_specs=[row_spec, kext_spec, vt_spec, row_spec],
        scratch_shapes=[pltpu.VMEM((tm, O_END), F32), pltpu.VMEM((tm, LANES), F32), pltpu.VMEM((1, LANES), F32)],
        compiler_params=pltpu.CompilerParams(dimension_semantics=("arbitrary",), vmem_limit_bytes=VMEM_LIMIT),
        name="odd_inproj",
    )(x, nrm, w, wf, fb, qg, kg, place)


NEG_BIG = -1e30


V_ROWS = 80


def _fox_kernel(q_ref, k_ref, vt_ref, gate_ref, wcast_ref, o_ref, wcast_out_ref, s_ref, *, tq, tk):
    wcast_out_ref[...] = wcast_ref[...].astype(BF16)
    qi = pl.program_id(2)
    ng = tq // tk
    key_i = lax.broadcasted_iota(jnp.int32, (tk, tk), 0)
    qry_i = lax.broadcasted_iota(jnp.int32, (tk, tk), 1)
    causal = key_i <= qry_i
    chains = [(hh, r) for r in range(ng) for hh in range(2)]
    feat = lax.broadcasted_iota(jnp.int32, (LANES, 1), 0)
    qms = []
    for hh, r in chains:
        q_t = q_ref[r * tk:(r + 1) * tk, :].astype(F32).T
        own = (feat < C_HEAD_DIM) if hh == 0 else (feat >= C_HEAD_DIM)
        qh = jnp.where(own, q_t, 0.0).astype(BF16)
        pick = (feat >= BIAS_PARTS * hh) & (feat < BIAS_PARTS * (hh + 1))
        minus_one = jnp.broadcast_to(jnp.where(pick, -1.0, 0.0).astype(BF16), qh.shape)
        qms.append(jnp.concatenate([qh, minus_one], axis=0))

    def key_rows(j):
        return pl.ds(pl.multiple_of(j * tk, tk), tk)

    def scores_to_scratch(j, slot, live, modes):
        kb = k_ref[key_rows(j), :]
        raw = [_dot(kb, qms[idx]) for idx in live]
        maxes = []
        for s, idx in zip(raw, live):
            hh, r = chains[idx]
            if modes[r] == "diag":
                s = jnp.where(causal, s, NEG_BIG)
            s_ref[slot, idx] = s
            maxes.append(jnp.max(s, axis=0, keepdims=True))
        return maxes

    ones_rows = (lax.broadcasted_iota(jnp.int32, (V_ROWS - C_HEAD_DIM, tk), 0) == 0).astype(BF16)

    def softmax_pv(j, slot, maxes, live, state):
        vt_pair = vt_ref[j]
        vtb = [jnp.concatenate([vt_pair[hh * C_HEAD_DIM:(hh + 1) * C_HEAD_DIM, :], ones_rows], axis=0)
               for hh in range(2)]
        new = list(state)
        probs = []
        for bm, idx in zip(maxes, live):
            m = state[2 * idx]
            m_new = jnp.maximum(m, bm)
            p = jnp.exp2(s_ref[slot, idx] - m_new)
            new[2 * idx] = m_new
            probs.append((jnp.exp2(m - m_new), p.astype(BF16)))
        for (alpha, p), idx in zip(probs, live):
            hh, r = chains[idx]
            new[2 * idx + 1] = alpha * state[2 * idx + 1] + _dot(vtb[hh], p)
        return new

    assert ng % 2 == 0
    all_chains = list(range(len(chains)))
    n_state = 2 * len(chains)
    state = []
    for _ in chains:
        state += [jnp.full((1, tk), NEG_BIG, F32), jnp.zeros((V_ROWS, tk), F32)]
    n_full = qi * ng
    full_modes = ("full",) * ng

    def diag_modes(g):
        return tuple("skip" if r < g else ("diag" if r == g else "full") for r in range(ng))

    def live_chains(g):
        return [idx for idx, (hh, r) in enumerate(chains) if r >= g]

    def two_blocks(j, carry, next_modes):
        st, mx0 = list(carry[:n_state]), carry[n_state:]
        mx1 = scores_to_scratch(j + 1, 1, all_chains, full_modes)
        st = softmax_pv(j, 0, mx0, all_chains, st)
        mx0 = scores_to_scratch(j + 2, 0, all_chains, next_modes)
        st = softmax_pv(j + 1, 1, mx1, all_chains, st)
        return tuple(st) + tuple(mx0)

    def with_full_blocks(_):
        first = scores_to_scratch(0, 0, all_chains, full_modes)
        carry = lax.fori_loop(0, n_full // 2 - 1, lambda i, c: two_blocks(2 * i, c, full_modes),
                              tuple(state) + tuple(first))
        return two_blocks(n_full - 2, carry, diag_modes(0))

    def no_full_blocks(_):
        return tuple(state) + tuple(scores_to_scratch(0, 0, all_chains, diag_modes(0)))

    carry = lax.cond(qi > 0, with_full_blocks, no_full_blocks, 0)
    state, mx = list(carry[:n_state]), carry[n_state:]
    for g in range(ng):
        if g + 1 < ng:
            mx_next = scores_to_scratch(n_full + g + 1, (g + 1) % 2, live_chains(g + 1), diag_modes(g + 1))
        state = softmax_pv(n_full + g, g % 2, mx, live_chains(g), state)
        if g + 1 < ng:
            mx = mx_next

    for r in range(ng):
        parts = []
        for hh in range(2):
            acc = state[2 * chains.index((hh, r)) + 1]
            parts.append(acc[:C_HEAD_DIM, :] / acc[C_HEAD_DIM:C_HEAD_DIM + 1, :])
        o = jnp.concatenate(parts, axis=0).T
        rows = slice(r * tk, (r + 1) * tk)
        o_ref[rows, :] = (o * gate_ref[rows, :].astype(F32)).astype(BF16)


def _fox_attention(q, k_ext, vt, gate, wcast, *, batch, seq, tq, tk):
    t = q.shape[0]
    nq = seq // tq
    kern = functools.partial(_fox_kernel, tq=tq, tk=tk)
    cast_in, cast_out, cast_shape = _side_cast_specs(wcast, batch * N_PAIRS * nq,
                                                     lambda b, p, i: ((b * N_PAIRS + p) * nq + i, 0))
    return pl.pallas_call(
        kern,
        out_shape=[jax.ShapeDtypeStruct((t, C_WIDTH), BF16), cast_shape],
        grid=(batch, N_PAIRS, nq),
        in_specs=[
            pl.BlockSpec((tq, LANES), lambda b, p, i: (b * nq + i, p)),
            pl.BlockSpec((seq, K_EXT), lambda b, p, i: (b, p)),
            pl.BlockSpec((None, None, seq // tk, LANES, tk), lambda b, p, i: (b, p, 0, 0, 0)),
            pl.BlockSpec((tq, LANES), lambda b, p, i: (b * nq + i, p)),
            cast_in,
        ],
        out_specs=[pl.BlockSpec((tq, LANES), lambda b, p, i: (b * nq + i, p)), cast_out],
        scratch_shapes=[pltpu.VMEM((2, 2 * (tq // tk), tk, tk), F32)],
        compiler_params=pltpu.CompilerParams(dimension_semantics=("arbitrary", "arbitrary", "arbitrary"),
                                             vmem_limit_bytes=VMEM_LIMIT),
        name="fox_attn",
    )(q, k_ext, vt, gate, wcast)


def _store_token_tiles(dst_ref, val, n_rows):
    for s in range(ROW_TILE):
        dst_ref[pl.ds(s, n_rows, stride=ROW_TILE), :] = val[:, s * LANES:(s + 1) * LANES]


def _load_token_tiles(src_ref, n_rows):
    return jnp.concatenate([src_ref[pl.ds(s, n_rows, stride=ROW_TILE), :] for s in range(ROW_TILE)], axis=1)


def _odd_out_kernel(x_ref, a_ref, wo_ref, nrm_ref, r_ref, wcast_ref, x3_ref, h_ref, route_ref, wcast_out_ref):
    wcast_out_ref[...] = wcast_ref[...].astype(BF16)
    x3 = x_ref[...] + _dot(a_ref[...], wo_ref[...])
    x3_ref[...] = x3
    h = _rms(x3, nrm_ref[...])
    _store_token_tiles(h_ref, h, h.shape[0])

    h_hi = h.astype(BF16)
    h_lo = (h - h_hi.astype(F32)).astype(BF16)
    r = r_ref[...]
    r_hi = r.astype(BF16)
    r_lo = (r - r_hi.astype(F32)).astype(BF16)
    n = h.shape[0]
    prod = _dot(jnp.concatenate([h_hi, h_lo], axis=0), jnp.concatenate([r_hi, r_lo], axis=1))
    logits = prod[:n, :LANES] + (prod[n:, :LANES] + prod[:n, LANES:])

    lane = lax.broadcasted_iota(jnp.int32, logits.shape, 1).astype(F32)
    neg_inf = jnp.float32(-jnp.inf)
    lg = jnp.where(lane < N_EXPERTS, logits, neg_inf)
    m1 = jnp.max(lg, axis=-1, keepdims=True)
    i1 = jnp.min(jnp.where(lg == m1, lane, float(LANES)), axis=-1, keepdims=True)
    lg2 = jnp.where(lane == i1, neg_inf, lg)
    m2 = jnp.max(lg2, axis=-1, keepdims=True)
    i2 = jnp.min(jnp.where(lg2 == m2, lane, float(LANES)), axis=-1, keepdims=True)
    e2 = jnp.exp(m2 - m1)
    g1 = 1.0 / (1.0 + e2)
    g2 = e2 / (1.0 + e2)
    route_ref[...] = jnp.where(lane == 0, i1, jnp.where(lane == 1, i2, jnp.where(lane == 2, g1,
                               jnp.where(lane == 3, g2, 0.0))))


def _odd_out(x, a, wo, nrm, router, wcast, *, tm):
    t = x.shape[0]
    row_spec = pl.BlockSpec((tm, D_MODEL), lambda i: (i, 0))
    cast_in, cast_out, cast_shape = _side_cast_specs(wcast, t // tm, lambda i: (i, 0))
    return pl.pallas_call(
        _odd_out_kernel,
        out_shape=[jax.ShapeDtypeStruct((t, D_MODEL), F32), jax.ShapeDtypeStruct((t * ROW_TILE, LANES), F32),
                   jax.ShapeDtypeStruct((t, LANES), F32), cast_shape],
        grid=(t // tm,),
        in_specs=[row_spec, row_spec, _const_spec(wo.shape), _const_spec(nrm.shape), _const_spec(router.shape),
                  cast_in],
        out_specs=[row_spec, pl.BlockSpec((tm * ROW_TILE, LANES), lambda i: (i, 0)),
                   pl.BlockSpec((tm, LANES), lambda i: (i, 0)), cast_out],
        compiler_params=pltpu.CompilerParams(dimension_semantics=("arbitrary",), vmem_limit_bytes=VMEM_LIMIT),
        name="odd_out",
    )(x, a, wo, nrm, router, wcast)


def _row_gather_copy(src_hbm, src_row, dst_ref, dst_row, sem):
    return pltpu.make_async_copy(
        src_hbm.at[pl.ds(pl.multiple_of(src_row * ROW_TILE, ROW_TILE), ROW_TILE), :],
        dst_ref.at[pl.ds(pl.multiple_of(dst_row * ROW_TILE, ROW_TILE), ROW_TILE), :],
        sem)


GATHER_UNROLL = 8


def _start_row_gathers(src_hbm, idx_ref, idx_row, first, count, dst_ref, sem, *, inline):
    if inline:
        for u in range(count):
            _row_gather_copy(src_hbm, idx_ref[idx_row, first + u], dst_ref, first + u, sem).start(priority=u % 2)
        return

    def issue(r2, c):
        for u in range(2):
            r = 2 * r2 + u
            _row_gather_copy(src_hbm, idx_ref[idx_row, r], dst_ref, r, sem).start(priority=u)
        return c

    assert first % 2 == 0 and count % 2 == 0
    lax.fori_loop(first // 2, (first + count) // 2, issue, 0, unroll=GATHER_UNROLL // 2)


def _wait_row_gathers(src_hbm, dst_ref, sem):
    pltpu.make_async_copy(src_hbm.at[pl.ds(0, dst_ref.shape[0]), :], dst_ref, sem).wait()


def _moe_ffn_kernel(te_ref, nu_ref, tok_ref, tok_next_ref, h_hbm, w1_ref, w3_ref, w2_ref, o_ref, xs_ref, buf_ref,
                    sem, *, tm, n_chunks):
    i = pl.program_id(0)
    n_used = nu_ref[0]
    used = i < n_used
    slot = i % 2

    @pl.when(i == 0)
    def _():
        _start_row_gathers(h_hbm, tok_ref, 0, 0, tm, buf_ref.at[0], sem.at[0], inline=False)

    @pl.when(i <= n_used)
    def _():
        _wait_row_gathers(h_hbm, buf_ref.at[slot], sem.at[slot])
        xs_ref[...] = _load_token_tiles(buf_ref.at[slot], tm).astype(BF16)

    @pl.when(used)
    def _():
        n_groups = 3 * n_chunks
        per_group = tm // n_groups
        starts = [(g * per_group, per_group if g + 1 < n_groups else tm - g * per_group) for g in range(n_groups)]

        def prefetch(g):
            first, count = starts[g]
            _start_row_gathers(h_hbm, tok_next_ref, 0, first, count, buf_ref.at[1 - slot], sem.at[1 - slot],
                               inline=True)

        x = xs_ref[...]
        tf = D_FF_EXPERT // n_chunks
        y = None
        for c in range(n_chunks):
            cols = slice(c * tf, (c + 1) * tf)
            prefetch(3 * c)
            a = _dot(x, w1_ref[:, cols])
            prefetch(3 * c + 1)
            b = _dot(x, w3_ref[:, cols])
            prefetch(3 * c + 2)
            part = _dot((a * _sigmoid(a) * b).astype(BF16), w2_ref[cols, :])
            y = part if y is None else y + part
        _store_token_tiles(o_ref, y, tm)

    @pl.when(jnp.logical_not(used))
    def _():
        o_ref[...] = jnp.zeros_like(o_ref)


def _moe_ffn(tile_expert, n_used, row_tok, h_tiles, w1, w3, w2, *, tm, n_chunks):
    n_tiles = row_tok.shape[0] - 1

    def expert_spec(shape):
        return pl.BlockSpec((None,) + shape, lambda i, te, nu: (te[i], 0, 0), pipeline_mode=pl.Buffered(1))

    grid_spec = pltpu.PrefetchScalarGridSpec(
        num_scalar_prefetch=2,
        grid=(n_tiles,),
        in_specs=[
            pl.BlockSpec((None, 1, tm), lambda i, te, nu: (i, 0, 0), memory_space=pltpu.SMEM),
            pl.BlockSpec((None, 1, tm), lambda i, te, nu: (i + 1, 0, 0), memory_space=pltpu.SMEM),
            pl.BlockSpec(memory_space=pl.ANY),
            expert_spec((D_MODEL, D_FF_EXPERT)), expert_spec((D_MODEL, D_FF_EXPERT)),
            expert_spec((D_FF_EXPERT, D_MODEL)),
        ],
        out_specs=pl.BlockSpec((tm * ROW_TILE, LANES), lambda i, te, nu: (i, 0)),
        scratch_shapes=[pltpu.VMEM((tm, D_MODEL), BF16), pltpu.VMEM((2, tm * ROW_TILE, LANES), F32),
                        pltpu.SemaphoreType.DMA((2,))],
    )
    return pl.pallas_call(
        functools.partial(_moe_ffn_kernel, tm=tm, n_chunks=n_chunks),
        out_shape=jax.ShapeDtypeStruct((n_tiles * tm * ROW_TILE, LANES), F32),
        grid_spec=grid_spec,
        compiler_params=pltpu.CompilerParams(dimension_semantics=("arbitrary",), vmem_limit_bytes=VMEM_LIMIT),
        name="moe_ffn",
    )(tile_expert, n_used, row_tok, row_tok, h_tiles, w1, w3, w2)


def _moe_combine_kernel(pos_ref, pos_next_ref, x_ref, route_ref, nrm_ref, y_hbm, o_ref, buf_ref, sem, *, tm):
    i = pl.program_id(0)
    slot = i % 2

    def start_tile(idx_ref, s, inline):
        for k in range(2):
            _start_row_gathers(y_hbm, idx_ref, k, 0, tm, buf_ref.at[s, k], sem.at[s, k], inline=inline)

    @pl.when(i == 0)
    def _():
        start_tile(pos_ref, 0, False)

    @pl.when(i + 1 < pl.num_programs(0))
    def _():
        start_tile(pos_next_ref, 1 - slot, True)

    for k in range(2):
        _wait_row_gathers(y_hbm, buf_ref.at[slot, k], sem.at[slot, k])
    g1 = route_ref[:, 2:3]
    g2 = route_ref[:, 3:4]
    x = x_ref[...] + (g1 * _load_token_tiles(buf_ref.at[slot, 0], tm) + g2 * _load_token_tiles(buf_ref.at[slot, 1], tm))
    o_ref[...] = _rms(x, nrm_ref[...])


def _moe_combine(pos, x, route, nrm, y_tiles, *, tm):
    t = x.shape[0]
    n = t // tm
    row_spec = pl.BlockSpec((tm, D_MODEL), lambda i: (i, 0))
    return pl.pallas_call(
        functools.partial(_moe_combine_kernel, tm=tm),
        out_shape=jax.ShapeDtypeStruct((t, D_MODEL), F32),
        grid=(n,),
        in_specs=[pl.BlockSpec((None, 2, tm), lambda i: (i, 0, 0), memory_space=pltpu.SMEM),
                  pl.BlockSpec((None, 2, tm), lambda i: (jnp.minimum(i + 1, n - 1), 0, 0), memory_space=pltpu.SMEM),
                  row_spec, pl.BlockSpec((tm, LANES), lambda i: (i, 0)), _const_spec(nrm.shape),
                  pl.BlockSpec(memory_space=pl.ANY)],
        out_specs=row_spec,
        scratch_shapes=[pltpu.VMEM((2, 2, tm * ROW_TILE, LANES), F32), pltpu.SemaphoreType.DMA((2, 2))],
        compiler_params=pltpu.CompilerParams(dimension_semantics=("arbitrary",), vmem_limit_bytes=VMEM_LIMIT),
        name="moe_combine",
    )(pos, pos, x, route, nrm, y_tiles)


SCATTER_CHUNK = 2048


def _invert_rows_kernel(pos_ref, out_ref, *, n_tokens):
    j = pl.program_id(0)

    @pl.when(j == 0)
    def _():
        def zero(r, c):
            out_ref[r] = 0
            return c

        lax.fori_loop(0, out_ref.shape[0], zero, 0, unroll=32)

    assert n_tokens & (n_tokens - 1) == 0
    base = j * SCATTER_CHUNK

    def place(u, c):
        out_ref[pos_ref[0, u]] = (base + u) & (n_tokens - 1)
        return c

    lax.fori_loop(0, SCATTER_CHUNK, place, 0, unroll=32)


def _invert_rows(pos, n_rows, n_tokens):
    n_chunks = pos.shape[0] // SCATTER_CHUNK
    assert n_chunks * SCATTER_CHUNK == pos.shape[0]
    return pl.pallas_call(
        functools.partial(_invert_rows_kernel, n_tokens=n_tokens),
        out_shape=jax.ShapeDtypeStruct((n_rows,), jnp.int32),
        grid=(n_chunks,),
        in_specs=[pl.BlockSpec((None, 1, SCATTER_CHUNK), lambda j: (j, 0, 0), memory_space=pltpu.SMEM)],
        out_specs=pl.BlockSpec(memory_space=pltpu.SMEM),
        compiler_params=pltpu.CompilerParams(dimension_semantics=("arbitrary",)),
        name="invert_rows",
    )(pos.reshape(n_chunks, 1, SCATTER_CHUNK))


def _routing_tables(idx1, idx2, *, tm, n_tiles):
    t = idx1.shape[0]
    e_flat = jnp.concatenate([idx1, idx2])
    onehot = (e_flat[:, None] == jnp.arange(N_EXPERTS, dtype=jnp.int32)[None, :]).astype(jnp.int32)
    csum = jnp.cumsum(onehot, axis=0)
    rank = jnp.sum((csum - onehot) * onehot, axis=1)
    counts = csum[-1]
    tiles_e = (counts + tm - 1) // tm
    tile_end = jnp.cumsum(tiles_e)
    tile_start = tile_end - tiles_e
    pos = jnp.sum(onehot * tile_start[None, :], axis=1) * tm + rank
    n_used = tile_end[-1]
    tile_ids = jnp.arange(n_tiles, dtype=jnp.int32)
    te = jnp.sum((tile_ids[:, None] >= tile_end[None, :]).astype(jnp.int32), axis=1)
    te_last = jnp.sum((n_used - 1 >= tile_end).astype(jnp.int32))
    tile_expert = jnp.where(tile_ids < n_used, te, te_last).astype(jnp.int32)
    row_tok = _invert_rows(pos.astype(jnp.int32), n_tiles * tm, t)
    return pos[:t], pos[t:], row_tok, tile_expert, n_used.reshape(1).astype(jnp.int32)


def _pad_cols(w, n):
    return jnp.pad(w, ((0, 0), (0, n - w.shape[1])))


def kernel(x, even_norm_mix, even_w_in, even_gate_up, even_gate_bias, even_w_s, even_b_s, even_ln_g, even_ln_b,
           even_head_g, even_w_o, even_norm_ffn, even_ffn_w1, even_ffn_w3, even_ffn_w2, odd_norm_mix, odd_w_in,
           odd_forget_bias, odd_q_g, odd_k_g, odd_w_o, odd_norm_ffn, odd_router, odd_exp_w1, odd_exp_w3,
           odd_exp_w2, final_norm):
    batch, seq, d = x.shape
    t = batch * seq
    xt = x.reshape(t, d)
    tm = min(512, seq)

    w_in = even_w_in[0]
    u_w, v_w, q_w, k_w, g_w, vb_w, og_w = jnp.split(w_in, [512, 1024, 1280, 1536, 1552, 2064], axis=1)
    win_e = jnp.concatenate([u_w, v_w, q_w, k_w, vb_w, og_w, _pad_cols(g_w, LANES)], axis=1).astype(BF16)
    gup = jnp.pad(even_gate_up[0], ((0, LANES - B_GATE_RANK), (0, 0))).astype(BF16)
    gb = even_gate_bias[0].reshape(1, B_QK_WIDTH)
    tril = jnp.tril(jnp.ones((CHUNK, CHUNK), dtype=bool))
    ws = jnp.where(tril[None], even_w_s[0], 0.0).astype(BF16)
    bs = jnp.broadcast_to(even_b_s[0][:, :, None], (A_GROUPS, CHUNK, LANES))
    lng = even_ln_g[0].reshape(1, A_WIDTH)
    lnb = even_ln_b[0].reshape(1, A_WIDTH)
    hg = even_head_g[0].reshape(B_HEADS, 1, B_VAL_DIM)
    ew1 = odd_exp_w1[0].reshape(N_EXPERTS * D_MODEL, D_FF_EXPERT)
    ew3 = odd_exp_w3[0].reshape(N_EXPERTS * D_MODEL, D_FF_EXPERT)
    ew2 = odd_exp_w2[0].reshape(N_EXPERTS * D_FF_EXPERT, D_MODEL)
    x1, ew1_b, fw1_b, fw3_b, fw2_b = _even_mixer(
        xt, even_norm_mix[0].reshape(1, d), win_e, gup, gb, ws, bs, lng, lnb, hg, even_w_o[0].astype(BF16),
        [ew1, even_ffn_w1[0], even_ffn_w3[0], even_ffn_w2[0]], seq=seq, tm=tm)
    x2, win_o = _dense_ffn(x1, even_norm_ffn[0].reshape(1, d), fw1_b, fw3_b, fw2_b, odd_w_in[0], O_F, tm=tm)

    wf_o = _pad_cols(odd_w_in[0][:, O_F:], LANES).astype(BF16)
    fb = jnp.pad(odd_forget_bias[0], (0, LANES - C_HEADS)).reshape(1, LANES)
    qg = jnp.tile(odd_q_g[0], C_HEADS).reshape(1, C_WIDTH)
    kg = jnp.tile(odd_k_g[0], C_HEADS).reshape(1, C_WIDTH)
    tk = min(256, seq)
    tq = min(1024, seq)
    q, k_ext, vt, gate = _odd_inproj(x2, odd_norm_mix[0].reshape(1, d), win_o, wf_o, fb, qg, kg, seq=seq, tm=tm,
                                     tk=tk)
    attn, ew2_b = _fox_attention(q, k_ext, vt, gate, ew2, batch=batch, seq=seq, tq=tq, tk=tk)

    router = _pad_cols(odd_router[0], LANES)
    x3, h_tiles, route, ew3_b = _odd_out(x2, attn, odd_w_o[0].astype(BF16), odd_norm_ffn[0].reshape(1, d), router,
                                         ew3, tm=tm)

    tm_moe = 512
    tm_comb = min(256, seq)
    n_tiles = (2 * t) // tm_moe + N_EXPERTS + 1
    idx1 = route[:, 0].astype(jnp.int32)
    idx2 = route[:, 1].astype(jnp.int32)
    pos1, pos2, row_tok, tile_expert, n_used = _routing_tables(idx1, idx2, tm=tm_moe, n_tiles=n_tiles + 1)
    y_tiles = _moe_ffn(tile_expert[:n_tiles], n_used, row_tok.reshape(n_tiles + 1, 1, tm_moe), h_tiles,
                       ew1_b.reshape(N_EXPERTS, D_MODEL, D_FF_EXPERT), ew3_b.reshape(N_EXPERTS, D_MODEL, D_FF_EXPERT),
                       ew2_b.reshape(N_EXPERTS, D_FF_EXPERT, D_MODEL), tm=tm_moe, n_chunks=2)
    pos = jnp.stack([pos1.reshape(t // tm_comb, tm_comb), pos2.reshape(t // tm_comb, tm_comb)], axis=1)
    out = _moe_combine(pos, x3, route, final_norm.reshape(1, d), y_tiles, tm=tm_comb)
    return out.reshape(batch, seq, d)
```

```python
import functools
import math

import jax
import jax.numpy as jnp
from jax import lax
from jax.experimental import pallas as pl
from jax.experimental.pallas import tpu as pltpu

F32 = jnp.float32
BF16 = jnp.bfloat16
HIGHEST = lax.Precision.HIGHEST

EPS = 1e-6
D_MODEL = 1024
CHUNK = 128
SUB = 32
N_SUB = CHUNK // SUB
A_GROUPS = 4
A_WIDTH = 512
B_HEADS = 4
B_KEY_DIM = 64
B_VAL_DIM = 128
B_QK_WIDTH = 256
B_V_WIDTH = 512
B_GATE_RANK = 16
B_GATE_NORMALIZER = 16.0
C_HEADS = 16
C_HEAD_DIM = 64
C_WIDTH = 1024
D_FF_DENSE = 2816
N_EXPERTS = 8
D_FF_EXPERT = 3584
LANES = 128
MAX_DECAY_EXP = 60.0
LOG2E = math.log2(math.e)
ROW_TILE = 8

E_U, E_V, E_Q, E_K, E_VB, E_OG, E_G, E_END = 0, 512, 1024, 1280, 1536, 2048, 2560, 2688
O_Q, O_K, O_V, O_OG, O_F, O_END = 0, 1024, 2048, 3072, 4096, 4224

VMEM_LIMIT = 56 * 1024 * 1024


def _rms(x, g):
    ms = jnp.mean(x * x, axis=-1, keepdims=True)
    return x * lax.rsqrt(ms + EPS) * g


def _gelu_tanh(x):
    c = math.sqrt(2.0 / math.pi)
    return x * (0.5 * (1.0 + jnp.tanh(c * (x + 0.044715 * (x * x * x)))))


def _sigmoid(x):
    return 1.0 / (1.0 + jnp.exp(-x))


def _log_sigmoid(x):
    return jnp.minimum(x, 0.0) - jnp.log(1.0 + jnp.exp(-jnp.abs(x)))


def _dot(a, b):
    return jnp.dot(a, b, preferred_element_type=F32)


def _dot_nt(a, b):
    return lax.dot_general(a, b, (((1,), (1,)), ((), ())), preferred_element_type=F32)


def _split3(x):
    hi = x.astype(BF16)
    r1 = x - hi.astype(F32)
    mid = r1.astype(BF16)
    lo = (r1 - mid.astype(F32)).astype(BF16)
    return hi, mid, lo


def _cumsum_rows(tril_b, x):
    hi, mid, lo = _split3(x)
    return _dot(tril_b, hi) + _dot(tril_b, mid) + _dot(tril_b, lo)


def _const_spec(shape):
    nd = len(shape)
    return pl.BlockSpec(shape, lambda *_: (0,) * nd)


def _side_cast_specs(w2d, n_steps, index_map):
    rows = w2d.shape[0] // n_steps
    assert rows * n_steps == w2d.shape[0] and rows % 16 == 0
    spec = pl.BlockSpec((rows, w2d.shape[1]), index_map)
    return spec, spec, jax.ShapeDtypeStruct(w2d.shape, BF16)


def _side_cast_specs_1d(w2d, n_steps):
    span = 1 if (w2d.shape[0] // n_steps) % 16 == 0 and w2d.shape[0] % n_steps == 0 else 2
    return _side_cast_specs(w2d, n_steps // span, lambda i: (i // span, 0))


def _even_mixer_kernel(x_ref, nrm_ref, win_ref, gup_ref, gb_ref, ws_ref, bs_ref, lng_ref, lnb_ref,
                       hg_ref, wo_ref, *rest, tiles_per_batch, n_chunks, n_casts):
    cast_refs, o_ref, cast_out_refs = rest[:n_casts], rest[n_casts], rest[n_casts + 1:2 * n_casts + 1]
    z_ref, mix_ref, st_ref = rest[2 * n_casts + 1:]
    i = pl.program_id(0)
    for src, dst in zip(cast_refs, cast_out_refs):
        dst[...] = src[...].astype(BF16)

    @pl.when(i % tiles_per_batch == 0)
    def _():
        st_ref[...] = jnp.zeros_like(st_ref)

    h = _rms(x_ref[...], nrm_ref[...]).astype(BF16)
    z_ref[...] = _dot(h, win_ref[...])

    row = lax.broadcasted_iota(jnp.int32, (CHUNK, CHUNK), 0)
    col = lax.broadcasted_iota(jnp.int32, (CHUNK, CHUNK), 1)
    tril_b = (col <= row).astype(BF16)
    sub_row = row & (SUB - 1)
    head_lane = lax.broadcasted_iota(jnp.int32, (1, B_QK_WIDTH), 1) // B_KEY_DIM
    bd_mask = (lax.broadcasted_iota(jnp.int32, (B_V_WIDTH, B_QK_WIDTH), 0) // B_VAL_DIM
               == lax.broadcasted_iota(jnp.int32, (B_V_WIDTH, B_QK_WIDTH), 1) // B_KEY_DIM)

    def chunk_body(c, carry):
        rows = pl.ds(pl.multiple_of(c * CHUNK, CHUNK), CHUNK)

        u = _gelu_tanh(z_ref[rows, E_U:E_V])
        v = _gelu_tanh(z_ref[rows, E_V:E_Q])
        mu = jnp.mean(v, axis=-1, keepdims=True)
        vc = v - mu
        var = jnp.mean(vc * vc, axis=-1, keepdims=True)
        vln = (vc * lax.rsqrt(var + EPS) * lng_ref[...] + lnb_ref[...]).astype(BF16)
        for g in range(A_GROUPS):
            sl = slice(g * LANES, (g + 1) * LANES)
            mixed = _dot(ws_ref[g], vln[:, sl]) + bs_ref[g]
            mix_ref[rows, sl] = (u[:, sl] * mixed).astype(BF16)

        q = z_ref[rows, E_Q:E_K] * (B_KEY_DIM ** -0.5)
        k = z_ref[rows, E_K:E_VB]
        vb = z_ref[rows, E_VB:E_OG]
        og = z_ref[rows, E_OG:E_G]
        glr = z_ref[rows, E_G:E_END].astype(BF16)
        logit = _dot(glr, gup_ref[...]) + gb_ref[...]
        log_a = _log_sigmoid(logit) * (1.0 / B_GATE_NORMALIZER)
        g_cum = _cumsum_rows(tril_b, log_a)
        g_last = g_cum[CHUNK - 1:CHUNK, :]
        st = st_ref[...]
        o = _dot_nt((q * jnp.exp(g_cum)).astype(BF16), st.astype(BF16))

        p_rows = [[None] * N_SUB for _ in range(B_HEADS)]
        for s in range(N_SUB):
            gs = g_cum[s * SUB:(s + 1) * SUB, :]
            if s == 0:
                qt = q[0:SUB, :] * jnp.exp(gs)
                kt = k * jnp.exp(jnp.minimum(-g_cum, MAX_DECAY_EXP))
            else:
                ref_g = g_cum[s * SUB - 1:s * SUB, :]
                qt = q[s * SUB:(s + 1) * SUB, :] * jnp.exp(gs - ref_g)
                kt = k * jnp.exp(jnp.minimum(ref_g - g_cum, MAX_DECAY_EXP))
            qs = jnp.concatenate([jnp.where(head_lane == hh, qt, 0.0) for hh in range(B_HEADS)],
                                 axis=0).astype(BF16)
            sc = _dot_nt(qs, kt.astype(BF16))
            sc = jnp.where(col <= (s * SUB + sub_row), sc, 0.0)
            for hh in range(B_HEADS):
                p_rows[hh][s] = sc[hh * SUB:(hh + 1) * SUB, :]

        vb_b = vb.astype(BF16)
        for hh in range(B_HEADS):
            sl = slice(hh * B_VAL_DIM, (hh + 1) * B_VAL_DIM)
            ph = jnp.concatenate(p_rows[hh], axis=0).astype(BF16)
            oh = o[:, sl] + _dot(ph, vb_b[:, sl])
            on = _rms(oh, hg_ref[hh])
            ogh = og[:, sl]
            mix_ref[rows, A_WIDTH + hh * B_VAL_DIM:A_WIDTH + (hh + 1) * B_VAL_DIM] = (
                on * (ogh * _sigmoid(ogh))).astype(BF16)

        k_dec = (k * jnp.exp(g_last - g_cum)).astype(BF16)
        upd = _dot(vb.T.astype(BF16), k_dec)
        st_ref[...] = jnp.exp(g_last) * st + jnp.where(bd_mask, upd, 0.0)
        return carry

    lax.fori_loop(0, n_chunks, chunk_body, 0, unroll=True)
    o_ref[...] = x_ref[...] + _dot(mix_ref[...], wo_ref[...])


def _even_mixer(x, nrm, win, gup, gb, ws, bs, lng, lnb, hg, wo, wcasts, *, seq, tm):
    t = x.shape[0]
    kern = functools.partial(_even_mixer_kernel, tiles_per_batch=seq // tm, n_chunks=tm // CHUNK,
                             n_casts=len(wcasts))
    casts = [_side_cast_specs_1d(w, t // tm) for w in wcasts]
    return pl.pallas_call(
        kern,
        out_shape=[jax.ShapeDtypeStruct((t, D_MODEL), F32)] + [c[2] for c in casts],
        grid=(t // tm,),
        in_specs=[
            pl.BlockSpec((tm, D_MODEL), lambda i: (i, 0)),
            _const_spec(nrm.shape), _resident_spec(win.shape), _const_spec(gup.shape), _const_spec(gb.shape),
            _const_spec(ws.shape), _const_spec(bs.shape), _const_spec(lng.shape), _const_spec(lnb.shape),
            _const_spec(hg.shape), _resident_spec(wo.shape),
        ] + [c[0] for c in casts],
        out_specs=[pl.BlockSpec((tm, D_MODEL), lambda i: (i, 0))] + [c[1] for c in casts],
        scratch_shapes=[
            pltpu.VMEM((tm, E_END), F32),
            pltpu.VMEM((tm, D_MODEL), BF16),
            pltpu.VMEM((B_V_WIDTH, B_QK_WIDTH), F32),
        ],
        compiler_params=pltpu.CompilerParams(dimension_semantics=("arbitrary",), vmem_limit_bytes=VMEM_LIMIT),
        name="even_mixer",
    )(x, nrm, win, gup, gb, ws, bs, lng, lnb, hg, wo, *wcasts)


def _dense_ffn_kernel(x_ref, nrm_ref, w1_ref, w3_ref, w2_ref, wcast_ref, o_ref, wcast_out_ref):
    wcast_out_ref[...] = wcast_ref[:, :wcast_out_ref.shape[1]].astype(BF16)
    x = x_ref[...]
    h = _rms(x, nrm_ref[...]).astype(BF16)
    a = _dot(h, w1_ref[...])
    b = _dot(h, w3_ref[...])
    o_ref[...] = x + _dot((a * _sigmoid(a) * b).astype(BF16), w2_ref[...])


def _resident_spec(shape):
    nd = len(shape)
    return pl.BlockSpec(shape, lambda *_: (0,) * nd, pipeline_mode=pl.Buffered(1))


def _dense_ffn(x, nrm, w1, w3, w2, wcast, wcast_cols, *, tm):
    t = x.shape[0]
    n_steps = t // tm
    row_spec = pl.BlockSpec((tm, D_MODEL), lambda i: (i, 0))
    rows = wcast.shape[0] // n_steps
    assert rows * n_steps == wcast.shape[0] and rows % 16 == 0
    return pl.pallas_call(
        _dense_ffn_kernel,
        out_shape=[jax.ShapeDtypeStruct((t, D_MODEL), F32), jax.ShapeDtypeStruct((wcast.shape[0], wcast_cols), BF16)],
        grid=(n_steps,),
        in_specs=[row_spec, _const_spec(nrm.shape), _resident_spec(w1.shape), _resident_spec(w3.shape),
                  _resident_spec(w2.shape), pl.BlockSpec((rows, wcast.shape[1]), lambda i: (i, 0))],
        out_specs=[row_spec, pl.BlockSpec((rows, wcast_cols), lambda i: (i, 0))],
        compiler_params=pltpu.CompilerParams(dimension_semantics=("arbitrary",), vmem_limit_bytes=VMEM_LIMIT),
        name="dense_ffn",
    )(x, nrm, w1, w3, w2, wcast)


def _head_rms(x, gain):
    lo = lax.broadcasted_iota(jnp.int32, (1, LANES), 1) < C_HEAD_DIM
    outs = []
    for t in range(C_WIDTH // LANES):
        xt = x[:, t * LANES:(t + 1) * LANES]
        sq = xt * xt
        s_lo = jnp.sum(jnp.where(lo, sq, 0.0), axis=-1, keepdims=True)
        s_hi = jnp.sum(jnp.where(lo, 0.0, sq), axis=-1, keepdims=True)
        inv = jnp.where(lo, lax.rsqrt(s_lo * (1.0 / C_HEAD_DIM) + EPS), lax.rsqrt(s_hi * (1.0 / C_HEAD_DIM) + EPS))
        outs.append(xt * inv)
    return jnp.concatenate(outs, axis=-1) * gain


N_PAIRS = C_HEADS // 2
K_EXT = 2 * LANES
BIAS_PARTS = 3


def _bias_placement():
    src = jnp.arange(BIAS_PARTS * LANES)
    part, head = src // LANES, src % LANES
    dst = (head // 2) * LANES + BIAS_PARTS * (head % 2) + part
    hit = (dst[:, None] == jnp.arange(N_PAIRS * LANES)[None, :]) & (head < C_HEADS)[:, None]
    return hit.astype(BF16)


def _odd_inproj_kernel(x_ref, nrm_ref, w_ref, wf_ref, fb_ref, qg_ref, kg_ref, place_ref,
                       q_ref, k_ref, vt_ref, gate_ref, z_ref, c_ref, carry_ref, *, tiles_per_batch, n_chunks, tk):
    i = pl.program_id(0)

    @pl.when(i % tiles_per_batch == 0)
    def _():
        carry_ref[...] = jnp.zeros_like(carry_ref)

    h = _rms(x_ref[...], nrm_ref[...]).astype(BF16)
    z_ref[:, :O_F] = _dot(h, w_ref[...])
    z_ref[:, O_F:] = _dot(h, wf_ref[...])
    q_ref[...] = (_head_rms(z_ref[:, O_Q:O_K], qg_ref[...]) * (C_HEAD_DIM ** -0.5 * LOG2E)).astype(BF16)
    kn = _head_rms(z_ref[:, O_K:O_V], kg_ref[...]).astype(BF16)
    for p in range(N_PAIRS):
        for kb in range(vt_ref.shape[1]):
            blk = z_ref[kb * tk:(kb + 1) * tk, O_V + p * LANES:O_V + (p + 1) * LANES]
            vt_ref[p, kb] = blk.T.astype(BF16)
    gate_ref[...] = _sigmoid(z_ref[:, O_OG:O_F]).astype(BF16)

    row = lax.broadcasted_iota(jnp.int32, (CHUNK, CHUNK), 0)
    col = lax.broadcasted_iota(jnp.int32, (CHUNK, CHUNK), 1)
    tril_b = (col <= row).astype(BF16)
    carry = carry_ref[...]
    for c in range(n_chunks):
        rows = slice(c * CHUNK, (c + 1) * CHUNK)
        log_f = _log_sigmoid(z_ref[rows, O_F:O_END] + fb_ref[...])
        cs = _cumsum_rows(tril_b, log_f) + carry
        c_ref[rows, :] = cs
        carry = cs[CHUNK - 1:CHUNK, :]
    carry_ref[...] = carry

    bias = _dot(jnp.concatenate(_split3(c_ref[...] * LOG2E), axis=1), place_ref[...]).astype(BF16)
    for p in range(N_PAIRS):
        k_ref[:, p * K_EXT:p * K_EXT + LANES] = kn[:, p * LANES:(p + 1) * LANES]
        k_ref[:, p * K_EXT + LANES:(p + 1) * K_EXT] = bias[:, p * LANES:(p + 1) * LANES]


def _odd_inproj(x, nrm, w, wf, fb, qg, kg, *, seq, tm, tk):
    t = x.shape[0]
    tiles_per_batch = seq // tm
    kern = functools.partial(_odd_inproj_kernel, tiles_per_batch=tiles_per_batch, n_chunks=tm // CHUNK, tk=tk)
    row_spec = pl.BlockSpec((tm, C_WIDTH), lambda i: (i, 0))
    kext_spec = pl.BlockSpec((tm, N_PAIRS * K_EXT), lambda i: (i, 0))
    vt_spec = pl.BlockSpec((None, N_PAIRS, tm // tk, LANES, tk),
                           lambda i: (i // tiles_per_batch, 0, i % tiles_per_batch, 0, 0))
    place = _bias_placement()
    wide = jax.ShapeDtypeStruct((t, C_WIDTH), BF16)
    return pl.pallas_call(
        kern,
        out_shape=[wide, jax.ShapeDtypeStruct((t, N_PAIRS * K_EXT), BF16),
                   jax.ShapeDtypeStruct((t // seq, N_PAIRS, seq // tk, LANES, tk), BF16), wide],
        grid=(t // tm,),
        in_specs=[row_spec, _const_spec(nrm.shape), _resident_spec(w.shape), _const_spec(wf.shape),
                  _const_spec(fb.shape), _const_spec(qg.shape), _const_spec(kg.shape), _const_spec(place.shape)],
        out_specs=[row_spec, kext_spec, vt_spec, row_spec],
        scratch_shapes=[pltpu.VMEM((tm, O_END), F32), pltpu.VMEM((tm, LANES), F32), pltpu.VMEM((1, LANES), F32)],
        compiler_params=pltpu.CompilerParams(dimension_semantics=("arbitrary",), vmem_limit_bytes=VMEM_LIMIT),
        name="odd_inproj",
    )(x, nrm, w, wf, fb, qg, kg, place)


NEG_BIG = -1e30


V_ROWS = 80


def _fox_kernel(q_ref, k_ref, vt_ref, gate_ref, wcast_ref, o_ref, wcast_out_ref, s_ref, *, tq, tk):
    wcast_out_ref[...] = wcast_ref[...].astype(BF16)
    qi = pl.program_id(2)
    ng = tq // tk
    key_i = lax.broadcasted_iota(jnp.int32, (tk, tk), 0)
    qry_i = lax.broadcasted_iota(jnp.int32, (tk, tk), 1)
    causal = key_i <= qry_i
    chains = [(hh, r) for r in range(ng) for hh in range(2)]
    feat = lax.broadcasted_iota(jnp.int32, (LANES, 1), 0)
    qms = []
    for hh, r in chains:
        q_t = q_ref[r * tk:(r + 1) * tk, :].astype(F32).T
        own = (feat < C_HEAD_DIM) if hh == 0 else (feat >= C_HEAD_DIM)
        qh = jnp.where(own, q_t, 0.0).astype(BF16)
        pick = (feat >= BIAS_PARTS * hh) & (feat < BIAS_PARTS * (hh + 1))
        minus_one = jnp.broadcast_to(jnp.where(pick, -1.0, 0.0).astype(BF16), qh.shape)
        qms.append(jnp.concatenate([qh, minus_one], axis=0))

    def key_rows(j):
        return pl.ds(pl.multiple_of(j * tk, tk), tk)

    def scores_to_scratch(j, slot, live, modes):
        kb = k_ref[key_rows(j), :]
        raw = [_dot(kb, qms[idx]) for idx in live]
        maxes = []
        for s, idx in zip(raw, live):
            hh, r = chains[idx]
            if modes[r] == "diag":
                s = jnp.where(causal, s, NEG_BIG)
            s_ref[slot, idx] = s
            maxes.append(jnp.max(s, axis=0, keepdims=True))
        return maxes

    ones_rows = (lax.broadcasted_iota(jnp.int32, (V_ROWS - C_HEAD_DIM, tk), 0) == 0).astype(BF16)

    def softmax_pv(j, slot, maxes, live, state):
        vt_pair = vt_ref[j]
        vtb = [jnp.concatenate([vt_pair[hh * C_HEAD_DIM:(hh + 1) * C_HEAD_DIM, :], ones_rows], axis=0)
               for hh in range(2)]
        new = list(state)
        probs = []
        for bm, idx in zip(maxes, live):
            m = state[2 * idx]
            m_new = jnp.maximum(m, bm)
            p = jnp.exp2(s_ref[slot, idx] - m_new)
            new[2 * idx] = m_new
            probs.append((jnp.exp2(m - m_new), p.astype(BF16)))
        for (alpha, p), idx in zip(probs, live):
            hh, r = chains[idx]
            new[2 * idx + 1] = alpha * state[2 * idx + 1] + _dot(vtb[hh], p)
        return new

    assert ng % 2 == 0
    all_chains = list(range(len(chains)))
    n_state = 2 * len(chains)
    state = []
    for _ in chains:
        state += [jnp.full((1, tk), NEG_BIG, F32), jnp.zeros((V_ROWS, tk), F32)]
    n_full = qi * ng
    full_modes = ("full",) * ng

    def diag_modes(g):
        return tuple("skip" if r < g else ("diag" if r == g else "full") for r in range(ng))

    def live_chains(g):
        return [idx for idx, (hh, r) in enumerate(chains) if r >= g]

    def trip(j, carry, next_modes):
        st, mx = list(carry[:n_state]), carry[n_state:]
        for b in range(ng):
            mx_next = scores_to_scratch(j + b + 1, (b + 1) % 2, all_chains, full_modes if b + 1 < ng else next_modes)
            st = softmax_pv(j + b, b % 2, mx, all_chains, st)
            mx = mx_next
        return tuple(st) + tuple(mx)

    def with_full_blocks(_):
        first = scores_to_scratch(0, 0, all_chains, full_modes)
        carry = lax.fori_loop(0, qi - 1, lambda i, c: trip(ng * i, c, full_modes), tuple(state) + tuple(first))
        return trip(n_full - ng, carry, diag_modes(0))

    def no_full_blocks(_):
        return tuple(state) + tuple(scores_to_scratch(0, 0, all_chains, diag_modes(0)))

    carry = lax.cond(qi > 0, with_full_blocks, no_full_blocks, 0)
    state, mx = list(carry[:n_state]), carry[n_state:]
    for g in range(ng):
        if g + 1 < ng:
            mx_next = scores_to_scratch(n_full + g + 1, (g + 1) % 2, live_chains(g + 1), diag_modes(g + 1))
        state = softmax_pv(n_full + g, g % 2, mx, live_chains(g), state)
        if g + 1 < ng:
            mx = mx_next

    for r in range(ng):
        parts = []
        for hh in range(2):
            acc = state[2 * chains.index((hh, r)) + 1]
            parts.append(acc[:C_HEAD_DIM, :] / acc[C_HEAD_DIM:C_HEAD_DIM + 1, :])
        o = jnp.concatenate(parts, axis=0).T
        rows = slice(r * tk, (r + 1) * tk)
        o_ref[rows, :] = (o * gate_ref[rows, :].astype(F32)).astype(BF16)


def _fox_attention(q, k_ext, vt, gate, wcast, *, batch, seq, tq, tk):
    t = q.shape[0]
    nq = seq // tq
    kern = functools.partial(_fox_kernel, tq=tq, tk=tk)
    cast_in, cast_out, cast_shape = _side_cast_specs(wcast, batch * N_PAIRS * nq,
                                                     lambda b, p, i: ((b * N_PAIRS + p) * nq + i, 0))
    return pl.pallas_call(
        kern,
        out_shape=[jax.ShapeDtypeStruct((t, C_WIDTH), BF16), cast_shape],
        grid=(batch, N_PAIRS, nq),
        in_specs=[
            pl.BlockSpec((tq, LANES), lambda b, p, i: (b * nq + i, p)),
            pl.BlockSpec((seq, K_EXT), lambda b, p, i: (b, p)),
            pl.BlockSpec((None, None, seq // tk, LANES, tk), lambda b, p, i: (b, p, 0, 0, 0)),
            pl.BlockSpec((tq, LANES), lambda b, p, i: (b * nq + i, p)),
            cast_in,
        ],
        out_specs=[pl.BlockSpec((tq, LANES), lambda b, p, i: (b * nq + i, p)), cast_out],
        scratch_shapes=[pltpu.VMEM((2, 2 * (tq // tk), tk, tk), F32)],
        compiler_params=pltpu.CompilerParams(dimension_semantics=("arbitrary", "arbitrary", "arbitrary"),
                                             vmem_limit_bytes=VMEM_LIMIT),
        name="fox_attn",
    )(q, k_ext, vt, gate, wcast)


def _store_token_tiles(dst_ref, val, n_rows):
    for s in range(ROW_TILE):
        dst_ref[pl.ds(s, n_rows, stride=ROW_TILE), :] = val[:, s * LANES:(s + 1) * LANES]


def _load_token_tiles(src_ref, n_rows):
    return jnp.concatenate([src_ref[pl.ds(s, n_rows, stride=ROW_TILE), :] for s in range(ROW_TILE)], axis=1)


def _odd_out_kernel(x_ref, a_ref, wo_ref, nrm_ref, r_ref, wcast_ref, x3_ref, h_ref, route_ref, wcast_out_ref):
    wcast_out_ref[...] = wcast_ref[...].astype(BF16)
    x3 = x_ref[...] + _dot(a_ref[...], wo_ref[...])
    x3_ref[...] = x3
    h = _rms(x3, nrm_ref[...])
    _store_token_tiles(h_ref, h, h.shape[0])

    h_hi = h.astype(BF16)
    h_lo = (h - h_hi.astype(F32)).astype(BF16)
    r = r_ref[...]
    r_hi = r.astype(BF16)
    r_lo = (r - r_hi.astype(F32)).astype(BF16)
    n = h.shape[0]
    prod = _dot(jnp.concatenate([h_hi, h_lo], axis=0), jnp.concatenate([r_hi, r_lo], axis=1))
    logits = prod[:n, :LANES] + (prod[n:, :LANES] + prod[:n, LANES:])

    lane = lax.broadcasted_iota(jnp.int32, logits.shape, 1).astype(F32)
    neg_inf = jnp.float32(-jnp.inf)
    lg = jnp.where(lane < N_EXPERTS, logits, neg_inf)
    m1 = jnp.max(lg, axis=-1, keepdims=True)
    i1 = jnp.min(jnp.where(lg == m1, lane, float(LANES)), axis=-1, keepdims=True)
    lg2 = jnp.where(lane == i1, neg_inf, lg)
    m2 = jnp.max(lg2, axis=-1, keepdims=True)
    i2 = jnp.min(jnp.where(lg2 == m2, lane, float(LANES)), axis=-1, keepdims=True)
    e2 = jnp.exp(m2 - m1)
    g1 = 1.0 / (1.0 + e2)
    g2 = e2 / (1.0 + e2)
    route_ref[...] = jnp.where(lane == 0, i1, jnp.where(lane == 1, i2, jnp.where(lane == 2, g1,
                               jnp.where(lane == 3, g2, 0.0))))


def _odd_out(x, a, wo, nrm, router, wcast, *, tm):
    t = x.shape[0]
    row_spec = pl.BlockSpec((tm, D_MODEL), lambda i: (i, 0))
    cast_in, cast_out, cast_shape = _side_cast_specs(wcast, t // tm, lambda i: (i, 0))
    return pl.pallas_call(
        _odd_out_kernel,
        out_shape=[jax.ShapeDtypeStruct((t, D_MODEL), F32), jax.ShapeDtypeStruct((t * ROW_TILE, LANES), F32),
                   jax.ShapeDtypeStruct((t, LANES), F32), cast_shape],
        grid=(t // tm,),
        in_specs=[row_spec, row_spec, _const_spec(wo.shape), _const_spec(nrm.shape), _const_spec(router.shape),
                  cast_in],
        out_specs=[row_spec, pl.BlockSpec((tm * ROW_TILE, LANES), lambda i: (i, 0)),
                   pl.BlockSpec((tm, LANES), lambda i: (i, 0)), cast_out],
        compiler_params=pltpu.CompilerParams(dimension_semantics=("arbitrary",), vmem_limit_bytes=VMEM_LIMIT),
        name="odd_out",
    )(x, a, wo, nrm, router, wcast)


def _row_gather_copy(src_hbm, src_row, dst_ref, dst_row, sem):
    return pltpu.make_async_copy(
        src_hbm.at[pl.ds(pl.multiple_of(src_row * ROW_TILE, ROW_TILE), ROW_TILE), :],
        dst_ref.at[pl.ds(pl.multiple_of(dst_row * ROW_TILE, ROW_TILE), ROW_TILE), :],
        sem)


GATHER_UNROLL = 8


def _start_row_gathers(src_hbm, idx_ref, idx_row, first, count, dst_ref, sem, *, inline):
    if inline:
        for u in range(count):
            _row_gather_copy(src_hbm, idx_ref[idx_row, first + u], dst_ref, first + u, sem).start(priority=u % 2)
        return

    def issue(r2, c):
        for u in range(2):
            r = 2 * r2 + u
            _row_gather_copy(src_hbm, idx_ref[idx_row, r], dst_ref, r, sem).start(priority=u)
        return c

    assert first % 2 == 0 and count % 2 == 0
    lax.fori_loop(first // 2, (first + count) // 2, issue, 0, unroll=GATHER_UNROLL // 2)


def _wait_row_gathers(src_hbm, dst_ref, sem):
    pltpu.make_async_copy(src_hbm.at[pl.ds(0, dst_ref.shape[0]), :], dst_ref, sem).wait()


def _moe_ffn_kernel(te_ref, nu_ref, tok_ref, tok_next_ref, h_hbm, w1_ref, w3_ref, w2_ref, o_ref, xs_ref, buf_ref,
                    sem, *, tm, n_chunks):
    i = pl.program_id(0)
    n_used = nu_ref[0]
    used = i < n_used
    slot = i % 2

    @pl.when(i == 0)
    def _():
        _start_row_gathers(h_hbm, tok_ref, 0, 0, tm, buf_ref.at[0], sem.at[0], inline=False)

    @pl.when(i <= n_used)
    def _():
        _wait_row_gathers(h_hbm, buf_ref.at[slot], sem.at[slot])
        xs_ref[...] = _load_token_tiles(buf_ref.at[slot], tm).astype(BF16)

    @pl.when(used)
    def _():
        n_groups = 3 * n_chunks
        per_group = tm // n_groups
        starts = [(g * per_group, per_group if g + 1 < n_groups else tm - g * per_group) for g in range(n_groups)]

        def prefetch(g):
            first, count = starts[g]
            _start_row_gathers(h_hbm, tok_next_ref, 0, first, count, buf_ref.at[1 - slot], sem.at[1 - slot],
                               inline=True)

        x = xs_ref[...]
        tf = D_FF_EXPERT // n_chunks
        y = None
        for c in range(n_chunks):
            cols = slice(c * tf, (c + 1) * tf)
            prefetch(3 * c)
            a = _dot(x, w1_ref[:, cols])
            prefetch(3 * c + 1)
            b = _dot(x, w3_ref[:, cols])
            prefetch(3 * c + 2)
            part = _dot((a * _sigmoid(a) * b).astype(BF16), w2_ref[cols, :])
            y = part if y is None else y + part
        _store_token_tiles(o_ref, y, tm)

    @pl.when(jnp.logical_not(used))
    def _():
        o_ref[...] = jnp.zeros_like(o_ref)


def _moe_ffn(tile_expert, n_used, row_tok, h_tiles, w1, w3, w2, *, tm, n_chunks):
    n_tiles = row_tok.shape[0] - 1

    def expert_spec(shape):
        return pl.BlockSpec((None,) + shape, lambda i, te, nu: (te[i], 0, 0), pipeline_mode=pl.Buffered(1))

    grid_spec = pltpu.PrefetchScalarGridSpec(
        num_scalar_prefetch=2,
        grid=(n_tiles,),
        in_specs=[
            pl.BlockSpec((None, 1, tm), lambda i, te, nu: (i, 0, 0), memory_space=pltpu.SMEM),
            pl.BlockSpec((None, 1, tm), lambda i, te, nu: (i + 1, 0, 0), memory_space=pltpu.SMEM),
            pl.BlockSpec(memory_space=pl.ANY),
            expert_spec((D_MODEL, D_FF_EXPERT)), expert_spec((D_MODEL, D_FF_EXPERT)),
            expert_spec((D_FF_EXPERT, D_MODEL)),
        ],
        out_specs=pl.BlockSpec((tm * ROW_TILE, LANES), lambda i, te, nu: (i, 0)),
        scratch_shapes=[pltpu.VMEM((tm, D_MODEL), BF16), pltpu.VMEM((2, tm * ROW_TILE, LANES), F32),
                        pltpu.SemaphoreType.DMA((2,))],
    )
    return pl.pallas_call(
        functools.partial(_moe_ffn_kernel, tm=tm, n_chunks=n_chunks),
        out_shape=jax.ShapeDtypeStruct((n_tiles * tm * ROW_TILE, LANES), F32),
        grid_spec=grid_spec,
        compiler_params=pltpu.CompilerParams(dimension_semantics=("arbitrary",), vmem_limit_bytes=VMEM_LIMIT),
        name="moe_ffn",
    )(tile_expert, n_used, row_tok, row_tok, h_tiles, w1, w3, w2)


def _moe_combine_kernel(pos_ref, pos_next_ref, x_ref, route_ref, nrm_ref, y_hbm, o_ref, buf_ref, sem, *, tm):
    i = pl.program_id(0)
    slot = i % 2

    def start_tile(idx_ref, s, inline):
        for k in range(2):
            _start_row_gathers(y_hbm, idx_ref, k, 0, tm, buf_ref.at[s, k], sem.at[s, k], inline=inline)

    @pl.when(i == 0)
    def _():
        start_tile(pos_ref, 0, False)

    @pl.when(i + 1 < pl.num_programs(0))
    def _():
        start_tile(pos_next_ref, 1 - slot, True)

    for k in range(2):
        _wait_row_gathers(y_hbm, buf_ref.at[slot, k], sem.at[slot, k])
    g1 = route_ref[:, 2:3]
    g2 = route_ref[:, 3:4]
    x = x_ref[...] + (g1 * _load_token_tiles(buf_ref.at[slot, 0], tm) + g2 * _load_token_tiles(buf_ref.at[slot, 1], tm))
    o_ref[...] = _rms(x, nrm_ref[...])


def _moe_combine(pos, x, route, nrm, y_tiles, *, tm):
    t = x.shape[0]
    n = t // tm
    row_spec = pl.BlockSpec((tm, D_MODEL), lambda i: (i, 0))
    return pl.pallas_call(
        functools.partial(_moe_combine_kernel, tm=tm),
        out_shape=jax.ShapeDtypeStruct((t, D_MODEL), F32),
        grid=(n,),
        in_specs=[pl.BlockSpec((None, 2, tm), lambda i: (i, 0, 0), memory_space=pltpu.SMEM),
                  pl.BlockSpec((None, 2, tm), lambda i: (jnp.minimum(i + 1, n - 1), 0, 0), memory_space=pltpu.SMEM),
                  row_spec, pl.BlockSpec((tm, LANES), lambda i: (i, 0)), _const_spec(nrm.shape),
                  pl.BlockSpec(memory_space=pl.ANY)],
        out_specs=row_spec,
        scratch_shapes=[pltpu.VMEM((2, 2, tm * ROW_TILE, LANES), F32), pltpu.SemaphoreType.DMA((2, 2))],
        compiler_params=pltpu.CompilerParams(dimension_semantics=("arbitrary",), vmem_limit_bytes=VMEM_LIMIT),
        name="moe_combine",
    )(pos, pos, x, route, nrm, y_tiles)


SCATTER_CHUNK = 2048


def _invert_rows_kernel(pos_ref, out_ref, *, n_tokens):
    j = pl.program_id(0)

    @pl.when(j == 0)
    def _():
        def zero(r, c):
            out_ref[r] = 0
            return c

        lax.fori_loop(0, out_ref.shape[0], zero, 0, unroll=32)

    assert n_tokens & (n_tokens - 1) == 0
    base = j * SCATTER_CHUNK

    def place(u, c):
        out_ref[pos_ref[0, u]] = (base + u) & (n_tokens - 1)
        return c

    lax.fori_loop(0, SCATTER_CHUNK, place, 0, unroll=32)


def _invert_rows(pos, n_rows, n_tokens):
    n_chunks = pos.shape[0] // SCATTER_CHUNK
    assert n_chunks * SCATTER_CHUNK == pos.shape[0]
    return pl.pallas_call(
        functools.partial(_invert_rows_kernel, n_tokens=n_tokens),
        out_shape=jax.ShapeDtypeStruct((n_rows,), jnp.int32),
        grid=(n_chunks,),
        in_specs=[pl.BlockSpec((None, 1, SCATTER_CHUNK), lambda j: (j, 0, 0), memory_space=pltpu.SMEM)],
        out_specs=pl.BlockSpec(memory_space=pltpu.SMEM),
        compiler_params=pltpu.CompilerParams(dimension_semantics=("arbitrary",)),
        name="invert_rows",
    )(pos.reshape(n_chunks, 1, SCATTER_CHUNK))


def _routing_tables(idx1, idx2, *, tm, n_tiles):
    t = idx1.shape[0]
    e_flat = jnp.concatenate([idx1, idx2])
    onehot = (e_flat[:, None] == jnp.arange(N_EXPERTS, dtype=jnp.int32)[None, :]).astype(jnp.int32)
    csum = jnp.cumsum(onehot, axis=0)
    rank = jnp.sum((csum - onehot) * onehot, axis=1)
    counts = csum[-1]
    tiles_e = (counts + tm - 1) // tm
    tile_end = jnp.cumsum(tiles_e)
    tile_start = tile_end - tiles_e
    pos = jnp.sum(onehot * tile_start[None, :], axis=1) * tm + rank
    n_used = tile_end[-1]
    tile_ids = jnp.arange(n_tiles, dtype=jnp.int32)
    te = jnp.sum((tile_ids[:, None] >= tile_end[None, :]).astype(jnp.int32), axis=1)
    te_last = jnp.sum((n_used - 1 >= tile_end).astype(jnp.int32))
    tile_expert = jnp.where(tile_ids < n_used, te, te_last).astype(jnp.int32)
    row_tok = _invert_rows(pos.astype(jnp.int32), n_tiles * tm, t)
    return pos[:t], pos[t:], row_tok, tile_expert, n_used.reshape(1).astype(jnp.int32)


def _pad_cols(w, n):
    return jnp.pad(w, ((0, 0), (0, n - w.shape[1])))


def kernel(x, even_norm_mix, even_w_in, even_gate_up, even_gate_bias, even_w_s, even_b_s, even_ln_g, even_ln_b,
           even_head_g, even_w_o, even_norm_ffn, even_ffn_w1, even_ffn_w3, even_ffn_w2, odd_norm_mix, odd_w_in,
           odd_forget_bias, odd_q_g, odd_k_g, odd_w_o, odd_norm_ffn, odd_router, odd_exp_w1, odd_exp_w3,
           odd_exp_w2, final_norm):
    batch, seq, d = x.shape
    t = batch * seq
    xt = x.reshape(t, d)
    tm = min(512, seq)

    w_in = even_w_in[0]
    u_w, v_w, q_w, k_w, g_w, vb_w, og_w = jnp.split(w_in, [512, 1024, 1280, 1536, 1552, 2064], axis=1)
    win_e = jnp.concatenate([u_w, v_w, q_w, k_w, vb_w, og_w, _pad_cols(g_w, LANES)], axis=1).astype(BF16)
    gup = jnp.pad(even_gate_up[0], ((0, LANES - B_GATE_RANK), (0, 0))).astype(BF16)
    gb = even_gate_bias[0].reshape(1, B_QK_WIDTH)
    tril = jnp.tril(jnp.ones((CHUNK, CHUNK), dtype=bool))
    ws = jnp.where(tril[None], even_w_s[0], 0.0).astype(BF16)
    bs = jnp.broadcast_to(even_b_s[0][:, :, None], (A_GROUPS, CHUNK, LANES))
    lng = even_ln_g[0].reshape(1, A_WIDTH)
    lnb = even_ln_b[0].reshape(1, A_WIDTH)
    hg = even_head_g[0].reshape(B_HEADS, 1, B_VAL_DIM)
    ew1 = odd_exp_w1[0].reshape(N_EXPERTS * D_MODEL, D_FF_EXPERT)
    ew3 = odd_exp_w3[0].reshape(N_EXPERTS * D_MODEL, D_FF_EXPERT)
    ew2 = odd_exp_w2[0].reshape(N_EXPERTS * D_FF_EXPERT, D_MODEL)
    x1, ew1_b, fw1_b, fw3_b, fw2_b = _even_mixer(
        xt, even_norm_mix[0].reshape(1, d), win_e, gup, gb, ws, bs, lng, lnb, hg, even_w_o[0].astype(BF16),
        [ew1, even_ffn_w1[0], even_ffn_w3[0], even_ffn_w2[0]], seq=seq, tm=tm)
    x2, win_o = _dense_ffn(x1, even_norm_ffn[0].reshape(1, d), fw1_b, fw3_b, fw2_b, odd_w_in[0], O_F, tm=tm)

    wf_o = _pad_cols(odd_w_in[0][:, O_F:], LANES).astype(BF16)
    fb = jnp.pad(odd_forget_bias[0], (0, LANES - C_HEADS)).reshape(1, LANES)
    qg = jnp.tile(odd_q_g[0], C_HEADS).reshape(1, C_WIDTH)
    kg = jnp.tile(odd_k_g[0], C_HEADS).reshape(1, C_WIDTH)
    tk = min(256, seq)
    tq = min(1024, seq)
    q, k_ext, vt, gate = _odd_inproj(x2, odd_norm_mix[0].reshape(1, d), win_o, wf_o, fb, qg, kg, seq=seq, tm=tm,
                                     tk=tk)
    attn, ew2_b = _fox_attention(q, k_ext, vt, gate, ew2, batch=batch, seq=seq, tq=tq, tk=tk)

    router = _pad_cols(odd_router[0], LANES)
    x3, h_tiles, route, ew3_b = _odd_out(x2, attn, odd_w_o[0].astype(BF16), odd_norm_ffn[0].reshape(1, d), router,
                                         ew3, tm=tm)

    tm_moe = 512
    tm_comb = min(256, seq)
    n_tiles = (2 * t) // tm_moe + N_EXPERTS + 1
    idx1 = route[:, 0].astype(jnp.int32)
    idx2 = route[:, 1].astype(jnp.int32)
    pos1, pos2, row_tok, tile_expert, n_used = _routing_tables(idx1, idx2, tm=tm_moe, n_tiles=n_tiles + 1)
    y_tiles = _moe_ffn(tile_expert[:n_tiles], n_used, row_tok.reshape(n_tiles + 1, 1, tm_moe), h_tiles,
                       ew1_b.reshape(N_EXPERTS, D_MODEL, D_FF_EXPERT), ew3_b.reshape(N_EXPERTS, D_MODEL, D_FF_EXPERT),
                       ew2_b.reshape(N_EXPERTS, D_FF_EXPERT, D_MODEL), tm=tm_moe, n_chunks=2)
    pos = jnp.stack([pos1.reshape(t // tm_comb, tm_comb), pos2.reshape(t // tm_comb, tm_comb)], axis=1)
    out = _moe_combine(pos, x3, route, final_norm.reshape(1, d), y_tiles, tm=tm_comb)
    return out.reshape(batch, seq, d)
```

```python
import functools
import math

import jax
import jax.numpy as jnp
from jax import lax
from jax.experimental import pallas as pl
from jax.experimental.pallas import tpu as pltpu

F32 = jnp.float32
BF16 = jnp.bfloat16
HIGHEST = lax.Precision.HIGHEST

EPS = 1e-6
D_MODEL = 1024
CHUNK = 128
SUB = 32
N_SUB = CHUNK // SUB
A_GROUPS = 4
A_WIDTH = 512
B_HEADS = 4
B_KEY_DIM = 64
B_VAL_DIM = 128
B_QK_WIDTH = 256
B_V_WIDTH = 512
B_GATE_RANK = 16
B_GATE_NORMALIZER = 16.0
C_HEADS = 16
C_HEAD_DIM = 64
C_WIDTH = 1024
D_FF_DENSE = 2816
N_EXPERTS = 8
D_FF_EXPERT = 3584
LANES = 128
MAX_DECAY_EXP = 60.0
LOG2E = math.log2(math.e)
ROW_TILE = 8

E_U, E_V, E_Q, E_K, E_VB, E_OG, E_G, E_END = 0, 512, 1024, 1280, 1536, 2048, 2560, 2688
O_Q, O_K, O_V, O_OG, O_F, O_END = 0, 1024, 2048, 3072, 4096, 4224

VMEM_LIMIT = 56 * 1024 * 1024


def _rms(x, g):
    ms = jnp.mean(x * x, axis=-1, keepdims=True)
    return x * lax.rsqrt(ms + EPS) * g


def _gelu_tanh(x):
    c = math.sqrt(2.0 / math.pi)
    return x * (0.5 * (1.0 + jnp.tanh(c * (x + 0.044715 * (x * x * x)))))


def _sigmoid(x):
    return 1.0 / (1.0 + jnp.exp(-x))


def _log_sigmoid(x):
    return jnp.minimum(x, 0.0) - jnp.log(1.0 + jnp.exp(-jnp.abs(x)))


def _dot(a, b):
    return jnp.dot(a, b, preferred_element_type=F32)


def _dot_nt(a, b):
    return lax.dot_general(a, b, (((1,), (1,)), ((), ())), preferred_element_type=F32)


def _split3(x):
    hi = x.astype(BF16)
    r1 = x - hi.astype(F32)
    mid = r1.astype(BF16)
    lo = (r1 - mid.astype(F32)).astype(BF16)
    return hi, mid, lo


def _cumsum_rows(tril_b, x):
    hi, mid, lo = _split3(x)
    return _dot(tril_b, hi) + _dot(tril_b, mid) + _dot(tril_b, lo)


def _const_spec(shape):
    nd = len(shape)
    return pl.BlockSpec(shape, lambda *_: (0,) * nd)


def _side_cast_specs(w2d, n_steps, index_map):
    rows = w2d.shape[0] // n_steps
    assert rows * n_steps == w2d.shape[0] and rows % 16 == 0
    spec = pl.BlockSpec((rows, w2d.shape[1]), index_map)
    return spec, spec, jax.ShapeDtypeStruct(w2d.shape, BF16)


def _side_cast_specs_1d(w2d, n_steps):
    span = 1 if (w2d.shape[0] // n_steps) % 16 == 0 and w2d.shape[0] % n_steps == 0 else 2
    return _side_cast_specs(w2d, n_steps // span, lambda i: (i // span, 0))


def _even_mixer_kernel(x_ref, nrm_ref, win_ref, gup_ref, gb_ref, ws_ref, bs_ref, lng_ref, lnb_ref,
                       hg_ref, wo_ref, *rest, tiles_per_batch, n_chunks, n_casts):
    cast_refs, o_ref, cast_out_refs = rest[:n_casts], rest[n_casts], rest[n_casts + 1:2 * n_casts + 1]
    z_ref, mix_ref, st_ref = rest[2 * n_casts + 1:]
    i = pl.program_id(0)
    for src, dst in zip(cast_refs, cast_out_refs):
        dst[...] = src[...].astype(BF16)

    @pl.when(i % tiles_per_batch == 0)
    def _():
        st_ref[...] = jnp.zeros_like(st_ref)

    h = _rms(x_ref[...], nrm_ref[...]).astype(BF16)
    z_ref[...] = _dot(h, win_ref[...])

    row = lax.broadcasted_iota(jnp.int32, (CHUNK, CHUNK), 0)
    col = lax.broadcasted_iota(jnp.int32, (CHUNK, CHUNK), 1)
    tril_b = (col <= row).astype(BF16)
    sub_row = row & (SUB - 1)
    head_lane = lax.broadcasted_iota(jnp.int32, (1, B_QK_WIDTH), 1) // B_KEY_DIM
    bd_mask = (lax.broadcasted_iota(jnp.int32, (B_V_WIDTH, B_QK_WIDTH), 0) // B_VAL_DIM
               == lax.broadcasted_iota(jnp.int32, (B_V_WIDTH, B_QK_WIDTH), 1) // B_KEY_DIM)

    def chunk_body(c, carry):
        rows = pl.ds(pl.multiple_of(c * CHUNK, CHUNK), CHUNK)

        u = _gelu_tanh(z_ref[rows, E_U:E_V])
        v = _gelu_tanh(z_ref[rows, E_V:E_Q])
        mu = jnp.mean(v, axis=-1, keepdims=True)
        vc = v - mu
        var = jnp.mean(vc * vc, axis=-1, keepdims=True)
        vln = (vc * lax.rsqrt(var + EPS) * lng_ref[...] + lnb_ref[...]).astype(BF16)
        for g in range(A_GROUPS):
            sl = slice(g * LANES, (g + 1) * LANES)
            mixed = _dot(ws_ref[g], vln[:, sl]) + bs_ref[g]
            mix_ref[rows, sl] = (u[:, sl] * mixed).astype(BF16)

        q = z_ref[rows, E_Q:E_K] * (B_KEY_DIM ** -0.5)
        k = z_ref[rows, E_K:E_VB]
        vb = z_ref[rows, E_VB:E_OG]
        og = z_ref[rows, E_OG:E_G]
        glr = z_ref[rows, E_G:E_END].astype(BF16)
        logit = _dot(glr, gup_ref[...]) + gb_ref[...]
        log_a = _log_sigmoid(logit) * (1.0 / B_GATE_NORMALIZER)
        g_cum = _cumsum_rows(tril_b, log_a)
        g_last = g_cum[CHUNK - 1:CHUNK, :]
        st = st_ref[...]
        o = _dot_nt((q * jnp.exp(g_cum)).astype(BF16), st.astype(BF16))

        p_rows = [[None] * N_SUB for _ in range(B_HEADS)]
        for s in range(N_SUB):
            gs = g_cum[s * SUB:(s + 1) * SUB, :]
            if s == 0:
                qt = q[0:SUB, :] * jnp.exp(gs)
                kt = k * jnp.exp(jnp.minimum(-g_cum, MAX_DECAY_EXP))
            else:
                ref_g = g_cum[s * SUB - 1:s * SUB, :]
                qt = q[s * SUB:(s + 1) * SUB, :] * jnp.exp(gs - ref_g)
                kt = k * jnp.exp(jnp.minimum(ref_g - g_cum, MAX_DECAY_EXP))
            qs = jnp.concatenate([jnp.where(head_lane == hh, qt, 0.0) for hh in range(B_HEADS)],
                                 axis=0).astype(BF16)
            sc = _dot_nt(qs, kt.astype(BF16))
            sc = jnp.where(col <= (s * SUB + sub_row), sc, 0.0)
            for hh in range(B_HEADS):
                p_rows[hh][s] = sc[hh * SUB:(hh + 1) * SUB, :]

        vb_b = vb.astype(BF16)
        for hh in range(B_HEADS):
            sl = slice(hh * B_VAL_DIM, (hh + 1) * B_VAL_DIM)
            ph = jnp.concatenate(p_rows[hh], axis=0).astype(BF16)
            oh = o[:, sl] + _dot(ph, vb_b[:, sl])
            on = _rms(oh, hg_ref[hh])
            ogh = og[:, sl]
            mix_ref[rows, A_WIDTH + hh * B_VAL_DIM:A_WIDTH + (hh + 1) * B_VAL_DIM] = (
                on * (ogh * _sigmoid(ogh))).astype(BF16)

        k_dec = (k * jnp.exp(g_last - g_cum)).astype(BF16)
        upd = _dot(vb.T.astype(BF16), k_dec)
        st_ref[...] = jnp.exp(g_last) * st + jnp.where(bd_mask, upd, 0.0)
        return carry

    lax.fori_loop(0, n_chunks, chunk_body, 0, unroll=True)
    o_ref[...] = x_ref[...] + _dot(mix_ref[...], wo_ref[...])


def _even_mixer(x, nrm, win, gup, gb, ws, bs, lng, lnb, hg, wo, wcasts, *, seq, tm):
    t = x.shape[0]
    kern = functools.partial(_even_mixer_kernel, tiles_per_batch=seq // tm, n_chunks=tm // CHUNK,
                             n_casts=len(wcasts))
    casts = [_side_cast_specs_1d(w, t // tm) for w in wcasts]
    return pl.pallas_call(
        kern,
        out_shape=[jax.ShapeDtypeStruct((t, D_MODEL), F32)] + [c[2] for c in casts],
        grid=(t // tm,),
        in_specs=[
            pl.BlockSpec((tm, D_MODEL), lambda i: (i, 0)),
            _const_spec(nrm.shape), _resident_spec(win.shape), _const_spec(gup.shape), _const_spec(gb.shape),
            _const_spec(ws.shape), _const_spec(bs.shape), _const_spec(lng.shape), _const_spec(lnb.shape),
            _const_spec(hg.shape), _resident_spec(wo.shape),
        ] + [c[0] for c in casts],
        out_specs=[pl.BlockSpec((tm, D_MODEL), lambda i: (i, 0))] + [c[1] for c in casts],
        scratch_shapes=[
            pltpu.VMEM((tm, E_END), F32),
            pltpu.VMEM((tm, D_MODEL), BF16),
            pltpu.VMEM((B_V_WIDTH, B_QK_WIDTH), F32),
        ],
        compiler_params=pltpu.CompilerParams(dimension_semantics=("arbitrary",), vmem_limit_bytes=VMEM_LIMIT),
        name="even_mixer",
    )(x, nrm, win, gup, gb, ws, bs, lng, lnb, hg, wo, *wcasts)


def _dense_ffn_kernel(x_ref, nrm_ref, w1_ref, w3_ref, w2_ref, wcast_ref, o_ref, wcast_out_ref):
    wcast_out_ref[...] = wcast_ref[:, :wcast_out_ref.shape[1]].astype(BF16)
    x = x_ref[...]
    h = _rms(x, nrm_ref[...]).astype(BF16)
    a = _dot(h, w1_ref[...])
    b = _dot(h, w3_ref[...])
    o_ref[...] = x + _dot((a * _sigmoid(a) * b).astype(BF16), w2_ref[...])


def _resident_spec(shape):
    nd = len(shape)
    return pl.BlockSpec(shape, lambda *_: (0,) * nd, pipeline_mode=pl.Buffered(1))


def _dense_ffn(x, nrm, w1, w3, w2, wcast, wcast_cols, *, tm):
    t = x.shape[0]
    n_steps = t // tm
    row_spec = pl.BlockSpec((tm, D_MODEL), lambda i: (i, 0))
    rows = wcast.shape[0] // n_steps
    assert rows * n_steps == wcast.shape[0] and rows % 16 == 0
    return pl.pallas_call(
        _dense_ffn_kernel,
        out_shape=[jax.ShapeDtypeStruct((t, D_MODEL), F32), jax.ShapeDtypeStruct((wcast.shape[0], wcast_cols), BF16)],
        grid=(n_steps,),
        in_specs=[row_spec, _const_spec(nrm.shape), _resident_spec(w1.shape), _resident_spec(w3.shape),
                  _resident_spec(w2.shape), pl.BlockSpec((rows, wcast.shape[1]), lambda i: (i, 0))],
        out_specs=[row_spec, pl.BlockSpec((rows, wcast_cols), lambda i: (i, 0))],
        compiler_params=pltpu.CompilerParams(dimension_semantics=("arbitrary",), vmem_limit_bytes=VMEM_LIMIT),
        name="dense_ffn",
    )(x, nrm, w1, w3, w2, wcast)


def _head_rms(x, gain):
    lo = lax.broadcasted_iota(jnp.int32, (1, LANES), 1) < C_HEAD_DIM
    outs = []
    for t in range(C_WIDTH // LANES):
        xt = x[:, t * LANES:(t + 1) * LANES]
        sq = xt * xt
        s_lo = jnp.sum(jnp.where(lo, sq, 0.0), axis=-1, keepdims=True)
        s_hi = jnp.sum(jnp.where(lo, 0.0, sq), axis=-1, keepdims=True)
        inv = jnp.where(lo, lax.rsqrt(s_lo * (1.0 / C_HEAD_DIM) + EPS), lax.rsqrt(s_hi * (1.0 / C_HEAD_DIM) + EPS))
        outs.append(xt * inv)
    return jnp.concatenate(outs, axis=-1) * gain


N_PAIRS = C_HEADS // 2
K_EXT = 2 * LANES
BIAS_PARTS = 3


def _bias_placement():
    src = jnp.arange(BIAS_PARTS * LANES)
    part, head = src // LANES, src % LANES
    dst = (head // 2) * LANES + BIAS_PARTS * (head % 2) + part
    hit = (dst[:, None] == jnp.arange(N_PAIRS * LANES)[None, :]) & (head < C_HEADS)[:, None]
    return hit.astype(BF16)


def _odd_inproj_kernel(x_ref, nrm_ref, w_ref, wf_ref, fb_ref, qg_ref, kg_ref, place_ref,
                       q_ref, k_ref, vt_ref, gate_ref, z_ref, c_ref, carry_ref, *, tiles_per_batch, n_chunks, tk):
    i = pl.program_id(0)

    @pl.when(i % tiles_per_batch == 0)
    def _():
        carry_ref[...] = jnp.zeros_like(carry_ref)

    h = _rms(x_ref[...], nrm_ref[...]).astype(BF16)
    z_ref[:, :O_F] = _dot(h, w_ref[...])
    z_ref[:, O_F:] = _dot(h, wf_ref[...])
    q_ref[...] = (_head_rms(z_ref[:, O_Q:O_K], qg_ref[...]) * (C_HEAD_DIM ** -0.5 * LOG2E)).astype(BF16)
    kn = _head_rms(z_ref[:, O_K:O_V], kg_ref[...]).astype(BF16)
    for p in range(N_PAIRS):
        for kb in range(vt_ref.shape[1]):
            blk = z_ref[kb * tk:(kb + 1) * tk, O_V + p * LANES:O_V + (p + 1) * LANES]
            vt_ref[p, kb] = blk.T.astype(BF16)
    gate_ref[...] = _sigmoid(z_ref[:, O_OG:O_F]).astype(BF16)

    row = lax.broadcasted_iota(jnp.int32, (CHUNK, CHUNK), 0)
    col = lax.broadcasted_iota(jnp.int32, (CHUNK, CHUNK), 1)
    tril_b = (col <= row).astype(BF16)
    carry = carry_ref[...]
    for c in range(n_chunks):
        rows = slice(c * CHUNK, (c + 1) * CHUNK)
        log_f = _log_sigmoid(z_ref[rows, O_F:O_END] + fb_ref[...])
        cs = _cumsum_rows(tril_b, log_f) + carry
        c_ref[rows, :] = cs
        carry = cs[CHUNK - 1:CHUNK, :]
    carry_ref[...] = carry

    bias = _dot(jnp.concatenate(_split3(c_ref[...] * LOG2E), axis=1), place_ref[...]).astype(BF16)
    for p in range(N_PAIRS):
        k_ref[:, p * K_EXT:p * K_EXT + LANES] = kn[:, p * LANES:(p + 1) * LANES]
        k_ref[:, p * K_EXT + LANES:(p + 1) * K_EXT] = bias[:, p * LANES:(p + 1) * LANES]


def _odd_inproj(x, nrm, w, wf, fb, qg, kg, *, seq, tm, tk):
    t = x.shape[0]
    tiles_per_batch = seq // tm
    kern = functools.partial(_odd_inproj_kernel, tiles_per_batch=tiles_per_batch, n_chunks=tm // CHUNK, tk=tk)
    row_spec = pl.BlockSpec((tm, C_WIDTH), lambda i: (i, 0))
    kext_spec = pl.BlockSpec((tm, N_PAIRS * K_EXT), lambda i: (i, 0))
    vt_spec = pl.BlockSpec((None, N_PAIRS, tm // tk, LANES, tk),
                           lambda i: (i // tiles_per_batch, 0, i % tiles_per_batch, 0, 0))
    place = _bias_placement()
    wide = jax.ShapeDtypeStruct((t, C_WIDTH), BF16)
    return pl.pallas_call(
        kern,
        out_shape=[wide, jax.ShapeDtypeStruct((t, N_PAIRS * K_EXT), BF16),
                   jax.ShapeDtypeStruct((t // seq, N_PAIRS, seq // tk, LANES, tk), BF16), wide],
        grid=(t // tm,),
        in_specs=[row_spec, _const_spec(nrm.shape), _resident_spec(w.shape), _const_spec(wf.shape),
                  _const_spec(fb.shape), _const_spec(qg.shape), _const_spec(kg.shape), _const_spec(place.shape)],
        out_specs=[row_spec, kext_spec, vt_spec, row_spec],
        scratch_shapes=[pltpu.VMEM((tm, O_END), F32), pltpu.VMEM((tm, LANES), F32), pltpu.VMEM((1, LANES), F32)],
        compiler_params=pltpu.CompilerParams(dimension_semantics=("arbitrary",), vmem_limit_bytes=VMEM_LIMIT),
        name="odd_inproj",
    )(x, nrm, w, wf, fb, qg, kg, place)


NEG_BIG = -1e30


V_ROWS = 80


def _fox_kernel(q_ref, k_ref, vt_ref, gate_ref, *rest, tq, tk, n_casts):
    cast_refs, o_ref, cast_out_refs, s_ref = rest[:n_casts], rest[n_casts], rest[n_casts + 1:-1], rest[-1]
    for src, dst in zip(cast_refs, cast_out_refs):
        dst[...] = src[...].astype(BF16)
    qi = pl.program_id(2)
    ng = tq // tk
    key_i = lax.broadcasted_iota(jnp.int32, (tk, tk), 0)
    qry_i = lax.broadcasted_iota(jnp.int32, (tk, tk), 1)
    causal = key_i <= qry_i
    chains = [(hh, r) for r in range(ng) for hh in range(2)]
    feat = lax.broadcasted_iota(jnp.int32, (LANES, 1), 0)
    qms = []
    for hh, r in chains:
        q_t = q_ref[r * tk:(r + 1) * tk, :].astype(F32).T
        own = (feat < C_HEAD_DIM) if hh == 0 else (feat >= C_HEAD_DIM)
        qh = jnp.where(own, q_t, 0.0).astype(BF16)
        pick = (feat >= BIAS_PARTS * hh) & (feat < BIAS_PARTS * (hh + 1))
        minus_one = jnp.broadcast_to(jnp.where(pick, -1.0, 0.0).astype(BF16), qh.shape)
        qms.append(jnp.concatenate([qh, minus_one], axis=0))

    def key_rows(j):
        return pl.ds(pl.multiple_of(j * tk, tk), tk)

    def scores_to_scratch(j, slot, live, modes):
        kb = k_ref[key_rows(j), :]
        raw = [_dot(kb, qms[idx]) for idx in live]
        maxes = []
        for s, idx in zip(raw, live):
            hh, r = chains[idx]
            if modes[r] == "diag":
                s = jnp.where(causal, s, NEG_BIG)
            s_ref[slot, idx] = s
            maxes.append(jnp.max(s, axis=0, keepdims=True))
        return maxes

    ones_rows = (lax.broadcasted_iota(jnp.int32, (V_ROWS - C_HEAD_DIM, tk), 0) == 0).astype(BF16)

    def softmax_pv(j, slot, maxes, live, state):
        vt_pair = vt_ref[j]
        vtb = [jnp.concatenate([vt_pair[hh * C_HEAD_DIM:(hh + 1) * C_HEAD_DIM, :], ones_rows], axis=0)
               for hh in range(2)]
        new = list(state)
        probs = []
        for bm, idx in zip(maxes, live):
            m = state[2 * idx]
            m_new = jnp.maximum(m, bm)
            p = jnp.exp2(s_ref[slot, idx] - m_new)
            new[2 * idx] = m_new
            probs.append((jnp.exp2(m - m_new), p.astype(BF16)))
        for (alpha, p), idx in zip(probs, live):
            hh, r = chains[idx]
            new[2 * idx + 1] = alpha * state[2 * idx + 1] + _dot(vtb[hh], p)
        return new

    assert ng % 2 == 0
    all_chains = list(range(len(chains)))
    n_state = 2 * len(chains)
    state = []
    for _ in chains:
        state += [jnp.full((1, tk), NEG_BIG, F32), jnp.zeros((V_ROWS, tk), F32)]
    n_full = qi * ng
    full_modes = ("full",) * ng

    def diag_modes(g):
        return tuple("skip" if r < g else ("diag" if r == g else "full") for r in range(ng))

    def live_chains(g):
        return [idx for idx, (hh, r) in enumerate(chains) if r >= g]

    def trip(j, carry, next_modes):
        st, mx = list(carry[:n_state]), carry[n_state:]
        for b in range(ng):
            mx_next = scores_to_scratch(j + b + 1, (b + 1) % 2, all_chains, full_modes if b + 1 < ng else next_modes)
            st = softmax_pv(j + b, b % 2, mx, all_chains, st)
            mx = mx_next
        return tuple(st) + tuple(mx)

    def with_full_blocks(_):
        first = scores_to_scratch(0, 0, all_chains, full_modes)
        carry = lax.fori_loop(0, qi - 1, lambda i, c: trip(ng * i, c, full_modes), tuple(state) + tuple(first))
        return trip(n_full - ng, carry, diag_modes(0))

    def no_full_blocks(_):
        return tuple(state) + tuple(scores_to_scratch(0, 0, all_chains, diag_modes(0)))

    carry = lax.cond(qi > 0, with_full_blocks, no_full_blocks, 0)
    state, mx = list(carry[:n_state]), carry[n_state:]
    for g in range(ng):
        if g + 1 < ng:
            mx_next = scores_to_scratch(n_full + g + 1, (g + 1) % 2, live_chains(g + 1), diag_modes(g + 1))
        state = softmax_pv(n_full + g, g % 2, mx, live_chains(g), state)
        if g + 1 < ng:
            mx = mx_next

    for r in range(ng):
        parts = []
        for hh in range(2):
            acc = state[2 * chains.index((hh, r)) + 1]
            parts.append(acc[:C_HEAD_DIM, :] / acc[C_HEAD_DIM:C_HEAD_DIM + 1, :])
        o = jnp.concatenate(parts, axis=0).T
        rows = slice(r * tk, (r + 1) * tk)
        o_ref[rows, :] = (o * gate_ref[rows, :].astype(F32)).astype(BF16)


def _fox_attention(q, k_ext, vt, gate, wcasts, *, batch, seq, tq, tk):
    t = q.shape[0]
    nq = seq // tq
    kern = functools.partial(_fox_kernel, tq=tq, tk=tk, n_casts=len(wcasts))
    casts = [_side_cast_specs(w, batch * N_PAIRS * nq, lambda b, p, i: ((b * N_PAIRS + p) * nq + i, 0))
             for w in wcasts]
    return pl.pallas_call(
        kern,
        out_shape=[jax.ShapeDtypeStruct((t, C_WIDTH), BF16)] + [c[2] for c in casts],
        grid=(batch, N_PAIRS, nq),
        in_specs=[
            pl.BlockSpec((tq, LANES), lambda b, p, i: (b * nq + i, p)),
            pl.BlockSpec((seq, K_EXT), lambda b, p, i: (b, p)),
            pl.BlockSpec((None, None, seq // tk, LANES, tk), lambda b, p, i: (b, p, 0, 0, 0)),
            pl.BlockSpec((tq, LANES), lambda b, p, i: (b * nq + i, p)),
        ] + [c[0] for c in casts],
        out_specs=[pl.BlockSpec((tq, LANES), lambda b, p, i: (b * nq + i, p))] + [c[1] for c in casts],
        scratch_shapes=[pltpu.VMEM((2, 2 * (tq // tk), tk, tk), F32)],
        compiler_params=pltpu.CompilerParams(dimension_semantics=("arbitrary", "arbitrary", "arbitrary"),
                                             vmem_limit_bytes=VMEM_LIMIT),
        name="fox_attn",
    )(q, k_ext, vt, gate, *wcasts)


def _store_token_tiles(dst_ref, val, n_rows):
    for s in range(ROW_TILE):
        dst_ref[pl.ds(s, n_rows, stride=ROW_TILE), :] = val[:, s * LANES:(s + 1) * LANES]


def _load_token_tiles(src_ref, n_rows):
    return jnp.concatenate([src_ref[pl.ds(s, n_rows, stride=ROW_TILE), :] for s in range(ROW_TILE)], axis=1)


def _odd_out_kernel(x_ref, a_ref, wo_ref, nrm_ref, r_ref, x3_ref, h_ref, route_ref):
    x3 = x_ref[...] + _dot(a_ref[...], wo_ref[...])
    x3_ref[...] = x3
    h = _rms(x3, nrm_ref[...])
    _store_token_tiles(h_ref, h, h.shape[0])

    h_hi = h.astype(BF16)
    h_lo = (h - h_hi.astype(F32)).astype(BF16)
    r = r_ref[...]
    r_hi = r.astype(BF16)
    r_lo = (r - r_hi.astype(F32)).astype(BF16)
    n = h.shape[0]
    prod = _dot(jnp.concatenate([h_hi, h_lo], axis=0), jnp.concatenate([r_hi, r_lo], axis=1))
    logits = prod[:n, :LANES] + (prod[n:, :LANES] + prod[:n, LANES:])

    lane = lax.broadcasted_iota(jnp.int32, logits.shape, 1).astype(F32)
    neg_inf = jnp.float32(-jnp.inf)
    lg = jnp.where(lane < N_EXPERTS, logits, neg_inf)
    m1 = jnp.max(lg, axis=-1, keepdims=True)
    i1 = jnp.min(jnp.where(lg == m1, lane, float(LANES)), axis=-1, keepdims=True)
    lg2 = jnp.where(lane == i1, neg_inf, lg)
    m2 = jnp.max(lg2, axis=-1, keepdims=True)
    i2 = jnp.min(jnp.where(lg2 == m2, lane, float(LANES)), axis=-1, keepdims=True)
    e2 = jnp.exp(m2 - m1)
    g1 = 1.0 / (1.0 + e2)
    g2 = e2 / (1.0 + e2)
    route_ref[...] = jnp.where(lane == 0, i1, jnp.where(lane == 1, i2, jnp.where(lane == 2, g1,
                               jnp.where(lane == 3, g2, 0.0))))


def _odd_out(x, a, wo, nrm, router, *, tm):
    t = x.shape[0]
    row_spec = pl.BlockSpec((tm, D_MODEL), lambda i: (i, 0))
    return pl.pallas_call(
        _odd_out_kernel,
        out_shape=[jax.ShapeDtypeStruct((t, D_MODEL), F32), jax.ShapeDtypeStruct((t * ROW_TILE, LANES), F32),
                   jax.ShapeDtypeStruct((t, LANES), F32)],
        grid=(t // tm,),
        in_specs=[row_spec, row_spec, _const_spec(wo.shape), _const_spec(nrm.shape), _const_spec(router.shape)],
        out_specs=[row_spec, pl.BlockSpec((tm * ROW_TILE, LANES), lambda i: (i, 0)),
                   pl.BlockSpec((tm, LANES), lambda i: (i, 0))],
        compiler_params=pltpu.CompilerParams(dimension_semantics=("arbitrary",), vmem_limit_bytes=VMEM_LIMIT),
        name="odd_out",
    )(x, a, wo, nrm, router)


def _row_gather_copy(src_hbm, src_row, dst_ref, dst_row, sem):
    return pltpu.make_async_copy(
        src_hbm.at[pl.ds(pl.multiple_of(src_row * ROW_TILE, ROW_TILE), ROW_TILE), :],
        dst_ref.at[pl.ds(pl.multiple_of(dst_row * ROW_TILE, ROW_TILE), ROW_TILE), :],
        sem)


GATHER_UNROLL = 8


def _start_row_gathers(src_hbm, idx_ref, idx_row, first, count, dst_ref, sem, *, inline):
    if inline:
        for u in range(count):
            _row_gather_copy(src_hbm, idx_ref[idx_row, first + u], dst_ref, first + u, sem).start(priority=u % 2)
        return

    def issue(r2, c):
        for u in range(2):
            r = 2 * r2 + u
            _row_gather_copy(src_hbm, idx_ref[idx_row, r], dst_ref, r, sem).start(priority=u)
        return c

    assert first % 2 == 0 and count % 2 == 0
    lax.fori_loop(first // 2, (first + count) // 2, issue, 0, unroll=GATHER_UNROLL // 2)


def _wait_row_gathers(src_hbm, dst_ref, sem):
    pltpu.make_async_copy(src_hbm.at[pl.ds(0, dst_ref.shape[0]), :], dst_ref, sem).wait()


def _moe_ffn_kernel(te_ref, nu_ref, tok_ref, tok_next_ref, h_hbm, w1_ref, w3_ref, w2_ref, o_ref, xs_ref, buf_ref,
                    sem, *, tm, n_chunks):
    i = pl.program_id(0)
    n_used = nu_ref[0]
    used = i < n_used
    slot = i % 2

    @pl.when(i == 0)
    def _():
        _start_row_gathers(h_hbm, tok_ref, 0, 0, tm, buf_ref.at[0], sem.at[0], inline=False)

    @pl.when(i <= n_used)
    def _():
        _wait_row_gathers(h_hbm, buf_ref.at[slot], sem.at[slot])
        xs_ref[...] = _load_token_tiles(buf_ref.at[slot], tm).astype(BF16)

    @pl.when(used)
    def _():
        n_groups = 3 * n_chunks
        per_group = tm // n_groups
        starts = [(g * per_group, per_group if g + 1 < n_groups else tm - g * per_group) for g in range(n_groups)]

        def prefetch(g):
            first, count = starts[g]
            _start_row_gathers(h_hbm, tok_next_ref, 0, first, count, buf_ref.at[1 - slot], sem.at[1 - slot],
                               inline=True)

        x = xs_ref[...]
        tf = D_FF_EXPERT // n_chunks
        y = None
        for c in range(n_chunks):
            cols = slice(c * tf, (c + 1) * tf)
            prefetch(3 * c)
            a = _dot(x, w1_ref[:, cols])
            prefetch(3 * c + 1)
            b = _dot(x, w3_ref[:, cols])
            prefetch(3 * c + 2)
            part = _dot((a * _sigmoid(a) * b).astype(BF16), w2_ref[cols, :])
            y = part if y is None else y + part
        _store_token_tiles(o_ref, y, tm)

    @pl.when(jnp.logical_not(used))
    def _():
        o_ref[...] = jnp.zeros_like(o_ref)


def _moe_ffn(tile_expert, n_used, row_tok, h_tiles, w1, w3, w2, *, tm, n_chunks):
    n_tiles = row_tok.shape[0] - 1

    def expert_spec(shape):
        return pl.BlockSpec((None,) + shape, lambda i, te, nu: (te[i], 0, 0), pipeline_mode=pl.Buffered(1))

    grid_spec = pltpu.PrefetchScalarGridSpec(
        num_scalar_prefetch=2,
        grid=(n_tiles,),
        in_specs=[
            pl.BlockSpec((None, 1, tm), lambda i, te, nu: (i, 0, 0), memory_space=pltpu.SMEM),
            pl.BlockSpec((None, 1, tm), lambda i, te, nu: (i + 1, 0, 0), memory_space=pltpu.SMEM),
            pl.BlockSpec(memory_space=pl.ANY),
            expert_spec((D_MODEL, D_FF_EXPERT)), expert_spec((D_MODEL, D_FF_EXPERT)),
            expert_spec((D_FF_EXPERT, D_MODEL)),
        ],
        out_specs=pl.BlockSpec((tm * ROW_TILE, LANES), lambda i, te, nu: (i, 0)),
        scratch_shapes=[pltpu.VMEM((tm, D_MODEL), BF16), pltpu.VMEM((2, tm * ROW_TILE, LANES), F32),
                        pltpu.SemaphoreType.DMA((2,))],
    )
    return pl.pallas_call(
        functools.partial(_moe_ffn_kernel, tm=tm, n_chunks=n_chunks),
        out_shape=jax.ShapeDtypeStruct((n_tiles * tm * ROW_TILE, LANES), F32),
        grid_spec=grid_spec,
        compiler_params=pltpu.CompilerParams(dimension_semantics=("arbitrary",), vmem_limit_bytes=VMEM_LIMIT),
        name="moe_ffn",
    )(tile_expert, n_used, row_tok, row_tok, h_tiles, w1, w3, w2)


def _moe_combine_kernel(pos_ref, pos_next_ref, x_ref, route_ref, nrm_ref, y_hbm, o_ref, buf_ref, sem, *, tm):
    i = pl.program_id(0)
    slot = i % 2

    def start_tile(idx_ref, s, inline):
        for k in range(2):
            _start_row_gathers(y_hbm, idx_ref, k, 0, tm, buf_ref.at[s, k], sem.at[s, k], inline=inline)

    @pl.when(i == 0)
    def _():
        start_tile(pos_ref, 0, False)

    @pl.when(i + 1 < pl.num_programs(0))
    def _():
        start_tile(pos_next_ref, 1 - slot, True)

    for k in range(2):
        _wait_row_gathers(y_hbm, buf_ref.at[slot, k], sem.at[slot, k])
    g1 = route_ref[:, 2:3]
    g2 = route_ref[:, 3:4]
    x = x_ref[...] + (g1 * _load_token_tiles(buf_ref.at[slot, 0], tm) + g2 * _load_token_tiles(buf_ref.at[slot, 1], tm))
    o_ref[...] = _rms(x, nrm_ref[...])


def _moe_combine(pos, x, route, nrm, y_tiles, *, tm):
    t = x.shape[0]
    n = t // tm
    row_spec = pl.BlockSpec((tm, D_MODEL), lambda i: (i, 0))
    return pl.pallas_call(
        functools.partial(_moe_combine_kernel, tm=tm),
        out_shape=jax.ShapeDtypeStruct((t, D_MODEL), F32),
        grid=(n,),
        in_specs=[pl.BlockSpec((None, 2, tm), lambda i: (i, 0, 0), memory_space=pltpu.SMEM),
                  pl.BlockSpec((None, 2, tm), lambda i: (jnp.minimum(i + 1, n - 1), 0, 0), memory_space=pltpu.SMEM),
                  row_spec, pl.BlockSpec((tm, LANES), lambda i: (i, 0)), _const_spec(nrm.shape),
                  pl.BlockSpec(memory_space=pl.ANY)],
        out_specs=row_spec,
        scratch_shapes=[pltpu.VMEM((2, 2, tm * ROW_TILE, LANES), F32), pltpu.SemaphoreType.DMA((2, 2))],
        compiler_params=pltpu.CompilerParams(dimension_semantics=("arbitrary",), vmem_limit_bytes=VMEM_LIMIT),
        name="moe_combine",
    )(pos, pos, x, route, nrm, y_tiles)


SCATTER_CHUNK = 2048


def _invert_rows_kernel(pos_ref, out_ref, *, n_tokens, chunk):
    j = pl.program_id(0)

    @pl.when(j == 0)
    def _():
        def zero(r, c):
            out_ref[r] = 0
            return c

        lax.fori_loop(0, out_ref.shape[0], zero, 0, unroll=32)

    base = lax.rem(j * chunk, n_tokens)

    def place(u, c):
        out_ref[pos_ref[0, u]] = base + u
        return c

    lax.fori_loop(0, chunk, place, 0, unroll=32)


def _invert_rows(pos, n_rows, n_tokens):
    chunk = min(SCATTER_CHUNK, n_tokens)
    n_chunks = pos.shape[0] // chunk
    assert n_chunks * chunk == pos.shape[0] and n_tokens % chunk == 0
    return pl.pallas_call(
        functools.partial(_invert_rows_kernel, n_tokens=n_tokens, chunk=chunk),
        out_shape=jax.ShapeDtypeStruct((n_rows,), jnp.int32),
        grid=(n_chunks,),
        in_specs=[pl.BlockSpec((None, 1, chunk), lambda j: (j, 0, 0), memory_space=pltpu.SMEM)],
        out_specs=pl.BlockSpec(memory_space=pltpu.SMEM),
        compiler_params=pltpu.CompilerParams(dimension_semantics=("arbitrary",)),
        name="invert_rows",
    )(pos.reshape(n_chunks, 1, chunk))


def _routing_tables(idx1, idx2, *, tm, n_tiles):
    t = idx1.shape[0]
    e_flat = jnp.concatenate([idx1, idx2])
    onehot = (e_flat[:, None] == jnp.arange(N_EXPERTS, dtype=jnp.int32)[None, :]).astype(jnp.int32)
    csum = jnp.cumsum(onehot, axis=0)
    rank = jnp.sum((csum - onehot) * onehot, axis=1)
    counts = csum[-1]
    tiles_e = (counts + tm - 1) // tm
    tile_end = jnp.cumsum(tiles_e)
    tile_start = tile_end - tiles_e
    pos = jnp.sum(onehot * tile_start[None, :], axis=1) * tm + rank
    n_used = tile_end[-1]
    tile_ids = jnp.arange(n_tiles, dtype=jnp.int32)
    te = jnp.sum((tile_ids[:, None] >= tile_end[None, :]).astype(jnp.int32), axis=1)
    te_last = jnp.sum((n_used - 1 >= tile_end).astype(jnp.int32))
    tile_expert = jnp.where(tile_ids < n_used, te, te_last).astype(jnp.int32)
    row_tok = _invert_rows(pos.astype(jnp.int32), n_tiles * tm, t)
    return pos[:t], pos[t:], row_tok, tile_expert, n_used.reshape(1).astype(jnp.int32)


def _pad_cols(w, n):
    return jnp.pad(w, ((0, 0), (0, n - w.shape[1])))


def kernel(x, even_norm_mix, even_w_in, even_gate_up, even_gate_bias, even_w_s, even_b_s, even_ln_g, even_ln_b,
           even_head_g, even_w_o, even_norm_ffn, even_ffn_w1, even_ffn_w3, even_ffn_w2, odd_norm_mix, odd_w_in,
           odd_forget_bias, odd_q_g, odd_k_g, odd_w_o, odd_norm_ffn, odd_router, odd_exp_w1, odd_exp_w3,
           odd_exp_w2, final_norm):
    batch, seq, d = x.shape
    t = batch * seq
    xt = x.reshape(t, d)
    tm = min(512, seq)

    w_in = even_w_in[0]
    u_w, v_w, q_w, k_w, g_w, vb_w, og_w = jnp.split(w_in, [512, 1024, 1280, 1536, 1552, 2064], axis=1)
    win_e = jnp.concatenate([u_w, v_w, q_w, k_w, vb_w, og_w, _pad_cols(g_w, LANES)], axis=1).astype(BF16)
    gup = jnp.pad(even_gate_up[0], ((0, LANES - B_GATE_RANK), (0, 0))).astype(BF16)
    gb = even_gate_bias[0].reshape(1, B_QK_WIDTH)
    tril = jnp.tril(jnp.ones((CHUNK, CHUNK), dtype=bool))
    ws = jnp.where(tril[None], even_w_s[0], 0.0).astype(BF16)
    bs = jnp.broadcast_to(even_b_s[0][:, :, None], (A_GROUPS, CHUNK, LANES))
    lng = even_ln_g[0].reshape(1, A_WIDTH)
    lnb = even_ln_b[0].reshape(1, A_WIDTH)
    hg = even_head_g[0].reshape(B_HEADS, 1, B_VAL_DIM)
    ew1 = odd_exp_w1[0].reshape(N_EXPERTS * D_MODEL, D_FF_EXPERT)
    ew3 = odd_exp_w3[0].reshape(N_EXPERTS * D_MODEL, D_FF_EXPERT)
    ew2 = odd_exp_w2[0].reshape(N_EXPERTS * D_FF_EXPERT, D_MODEL)
    x1, ew1_b, fw1_b, fw3_b, fw2_b = _even_mixer(
        xt, even_norm_mix[0].reshape(1, d), win_e, gup, gb, ws, bs, lng, lnb, hg, even_w_o[0].astype(BF16),
        [ew1, even_ffn_w1[0], even_ffn_w3[0], even_ffn_w2[0]], seq=seq, tm=tm)
    x2, win_o = _dense_ffn(x1, even_norm_ffn[0].reshape(1, d), fw1_b, fw3_b, fw2_b, odd_w_in[0], O_F, tm=tm)

    wf_o = _pad_cols(odd_w_in[0][:, O_F:], LANES).astype(BF16)
    fb = jnp.pad(odd_forget_bias[0], (0, LANES - C_HEADS)).reshape(1, LANES)
    qg = jnp.tile(odd_q_g[0], C_HEADS).reshape(1, C_WIDTH)
    kg = jnp.tile(odd_k_g[0], C_HEADS).reshape(1, C_WIDTH)
    tk = min(256, seq)
    tq = min(1024, seq)
    q, k_ext, vt, gate = _odd_inproj(x2, odd_norm_mix[0].reshape(1, d), win_o, wf_o, fb, qg, kg, seq=seq, tm=tm,
                                     tk=tk)
    attn, ew2_b, ew3_b = _fox_attention(q, k_ext, vt, gate, [ew2, ew3], batch=batch, seq=seq, tq=tq, tk=tk)

    router = _pad_cols(odd_router[0], LANES)
    x3, h_tiles, route = _odd_out(x2, attn, odd_w_o[0].astype(BF16), odd_norm_ffn[0].reshape(1, d), router, tm=tm)

    tm_moe = 512
    tm_comb = min(256, seq)
    n_tiles = (2 * t) // tm_moe + N_EXPERTS + 1
    idx1 = route[:, 0].astype(jnp.int32)
    idx2 = route[:, 1].astype(jnp.int32)
    pos1, pos2, row_tok, tile_expert, n_used = _routing_tables(idx1, idx2, tm=tm_moe, n_tiles=n_tiles + 1)
    y_tiles = _moe_ffn(tile_expert[:n_tiles], n_used, row_tok.reshape(n_tiles + 1, 1, tm_moe), h_tiles,
                       ew1_b.reshape(N_EXPERTS, D_MODEL, D_FF_EXPERT), ew3_b.reshape(N_EXPERTS, D_MODEL, D_FF_EXPERT),
                       ew2_b.reshape(N_EXPERTS, D_FF_EXPERT, D_MODEL), tm=tm_moe, n_chunks=2)
    pos = jnp.stack([pos1.reshape(t // tm_comb, tm_comb), pos2.reshape(t // tm_comb, tm_comb)], axis=1)
    out = _moe_combine(pos, x3, route, final_norm.reshape(1, d), y_tiles, tm=tm_comb)
    return out.reshape(batch, seq, d)
```

```python
import functools
import math

import jax
import jax.numpy as jnp
from jax import lax
from jax.experimental import pallas as pl
from jax.experimental.pallas import tpu as pltpu

F32 = jnp.float32
BF16 = jnp.bfloat16
HIGHEST = lax.Precision.HIGHEST

EPS = 1e-6
D_MODEL = 1024
CHUNK = 128
SUB = 32
N_SUB = CHUNK // SUB
A_GROUPS = 4
A_WIDTH = 512
B_HEADS = 4
B_KEY_DIM = 64
B_VAL_DIM = 128
B_QK_WIDTH = 256
B_V_WIDTH = 512
B_GATE_RANK = 16
B_GATE_NORMALIZER = 16.0
C_HEADS = 16
C_HEAD_DIM = 64
C_WIDTH = 1024
D_FF_DENSE = 2816
N_EXPERTS = 8
D_FF_EXPERT = 3584
LANES = 128
MAX_DECAY_EXP = 60.0
LOG2E = math.log2(math.e)
ROW_TILE = 8

E_U, E_V, E_Q, E_K, E_VB, E_OG, E_G, E_END = 0, 512, 1024, 1280, 1536, 2048, 2560, 2688
O_Q, O_K, O_V, O_OG, O_F, O_END = 0, 1024, 2048, 3072, 4096, 4224

VMEM_LIMIT = 56 * 1024 * 1024


def _rms(x, g):
    ms = jnp.mean(x * x, axis=-1, keepdims=True)
    return x * lax.rsqrt(ms + EPS) * g


def _gelu_tanh(x):
    c = math.sqrt(2.0 / math.pi)
    return x * (0.5 * (1.0 + jnp.tanh(c * (x + 0.044715 * (x * x * x)))))


def _sigmoid(x):
    return 1.0 / (1.0 + jnp.exp(-x))


def _log_sigmoid(x):
    return jnp.minimum(x, 0.0) - jnp.log(1.0 + jnp.exp(-jnp.abs(x)))


def _dot(a, b):
    return jnp.dot(a, b, preferred_element_type=F32)


def _dot_nt(a, b):
    return lax.dot_general(a, b, (((1,), (1,)), ((), ())), preferred_element_type=F32)


def _split3(x):
    hi = x.astype(BF16)
    r1 = x - hi.astype(F32)
    mid = r1.astype(BF16)
    lo = (r1 - mid.astype(F32)).astype(BF16)
    return hi, mid, lo


def _cumsum_rows(tril_b, x):
    hi, mid, lo = _split3(x)
    return _dot(tril_b, hi) + _dot(tril_b, mid) + _dot(tril_b, lo)


def _const_spec(shape):
    nd = len(shape)
    return pl.BlockSpec(shape, lambda *_: (0,) * nd)


def _side_cast_specs(w2d, n_steps, index_map):
    rows = w2d.shape[0] // n_steps
    assert rows * n_steps == w2d.shape[0] and rows % 16 == 0
    spec = pl.BlockSpec((rows, w2d.shape[1]), index_map)
    return spec, spec, jax.ShapeDtypeStruct(w2d.shape, BF16)


def _side_cast_specs_1d(w2d, n_steps):
    span = 1 if (w2d.shape[0] // n_steps) % 16 == 0 and w2d.shape[0] % n_steps == 0 else 2
    return _side_cast_specs(w2d, n_steps // span, lambda i: (i // span, 0))


def _even_mixer_kernel(x_ref, nrm_ref, win_ref, gup_ref, gb_ref, ws_ref, bs_ref, lng_ref, lnb_ref,
                       hg_ref, wo_ref, *rest, tiles_per_batch, n_chunks, n_casts):
    cast_refs, o_ref, cast_out_refs = rest[:n_casts], rest[n_casts], rest[n_casts + 1:2 * n_casts + 1]
    z_ref, mix_ref, st_ref = rest[2 * n_casts + 1:]
    i = pl.program_id(0)
    for src, dst in zip(cast_refs, cast_out_refs):
        dst[...] = src[...].astype(BF16)

    @pl.when(i % tiles_per_batch == 0)
    def _():
        st_ref[...] = jnp.zeros_like(st_ref)

    h = _rms(x_ref[...], nrm_ref[...]).astype(BF16)
    z_ref[...] = _dot(h, win_ref[...])

    row = lax.broadcasted_iota(jnp.int32, (CHUNK, CHUNK), 0)
    col = lax.broadcasted_iota(jnp.int32, (CHUNK, CHUNK), 1)
    tril_b = (col <= row).astype(BF16)
    sub_row = row & (SUB - 1)
    head_lane = lax.broadcasted_iota(jnp.int32, (1, B_QK_WIDTH), 1) // B_KEY_DIM
    bd_mask = (lax.broadcasted_iota(jnp.int32, (B_V_WIDTH, B_QK_WIDTH), 0) // B_VAL_DIM
               == lax.broadcasted_iota(jnp.int32, (B_V_WIDTH, B_QK_WIDTH), 1) // B_KEY_DIM)

    def chunk_body(c, carry):
        rows = pl.ds(pl.multiple_of(c * CHUNK, CHUNK), CHUNK)

        u = _gelu_tanh(z_ref[rows, E_U:E_V])
        v = _gelu_tanh(z_ref[rows, E_V:E_Q])
        mu = jnp.mean(v, axis=-1, keepdims=True)
        vc = v - mu
        var = jnp.mean(vc * vc, axis=-1, keepdims=True)
        vln = (vc * lax.rsqrt(var + EPS) * lng_ref[...] + lnb_ref[...]).astype(BF16)
        for g in range(A_GROUPS):
            sl = slice(g * LANES, (g + 1) * LANES)
            mixed = _dot(ws_ref[g], vln[:, sl]) + bs_ref[g]
            mix_ref[rows, sl] = (u[:, sl] * mixed).astype(BF16)

        q = z_ref[rows, E_Q:E_K] * (B_KEY_DIM ** -0.5)
        k = z_ref[rows, E_K:E_VB]
        vb = z_ref[rows, E_VB:E_OG]
        og = z_ref[rows, E_OG:E_G]
        glr = z_ref[rows, E_G:E_END].astype(BF16)
        logit = _dot(glr, gup_ref[...]) + gb_ref[...]
        log_a = _log_sigmoid(logit) * (1.0 / B_GATE_NORMALIZER)
        g_cum = _cumsum_rows(tril_b, log_a)
        g_last = g_cum[CHUNK - 1:CHUNK, :]
        st = st_ref[...]
        o = _dot_nt((q * jnp.exp(g_cum)).astype(BF16), st.astype(BF16))

        p_rows = [[None] * N_SUB for _ in range(B_HEADS)]
        for s in range(N_SUB):
            gs = g_cum[s * SUB:(s + 1) * SUB, :]
            if s == 0:
                qt = q[0:SUB, :] * jnp.exp(gs)
                kt = k * jnp.exp(jnp.minimum(-g_cum, MAX_DECAY_EXP))
            else:
                ref_g = g_cum[s * SUB - 1:s * SUB, :]
                qt = q[s * SUB:(s + 1) * SUB, :] * jnp.exp(gs - ref_g)
                kt = k * jnp.exp(jnp.minimum(ref_g - g_cum, MAX_DECAY_EXP))
            qs = jnp.concatenate([jnp.where(head_lane == hh, qt, 0.0) for hh in range(B_HEADS)],
                                 axis=0).astype(BF16)
            sc = _dot_nt(qs, kt.astype(BF16))
            sc = jnp.where(col <= (s * SUB + sub_row), sc, 0.0)
            for hh in range(B_HEADS):
                p_rows[hh][s] = sc[hh * SUB:(hh + 1) * SUB, :]

        vb_b = vb.astype(BF16)
        for hh in range(B_HEADS):
            sl = slice(hh * B_VAL_DIM, (hh + 1) * B_VAL_DIM)
            ph = jnp.concatenate(p_rows[hh], axis=0).astype(BF16)
            oh = o[:, sl] + _dot(ph, vb_b[:, sl])
            on = _rms(oh, hg_ref[hh])
            ogh = og[:, sl]
            mix_ref[rows, A_WIDTH + hh * B_VAL_DIM:A_WIDTH + (hh + 1) * B_VAL_DIM] = (
                on * (ogh * _sigmoid(ogh))).astype(BF16)

        k_dec = (k * jnp.exp(g_last - g_cum)).astype(BF16)
        upd = _dot(vb.T.astype(BF16), k_dec)
        st_ref[...] = jnp.exp(g_last) * st + jnp.where(bd_mask, upd, 0.0)
        return carry

    lax.fori_loop(0, n_chunks, chunk_body, 0, unroll=True)
    o_ref[...] = x_ref[...] + _dot(mix_ref[...], wo_ref[...])


def _even_mixer(x, nrm, win, gup, gb, ws, bs, lng, lnb, hg, wo, wcasts, *, seq, tm):
    t = x.shape[0]
    kern = functools.partial(_even_mixer_kernel, tiles_per_batch=seq // tm, n_chunks=tm // CHUNK,
                             n_casts=len(wcasts))
    casts = [_side_cast_specs_1d(w, t // tm) for w in wcasts]
    return pl.pallas_call(
        kern,
        out_shape=[jax.ShapeDtypeStruct((t, D_MODEL), F32)] + [c[2] for c in casts],
        grid=(t // tm,),
        in_specs=[
            pl.BlockSpec((tm, D_MODEL), lambda i: (i, 0)),
            _const_spec(nrm.shape), _resident_spec(win.shape), _const_spec(gup.shape), _const_spec(gb.shape),
            _const_spec(ws.shape), _const_spec(bs.shape), _const_spec(lng.shape), _const_spec(lnb.shape),
            _const_spec(hg.shape), _resident_spec(wo.shape),
        ] + [c[0] for c in casts],
        out_specs=[pl.BlockSpec((tm, D_MODEL), lambda i: (i, 0))] + [c[1] for c in casts],
        scratch_shapes=[
            pltpu.VMEM((tm, E_END), F32),
            pltpu.VMEM((tm, D_MODEL), BF16),
            pltpu.VMEM((B_V_WIDTH, B_QK_WIDTH), F32),
        ],
        compiler_params=pltpu.CompilerParams(dimension_semantics=("arbitrary",), vmem_limit_bytes=VMEM_LIMIT),
        name="even_mixer",
    )(x, nrm, win, gup, gb, ws, bs, lng, lnb, hg, wo, *wcasts)


def _dense_ffn_kernel(x_ref, nrm_ref, w1_ref, w3_ref, w2_ref, wcast_ref, o_ref, wcast_out_ref):
    wcast_out_ref[...] = wcast_ref[:, :wcast_out_ref.shape[1]].astype(BF16)
    x = x_ref[...]
    h = _rms(x, nrm_ref[...]).astype(BF16)
    a = _dot(h, w1_ref[...])
    b = _dot(h, w3_ref[...])
    o_ref[...] = x + _dot((a * _sigmoid(a) * b).astype(BF16), w2_ref[...])


def _resident_spec(shape):
    nd = len(shape)
    return pl.BlockSpec(shape, lambda *_: (0,) * nd, pipeline_mode=pl.Buffered(1))


def _dense_ffn(x, nrm, w1, w3, w2, wcast, wcast_cols, *, tm):
    t = x.shape[0]
    n_steps = t // tm
    row_spec = pl.BlockSpec((tm, D_MODEL), lambda i: (i, 0))
    rows = wcast.shape[0] // n_steps
    assert rows * n_steps == wcast.shape[0] and rows % 16 == 0
    return pl.pallas_call(
        _dense_ffn_kernel,
        out_shape=[jax.ShapeDtypeStruct((t, D_MODEL), F32), jax.ShapeDtypeStruct((wcast.shape[0], wcast_cols), BF16)],
        grid=(n_steps,),
        in_specs=[row_spec, _const_spec(nrm.shape), _resident_spec(w1.shape), _resident_spec(w3.shape),
                  _resident_spec(w2.shape), pl.BlockSpec((rows, wcast.shape[1]), lambda i: (i, 0))],
        out_specs=[row_spec, pl.BlockSpec((rows, wcast_cols), lambda i: (i, 0))],
        compiler_params=pltpu.CompilerParams(dimension_semantics=("arbitrary",), vmem_limit_bytes=VMEM_LIMIT),
        name="dense_ffn",
    )(x, nrm, w1, w3, w2, wcast)


def _head_rms(x, gain):
    lo = lax.broadcasted_iota(jnp.int32, (1, LANES), 1) < C_HEAD_DIM
    outs = []
    for t in range(C_WIDTH // LANES):
        xt = x[:, t * LANES:(t + 1) * LANES]
        sq = xt * xt
        s_lo = jnp.sum(jnp.where(lo, sq, 0.0), axis=-1, keepdims=True)
        s_hi = jnp.sum(jnp.where(lo, 0.0, sq), axis=-1, keepdims=True)
        inv = jnp.where(lo, lax.rsqrt(s_lo * (1.0 / C_HEAD_DIM) + EPS), lax.rsqrt(s_hi * (1.0 / C_HEAD_DIM) + EPS))
        outs.append(xt * inv)
    return jnp.concatenate(outs, axis=-1) * gain


N_PAIRS = C_HEADS // 2
K_EXT = 2 * LANES
BIAS_PARTS = 3


def _bias_placement():
    src = jnp.arange(BIAS_PARTS * LANES)
    part, head = src // LANES, src % LANES
    dst = (head // 2) * LANES + BIAS_PARTS * (head % 2) + part
    hit = (dst[:, None] == jnp.arange(N_PAIRS * LANES)[None, :]) & (head < C_HEADS)[:, None]
    return hit.astype(BF16)


def _odd_inproj_kernel(x_ref, nrm_ref, w_ref, wf_ref, fb_ref, qg_ref, kg_ref, place_ref,
                       q_ref, k_ref, vt_ref, gate_ref, z_ref, c_ref, carry_ref, *, tiles_per_batch, n_chunks, tk):
    i = pl.program_id(0)

    @pl.when(i % tiles_per_batch == 0)
    def _():
        carry_ref[...] = jnp.zeros_like(carry_ref)

    h = _rms(x_ref[...], nrm_ref[...]).astype(BF16)
    z_ref[:, :O_F] = _dot(h, w_ref[...])
    z_ref[:, O_F:] = _dot(h, wf_ref[...])
    qn = _head_rms(z_ref[:, O_Q:O_K], qg_ref[...]) * (C_HEAD_DIM ** -0.5 * LOG2E)
    for p in range(N_PAIRS):
        q_ref[p] = qn[:, p * LANES:(p + 1) * LANES].T.astype(BF16)
    kn = _head_rms(z_ref[:, O_K:O_V], kg_ref[...]).astype(BF16)
    for p in range(N_PAIRS):
        for kb in range(vt_ref.shape[1]):
            blk = z_ref[kb * tk:(kb + 1) * tk, O_V + p * LANES:O_V + (p + 1) * LANES]
            vt_ref[p, kb] = blk.T.astype(BF16)
    gate_ref[...] = _sigmoid(z_ref[:, O_OG:O_F]).astype(BF16)

    row = lax.broadcasted_iota(jnp.int32, (CHUNK, CHUNK), 0)
    col = lax.broadcasted_iota(jnp.int32, (CHUNK, CHUNK), 1)
    tril_b = (col <= row).astype(BF16)
    carry = carry_ref[...]
    for c in range(n_chunks):
        rows = slice(c * CHUNK, (c + 1) * CHUNK)
        log_f = _log_sigmoid(z_ref[rows, O_F:O_END] + fb_ref[...])
        cs = _cumsum_rows(tril_b, log_f) + carry
        c_ref[rows, :] = cs
        carry = cs[CHUNK - 1:CHUNK, :]
    carry_ref[...] = carry

    bias = _dot(jnp.concatenate(_split3(c_ref[...] * LOG2E), axis=1), place_ref[...]).astype(BF16)
    for p in range(N_PAIRS):
        k_ref[:, p * K_EXT:p * K_EXT + LANES] = kn[:, p * LANES:(p + 1) * LANES]
        k_ref[:, p * K_EXT + LANES:(p + 1) * K_EXT] = bias[:, p * LANES:(p + 1) * LANES]


def _odd_inproj(x, nrm, w, wf, fb, qg, kg, *, seq, tm, tk):
    t = x.shape[0]
    tiles_per_batch = seq // tm
    kern = functools.partial(_odd_inproj_kernel, tiles_per_batch=tiles_per_batch, n_chunks=tm // CHUNK, tk=tk)
    row_spec = pl.BlockSpec((tm, C_WIDTH), lambda i: (i, 0))
    kext_spec = pl.BlockSpec((tm, N_PAIRS * K_EXT), lambda i: (i, 0))
    vt_spec = pl.BlockSpec((None, N_PAIRS, tm // tk, LANES, tk),
                           lambda i: (i // tiles_per_batch, 0, i % tiles_per_batch, 0, 0))
    place = _bias_placement()
    wide = jax.ShapeDtypeStruct((t, C_WIDTH), BF16)
    return pl.pallas_call(
        kern,
        out_shape=[jax.ShapeDtypeStruct((t // seq, N_PAIRS, LANES, seq), BF16),
                   jax.ShapeDtypeStruct((t, N_PAIRS * K_EXT), BF16),
                   jax.ShapeDtypeStruct((t // seq, N_PAIRS, seq // tk, LANES, tk), BF16), wide],
        grid=(t // tm,),
        in_specs=[row_spec, _const_spec(nrm.shape), _resident_spec(w.shape), _const_spec(wf.shape),
                  _const_spec(fb.shape), _const_spec(qg.shape), _const_spec(kg.shape), _const_spec(place.shape)],
        out_specs=[pl.BlockSpec((None, N_PAIRS, LANES, tm), lambda i: (i // tiles_per_batch, 0, 0, i % tiles_per_batch)),
                   kext_spec, vt_spec, row_spec],
        scratch_shapes=[pltpu.VMEM((tm, O_END), F32), pltpu.VMEM((tm, LANES), F32), pltpu.VMEM((1, LANES), F32)],
        compiler_params=pltpu.CompilerParams(dimension_semantics=("arbitrary",), vmem_limit_bytes=VMEM_LIMIT),
        name="odd_inproj",
    )(x, nrm, w, wf, fb, qg, kg, place)


NEG_BIG = -1e30


V_ROWS = 80


def _fox_kernel(q_ref, k_ref, vt_ref, gate_ref, *rest, tq, tk, n_casts):
    cast_refs, o_ref, cast_out_refs, s_ref = rest[:n_casts], rest[n_casts], rest[n_casts + 1:-1], rest[-1]
    for src, dst in zip(cast_refs, cast_out_refs):
        dst[...] = src[...].astype(BF16)
    qi = pl.program_id(2)
    ng = tq // tk
    key_i = lax.broadcasted_iota(jnp.int32, (tk, tk), 0)
    qry_i = lax.broadcasted_iota(jnp.int32, (tk, tk), 1)
    causal = key_i <= qry_i
    chains = [(hh, r) for r in range(ng) for hh in range(2)]
    feat = lax.broadcasted_iota(jnp.int32, (LANES, 1), 0)
    qms = []
    for hh, r in chains:
        q_t = q_ref[:, r * tk:(r + 1) * tk]
        own = (feat < C_HEAD_DIM) if hh == 0 else (feat >= C_HEAD_DIM)
        qh = jnp.where(own, q_t, jnp.zeros_like(q_t))
        pick = (feat >= BIAS_PARTS * hh) & (feat < BIAS_PARTS * (hh + 1))
        minus_one = jnp.broadcast_to(jnp.where(pick, -1.0, 0.0).astype(BF16), qh.shape)
        qms.append(jnp.concatenate([qh, minus_one], axis=0))

    def key_rows(j):
        return pl.ds(pl.multiple_of(j * tk, tk), tk)

    def scores_to_scratch(j, slot, live, modes):
        kb = k_ref[key_rows(j), :]
        raw = [_dot(kb, qms[idx]) for idx in live]
        maxes = []
        for s, idx in zip(raw, live):
            hh, r = chains[idx]
            if modes[r] == "diag":
                s = jnp.where(causal, s, NEG_BIG)
            s_ref[slot, idx] = s
            maxes.append(jnp.max(s, axis=0, keepdims=True))
        return maxes

    ones_rows = (lax.broadcasted_iota(jnp.int32, (V_ROWS - C_HEAD_DIM, tk), 0) == 0).astype(BF16)

    def softmax_pv(j, slot, maxes, live, state):
        vt_pair = vt_ref[j]
        vtb = [jnp.concatenate([vt_pair[hh * C_HEAD_DIM:(hh + 1) * C_HEAD_DIM, :], ones_rows], axis=0)
               for hh in range(2)]
        new = list(state)
        probs = []
        for bm, idx in zip(maxes, live):
            m = state[2 * idx]
            m_new = jnp.maximum(m, bm)
            p = jnp.exp2(s_ref[slot, idx] - m_new)
            new[2 * idx] = m_new
            probs.append((jnp.exp2(m - m_new), p.astype(BF16)))
        for (alpha, p), idx in zip(probs, live):
            hh, r = chains[idx]
            new[2 * idx + 1] = alpha * state[2 * idx + 1] + _dot(vtb[hh], p)
        return new

    assert ng % 2 == 0
    all_chains = list(range(len(chains)))
    n_state = 2 * len(chains)
    state = []
    for _ in chains:
        state += [jnp.full((1, tk), NEG_BIG, F32), jnp.zeros((V_ROWS, tk), F32)]
    n_full = qi * ng
    full_modes = ("full",) * ng

    def diag_modes(g):
        return tuple("skip" if r < g else ("diag" if r == g else "full") for r in range(ng))

    def live_chains(g):
        return [idx for idx, (hh, r) in enumerate(chains) if r >= g]

    def trip(j, carry, next_modes):
        st, mx = list(carry[:n_state]), carry[n_state:]
        for b in range(ng):
            mx_next = scores_to_scratch(j + b + 1, (b + 1) % 2, all_chains, full_modes if b + 1 < ng else next_modes)
            st = softmax_pv(j + b, b % 2, mx, all_chains, st)
            mx = mx_next
        return tuple(st) + tuple(mx)

    def with_full_blocks(_):
        first = scores_to_scratch(0, 0, all_chains, full_modes)
        carry = lax.fori_loop(0, qi - 1, lambda i, c: trip(ng * i, c, full_modes), tuple(state) + tuple(first))
        return trip(n_full - ng, carry, diag_modes(0))

    def no_full_blocks(_):
        return tuple(state) + tuple(scores_to_scratch(0, 0, all_chains, diag_modes(0)))

    carry = lax.cond(qi > 0, with_full_blocks, no_full_blocks, 0)
    state, mx = list(carry[:n_state]), carry[n_state:]
    for g in range(ng):
        if g + 1 < ng:
            mx_next = scores_to_scratch(n_full + g + 1, (g + 1) % 2, live_chains(g + 1), diag_modes(g + 1))
        state = softmax_pv(n_full + g, g % 2, mx, live_chains(g), state)
        if g + 1 < ng:
            mx = mx_next

    for r in range(ng):
        parts = []
        for hh in range(2):
            acc = state[2 * chains.index((hh, r)) + 1]
            parts.append(acc[:C_HEAD_DIM, :] / acc[C_HEAD_DIM:C_HEAD_DIM + 1, :])
        o = jnp.concatenate(parts, axis=0).T
        rows = slice(r * tk, (r + 1) * tk)
        o_ref[rows, :] = (o * gate_ref[rows, :].astype(F32)).astype(BF16)


def _fox_attention(q, k_ext, vt, gate, wcasts, *, batch, seq, tq, tk):
    t = batch * seq
    nq = seq // tq
    kern = functools.partial(_fox_kernel, tq=tq, tk=tk, n_casts=len(wcasts))
    casts = [_side_cast_specs(w, batch * N_PAIRS * nq, lambda b, p, i: ((b * N_PAIRS + p) * nq + i, 0))
             for w in wcasts]
    return pl.pallas_call(
        kern,
        out_shape=[jax.ShapeDtypeStruct((t, C_WIDTH), BF16)] + [c[2] for c in casts],
        grid=(batch, N_PAIRS, nq),
        in_specs=[
            pl.BlockSpec((None, None, LANES, tq), lambda b, p, i: (b, p, 0, i)),
            pl.BlockSpec((seq, K_EXT), lambda b, p, i: (b, p)),
            pl.BlockSpec((None, None, seq // tk, LANES, tk), lambda b, p, i: (b, p, 0, 0, 0)),
            pl.BlockSpec((tq, LANES), lambda b, p, i: (b * nq + i, p)),
        ] + [c[0] for c in casts],
        out_specs=[pl.BlockSpec((tq, LANES), lambda b, p, i: (b * nq + i, p))] + [c[1] for c in casts],
        scratch_shapes=[pltpu.VMEM((2, 2 * (tq // tk), tk, tk), F32)],
        compiler_params=pltpu.CompilerParams(dimension_semantics=("arbitrary", "arbitrary", "arbitrary"),
                                             vmem_limit_bytes=VMEM_LIMIT),
        name="fox_attn",
    )(q, k_ext, vt, gate, *wcasts)


def _store_token_tiles(dst_ref, val, n_rows):
    for s in range(ROW_TILE):
        dst_ref[pl.ds(s, n_rows, stride=ROW_TILE), :] = val[:, s * LANES:(s + 1) * LANES]


def _load_token_tiles(src_ref, n_rows):
    return jnp.concatenate([src_ref[pl.ds(s, n_rows, stride=ROW_TILE), :] for s in range(ROW_TILE)], axis=1)


def _odd_out_kernel(x_ref, a_ref, wo_ref, nrm_ref, r_ref, x3_ref, h_ref, route_ref):
    x3 = x_ref[...] + _dot(a_ref[...], wo_ref[...])
    x3_ref[...] = x3
    h = _rms(x3, nrm_ref[...])
    _store_token_tiles(h_ref, h, h.shape[0])

    h_hi = h.astype(BF16)
    h_lo = (h - h_hi.astype(F32)).astype(BF16)
    r = r_ref[...]
    r_hi = r.astype(BF16)
    r_lo = (r - r_hi.astype(F32)).astype(BF16)
    n = h.shape[0]
    prod = _dot(jnp.concatenate([h_hi, h_lo], axis=0), jnp.concatenate([r_hi, r_lo], axis=1))
    logits = prod[:n, :LANES] + (prod[n:, :LANES] + prod[:n, LANES:])

    lane = lax.broadcasted_iota(jnp.int32, logits.shape, 1).astype(F32)
    neg_inf = jnp.float32(-jnp.inf)
    lg = jnp.where(lane < N_EXPERTS, logits, neg_inf)
    m1 = jnp.max(lg, axis=-1, keepdims=True)
    i1 = jnp.min(jnp.where(lg == m1, lane, float(LANES)), axis=-1, keepdims=True)
    lg2 = jnp.where(lane == i1, neg_inf, lg)
    m2 = jnp.max(lg2, axis=-1, keepdims=True)
    i2 = jnp.min(jnp.where(lg2 == m2, lane, float(LANES)), axis=-1, keepdims=True)
    e2 = jnp.exp(m2 - m1)
    g1 = 1.0 / (1.0 + e2)
    g2 = e2 / (1.0 + e2)
    route_ref[...] = jnp.where(lane == 0, i1, jnp.where(lane == 1, i2, jnp.where(lane == 2, g1,
                               jnp.where(lane == 3, g2, 0.0))))


def _odd_out(x, a, wo, nrm, router, *, tm):
    t = x.shape[0]
    row_spec = pl.BlockSpec((tm, D_MODEL), lambda i: (i, 0))
    return pl.pallas_call(
        _odd_out_kernel,
        out_shape=[jax.ShapeDtypeStruct((t, D_MODEL), F32), jax.ShapeDtypeStruct((t * ROW_TILE, LANES), F32),
                   jax.ShapeDtypeStruct((t, LANES), F32)],
        grid=(t // tm,),
        in_specs=[row_spec, row_spec, _const_spec(wo.shape), _const_spec(nrm.shape), _const_spec(router.shape)],
        out_specs=[row_spec, pl.BlockSpec((tm * ROW_TILE, LANES), lambda i: (i, 0)),
                   pl.BlockSpec((tm, LANES), lambda i: (i, 0))],
        compiler_params=pltpu.CompilerParams(dimension_semantics=("arbitrary",), vmem_limit_bytes=VMEM_LIMIT),
        name="odd_out",
    )(x, a, wo, nrm, router)


def _row_gather_copy(src_hbm, src_row, dst_ref, dst_row, sem):
    return pltpu.make_async_copy(
        src_hbm.at[pl.ds(pl.multiple_of(src_row * ROW_TILE, ROW_TILE), ROW_TILE), :],
        dst_ref.at[pl.ds(pl.multiple_of(dst_row * ROW_TILE, ROW_TILE), ROW_TILE), :],
        sem)


GATHER_UNROLL = 8


def _start_row_gathers(src_hbm, idx_ref, idx_row, first, count, dst_ref, sem, *, inline):
    if inline:
        for u in range(count):
            _row_gather_copy(src_hbm, idx_ref[idx_row, first + u], dst_ref, first + u, sem).start(priority=u % 2)
        return

    def issue(r2, c):
        for u in range(2):
            r = 2 * r2 + u
            _row_gather_copy(src_hbm, idx_ref[idx_row, r], dst_ref, r, sem).start(priority=u)
        return c

    assert first % 2 == 0 and count % 2 == 0
    lax.fori_loop(first // 2, (first + count) // 2, issue, 0, unroll=GATHER_UNROLL // 2)


def _wait_row_gathers(src_hbm, dst_ref, sem):
    pltpu.make_async_copy(src_hbm.at[pl.ds(0, dst_ref.shape[0]), :], dst_ref, sem).wait()


def _moe_ffn_kernel(te_ref, nu_ref, tok_ref, tok_next_ref, h_hbm, w1_ref, w3_ref, w2_ref, o_ref, xs_ref, buf_ref,
                    sem, *, tm, n_chunks):
    i = pl.program_id(0)
    n_used = nu_ref[0]
    used = i < n_used
    slot = i % 2

    @pl.when(i == 0)
    def _():
        _start_row_gathers(h_hbm, tok_ref, 0, 0, tm, buf_ref.at[0], sem.at[0], inline=False)

    @pl.when(i <= n_used)
    def _():
        _wait_row_gathers(h_hbm, buf_ref.at[slot], sem.at[slot])
        xs_ref[...] = _load_token_tiles(buf_ref.at[slot], tm).astype(BF16)

    @pl.when(used)
    def _():
        n_groups = 3 * n_chunks
        per_group = tm // n_groups
        starts = [(g * per_group, per_group if g + 1 < n_groups else tm - g * per_group) for g in range(n_groups)]

        def prefetch(g):
            first, count = starts[g]
            _start_row_gathers(h_hbm, tok_next_ref, 0, first, count, buf_ref.at[1 - slot], sem.at[1 - slot],
                               inline=True)

        x = xs_ref[...]
        tf = D_FF_EXPERT // n_chunks
        y = None
        for c in range(n_chunks):
            cols = slice(c * tf, (c + 1) * tf)
            prefetch(3 * c)
            a = _dot(x, w1_ref[:, cols])
            prefetch(3 * c + 1)
            b = _dot(x, w3_ref[:, cols])
            prefetch(3 * c + 2)
            part = _dot((a * _sigmoid(a) * b).astype(BF16), w2_ref[cols, :])
            y = part if y is None else y + part
        _store_token_tiles(o_ref, y, tm)

    @pl.when(jnp.logical_not(used))
    def _():
        o_ref[...] = jnp.zeros_like(o_ref)


def _moe_ffn(tile_expert, n_used, row_tok, h_tiles, w1, w3, w2, *, tm, n_chunks):
    n_tiles = row_tok.shape[0] - 1

    def expert_spec(shape):
        return pl.BlockSpec((None,) + shape, lambda i, te, nu: (te[i], 0, 0), pipeline_mode=pl.Buffered(1))

    grid_spec = pltpu.PrefetchScalarGridSpec(
        num_scalar_prefetch=2,
        grid=(n_tiles,),
        in_specs=[
            pl.BlockSpec((None, 1, tm), lambda i, te, nu: (i, 0, 0), memory_space=pltpu.SMEM),
            pl.BlockSpec((None, 1, tm), lambda i, te, nu: (i + 1, 0, 0), memory_space=pltpu.SMEM),
            pl.BlockSpec(memory_space=pl.ANY),
            expert_spec((D_MODEL, D_FF_EXPERT)), expert_spec((D_MODEL, D_FF_EXPERT)),
            expert_spec((D_FF_EXPERT, D_MODEL)),
        ],
        out_specs=pl.BlockSpec((tm * ROW_TILE, LANES), lambda i, te, nu: (i, 0)),
        scratch_shapes=[pltpu.VMEM((tm, D_MODEL), BF16), pltpu.VMEM((2, tm * ROW_TILE, LANES), F32),
                        pltpu.SemaphoreType.DMA((2,))],
    )
    return pl.pallas_call(
        functools.partial(_moe_ffn_kernel, tm=tm, n_chunks=n_chunks),
        out_shape=jax.ShapeDtypeStruct((n_tiles * tm * ROW_TILE, LANES), F32),
        grid_spec=grid_spec,
        compiler_params=pltpu.CompilerParams(dimension_semantics=("arbitrary",), vmem_limit_bytes=VMEM_LIMIT),
        name="moe_ffn",
    )(tile_expert, n_used, row_tok, row_tok, h_tiles, w1, w3, w2)


def _moe_combine_kernel(pos_ref, pos_next_ref, x_ref, route_ref, nrm_ref, y_hbm, o_ref, buf_ref, sem, *, tm):
    i = pl.program_id(0)
    slot = i % 2

    def start_tile(idx_ref, s, inline):
        for k in range(2):
            _start_row_gathers(y_hbm, idx_ref, k, 0, tm, buf_ref.at[s, k], sem.at[s, k], inline=inline)

    @pl.when(i == 0)
    def _():
        start_tile(pos_ref, 0, False)

    @pl.when(i + 1 < pl.num_programs(0))
    def _():
        start_tile(pos_next_ref, 1 - slot, True)

    for k in range(2):
        _wait_row_gathers(y_hbm, buf_ref.at[slot, k], sem.at[slot, k])
    g1 = route_ref[:, 2:3]
    g2 = route_ref[:, 3:4]
    x = x_ref[...] + (g1 * _load_token_tiles(buf_ref.at[slot, 0], tm) + g2 * _load_token_tiles(buf_ref.at[slot, 1], tm))
    o_ref[...] = _rms(x, nrm_ref[...])


def _moe_combine(pos, x, route, nrm, y_tiles, *, tm):
    t = x.shape[0]
    n = t // tm
    row_spec = pl.BlockSpec((tm, D_MODEL), lambda i: (i, 0))
    return pl.pallas_call(
        functools.partial(_moe_combine_kernel, tm=tm),
        out_shape=jax.ShapeDtypeStruct((t, D_MODEL), F32),
        grid=(n,),
        in_specs=[pl.BlockSpec((None, 2, tm), lambda i: (i, 0, 0), memory_space=pltpu.SMEM),
                  pl.BlockSpec((None, 2, tm), lambda i: (jnp.minimum(i + 1, n - 1), 0, 0), memory_space=pltpu.SMEM),
                  row_spec, pl.BlockSpec((tm, LANES), lambda i: (i, 0)), _const_spec(nrm.shape),
                  pl.BlockSpec(memory_space=pl.ANY)],
        out_specs=row_spec,
        scratch_shapes=[pltpu.VMEM((2, 2, tm * ROW_TILE, LANES), F32), pltpu.SemaphoreType.DMA((2, 2))],
        compiler_params=pltpu.CompilerParams(dimension_semantics=("arbitrary",), vmem_limit_bytes=VMEM_LIMIT),
        name="moe_combine",
    )(pos, pos, x, route, nrm, y_tiles)


SCATTER_CHUNK = 2048


def _invert_rows_kernel(pos_ref, zeros_hbm, out_ref, sem, *, n_tokens, chunk):
    j = pl.program_id(0)

    @pl.when(j == 0)
    def _():
        fill = pltpu.make_async_copy(zeros_hbm, out_ref, sem.at[0])
        fill.start()
        fill.wait()

    base = lax.rem(j * chunk, n_tokens)

    def place(u, c):
        out_ref[pos_ref[0, u]] = base + u
        return c

    lax.fori_loop(0, chunk, place, 0, unroll=32)


def _invert_rows(pos, n_rows, n_tokens):
    chunk = min(SCATTER_CHUNK, n_tokens)
    n_chunks = pos.shape[0] // chunk
    assert n_chunks * chunk == pos.shape[0] and n_tokens % chunk == 0
    return pl.pallas_call(
        functools.partial(_invert_rows_kernel, n_tokens=n_tokens, chunk=chunk),
        out_shape=jax.ShapeDtypeStruct((n_rows,), jnp.int32),
        grid=(n_chunks,),
        in_specs=[pl.BlockSpec((None, 1, chunk), lambda j: (j, 0, 0), memory_space=pltpu.SMEM),
                  pl.BlockSpec(memory_space=pl.ANY)],
        out_specs=pl.BlockSpec(memory_space=pltpu.SMEM),
        scratch_shapes=[pltpu.SemaphoreType.DMA((1,))],
        compiler_params=pltpu.CompilerParams(dimension_semantics=("arbitrary",)),
        name="invert_rows",
    )(pos.reshape(n_chunks, 1, chunk), jnp.zeros((n_rows,), jnp.int32))


def _routing_tables(idx1, idx2, *, tm, n_tiles):
    t = idx1.shape[0]
    e_flat = jnp.concatenate([idx1, idx2])
    onehot = (e_flat[:, None] == jnp.arange(N_EXPERTS, dtype=jnp.int32)[None, :]).astype(jnp.int32)
    csum = jnp.cumsum(onehot, axis=0)
    rank = jnp.sum((csum - onehot) * onehot, axis=1)
    counts = csum[-1]
    tiles_e = (counts + tm - 1) // tm
    tile_end = jnp.cumsum(tiles_e)
    tile_start = tile_end - tiles_e
    pos = jnp.sum(onehot * tile_start[None, :], axis=1) * tm + rank
    n_used = tile_end[-1]
    tile_ids = jnp.arange(n_tiles, dtype=jnp.int32)
    te = jnp.sum((tile_ids[:, None] >= tile_end[None, :]).astype(jnp.int32), axis=1)
    te_last = jnp.sum((n_used - 1 >= tile_end).astype(jnp.int32))
    tile_expert = jnp.where(tile_ids < n_used, te, te_last).astype(jnp.int32)
    row_tok = _invert_rows(pos.astype(jnp.int32), n_tiles * tm, t)
    return pos[:t], pos[t:], row_tok, tile_expert, n_used.reshape(1).astype(jnp.int32)


def _pad_cols(w, n):
    return jnp.pad(w, ((0, 0), (0, n - w.shape[1])))


def kernel(x, even_norm_mix, even_w_in, even_gate_up, even_gate_bias, even_w_s, even_b_s, even_ln_g, even_ln_b,
           even_head_g, even_w_o, even_norm_ffn, even_ffn_w1, even_ffn_w3, even_ffn_w2, odd_norm_mix, odd_w_in,
           odd_forget_bias, odd_q_g, odd_k_g, odd_w_o, odd_norm_ffn, odd_router, odd_exp_w1, odd_exp_w3,
           odd_exp_w2, final_norm):
    batch, seq, d = x.shape
    t = batch * seq
    xt = x.reshape(t, d)
    tm = min(512, seq)

    w_in = even_w_in[0]
    u_w, v_w, q_w, k_w, g_w, vb_w, og_w = jnp.split(w_in, [512, 1024, 1280, 1536, 1552, 2064], axis=1)
    win_e = jnp.concatenate([u_w, v_w, q_w, k_w, vb_w, og_w, _pad_cols(g_w, LANES)], axis=1).astype(BF16)
    gup = jnp.pad(even_gate_up[0], ((0, LANES - B_GATE_RANK), (0, 0))).astype(BF16)
    gb = even_gate_bias[0].reshape(1, B_QK_WIDTH)
    tril = jnp.tril(jnp.ones((CHUNK, CHUNK), dtype=bool))
    ws = jnp.where(tril[None], even_w_s[0], 0.0).astype(BF16)
    bs = jnp.broadcast_to(even_b_s[0][:, :, None], (A_GROUPS, CHUNK, LANES))
    lng = even_ln_g[0].reshape(1, A_WIDTH)
    lnb = even_ln_b[0].reshape(1, A_WIDTH)
    hg = even_head_g[0].reshape(B_HEADS, 1, B_VAL_DIM)
    ew1 = odd_exp_w1[0].reshape(N_EXPERTS * D_MODEL, D_FF_EXPERT)
    ew3 = odd_exp_w3[0].reshape(N_EXPERTS * D_MODEL, D_FF_EXPERT)
    ew2 = odd_exp_w2[0].reshape(N_EXPERTS * D_FF_EXPERT, D_MODEL)
    x1, ew1_b, fw1_b, fw3_b, fw2_b = _even_mixer(
        xt, even_norm_mix[0].reshape(1, d), win_e, gup, gb, ws, bs, lng, lnb, hg, even_w_o[0].astype(BF16),
        [ew1, even_ffn_w1[0], even_ffn_w3[0], even_ffn_w2[0]], seq=seq, tm=tm)
    x2, win_o = _dense_ffn(x1, even_norm_ffn[0].reshape(1, d), fw1_b, fw3_b, fw2_b, odd_w_in[0], O_F, tm=tm)

    wf_o = _pad_cols(odd_w_in[0][:, O_F:], LANES).astype(BF16)
    fb = jnp.pad(odd_forget_bias[0], (0, LANES - C_HEADS)).reshape(1, LANES)
    qg = jnp.tile(odd_q_g[0], C_HEADS).reshape(1, C_WIDTH)
    kg = jnp.tile(odd_k_g[0], C_HEADS).reshape(1, C_WIDTH)
    tk = min(256, seq)
    tq = min(1024, seq)
    q, k_ext, vt, gate = _odd_inproj(x2, odd_norm_mix[0].reshape(1, d), win_o, wf_o, fb, qg, kg, seq=seq, tm=tm,
                                     tk=tk)
    attn, ew2_b, ew3_b = _fox_attention(q, k_ext, vt, gate, [ew2, ew3], batch=batch, seq=seq, tq=tq, tk=tk)

    router = _pad_cols(odd_router[0], LANES)
    x3, h_tiles, route = _odd_out(x2, attn, odd_w_o[0].astype(BF16), odd_norm_ffn[0].reshape(1, d), router, tm=tm)

    tm_moe = 512
    tm_comb = min(256, seq)
    n_tiles = (2 * t) // tm_moe + N_EXPERTS + 1
    idx1 = route[:, 0].astype(jnp.int32)
    idx2 = route[:, 1].astype(jnp.int32)
    pos1, pos2, row_tok, tile_expert, n_used = _routing_tables(idx1, idx2, tm=tm_moe, n_tiles=n_tiles + 1)
    y_tiles = _moe_ffn(tile_expert[:n_tiles], n_used, row_tok.reshape(n_tiles + 1, 1, tm_moe), h_tiles,
                       ew1_b.reshape(N_EXPERTS, D_MODEL, D_FF_EXPERT), ew3_b.reshape(N_EXPERTS, D_MODEL, D_FF_EXPERT),
                       ew2_b.reshape(N_EXPERTS, D_FF_EXPERT, D_MODEL), tm=tm_moe, n_chunks=2)
    pos = jnp.stack([pos1.reshape(t // tm_comb, tm_comb), pos2.reshape(t // tm_comb, tm_comb)], axis=1)
    out = _moe_combine(pos, x3, route, final_norm.reshape(1, d), y_tiles, tm=tm_comb)
    return out.reshape(batch, seq, d)
```

```python
import collections
import functools
import math

import jax
import jax.numpy as jnp
from jax import lax
from jax.experimental import pallas as pl
from jax.experimental.pallas import tpu as pltpu

F32 = jnp.float32
BF16 = jnp.bfloat16

EPS = 1e-6
D_MODEL = 1024
CHUNK = 128
SUB = 32
N_SUB = CHUNK // SUB
A_GROUPS = 4
A_WIDTH = 512
B_HEADS = 4
B_KEY_DIM = 64
B_VAL_DIM = 128
B_QK_WIDTH = 256
B_V_WIDTH = 512
B_GATE_RANK = 16
B_GATE_NORMALIZER = 16.0
C_HEADS = 16
C_HEAD_DIM = 64
C_WIDTH = 1024
D_FF_DENSE = 2816
N_EXPERTS = 8
D_FF_EXPERT = 3584
LANES = 128
MAX_DECAY_EXP = 60.0
LOG2E = math.log2(math.e)
ROW_TILE = 8

E_U, E_V, E_Q, E_K, E_VB, E_OG, E_G, E_END = 0, 512, 1024, 1280, 1536, 2048, 2560, 2688
O_Q, O_K, O_V, O_OG, O_F, O_END = 0, 1024, 2048, 3072, 4096, 4224

V7X_VMEM_BYTES = 64 * 1024 * 1024
VMEM_LIMIT = V7X_VMEM_BYTES - 8 * 1024 * 1024

Tiles = collections.namedtuple("Tiles", ["row", "attn_q", "attn_k", "moe", "combine"])


def _tiles(seq):
    return Tiles(row=min(512, seq), attn_q=min(1024, seq), attn_k=min(256, seq), moe=512, combine=min(512, seq))


def _rms(x, g):
    ms = jnp.mean(x * x, axis=-1, keepdims=True)
    return x * lax.rsqrt(ms + EPS) * g


def _gelu_tanh(x):
    c = math.sqrt(2.0 / math.pi)
    return x * (0.5 * (1.0 + jnp.tanh(c * (x + 0.044715 * (x * x * x)))))


def _sigmoid(x):
    return 1.0 / (1.0 + jnp.exp(-x))


def _log_sigmoid(x):
    return jnp.minimum(x, 0.0) - jnp.log(1.0 + jnp.exp(-jnp.abs(x)))


def _dot(a, b):
    return jnp.dot(a, b, preferred_element_type=F32)


def _dot_nt(a, b):
    return lax.dot_general(a, b, (((1,), (1,)), ((), ())), preferred_element_type=F32)


def _split3(x):
    hi = x.astype(BF16)
    r1 = x - hi.astype(F32)
    mid = r1.astype(BF16)
    lo = (r1 - mid.astype(F32)).astype(BF16)
    return hi, mid, lo


def _cumsum_rows(tril_b, x):
    hi, mid, lo = _split3(x)
    return _dot(tril_b, hi) + _dot(tril_b, mid) + _dot(tril_b, lo)


def _const_spec(shape):
    nd = len(shape)
    return pl.BlockSpec(shape, lambda *_: (0,) * nd)


def _side_cast_specs(w2d, n_steps, index_map):
    rows = w2d.shape[0] // n_steps
    assert rows * n_steps == w2d.shape[0] and rows % 16 == 0
    spec = pl.BlockSpec((rows, w2d.shape[1]), index_map)
    return spec, spec, jax.ShapeDtypeStruct(w2d.shape, BF16)


def _side_cast_specs_1d(w2d, n_steps):
    span = 1 if (w2d.shape[0] // n_steps) % 16 == 0 and w2d.shape[0] % n_steps == 0 else 2
    return _side_cast_specs(w2d, n_steps // span, lambda i: (i // span, 0))


def _even_mixer_kernel(x_ref, nrm_ref, win_ref, gup_ref, gb_ref, ws_ref, bs_ref, lng_ref, lnb_ref,
                       hg_ref, wo_ref, *rest, tiles_per_batch, n_chunks, n_casts):
    cast_refs, o_ref, cast_out_refs = rest[:n_casts], rest[n_casts], rest[n_casts + 1:2 * n_casts + 1]
    z_ref, mix_ref, st_ref = rest[2 * n_casts + 1:]
    i = pl.program_id(0)
    for src, dst in zip(cast_refs, cast_out_refs):
        dst[...] = src[...].astype(BF16)

    @pl.when(i % tiles_per_batch == 0)
    def _():
        st_ref[...] = jnp.zeros_like(st_ref)

    h = _rms(x_ref[...], nrm_ref[...]).astype(BF16)
    z_ref[...] = _dot(h, win_ref[...])

    row = lax.broadcasted_iota(jnp.int32, (CHUNK, CHUNK), 0)
    col = lax.broadcasted_iota(jnp.int32, (CHUNK, CHUNK), 1)
    tril_b = (col <= row).astype(BF16)
    sub_row = row & (SUB - 1)
    head_lane = lax.broadcasted_iota(jnp.int32, (1, B_QK_WIDTH), 1) // B_KEY_DIM
    bd_mask = (lax.broadcasted_iota(jnp.int32, (B_V_WIDTH, B_QK_WIDTH), 0) // B_VAL_DIM
               == lax.broadcasted_iota(jnp.int32, (B_V_WIDTH, B_QK_WIDTH), 1) // B_KEY_DIM)

    def chunk_body(c, carry):
        rows = pl.ds(pl.multiple_of(c * CHUNK, CHUNK), CHUNK)

        u = _gelu_tanh(z_ref[rows, E_U:E_V])
        v = _gelu_tanh(z_ref[rows, E_V:E_Q])
        mu = jnp.mean(v, axis=-1, keepdims=True)
        vc = v - mu
        var = jnp.mean(vc * vc, axis=-1, keepdims=True)
        vln = (vc * lax.rsqrt(var + EPS) * lng_ref[...] + lnb_ref[...]).astype(BF16)
        for g in range(A_GROUPS):
            sl = slice(g * LANES, (g + 1) * LANES)
            mixed = _dot(ws_ref[g], vln[:, sl]) + bs_ref[g]
            mix_ref[rows, sl] = (u[:, sl] * mixed).astype(BF16)

        q = z_ref[rows, E_Q:E_K] * (B_KEY_DIM ** -0.5)
        k = z_ref[rows, E_K:E_VB]
        vb = z_ref[rows, E_VB:E_OG]
        og = z_ref[rows, E_OG:E_G]
        glr = z_ref[rows, E_G:E_END].astype(BF16)
        logit = _dot(glr, gup_ref[...]) + gb_ref[...]
        log_a = _log_sigmoid(logit) * (1.0 / B_GATE_NORMALIZER)
        g_cum = _cumsum_rows(tril_b, log_a)
        g_last = g_cum[CHUNK - 1:CHUNK, :]
        st = st_ref[...]
        o = _dot_nt((q * jnp.exp(g_cum)).astype(BF16), st.astype(BF16))

        p_rows = [[None] * N_SUB for _ in range(B_HEADS)]
        for s in range(N_SUB):
            gs = g_cum[s * SUB:(s + 1) * SUB, :]
            if s == 0:
                qt = q[0:SUB, :] * jnp.exp(gs)
                kt = k * jnp.exp(jnp.minimum(-g_cum, MAX_DECAY_EXP))
            else:
                ref_g = g_cum[s * SUB - 1:s * SUB, :]
                qt = q[s * SUB:(s + 1) * SUB, :] * jnp.exp(gs - ref_g)
                kt = k * jnp.exp(jnp.minimum(ref_g - g_cum, MAX_DECAY_EXP))
            qs = jnp.concatenate([jnp.where(head_lane == hh, qt, 0.0) for hh in range(B_HEADS)],
                                 axis=0).astype(BF16)
            sc = _dot_nt(qs, kt.astype(BF16))
            sc = jnp.where(col <= (s * SUB + sub_row), sc, 0.0)
            for hh in range(B_HEADS):
                p_rows[hh][s] = sc[hh * SUB:(hh + 1) * SUB, :]

        vb_b = vb.astype(BF16)
        for hh in range(B_HEADS):
            sl = slice(hh * B_VAL_DIM, (hh + 1) * B_VAL_DIM)
            ph = jnp.concatenate(p_rows[hh], axis=0).astype(BF16)
            oh = o[:, sl] + _dot(ph, vb_b[:, sl])
            on = _rms(oh, hg_ref[hh])
            ogh = og[:, sl]
            mix_ref[rows, A_WIDTH + hh * B_VAL_DIM:A_WIDTH + (hh + 1) * B_VAL_DIM] = (
                on * (ogh * _sigmoid(ogh))).astype(BF16)

        k_dec = (k * jnp.exp(g_last - g_cum)).astype(BF16)
        upd = _dot(vb.T.astype(BF16), k_dec)
        st_ref[...] = jnp.exp(g_last) * st + jnp.where(bd_mask, upd, 0.0)
        return carry

    lax.fori_loop(0, n_chunks, chunk_body, 0, unroll=True)
    o_ref[...] = x_ref[...] + _dot(mix_ref[...], wo_ref[...])


def _even_mixer(x, nrm, win, gup, gb, ws, bs, lng, lnb, hg, wo, wcasts, *, seq, tm):
    t = x.shape[0]
    kern = functools.partial(_even_mixer_kernel, tiles_per_batch=seq // tm, n_chunks=tm // CHUNK,
                             n_casts=len(wcasts))
    casts = [_side_cast_specs_1d(w, t // tm) for w in wcasts]
    return pl.pallas_call(
        kern,
        out_shape=[jax.ShapeDtypeStruct((t, D_MODEL), F32)] + [c[2] for c in casts],
        grid=(t // tm,),
        in_specs=[
            pl.BlockSpec((tm, D_MODEL), lambda i: (i, 0)),
            _const_spec(nrm.shape), _resident_spec(win.shape), _const_spec(gup.shape), _const_spec(gb.shape),
            _const_spec(ws.shape), _const_spec(bs.shape), _const_spec(lng.shape), _const_spec(lnb.shape),
            _const_spec(hg.shape), _resident_spec(wo.shape),
        ] + [c[0] for c in casts],
        out_specs=[pl.BlockSpec((tm, D_MODEL), lambda i: (i, 0))] + [c[1] for c in casts],
        scratch_shapes=[
            pltpu.VMEM((tm, E_END), F32),
            pltpu.VMEM((tm, D_MODEL), BF16),
            pltpu.VMEM((B_V_WIDTH, B_QK_WIDTH), F32),
        ],
        compiler_params=pltpu.CompilerParams(dimension_semantics=("arbitrary",), vmem_limit_bytes=VMEM_LIMIT),
        name="even_mixer",
    )(x, nrm, win, gup, gb, ws, bs, lng, lnb, hg, wo, *wcasts)


def _dense_ffn_kernel(x_ref, nrm_ref, w1_ref, w3_ref, w2_ref, wcast_ref, o_ref, wcast_out_ref):
    wcast_out_ref[...] = wcast_ref[:, :wcast_out_ref.shape[1]].astype(BF16)
    x = x_ref[...]
    h = _rms(x, nrm_ref[...]).astype(BF16)
    a = _dot(h, w1_ref[...])
    b = _dot(h, w3_ref[...])
    o_ref[...] = x + _dot((a * _sigmoid(a) * b).astype(BF16), w2_ref[...])


def _resident_spec(shape):
    nd = len(shape)
    return pl.BlockSpec(shape, lambda *_: (0,) * nd, pipeline_mode=pl.Buffered(1))


def _dense_ffn(x, nrm, w1, w3, w2, wcast, wcast_cols, *, tm):
    t = x.shape[0]
    n_steps = t // tm
    row_spec = pl.BlockSpec((tm, D_MODEL), lambda i: (i, 0))
    rows = wcast.shape[0] // n_steps
    assert rows * n_steps == wcast.shape[0] and rows % 16 == 0
    return pl.pallas_call(
        _dense_ffn_kernel,
        out_shape=[jax.ShapeDtypeStruct((t, D_MODEL), F32), jax.ShapeDtypeStruct((wcast.shape[0], wcast_cols), BF16)],
        grid=(n_steps,),
        in_specs=[row_spec, _const_spec(nrm.shape), _resident_spec(w1.shape), _resident_spec(w3.shape),
                  _resident_spec(w2.shape), pl.BlockSpec((rows, wcast.shape[1]), lambda i: (i, 0))],
        out_specs=[row_spec, pl.BlockSpec((rows, wcast_cols), lambda i: (i, 0))],
        compiler_params=pltpu.CompilerParams(dimension_semantics=("arbitrary",), vmem_limit_bytes=VMEM_LIMIT),
        name="dense_ffn",
    )(x, nrm, w1, w3, w2, wcast)


def _head_rms(x, gain):
    lo = lax.broadcasted_iota(jnp.int32, (1, LANES), 1) < C_HEAD_DIM
    outs = []
    for t in range(C_WIDTH // LANES):
        xt = x[:, t * LANES:(t + 1) * LANES]
        sq = xt * xt
        s_lo = jnp.sum(jnp.where(lo, sq, 0.0), axis=-1, keepdims=True)
        s_hi = jnp.sum(jnp.where(lo, 0.0, sq), axis=-1, keepdims=True)
        inv = jnp.where(lo, lax.rsqrt(s_lo * (1.0 / C_HEAD_DIM) + EPS), lax.rsqrt(s_hi * (1.0 / C_HEAD_DIM) + EPS))
        outs.append(xt * inv)
    return jnp.concatenate(outs, axis=-1) * gain


N_PAIRS = C_HEADS // 2
K_EXT = 2 * LANES
BIAS_PARTS = 3


def _bias_placement():
    src = jnp.arange(BIAS_PARTS * LANES)
    part, head = src // LANES, src % LANES
    dst = (head // 2) * LANES + BIAS_PARTS * (head % 2) + part
    hit = (dst[:, None] == jnp.arange(N_PAIRS * LANES)[None, :]) & (head < C_HEADS)[:, None]
    return hit.astype(BF16)


def _odd_inproj_kernel(x_ref, nrm_ref, w_ref, wf_ref, fb_ref, qg_ref, kg_ref, place_ref,
                       q_ref, k_ref, vt_ref, gate_ref, z_ref, c_ref, carry_ref, *, tiles_per_batch, n_chunks, tk):
    i = pl.program_id(0)

    @pl.when(i % tiles_per_batch == 0)
    def _():
        carry_ref[...] = jnp.zeros_like(carry_ref)

    h = _rms(x_ref[...], nrm_ref[...]).astype(BF16)
    z_ref[:, :O_F] = _dot(h, w_ref[...])
    z_ref[:, O_F:] = _dot(h, wf_ref[...])
    qn = _head_rms(z_ref[:, O_Q:O_K], qg_ref[...]) * (C_HEAD_DIM ** -0.5 * LOG2E)
    for p in range(N_PAIRS):
        q_ref[p] = qn[:, p * LANES:(p + 1) * LANES].T.astype(BF16)
    kn = _head_rms(z_ref[:, O_K:O_V], kg_ref[...]).astype(BF16)
    for p in range(N_PAIRS):
        for kb in range(vt_ref.shape[1]):
            blk = z_ref[kb * tk:(kb + 1) * tk, O_V + p * LANES:O_V + (p + 1) * LANES]
            vt_ref[p, kb] = blk.T.astype(BF16)
    gate_ref[...] = _sigmoid(z_ref[:, O_OG:O_F]).astype(BF16)

    row = lax.broadcasted_iota(jnp.int32, (CHUNK, CHUNK), 0)
    col = lax.broadcasted_iota(jnp.int32, (CHUNK, CHUNK), 1)
    tril_b = (col <= row).astype(BF16)
    carry = carry_ref[...]
    for c in range(n_chunks):
        rows = slice(c * CHUNK, (c + 1) * CHUNK)
        log_f = _log_sigmoid(z_ref[rows, O_F:O_END] + fb_ref[...])
        cs = _cumsum_rows(tril_b, log_f) + carry
        c_ref[rows, :] = cs
        carry = cs[CHUNK - 1:CHUNK, :]
    carry_ref[...] = carry

    bias = _dot(jnp.concatenate(_split3(c_ref[...] * LOG2E), axis=1), place_ref[...]).astype(BF16)
    for p in range(N_PAIRS):
        k_ref[:, p * K_EXT:p * K_EXT + LANES] = kn[:, p * LANES:(p + 1) * LANES]
        k_ref[:, p * K_EXT + LANES:(p + 1) * K_EXT] = bias[:, p * LANES:(p + 1) * LANES]


def _odd_inproj(x, nrm, w, wf, fb, qg, kg, *, seq, tm, tk):
    t = x.shape[0]
    tiles_per_batch = seq // tm
    kern = functools.partial(_odd_inproj_kernel, tiles_per_batch=tiles_per_batch, n_chunks=tm // CHUNK, tk=tk)
    row_spec = pl.BlockSpec((tm, C_WIDTH), lambda i: (i, 0))
    kext_spec = pl.BlockSpec((tm, N_PAIRS * K_EXT), lambda i: (i, 0))
    vt_spec = pl.BlockSpec((None, N_PAIRS, tm // tk, LANES, tk),
                           lambda i: (i // tiles_per_batch, 0, i % tiles_per_batch, 0, 0))
    place = _bias_placement()
    wide = jax.ShapeDtypeStruct((t, C_WIDTH), BF16)
    return pl.pallas_call(
        kern,
        out_shape=[jax.ShapeDtypeStruct((t // seq, N_PAIRS, LANES, seq), BF16),
                   jax.ShapeDtypeStruct((t, N_PAIRS * K_EXT), BF16),
                   jax.ShapeDtypeStruct((t // seq, N_PAIRS, seq // tk, LANES, tk), BF16), wide],
        grid=(t // tm,),
        in_specs=[row_spec, _const_spec(nrm.shape), _resident_spec(w.shape), _const_spec(wf.shape),
                  _const_spec(fb.shape), _const_spec(qg.shape), _const_spec(kg.shape), _const_spec(place.shape)],
        out_specs=[pl.BlockSpec((None, N_PAIRS, LANES, tm), lambda i: (i // tiles_per_batch, 0, 0, i % tiles_per_batch)),
                   kext_spec, vt_spec, row_spec],
        scratch_shapes=[pltpu.VMEM((tm, O_END), F32), pltpu.VMEM((tm, LANES), F32), pltpu.VMEM((1, LANES), F32)],
        compiler_params=pltpu.CompilerParams(dimension_semantics=("arbitrary",), vmem_limit_bytes=VMEM_LIMIT),
        name="odd_inproj",
    )(x, nrm, w, wf, fb, qg, kg, place)


NEG_BIG = -1e30


V_ROWS = 80


def _fox_kernel(q_ref, k_ref, vt_ref, gate_ref, *rest, tq, tk, n_casts):
    cast_refs, o_ref, cast_out_refs, s_ref = rest[:n_casts], rest[n_casts], rest[n_casts + 1:-1], rest[-1]
    for src, dst in zip(cast_refs, cast_out_refs):
        dst[...] = src[...].astype(BF16)
    qi = pl.program_id(2)
    ng = tq // tk
    key_i = lax.broadcasted_iota(jnp.int32, (tk, tk), 0)
    qry_i = lax.broadcasted_iota(jnp.int32, (tk, tk), 1)
    causal = key_i <= qry_i
    chains = [(hh, r) for r in range(ng) for hh in range(2)]
    feat = lax.broadcasted_iota(jnp.int32, (LANES, 1), 0)
    qms = []
    for hh, r in chains:
        q_t = q_ref[:, r * tk:(r + 1) * tk]
        own = (feat < C_HEAD_DIM) if hh == 0 else (feat >= C_HEAD_DIM)
        qh = jnp.where(own, q_t, jnp.zeros_like(q_t))
        pick = (feat >= BIAS_PARTS * hh) & (feat < BIAS_PARTS * (hh + 1))
        minus_one = jnp.broadcast_to(jnp.where(pick, -1.0, 0.0).astype(BF16), qh.shape)
        qms.append(jnp.concatenate([qh, minus_one], axis=0))

    def key_rows(j):
        return pl.ds(pl.multiple_of(j * tk, tk), tk)

    def scores_to_scratch(j, slot, live, modes):
        kb = k_ref[key_rows(j), :]
        raw = [_dot(kb, qms[idx]) for idx in live]
        maxes = []
        for s, idx in zip(raw, live):
            hh, r = chains[idx]
            if modes[r] == "diag":
                s = jnp.where(causal, s, NEG_BIG)
            s_ref[slot, idx] = s
            maxes.append(jnp.max(s, axis=0, keepdims=True))
        return maxes

    ones_rows = (lax.broadcasted_iota(jnp.int32, (V_ROWS - C_HEAD_DIM, tk), 0) == 0).astype(BF16)

    def softmax_pv(j, slot, maxes, live, state):
        vt_pair = vt_ref[j]
        vtb = [jnp.concatenate([vt_pair[hh * C_HEAD_DIM:(hh + 1) * C_HEAD_DIM, :], ones_rows], axis=0)
               for hh in range(2)]
        new = list(state)
        probs = []
        for bm, idx in zip(maxes, live):
            m = state[2 * idx]
            m_new = jnp.maximum(m, bm)
            p = jnp.exp2(s_ref[slot, idx] - m_new)
            new[2 * idx] = m_new
            probs.append((jnp.exp2(m - m_new), p.astype(BF16)))
        for (alpha, p), idx in zip(probs, live):
            hh, r = chains[idx]
            new[2 * idx + 1] = alpha * state[2 * idx + 1] + _dot(vtb[hh], p)
        return new

    assert ng % 2 == 0
    all_chains = list(range(len(chains)))
    n_state = 2 * len(chains)
    state = []
    for _ in chains:
        state += [jnp.full((1, tk), NEG_BIG, F32), jnp.zeros((V_ROWS, tk), F32)]
    n_full = qi * ng
    full_modes = ("full",) * ng

    def diag_modes(g):
        return tuple("skip" if r < g else ("diag" if r == g else "full") for r in range(ng))

    def live_chains(g):
        return [idx for idx, (hh, r) in enumerate(chains) if r >= g]

    def trip(j, carry, next_modes):
        st, mx = list(carry[:n_state]), carry[n_state:]
        for b in range(ng):
            mx_next = scores_to_scratch(j + b + 1, (b + 1) % 2, all_chains, full_modes if b + 1 < ng else next_modes)
            st = softmax_pv(j + b, b % 2, mx, all_chains, st)
            mx = mx_next
        return tuple(st) + tuple(mx)

    def with_full_blocks(_):
        first = scores_to_scratch(0, 0, all_chains, full_modes)
        carry = lax.fori_loop(0, qi - 1, lambda i, c: trip(ng * i, c, full_modes), tuple(state) + tuple(first))
        return trip(n_full - ng, carry, diag_modes(0))

    def no_full_blocks(_):
        return tuple(state) + tuple(scores_to_scratch(0, 0, all_chains, diag_modes(0)))

    carry = lax.cond(qi > 0, with_full_blocks, no_full_blocks, 0)
    state, mx = list(carry[:n_state]), carry[n_state:]
    for g in range(ng):
        if g + 1 < ng:
            mx_next = scores_to_scratch(n_full + g + 1, (g + 1) % 2, live_chains(g + 1), diag_modes(g + 1))
        state = softmax_pv(n_full + g, g % 2, mx, live_chains(g), state)
        if g + 1 < ng:
            mx = mx_next

    for r in range(ng):
        parts = []
        for hh in range(2):
            acc = state[2 * chains.index((hh, r)) + 1]
            parts.append(acc[:C_HEAD_DIM, :] / acc[C_HEAD_DIM:C_HEAD_DIM + 1, :])
        o = jnp.concatenate(parts, axis=0).T
        rows = slice(r * tk, (r + 1) * tk)
        o_ref[rows, :] = (o * gate_ref[rows, :].astype(F32)).astype(BF16)


def _fox_attention(q, k_ext, vt, gate, wcasts, *, batch, seq, tq, tk):
    t = batch * seq
    nq = seq // tq
    kern = functools.partial(_fox_kernel, tq=tq, tk=tk, n_casts=len(wcasts))
    casts = [_side_cast_specs(w, batch * N_PAIRS * nq, lambda b, p, i: ((b * N_PAIRS + p) * nq + i, 0))
             for w in wcasts]
    return pl.pallas_call(
        kern,
        out_shape=[jax.ShapeDtypeStruct((t, C_WIDTH), BF16)] + [c[2] for c in casts],
        grid=(batch, N_PAIRS, nq),
        in_specs=[
            pl.BlockSpec((None, None, LANES, tq), lambda b, p, i: (b, p, 0, i)),
            pl.BlockSpec((seq, K_EXT), lambda b, p, i: (b, p)),
            pl.BlockSpec((None, None, seq // tk, LANES, tk), lambda b, p, i: (b, p, 0, 0, 0)),
            pl.BlockSpec((tq, LANES), lambda b, p, i: (b * nq + i, p)),
        ] + [c[0] for c in casts],
        out_specs=[pl.BlockSpec((tq, LANES), lambda b, p, i: (b * nq + i, p))] + [c[1] for c in casts],
        scratch_shapes=[pltpu.VMEM((2, 2 * (tq // tk), tk, tk), F32)],
        compiler_params=pltpu.CompilerParams(dimension_semantics=("arbitrary", "arbitrary", "arbitrary"),
                                             vmem_limit_bytes=VMEM_LIMIT),
        name="fox_attn",
    )(q, k_ext, vt, gate, *wcasts)


def _store_token_tiles(dst_ref, val, n_rows):
    for s in range(ROW_TILE):
        dst_ref[pl.ds(s, n_rows, stride=ROW_TILE), :] = val[:, s * LANES:(s + 1) * LANES]


def _load_token_tiles(src_ref, n_rows):
    return jnp.concatenate([src_ref[pl.ds(s, n_rows, stride=ROW_TILE), :] for s in range(ROW_TILE)], axis=1)


def _odd_out_kernel(x_ref, a_ref, wo_ref, nrm_ref, r_ref, x3_ref, h_ref, route_ref):
    x3 = x_ref[...] + _dot(a_ref[...], wo_ref[...])
    x3_ref[...] = x3
    h = _rms(x3, nrm_ref[...])
    _store_token_tiles(h_ref, h, h.shape[0])

    h_hi = h.astype(BF16)
    h_lo = (h - h_hi.astype(F32)).astype(BF16)
    r = r_ref[...]
    r_hi = r.astype(BF16)
    r_lo = (r - r_hi.astype(F32)).astype(BF16)
    n = h.shape[0]
    prod = _dot(jnp.concatenate([h_hi, h_lo], axis=0), jnp.concatenate([r_hi, r_lo], axis=1))
    logits = prod[:n, :LANES] + (prod[n:, :LANES] + prod[:n, LANES:])

    lane = lax.broadcasted_iota(jnp.int32, logits.shape, 1).astype(F32)
    neg_inf = jnp.float32(-jnp.inf)
    lg = jnp.where(lane < N_EXPERTS, logits, neg_inf)
    m1 = jnp.max(lg, axis=-1, keepdims=True)
    i1 = jnp.min(jnp.where(lg == m1, lane, float(LANES)), axis=-1, keepdims=True)
    lg2 = jnp.where(lane == i1, neg_inf, lg)
    m2 = jnp.max(lg2, axis=-1, keepdims=True)
    i2 = jnp.min(jnp.where(lg2 == m2, lane, float(LANES)), axis=-1, keepdims=True)
    e2 = jnp.exp(m2 - m1)
    g1 = 1.0 / (1.0 + e2)
    g2 = e2 / (1.0 + e2)
    route_ref[...] = jnp.where(lane == 0, i1, jnp.where(lane == 1, i2, jnp.where(lane == 2, g1,
                               jnp.where(lane == 3, g2, 0.0))))


def _odd_out(x, a, wo, nrm, router, *, tm):
    t = x.shape[0]
    row_spec = pl.BlockSpec((tm, D_MODEL), lambda i: (i, 0))
    return pl.pallas_call(
        _odd_out_kernel,
        out_shape=[jax.ShapeDtypeStruct((t, D_MODEL), F32), jax.ShapeDtypeStruct((t * ROW_TILE, LANES), F32),
                   jax.ShapeDtypeStruct((t, LANES), F32)],
        grid=(t // tm,),
        in_specs=[row_spec, row_spec, _const_spec(wo.shape), _const_spec(nrm.shape), _const_spec(router.shape)],
        out_specs=[row_spec, pl.BlockSpec((tm * ROW_TILE, LANES), lambda i: (i, 0)),
                   pl.BlockSpec((tm, LANES), lambda i: (i, 0))],
        compiler_params=pltpu.CompilerParams(dimension_semantics=("arbitrary",), vmem_limit_bytes=VMEM_LIMIT),
        name="odd_out",
    )(x, a, wo, nrm, router)


def _row_gather_copy(src_hbm, src_row, dst_ref, dst_row, sem):
    return pltpu.make_async_copy(
        src_hbm.at[pl.ds(pl.multiple_of(src_row * ROW_TILE, ROW_TILE), ROW_TILE), :],
        dst_ref.at[pl.ds(pl.multiple_of(dst_row * ROW_TILE, ROW_TILE), ROW_TILE), :],
        sem)


GATHER_UNROLL = 8
MXU_TILE = 256
MOE_FF_CHUNKS = 2
assert (D_FF_EXPERT // MOE_FF_CHUNKS) % MXU_TILE == 0


def _start_row_gathers(src_hbm, idx_ref, idx_row, first, count, dst_ref, sem, *, inline):
    if inline:
        for u in range(count):
            _row_gather_copy(src_hbm, idx_ref[idx_row, first + u], dst_ref, first + u, sem).start(priority=u % 2)
        return

    def issue(r2, c):
        for u in range(2):
            r = 2 * r2 + u
            _row_gather_copy(src_hbm, idx_ref[idx_row, r], dst_ref, r, sem).start(priority=u)
        return c

    assert first % 2 == 0 and count % 2 == 0
    lax.fori_loop(first // 2, (first + count) // 2, issue, 0, unroll=GATHER_UNROLL // 2)


def _wait_row_gathers(src_hbm, dst_ref, sem):
    pltpu.make_async_copy(src_hbm.at[pl.ds(0, dst_ref.shape[0]), :], dst_ref, sem).wait()


def _moe_ffn_kernel(te_ref, nu_ref, tok_ref, tok_next_ref, h_hbm, w1_ref, w3_ref, w2_ref, o_ref, xs_ref, buf_ref,
                    sem, *, tm, n_chunks):
    i = pl.program_id(0)
    n_used = nu_ref[0]
    used = i < n_used
    slot = i % 2

    @pl.when(i == 0)
    def _():
        _start_row_gathers(h_hbm, tok_ref, 0, 0, tm, buf_ref.at[0], sem.at[0], inline=False)

    @pl.when(i <= n_used)
    def _():
        _wait_row_gathers(h_hbm, buf_ref.at[slot], sem.at[slot])
        xs_ref[...] = _load_token_tiles(buf_ref.at[slot], tm).astype(BF16)

    @pl.when(used)
    def _():
        n_groups = 3 * n_chunks
        per_group = tm // n_groups
        starts = [(g * per_group, per_group if g + 1 < n_groups else tm - g * per_group) for g in range(n_groups)]

        def prefetch(g):
            first, count = starts[g]
            _start_row_gathers(h_hbm, tok_next_ref, 0, first, count, buf_ref.at[1 - slot], sem.at[1 - slot],
                               inline=True)

        x = xs_ref[...]
        tf = D_FF_EXPERT // n_chunks
        y = None
        for c in range(n_chunks):
            cols = slice(c * tf, (c + 1) * tf)
            prefetch(3 * c)
            a = _dot(x, w1_ref[:, cols])
            prefetch(3 * c + 1)
            b = _dot(x, w3_ref[:, cols])
            prefetch(3 * c + 2)
            part = _dot((a * _sigmoid(a) * b).astype(BF16), w2_ref[cols, :])
            y = part if y is None else y + part
        _store_token_tiles(o_ref, y, tm)

    @pl.when(jnp.logical_not(used))
    def _():
        o_ref[...] = jnp.zeros_like(o_ref)


def _moe_ffn(tile_expert, n_used, row_tok, h_tiles, w1, w3, w2, *, tm, n_chunks):
    n_tiles = row_tok.shape[0] - 1

    def expert_spec(shape):
        return pl.BlockSpec((None,) + shape, lambda i, te, nu: (te[i], 0, 0), pipeline_mode=pl.Buffered(1))

    grid_spec = pltpu.PrefetchScalarGridSpec(
        num_scalar_prefetch=2,
        grid=(n_tiles,),
        in_specs=[
            pl.BlockSpec((None, 1, tm), lambda i, te, nu: (i, 0, 0), memory_space=pltpu.SMEM),
            pl.BlockSpec((None, 1, tm), lambda i, te, nu: (i + 1, 0, 0), memory_space=pltpu.SMEM),
            pl.BlockSpec(memory_space=pl.ANY),
            expert_spec((D_MODEL, D_FF_EXPERT)), expert_spec((D_MODEL, D_FF_EXPERT)),
            expert_spec((D_FF_EXPERT, D_MODEL)),
        ],
        out_specs=pl.BlockSpec((tm * ROW_TILE, LANES), lambda i, te, nu: (i, 0)),
        scratch_shapes=[pltpu.VMEM((tm, D_MODEL), BF16), pltpu.VMEM((2, tm * ROW_TILE, LANES), F32),
                        pltpu.SemaphoreType.DMA((2,))],
    )
    return pl.pallas_call(
        functools.partial(_moe_ffn_kernel, tm=tm, n_chunks=n_chunks),
        out_shape=jax.ShapeDtypeStruct((n_tiles * tm * ROW_TILE, LANES), F32),
        grid_spec=grid_spec,
        compiler_params=pltpu.CompilerParams(dimension_semantics=("arbitrary",), vmem_limit_bytes=VMEM_LIMIT),
        name="moe_ffn",
    )(tile_expert, n_used, row_tok, row_tok, h_tiles, w1, w3, w2)


def _moe_combine_kernel(pos_ref, pos_next_ref, x_ref, route_ref, nrm_ref, y_hbm, o_ref, buf_ref, sem, *, tm):
    i = pl.program_id(0)
    slot = i % 2

    def start_tile(idx_ref, s, inline):
        for k in range(2):
            _start_row_gathers(y_hbm, idx_ref, k, 0, tm, buf_ref.at[s, k], sem.at[s, k], inline=inline)

    @pl.when(i == 0)
    def _():
        start_tile(pos_ref, 0, False)

    @pl.when(i + 1 < pl.num_programs(0))
    def _():
        start_tile(pos_next_ref, 1 - slot, True)

    for k in range(2):
        _wait_row_gathers(y_hbm, buf_ref.at[slot, k], sem.at[slot, k])
    g1 = route_ref[:, 2:3]
    g2 = route_ref[:, 3:4]
    x = x_ref[...] + (g1 * _load_token_tiles(buf_ref.at[slot, 0], tm) + g2 * _load_token_tiles(buf_ref.at[slot, 1], tm))
    o_ref[...] = _rms(x, nrm_ref[...])


def _moe_combine(pos, x, route, nrm, y_tiles, *, tm):
    t = x.shape[0]
    n = t // tm
    row_spec = pl.BlockSpec((tm, D_MODEL), lambda i: (i, 0))
    return pl.pallas_call(
        functools.partial(_moe_combine_kernel, tm=tm),
        out_shape=jax.ShapeDtypeStruct((t, D_MODEL), F32),
        grid=(n,),
        in_specs=[pl.BlockSpec((None, 2, tm), lambda i: (i, 0, 0), memory_space=pltpu.SMEM),
                  pl.BlockSpec((None, 2, tm), lambda i: (jnp.minimum(i + 1, n - 1), 0, 0), memory_space=pltpu.SMEM),
                  row_spec, pl.BlockSpec((tm, LANES), lambda i: (i, 0)), _const_spec(nrm.shape),
                  pl.BlockSpec(memory_space=pl.ANY)],
        out_specs=row_spec,
        scratch_shapes=[pltpu.VMEM((2, 2, tm * ROW_TILE, LANES), F32), pltpu.SemaphoreType.DMA((2, 2))],
        compiler_params=pltpu.CompilerParams(dimension_semantics=("arbitrary",), vmem_limit_bytes=VMEM_LIMIT),
        name="moe_combine",
    )(pos, pos, x, route, nrm, y_tiles)


SCATTER_CHUNK = 2048


def _invert_rows_kernel(pos_ref, zeros_hbm, out_ref, sem, *, n_tokens, chunk):
    j = pl.program_id(0)

    @pl.when(j == 0)
    def _():
        fill = pltpu.make_async_copy(zeros_hbm, out_ref, sem.at[0])
        fill.start()
        fill.wait()

    base = lax.rem(j * chunk, n_tokens)

    def place(u, c):
        out_ref[pos_ref[0, u]] = base + u
        return c

    lax.fori_loop(0, chunk, place, 0, unroll=32)


def _invert_rows(pos, n_rows, n_tokens):
    chunk = min(SCATTER_CHUNK, n_tokens)
    n_chunks = pos.shape[0] // chunk
    assert n_chunks * chunk == pos.shape[0] and n_tokens % chunk == 0
    return pl.pallas_call(
        functools.partial(_invert_rows_kernel, n_tokens=n_tokens, chunk=chunk),
        out_shape=jax.ShapeDtypeStruct((n_rows,), jnp.int32),
        grid=(n_chunks,),
        in_specs=[pl.BlockSpec((None, 1, chunk), lambda j: (j, 0, 0), memory_space=pltpu.SMEM),
                  pl.BlockSpec(memory_space=pl.ANY)],
        out_specs=pl.BlockSpec(memory_space=pltpu.SMEM),
        scratch_shapes=[pltpu.SemaphoreType.DMA((1,))],
        compiler_params=pltpu.CompilerParams(dimension_semantics=("arbitrary",)),
        name="invert_rows",
    )(pos.reshape(n_chunks, 1, chunk), jnp.zeros((n_rows,), jnp.int32))


def _routing_tables(idx1, idx2, *, tm, n_tiles):
    t = idx1.shape[0]
    e_flat = jnp.concatenate([idx1, idx2])
    onehot = (e_flat[:, None] == jnp.arange(N_EXPERTS, dtype=jnp.int32)[None, :]).astype(jnp.int32)
    csum = jnp.cumsum(onehot, axis=0)
    rank = jnp.sum((csum - onehot) * onehot, axis=1)
    counts = csum[-1]
    tiles_e = (counts + tm - 1) // tm
    tile_end = jnp.cumsum(tiles_e)
    tile_start = tile_end - tiles_e
    pos = jnp.sum(onehot * tile_start[None, :], axis=1) * tm + rank
    n_used = tile_end[-1]
    tile_ids = jnp.arange(n_tiles, dtype=jnp.int32)
    te = jnp.sum((tile_ids[:, None] >= tile_end[None, :]).astype(jnp.int32), axis=1)
    te_last = jnp.sum((n_used - 1 >= tile_end).astype(jnp.int32))
    tile_expert = jnp.where(tile_ids < n_used, te, te_last).astype(jnp.int32)
    row_tok = _invert_rows(pos.astype(jnp.int32), n_tiles * tm, t)
    return pos[:t], pos[t:], row_tok, tile_expert, n_used.reshape(1).astype(jnp.int32)


def _pad_cols(w, n):
    return jnp.pad(w, ((0, 0), (0, n - w.shape[1])))


def kernel(x, even_norm_mix, even_w_in, even_gate_up, even_gate_bias, even_w_s, even_b_s, even_ln_g, even_ln_b,
           even_head_g, even_w_o, even_norm_ffn, even_ffn_w1, even_ffn_w3, even_ffn_w2, odd_norm_mix, odd_w_in,
           odd_forget_bias, odd_q_g, odd_k_g, odd_w_o, odd_norm_ffn, odd_router, odd_exp_w1, odd_exp_w3,
           odd_exp_w2, final_norm):
    batch, seq, d = x.shape
    assert d == D_MODEL and seq % CHUNK == 0
    t = batch * seq
    xt = x.reshape(t, d)
    tiles = _tiles(seq)
    tm = tiles.row

    w_in = even_w_in[0]
    even_cuts = [A_WIDTH, 2 * A_WIDTH, 2 * A_WIDTH + B_QK_WIDTH, 2 * A_WIDTH + 2 * B_QK_WIDTH,
                 2 * A_WIDTH + 2 * B_QK_WIDTH + B_GATE_RANK, 2 * A_WIDTH + 2 * B_QK_WIDTH + B_GATE_RANK + B_V_WIDTH]
    u_w, v_w, q_w, k_w, g_w, vb_w, og_w = jnp.split(w_in, even_cuts, axis=1)
    win_e = jnp.concatenate([u_w, v_w, q_w, k_w, vb_w, og_w, _pad_cols(g_w, LANES)], axis=1).astype(BF16)
    gup = jnp.pad(even_gate_up[0], ((0, LANES - B_GATE_RANK), (0, 0))).astype(BF16)
    gb = even_gate_bias[0].reshape(1, B_QK_WIDTH)
    tril = jnp.tril(jnp.ones((CHUNK, CHUNK), dtype=bool))
    ws = jnp.where(tril[None], even_w_s[0], 0.0).astype(BF16)
    bs = jnp.broadcast_to(even_b_s[0][:, :, None], (A_GROUPS, CHUNK, LANES))
    lng = even_ln_g[0].reshape(1, A_WIDTH)
    lnb = even_ln_b[0].reshape(1, A_WIDTH)
    hg = even_head_g[0].reshape(B_HEADS, 1, B_VAL_DIM)
    ew1 = odd_exp_w1[0].reshape(N_EXPERTS * D_MODEL, D_FF_EXPERT)
    ew3 = odd_exp_w3[0].reshape(N_EXPERTS * D_MODEL, D_FF_EXPERT)
    ew2 = odd_exp_w2[0].reshape(N_EXPERTS * D_FF_EXPERT, D_MODEL)
    x1, ew1_b, fw1_b, fw3_b, fw2_b = _even_mixer(
        xt, even_norm_mix[0].reshape(1, d), win_e, gup, gb, ws, bs, lng, lnb, hg, even_w_o[0].astype(BF16),
        [ew1, even_ffn_w1[0], even_ffn_w3[0], even_ffn_w2[0]], seq=seq, tm=tm)
    x2, win_o = _dense_ffn(x1, even_norm_ffn[0].reshape(1, d), fw1_b, fw3_b, fw2_b, odd_w_in[0], O_F, tm=tm)

    wf_o = _pad_cols(odd_w_in[0][:, O_F:], LANES).astype(BF16)
    fb = jnp.pad(odd_forget_bias[0], (0, LANES - C_HEADS)).reshape(1, LANES)
    qg = jnp.tile(odd_q_g[0], C_HEADS).reshape(1, C_WIDTH)
    kg = jnp.tile(odd_k_g[0], C_HEADS).reshape(1, C_WIDTH)
    q_t, k_ext, vt, gate = _odd_inproj(x2, odd_norm_mix[0].reshape(1, d), win_o, wf_o, fb, qg, kg, seq=seq, tm=tm,
                                       tk=tiles.attn_k)
    attn, ew2_b, ew3_b = _fox_attention(q_t, k_ext, vt, gate, [ew2, ew3], batch=batch, seq=seq, tq=tiles.attn_q,
                                        tk=tiles.attn_k)

    router = _pad_cols(odd_router[0], LANES)
    x3, h_tiles, route = _odd_out(x2, attn, odd_w_o[0].astype(BF16), odd_norm_ffn[0].reshape(1, d), router, tm=tm)

    tm_moe, tm_comb = tiles.moe, tiles.combine
    n_tiles = (2 * t) // tm_moe + N_EXPERTS + 1
    idx1 = route[:, 0].astype(jnp.int32)
    idx2 = route[:, 1].astype(jnp.int32)
    pos1, pos2, row_tok, tile_expert, n_used = _routing_tables(idx1, idx2, tm=tm_moe, n_tiles=n_tiles + 1)
    y_tiles = _moe_ffn(tile_expert[:n_tiles], n_used, row_tok.reshape(n_tiles + 1, 1, tm_moe), h_tiles,
                       ew1_b.reshape(N_EXPERTS, D_MODEL, D_FF_EXPERT), ew3_b.reshape(N_EXPERTS, D_MODEL, D_FF_EXPERT),
                       ew2_b.reshape(N_EXPERTS, D_FF_EXPERT, D_MODEL), tm=tm_moe, n_chunks=MOE_FF_CHUNKS)
    pos = jnp.stack([pos1.reshape(t // tm_comb, tm_comb), pos2.reshape(t // tm_comb, tm_comb)], axis=1)
    out = _moe_combine(pos, x3, route, final_norm.reshape(1, d), y_tiles, tm=tm_comb)
    return out.reshape(batch, seq, d)
```

```python
import collections
import functools
import math

import jax
import jax.numpy as jnp
from jax import lax
from jax.experimental import pallas as pl
from jax.experimental.pallas import tpu as pltpu

F32 = jnp.float32
BF16 = jnp.bfloat16

EPS = 1e-6
D_MODEL = 1024
CHUNK = 128
SUB = 32
N_SUB = CHUNK // SUB
A_GROUPS = 4
A_WIDTH = 512
B_HEADS = 4
B_KEY_DIM = 64
B_VAL_DIM = 128
B_QK_WIDTH = 256
B_V_WIDTH = 512
B_GATE_RANK = 16
B_GATE_NORMALIZER = 16.0
C_HEADS = 16
C_HEAD_DIM = 64
C_WIDTH = 1024
D_FF_DENSE = 2816
N_EXPERTS = 8
D_FF_EXPERT = 3584
LANES = 128
MAX_DECAY_EXP = 60.0
LOG2E = math.log2(math.e)
ROW_TILE = 8

E_U, E_V, E_Q, E_K, E_VB, E_OG, E_G, E_END = 0, 512, 1024, 1280, 1536, 2048, 2560, 2688
O_Q, O_K, O_V, O_OG, O_F, O_END = 0, 1024, 2048, 3072, 4096, 4224

V7X_VMEM_BYTES = 64 * 1024 * 1024
VMEM_LIMIT = V7X_VMEM_BYTES - 8 * 1024 * 1024

Tiles = collections.namedtuple("Tiles", ["row", "attn_q", "attn_k", "moe", "combine"])


def _tiles(seq):
    return Tiles(row=min(512, seq), attn_q=min(1024, seq), attn_k=min(256, seq), moe=512, combine=min(256, seq))


def _rms(x, g):
    ms = jnp.mean(x * x, axis=-1, keepdims=True)
    return x * lax.rsqrt(ms + EPS) * g


def _gelu_tanh(x):
    c = math.sqrt(2.0 / math.pi)
    return x * (0.5 * (1.0 + jnp.tanh(c * (x + 0.044715 * (x * x * x)))))


def _sigmoid(x):
    return 1.0 / (1.0 + jnp.exp(-x))


def _log_sigmoid(x):
    return jnp.minimum(x, 0.0) - jnp.log(1.0 + jnp.exp(-jnp.abs(x)))


def _dot(a, b):
    return jnp.dot(a, b, preferred_element_type=F32)


def _dot_nt(a, b):
    return lax.dot_general(a, b, (((1,), (1,)), ((), ())), preferred_element_type=F32)


def _split3(x):
    hi = x.astype(BF16)
    r1 = x - hi.astype(F32)
    mid = r1.astype(BF16)
    lo = (r1 - mid.astype(F32)).astype(BF16)
    return hi, mid, lo


def _cumsum_rows(tril_b, x):
    hi, mid, lo = _split3(x)
    return _dot(tril_b, hi) + _dot(tril_b, mid) + _dot(tril_b, lo)


def _const_spec(shape):
    nd = len(shape)
    return pl.BlockSpec(shape, lambda *_: (0,) * nd)


def _side_cast_specs(w2d, n_steps, index_map):
    rows = w2d.shape[0] // n_steps
    assert rows * n_steps == w2d.shape[0] and rows % 16 == 0
    spec = pl.BlockSpec((rows, w2d.shape[1]), index_map)
    return spec, spec, jax.ShapeDtypeStruct(w2d.shape, BF16)


def _side_cast_specs_1d(w2d, n_steps):
    span = 1 if (w2d.shape[0] // n_steps) % 16 == 0 and w2d.shape[0] % n_steps == 0 else 2
    return _side_cast_specs(w2d, n_steps // span, lambda i: (i // span, 0))


def _even_mixer_kernel(x_ref, nrm_ref, win_ref, gup_ref, gb_ref, ws_ref, bs_ref, lng_ref, lnb_ref,
                       hg_ref, wo_ref, *rest, tiles_per_batch, n_chunks, n_casts):
    cast_refs, o_ref, cast_out_refs = rest[:n_casts], rest[n_casts], rest[n_casts + 1:2 * n_casts + 1]
    z_ref, mix_ref, st_ref = rest[2 * n_casts + 1:]
    i = pl.program_id(0)
    for src, dst in zip(cast_refs, cast_out_refs):
        dst[...] = src[...].astype(BF16)

    @pl.when(i % tiles_per_batch == 0)
    def _():
        st_ref[...] = jnp.zeros_like(st_ref)

    h = _rms(x_ref[...], nrm_ref[...]).astype(BF16)
    z_ref[...] = _dot(h, win_ref[...])

    row = lax.broadcasted_iota(jnp.int32, (CHUNK, CHUNK), 0)
    col = lax.broadcasted_iota(jnp.int32, (CHUNK, CHUNK), 1)
    tril_b = (col <= row).astype(BF16)
    sub_row = row & (SUB - 1)
    head_lane = lax.broadcasted_iota(jnp.int32, (1, B_QK_WIDTH), 1) // B_KEY_DIM
    bd_mask = (lax.broadcasted_iota(jnp.int32, (B_V_WIDTH, B_QK_WIDTH), 0) // B_VAL_DIM
               == lax.broadcasted_iota(jnp.int32, (B_V_WIDTH, B_QK_WIDTH), 1) // B_KEY_DIM)

    def chunk_body(c, carry):
        rows = pl.ds(pl.multiple_of(c * CHUNK, CHUNK), CHUNK)

        u = _gelu_tanh(z_ref[rows, E_U:E_V])
        v = _gelu_tanh(z_ref[rows, E_V:E_Q])
        mu = jnp.mean(v, axis=-1, keepdims=True)
        vc = v - mu
        var = jnp.mean(vc * vc, axis=-1, keepdims=True)
        vln = (vc * lax.rsqrt(var + EPS) * lng_ref[...] + lnb_ref[...]).astype(BF16)
        for g in range(A_GROUPS):
            sl = slice(g * LANES, (g + 1) * LANES)
            mixed = _dot(ws_ref[g], vln[:, sl]) + bs_ref[g]
            mix_ref[rows, sl] = (u[:, sl] * mixed).astype(BF16)

        q = z_ref[rows, E_Q:E_K] * (B_KEY_DIM ** -0.5)
        k = z_ref[rows, E_K:E_VB]
        vb = z_ref[rows, E_VB:E_OG]
        og = z_ref[rows, E_OG:E_G]
        glr = z_ref[rows, E_G:E_END].astype(BF16)
        logit = _dot(glr, gup_ref[...]) + gb_ref[...]
        log_a = _log_sigmoid(logit) * (1.0 / B_GATE_NORMALIZER)
        g_cum = _cumsum_rows(tril_b, log_a)
        g_last = g_cum[CHUNK - 1:CHUNK, :]
        st = st_ref[...]
        o = _dot_nt((q * jnp.exp(g_cum)).astype(BF16), st.astype(BF16))

        p_rows = [[None] * N_SUB for _ in range(B_HEADS)]
        for s in range(N_SUB):
            gs = g_cum[s * SUB:(s + 1) * SUB, :]
            if s == 0:
                qt = q[0:SUB, :] * jnp.exp(gs)
                kt = k * jnp.exp(jnp.minimum(-g_cum, MAX_DECAY_EXP))
            else:
                ref_g = g_cum[s * SUB - 1:s * SUB, :]
                qt = q[s * SUB:(s + 1) * SUB, :] * jnp.exp(gs - ref_g)
                kt = k * jnp.exp(jnp.minimum(ref_g - g_cum, MAX_DECAY_EXP))
            qs = jnp.concatenate([jnp.where(head_lane == hh, qt, 0.0) for hh in range(B_HEADS)],
                                 axis=0).astype(BF16)
            sc = _dot_nt(qs, kt.astype(BF16))
            sc = jnp.where(col <= (s * SUB + sub_row), sc, 0.0)
            for hh in range(B_HEADS):
                p_rows[hh][s] = sc[hh * SUB:(hh + 1) * SUB, :]

        vb_b = vb.astype(BF16)
        for hh in range(B_HEADS):
            sl = slice(hh * B_VAL_DIM, (hh + 1) * B_VAL_DIM)
            ph = jnp.concatenate(p_rows[hh], axis=0).astype(BF16)
            oh = o[:, sl] + _dot(ph, vb_b[:, sl])
            on = _rms(oh, hg_ref[hh])
            ogh = og[:, sl]
            mix_ref[rows, A_WIDTH + hh * B_VAL_DIM:A_WIDTH + (hh + 1) * B_VAL_DIM] = (
                on * (ogh * _sigmoid(ogh))).astype(BF16)

        k_dec = (k * jnp.exp(g_last - g_cum)).astype(BF16)
        upd = _dot(vb.T.astype(BF16), k_dec)
        st_ref[...] = jnp.exp(g_last) * st + jnp.where(bd_mask, upd, 0.0)
        return carry

    lax.fori_loop(0, n_chunks, chunk_body, 0, unroll=True)
    o_ref[...] = x_ref[...] + _dot(mix_ref[...], wo_ref[...])


def _even_mixer(x, nrm, win, gup, gb, ws, bs, lng, lnb, hg, wo, wcasts, *, seq, tm):
    t = x.shape[0]
    kern = functools.partial(_even_mixer_kernel, tiles_per_batch=seq // tm, n_chunks=tm // CHUNK,
                             n_casts=len(wcasts))
    casts = [_side_cast_specs_1d(w, t // tm) for w in wcasts]
    return pl.pallas_call(
        kern,
        out_shape=[jax.ShapeDtypeStruct((t, D_MODEL), F32)] + [c[2] for c in casts],
        grid=(t // tm,),
        in_specs=[
            pl.BlockSpec((tm, D_MODEL), lambda i: (i, 0)),
            _const_spec(nrm.shape), _resident_spec(win.shape), _const_spec(gup.shape), _const_spec(gb.shape),
            _const_spec(ws.shape), _const_spec(bs.shape), _const_spec(lng.shape), _const_spec(lnb.shape),
            _const_spec(hg.shape), _resident_spec(wo.shape),
        ] + [c[0] for c in casts],
        out_specs=[pl.BlockSpec((tm, D_MODEL), lambda i: (i, 0))] + [c[1] for c in casts],
        scratch_shapes=[
            pltpu.VMEM((tm, E_END), F32),
            pltpu.VMEM((tm, D_MODEL), BF16),
            pltpu.VMEM((B_V_WIDTH, B_QK_WIDTH), F32),
        ],
        compiler_params=pltpu.CompilerParams(dimension_semantics=("arbitrary",), vmem_limit_bytes=VMEM_LIMIT),
        name="even_mixer",
    )(x, nrm, win, gup, gb, ws, bs, lng, lnb, hg, wo, *wcasts)


def _dense_ffn_kernel(x_ref, nrm_ref, w1_ref, w3_ref, w2_ref, wcast_ref, o_ref, wcast_out_ref):
    wcast_out_ref[...] = wcast_ref[:, :wcast_out_ref.shape[1]].astype(BF16)
    x = x_ref[...]
    h = _rms(x, nrm_ref[...]).astype(BF16)
    a = _dot(h, w1_ref[...])
    b = _dot(h, w3_ref[...])
    o_ref[...] = x + _dot((a * _sigmoid(a) * b).astype(BF16), w2_ref[...])


def _resident_spec(shape):
    nd = len(shape)
    return pl.BlockSpec(shape, lambda *_: (0,) * nd, pipeline_mode=pl.Buffered(1))


def _dense_ffn(x, nrm, w1, w3, w2, wcast, wcast_cols, *, tm):
    t = x.shape[0]
    n_steps = t // tm
    row_spec = pl.BlockSpec((tm, D_MODEL), lambda i: (i, 0))
    rows = wcast.shape[0] // n_steps
    assert rows * n_steps == wcast.shape[0] and rows % 16 == 0
    return pl.pallas_call(
        _dense_ffn_kernel,
        out_shape=[jax.ShapeDtypeStruct((t, D_MODEL), F32), jax.ShapeDtypeStruct((wcast.shape[0], wcast_cols), BF16)],
        grid=(n_steps,),
        in_specs=[row_spec, _const_spec(nrm.shape), _resident_spec(w1.shape), _resident_spec(w3.shape),
                  _resident_spec(w2.shape), pl.BlockSpec((rows, wcast.shape[1]), lambda i: (i, 0))],
        out_specs=[row_spec, pl.BlockSpec((rows, wcast_cols), lambda i: (i, 0))],
        compiler_params=pltpu.CompilerParams(dimension_semantics=("arbitrary",), vmem_limit_bytes=VMEM_LIMIT),
        name="dense_ffn",
    )(x, nrm, w1, w3, w2, wcast)


def _head_rms(x, gain):
    lo = lax.broadcasted_iota(jnp.int32, (1, LANES), 1) < C_HEAD_DIM
    outs = []
    for t in range(C_WIDTH // LANES):
        xt = x[:, t * LANES:(t + 1) * LANES]
        sq = xt * xt
        s_lo = jnp.sum(jnp.where(lo, sq, 0.0), axis=-1, keepdims=True)
        s_hi = jnp.sum(jnp.where(lo, 0.0, sq), axis=-1, keepdims=True)
        inv = jnp.where(lo, lax.rsqrt(s_lo * (1.0 / C_HEAD_DIM) + EPS), lax.rsqrt(s_hi * (1.0 / C_HEAD_DIM) + EPS))
        outs.append(xt * inv)
    return jnp.concatenate(outs, axis=-1) * gain


N_PAIRS = C_HEADS // 2
K_EXT = 2 * LANES
BIAS_PARTS = 3


def _bias_placement():
    src = jnp.arange(BIAS_PARTS * LANES)
    part, head = src // LANES, src % LANES
    dst = (head // 2) * LANES + BIAS_PARTS * (head % 2) + part
    hit = (dst[:, None] == jnp.arange(N_PAIRS * LANES)[None, :]) & (head < C_HEADS)[:, None]
    return hit.astype(BF16)


def _odd_inproj_kernel(x_ref, nrm_ref, w_ref, wf_ref, fb_ref, qg_ref, kg_ref, place_ref,
                       q_ref, k_ref, vt_ref, gate_ref, z_ref, c_ref, carry_ref, *, tiles_per_batch, n_chunks, tk):
    i = pl.program_id(0)

    @pl.when(i % tiles_per_batch == 0)
    def _():
        carry_ref[...] = jnp.zeros_like(carry_ref)

    h = _rms(x_ref[...], nrm_ref[...]).astype(BF16)
    z_ref[:, :O_F] = _dot(h, w_ref[...])
    z_ref[:, O_F:] = _dot(h, wf_ref[...])
    qn = _head_rms(z_ref[:, O_Q:O_K], qg_ref[...]) * (C_HEAD_DIM ** -0.5 * LOG2E)
    for p in range(N_PAIRS):
        q_ref[p] = qn[:, p * LANES:(p + 1) * LANES].T.astype(BF16)
    kn = _head_rms(z_ref[:, O_K:O_V], kg_ref[...]).astype(BF16)
    for p in range(N_PAIRS):
        for kb in range(vt_ref.shape[1]):
            blk = z_ref[kb * tk:(kb + 1) * tk, O_V + p * LANES:O_V + (p + 1) * LANES]
            vt_ref[p, kb] = blk.T.astype(BF16)
    gate_ref[...] = _sigmoid(z_ref[:, O_OG:O_F]).astype(BF16)

    row = lax.broadcasted_iota(jnp.int32, (CHUNK, CHUNK), 0)
    col = lax.broadcasted_iota(jnp.int32, (CHUNK, CHUNK), 1)
    tril_b = (col <= row).astype(BF16)
    carry = carry_ref[...]
    for c in range(n_chunks):
        rows = slice(c * CHUNK, (c + 1) * CHUNK)
        log_f = _log_sigmoid(z_ref[rows, O_F:O_END] + fb_ref[...])
        cs = _cumsum_rows(tril_b, log_f) + carry
        c_ref[rows, :] = cs
        carry = cs[CHUNK - 1:CHUNK, :]
    carry_ref[...] = carry

    bias = _dot(jnp.concatenate(_split3(c_ref[...] * LOG2E), axis=1), place_ref[...]).astype(BF16)
    for p in range(N_PAIRS):
        k_ref[:, p * K_EXT:p * K_EXT + LANES] = kn[:, p * LANES:(p + 1) * LANES]
        k_ref[:, p * K_EXT + LANES:(p + 1) * K_EXT] = bias[:, p * LANES:(p + 1) * LANES]


def _odd_inproj(x, nrm, w, wf, fb, qg, kg, *, seq, tm, tk):
    t = x.shape[0]
    tiles_per_batch = seq // tm
    kern = functools.partial(_odd_inproj_kernel, tiles_per_batch=tiles_per_batch, n_chunks=tm // CHUNK, tk=tk)
    row_spec = pl.BlockSpec((tm, C_WIDTH), lambda i: (i, 0))
    kext_spec = pl.BlockSpec((tm, N_PAIRS * K_EXT), lambda i: (i, 0))
    vt_spec = pl.BlockSpec((None, N_PAIRS, tm // tk, LANES, tk),
                           lambda i: (i // tiles_per_batch, 0, i % tiles_per_batch, 0, 0))
    place = _bias_placement()
    wide = jax.ShapeDtypeStruct((t, C_WIDTH), BF16)
    return pl.pallas_call(
        kern,
        out_shape=[jax.ShapeDtypeStruct((t // seq, N_PAIRS, LANES, seq), BF16),
                   jax.ShapeDtypeStruct((t, N_PAIRS * K_EXT), BF16),
                   jax.ShapeDtypeStruct((t // seq, N_PAIRS, seq // tk, LANES, tk), BF16), wide],
        grid=(t // tm,),
        in_specs=[row_spec, _const_spec(nrm.shape), _resident_spec(w.shape), _const_spec(wf.shape),
                  _const_spec(fb.shape), _const_spec(qg.shape), _const_spec(kg.shape), _const_spec(place.shape)],
        out_specs=[pl.BlockSpec((None, N_PAIRS, LANES, tm), lambda i: (i // tiles_per_batch, 0, 0, i % tiles_per_batch)),
                   kext_spec, vt_spec, row_spec],
        scratch_shapes=[pltpu.VMEM((tm, O_END), F32), pltpu.VMEM((tm, LANES), F32), pltpu.VMEM((1, LANES), F32)],
        compiler_params=pltpu.CompilerParams(dimension_semantics=("arbitrary",), vmem_limit_bytes=VMEM_LIMIT),
        name="odd_inproj",
    )(x, nrm, w, wf, fb, qg, kg, place)


NEG_BIG = -1e30


V_ROWS = 80


def _fox_kernel(q_ref, k_ref, vt_ref, gate_ref, *rest, tq, tk, n_casts):
    cast_refs, o_ref, cast_out_refs, s_ref = rest[:n_casts], rest[n_casts], rest[n_casts + 1:-1], rest[-1]
    for src, dst in zip(cast_refs, cast_out_refs):
        dst[...] = src[...].astype(BF16)
    qi = pl.program_id(2)
    ng = tq // tk
    key_i = lax.broadcasted_iota(jnp.int32, (tk, tk), 0)
    qry_i = lax.broadcasted_iota(jnp.int32, (tk, tk), 1)
    causal = key_i <= qry_i
    chains = [(hh, r) for r in range(ng) for hh in range(2)]
    feat = lax.broadcasted_iota(jnp.int32, (LANES, 1), 0)
    qms = []
    for hh, r in chains:
        q_t = q_ref[:, r * tk:(r + 1) * tk]
        own = (feat < C_HEAD_DIM) if hh == 0 else (feat >= C_HEAD_DIM)
        qh = jnp.where(own, q_t, jnp.zeros_like(q_t))
        pick = (feat >= BIAS_PARTS * hh) & (feat < BIAS_PARTS * (hh + 1))
        minus_one = jnp.broadcast_to(jnp.where(pick, -1.0, 0.0).astype(BF16), qh.shape)
        qms.append(jnp.concatenate([qh, minus_one], axis=0))

    def key_rows(j):
        return pl.ds(pl.multiple_of(j * tk, tk), tk)

    def scores_to_scratch(j, slot, live, modes):
        kb = k_ref[key_rows(j), :]
        raw = [_dot(kb, qms[idx]) for idx in live]
        maxes = []
        for s, idx in zip(raw, live):
            hh, r = chains[idx]
            if modes[r] == "diag":
                s = jnp.where(causal, s, NEG_BIG)
            s_ref[slot, idx] = s
            maxes.append(jnp.max(s, axis=0, keepdims=True))
        return maxes

    ones_rows = (lax.broadcasted_iota(jnp.int32, (V_ROWS - C_HEAD_DIM, tk), 0) == 0).astype(BF16)

    def softmax_pv(j, slot, maxes, live, state):
        vt_pair = vt_ref[j]
        vtb = [jnp.concatenate([vt_pair[hh * C_HEAD_DIM:(hh + 1) * C_HEAD_DIM, :], ones_rows], axis=0)
               for hh in range(2)]
        new = list(state)
        probs = []
        for bm, idx in zip(maxes, live):
            m = state[2 * idx]
            m_new = jnp.maximum(m, bm)
            p = jnp.exp2(s_ref[slot, idx] - m_new)
            new[2 * idx] = m_new
            probs.append((jnp.exp2(m - m_new), p.astype(BF16)))
        for (alpha, p), idx in zip(probs, live):
            hh, r = chains[idx]
            new[2 * idx + 1] = alpha * state[2 * idx + 1] + _dot(vtb[hh], p)
        return new

    assert ng % 2 == 0
    all_chains = list(range(len(chains)))
    n_state = 2 * len(chains)
    state = []
    for _ in chains:
        state += [jnp.full((1, tk), NEG_BIG, F32), jnp.zeros((V_ROWS, tk), F32)]
    n_full = qi * ng
    full_modes = ("full",) * ng

    def diag_modes(g):
        return tuple("skip" if r < g else ("diag" if r == g else "full") for r in range(ng))

    def live_chains(g):
        return [idx for idx, (hh, r) in enumerate(chains) if r >= g]

    def trip(j, carry, next_modes):
        st, mx = list(carry[:n_state]), carry[n_state:]
        for b in range(ng):
            mx_next = scores_to_scratch(j + b + 1, (b + 1) % 2, all_chains, full_modes if b + 1 < ng else next_modes)
            st = softmax_pv(j + b, b % 2, mx, all_chains, st)
            mx = mx_next
        return tuple(st) + tuple(mx)

    def with_full_blocks(_):
        first = scores_to_scratch(0, 0, all_chains, full_modes)
        carry = lax.fori_loop(0, qi - 1, lambda i, c: trip(ng * i, c, full_modes), tuple(state) + tuple(first))
        return trip(n_full - ng, carry, diag_modes(0))

    def no_full_blocks(_):
        return tuple(state) + tuple(scores_to_scratch(0, 0, all_chains, diag_modes(0)))

    carry = lax.cond(qi > 0, with_full_blocks, no_full_blocks, 0)
    state, mx = list(carry[:n_state]), carry[n_state:]
    for g in range(ng):
        if g + 1 < ng:
            mx_next = scores_to_scratch(n_full + g + 1, (g + 1) % 2, live_chains(g + 1), diag_modes(g + 1))
        state = softmax_pv(n_full + g, g % 2, mx, live_chains(g), state)
        if g + 1 < ng:
            mx = mx_next

    for r in range(ng):
        parts = []
        for hh in range(2):
            acc = state[2 * chains.index((hh, r)) + 1]
            parts.append(acc[:C_HEAD_DIM, :] / acc[C_HEAD_DIM:C_HEAD_DIM + 1, :])
        o = jnp.concatenate(parts, axis=0).T
        rows = slice(r * tk, (r + 1) * tk)
        o_ref[rows, :] = (o * gate_ref[rows, :].astype(F32)).astype(BF16)


def _fox_attention(q, k_ext, vt, gate, wcasts, *, batch, seq, tq, tk):
    t = batch * seq
    nq = seq // tq
    kern = functools.partial(_fox_kernel, tq=tq, tk=tk, n_casts=len(wcasts))
    casts = [_side_cast_specs(w, batch * N_PAIRS * nq, lambda b, p, i: ((b * N_PAIRS + p) * nq + i, 0))
             for w in wcasts]
    return pl.pallas_call(
        kern,
        out_shape=[jax.ShapeDtypeStruct((t, C_WIDTH), BF16)] + [c[2] for c in casts],
        grid=(batch, N_PAIRS, nq),
        in_specs=[
            pl.BlockSpec((None, None, LANES, tq), lambda b, p, i: (b, p, 0, i)),
            pl.BlockSpec((seq, K_EXT), lambda b, p, i: (b, p)),
            pl.BlockSpec((None, None, seq // tk, LANES, tk), lambda b, p, i: (b, p, 0, 0, 0)),
            pl.BlockSpec((tq, LANES), lambda b, p, i: (b * nq + i, p)),
        ] + [c[0] for c in casts],
        out_specs=[pl.BlockSpec((tq, LANES), lambda b, p, i: (b * nq + i, p))] + [c[1] for c in casts],
        scratch_shapes=[pltpu.VMEM((2, 2 * (tq // tk), tk, tk), F32)],
        compiler_params=pltpu.CompilerParams(dimension_semantics=("arbitrary", "arbitrary", "arbitrary"),
                                             vmem_limit_bytes=VMEM_LIMIT),
        name="fox_attn",
    )(q, k_ext, vt, gate, *wcasts)


def _store_token_tiles(dst_ref, val, n_rows):
    for s in range(ROW_TILE):
        dst_ref[pl.ds(s, n_rows, stride=ROW_TILE), :] = val[:, s * LANES:(s + 1) * LANES]


def _load_token_tiles(src_ref, n_rows):
    return jnp.concatenate([src_ref[pl.ds(s, n_rows, stride=ROW_TILE), :] for s in range(ROW_TILE)], axis=1)


def _odd_out_kernel(x_ref, a_ref, wo_ref, nrm_ref, r_ref, x3_ref, h_ref, route_ref):
    x3 = x_ref[...] + _dot(a_ref[...], wo_ref[...])
    x3_ref[...] = x3
    h = _rms(x3, nrm_ref[...])
    _store_token_tiles(h_ref, h, h.shape[0])

    h_hi = h.astype(BF16)
    h_lo = (h - h_hi.astype(F32)).astype(BF16)
    r = r_ref[...]
    r_hi = r.astype(BF16)
    r_lo = (r - r_hi.astype(F32)).astype(BF16)
    n = h.shape[0]
    prod = _dot(jnp.concatenate([h_hi, h_lo], axis=0), jnp.concatenate([r_hi, r_lo], axis=1))
    logits = prod[:n, :LANES] + (prod[n:, :LANES] + prod[:n, LANES:])

    lane = lax.broadcasted_iota(jnp.int32, logits.shape, 1).astype(F32)
    neg_inf = jnp.float32(-jnp.inf)
    lg = jnp.where(lane < N_EXPERTS, logits, neg_inf)
    m1 = jnp.max(lg, axis=-1, keepdims=True)
    i1 = jnp.min(jnp.where(lg == m1, lane, float(LANES)), axis=-1, keepdims=True)
    lg2 = jnp.where(lane == i1, neg_inf, lg)
    m2 = jnp.max(lg2, axis=-1, keepdims=True)
    i2 = jnp.min(jnp.where(lg2 == m2, lane, float(LANES)), axis=-1, keepdims=True)
    e2 = jnp.exp(m2 - m1)
    g1 = 1.0 / (1.0 + e2)
    g2 = e2 / (1.0 + e2)
    route_ref[...] = jnp.where(lane == 0, i1, jnp.where(lane == 1, i2, jnp.where(lane == 2, g1,
                               jnp.where(lane == 3, g2, 0.0))))


def _odd_out(x, a, wo, nrm, router, *, tm):
    t = x.shape[0]
    row_spec = pl.BlockSpec((tm, D_MODEL), lambda i: (i, 0))
    return pl.pallas_call(
        _odd_out_kernel,
        out_shape=[jax.ShapeDtypeStruct((t, D_MODEL), F32), jax.ShapeDtypeStruct((t * ROW_TILE, LANES), F32),
                   jax.ShapeDtypeStruct((t, LANES), F32)],
        grid=(t // tm,),
        in_specs=[row_spec, row_spec, _const_spec(wo.shape), _const_spec(nrm.shape), _const_spec(router.shape)],
        out_specs=[row_spec, pl.BlockSpec((tm * ROW_TILE, LANES), lambda i: (i, 0)),
                   pl.BlockSpec((tm, LANES), lambda i: (i, 0))],
        compiler_params=pltpu.CompilerParams(dimension_semantics=("arbitrary",), vmem_limit_bytes=VMEM_LIMIT),
        name="odd_out",
    )(x, a, wo, nrm, router)


def _row_gather_copy(src_hbm, src_row, dst_ref, dst_row, sem):
    return pltpu.make_async_copy(
        src_hbm.at[pl.ds(pl.multiple_of(src_row * ROW_TILE, ROW_TILE), ROW_TILE), :],
        dst_ref.at[pl.ds(pl.multiple_of(dst_row * ROW_TILE, ROW_TILE), ROW_TILE), :],
        sem)


GATHER_UNROLL = 8
MXU_TILE = 256
MOE_FF_CHUNKS = 2
assert (D_FF_EXPERT // MOE_FF_CHUNKS) % MXU_TILE == 0


def _start_row_gathers(src_hbm, idx_ref, idx_row, first, count, dst_ref, sem, *, inline):
    if inline:
        for u in range(count):
            _row_gather_copy(src_hbm, idx_ref[idx_row, first + u], dst_ref, first + u, sem).start(priority=u % 2)
        return

    def issue(r2, c):
        for u in range(2):
            r = 2 * r2 + u
            _row_gather_copy(src_hbm, idx_ref[idx_row, r], dst_ref, r, sem).start(priority=u)
        return c

    assert first % 2 == 0 and count % 2 == 0
    lax.fori_loop(first // 2, (first + count) // 2, issue, 0, unroll=GATHER_UNROLL // 2)


def _wait_row_gathers(src_hbm, dst_ref, sem):
    pltpu.make_async_copy(src_hbm.at[pl.ds(0, dst_ref.shape[0]), :], dst_ref, sem).wait()


def _moe_ffn_kernel(te_ref, nu_ref, tok_ref, tok_next_ref, h_hbm, w1_ref, w3_ref, w2_ref, o_ref, xs_ref, buf_ref,
                    sem, *, tm, n_chunks):
    i = pl.program_id(0)
    n_used = nu_ref[0]
    used = i < n_used
    slot = i % 2

    @pl.when(i == 0)
    def _():
        _start_row_gathers(h_hbm, tok_ref, 0, 0, tm, buf_ref.at[0], sem.at[0], inline=False)

    @pl.when(i <= n_used)
    def _():
        _wait_row_gathers(h_hbm, buf_ref.at[slot], sem.at[slot])
        xs_ref[...] = _load_token_tiles(buf_ref.at[slot], tm).astype(BF16)

    @pl.when(used)
    def _():
        n_groups = 3 * n_chunks
        per_group = tm // n_groups
        starts = [(g * per_group, per_group if g + 1 < n_groups else tm - g * per_group) for g in range(n_groups)]

        def prefetch(g):
            first, count = starts[g]
            _start_row_gathers(h_hbm, tok_next_ref, 0, first, count, buf_ref.at[1 - slot], sem.at[1 - slot],
                               inline=True)

        x = xs_ref[...]
        tf = D_FF_EXPERT // n_chunks
        y = None
        for c in range(n_chunks):
            cols = slice(c * tf, (c + 1) * tf)
            prefetch(3 * c)
            a = _dot(x, w1_ref[:, cols])
            prefetch(3 * c + 1)
            b = _dot(x, w3_ref[:, cols])
            prefetch(3 * c + 2)
            part = _dot((a * _sigmoid(a) * b).astype(BF16), w2_ref[cols, :])
            y = part if y is None else y + part
        _store_token_tiles(o_ref, y, tm)

    @pl.when(jnp.logical_not(used))
    def _():
        o_ref[...] = jnp.zeros_like(o_ref)


def _moe_ffn(tile_expert, n_used, row_tok, h_tiles, w1, w3, w2, *, tm, n_chunks):
    n_tiles = row_tok.shape[0] - 1

    def expert_spec(shape):
        return pl.BlockSpec((None,) + shape, lambda i, te, nu: (te[i], 0, 0), pipeline_mode=pl.Buffered(1))

    grid_spec = pltpu.PrefetchScalarGridSpec(
        num_scalar_prefetch=2,
        grid=(n_tiles,),
        in_specs=[
            pl.BlockSpec((None, 1, tm), lambda i, te, nu: (i, 0, 0), memory_space=pltpu.SMEM),
            pl.BlockSpec((None, 1, tm), lambda i, te, nu: (i + 1, 0, 0), memory_space=pltpu.SMEM),
            pl.BlockSpec(memory_space=pl.ANY),
            expert_spec((D_MODEL, D_FF_EXPERT)), expert_spec((D_MODEL, D_FF_EXPERT)),
            expert_spec((D_FF_EXPERT, D_MODEL)),
        ],
        out_specs=pl.BlockSpec((tm * ROW_TILE, LANES), lambda i, te, nu: (i, 0)),
        scratch_shapes=[pltpu.VMEM((tm, D_MODEL), BF16), pltpu.VMEM((2, tm * ROW_TILE, LANES), F32),
                        pltpu.SemaphoreType.DMA((2,))],
    )
    return pl.pallas_call(
        functools.partial(_moe_ffn_kernel, tm=tm, n_chunks=n_chunks),
        out_shape=jax.ShapeDtypeStruct((n_tiles * tm * ROW_TILE, LANES), F32),
        grid_spec=grid_spec,
        compiler_params=pltpu.CompilerParams(dimension_semantics=("arbitrary",), vmem_limit_bytes=VMEM_LIMIT),
        name="moe_ffn",
    )(tile_expert, n_used, row_tok, row_tok, h_tiles, w1, w3, w2)


def _moe_combine_kernel(pos_ref, pos_next_ref, x_ref, route_ref, nrm_ref, y_hbm, o_ref, buf_ref, sem, *, tm):
    i = pl.program_id(0)
    slot = i % 2

    def start_tile(idx_ref, s, inline):
        for k in range(2):
            _start_row_gathers(y_hbm, idx_ref, k, 0, tm, buf_ref.at[s, k], sem.at[s, k], inline=inline)

    @pl.when(i == 0)
    def _():
        start_tile(pos_ref, 0, False)

    @pl.when(i + 1 < pl.num_programs(0))
    def _():
        start_tile(pos_next_ref, 1 - slot, True)

    for k in range(2):
        _wait_row_gathers(y_hbm, buf_ref.at[slot, k], sem.at[slot, k])
    g1 = route_ref[:, 2:3]
    g2 = route_ref[:, 3:4]
    x = x_ref[...] + (g1 * _load_token_tiles(buf_ref.at[slot, 0], tm) + g2 * _load_token_tiles(buf_ref.at[slot, 1], tm))
    o_ref[...] = _rms(x, nrm_ref[...])


def _moe_combine(pos, x, route, nrm, y_tiles, *, tm):
    t = x.shape[0]
    n = t // tm
    row_spec = pl.BlockSpec((tm, D_MODEL), lambda i: (i, 0))
    return pl.pallas_call(
        functools.partial(_moe_combine_kernel, tm=tm),
        out_shape=jax.ShapeDtypeStruct((t, D_MODEL), F32),
        grid=(n,),
        in_specs=[pl.BlockSpec((None, 2, tm), lambda i: (i, 0, 0), memory_space=pltpu.SMEM),
                  pl.BlockSpec((None, 2, tm), lambda i: (jnp.minimum(i + 1, n - 1), 0, 0), memory_space=pltpu.SMEM),
                  row_spec, pl.BlockSpec((tm, LANES), lambda i: (i, 0)), _const_spec(nrm.shape),
                  pl.BlockSpec(memory_space=pl.ANY)],
        out_specs=row_spec,
        scratch_shapes=[pltpu.VMEM((2, 2, tm * ROW_TILE, LANES), F32), pltpu.SemaphoreType.DMA((2, 2))],
        compiler_params=pltpu.CompilerParams(dimension_semantics=("arbitrary",), vmem_limit_bytes=VMEM_LIMIT),
        name="moe_combine",
    )(pos, pos, x, route, nrm, y_tiles)


SCATTER_CHUNK = 2048


def _invert_rows_kernel(pos_ref, zeros_hbm, out_ref, sem, *, n_tokens, chunk):
    j = pl.program_id(0)

    @pl.when(j == 0)
    def _():
        fill = pltpu.make_async_copy(zeros_hbm, out_ref, sem.at[0])
        fill.start()
        fill.wait()

    base = lax.rem(j * chunk, n_tokens)

    def place(u, c):
        out_ref[pos_ref[0, u]] = base + u
        return c

    lax.fori_loop(0, chunk, place, 0, unroll=32)


def _invert_rows(pos, n_rows, n_tokens):
    chunk = min(SCATTER_CHUNK, n_tokens)
    n_chunks = pos.shape[0] // chunk
    assert n_chunks * chunk == pos.shape[0] and n_tokens % chunk == 0
    return pl.pallas_call(
        functools.partial(_invert_rows_kernel, n_tokens=n_tokens, chunk=chunk),
        out_shape=jax.ShapeDtypeStruct((n_rows,), jnp.int32),
        grid=(n_chunks,),
        in_specs=[pl.BlockSpec((None, 1, chunk), lambda j: (j, 0, 0), memory_space=pltpu.SMEM),
                  pl.BlockSpec(memory_space=pl.ANY)],
        out_specs=pl.BlockSpec(memory_space=pltpu.SMEM),
        scratch_shapes=[pltpu.SemaphoreType.DMA((1,))],
        compiler_params=pltpu.CompilerParams(dimension_semantics=("arbitrary",)),
        name="invert_rows",
    )(pos.reshape(n_chunks, 1, chunk), jnp.zeros((n_rows,), jnp.int32))


def _routing_tables(idx1, idx2, *, tm, n_tiles):
    t = idx1.shape[0]
    e_flat = jnp.concatenate([idx1, idx2])
    onehot = (e_flat[:, None] == jnp.arange(N_EXPERTS, dtype=jnp.int32)[None, :]).astype(jnp.int32)
    csum = jnp.cumsum(onehot, axis=0)
    rank = jnp.sum((csum - onehot) * onehot, axis=1)
    counts = csum[-1]
    tiles_e = (counts + tm - 1) // tm
    tile_end = jnp.cumsum(tiles_e)
    tile_start = tile_end - tiles_e
    pos = jnp.sum(onehot * tile_start[None, :], axis=1) * tm + rank
    n_used = tile_end[-1]
    tile_ids = jnp.arange(n_tiles, dtype=jnp.int32)
    te = jnp.sum((tile_ids[:, None] >= tile_end[None, :]).astype(jnp.int32), axis=1)
    te_last = jnp.sum((n_used - 1 >= tile_end).astype(jnp.int32))
    tile_expert = jnp.where(tile_ids < n_used, te, te_last).astype(jnp.int32)
    row_tok = _invert_rows(pos.astype(jnp.int32), n_tiles * tm, t)
    return pos[:t], pos[t:], row_tok, tile_expert, n_used.reshape(1).astype(jnp.int32)


def _pad_cols(w, n):
    return jnp.pad(w, ((0, 0), (0, n - w.shape[1])))


def kernel(x, even_norm_mix, even_w_in, even_gate_up, even_gate_bias, even_w_s, even_b_s, even_ln_g, even_ln_b,
           even_head_g, even_w_o, even_norm_ffn, even_ffn_w1, even_ffn_w3, even_ffn_w2, odd_norm_mix, odd_w_in,
           odd_forget_bias, odd_q_g, odd_k_g, odd_w_o, odd_norm_ffn, odd_router, odd_exp_w1, odd_exp_w3,
           odd_exp_w2, final_norm):
    batch, seq, d = x.shape
    assert d == D_MODEL and seq % CHUNK == 0
    t = batch * seq
    xt = x.reshape(t, d)
    tiles = _tiles(seq)
    tm = tiles.row

    w_in = even_w_in[0]
    even_cuts = [A_WIDTH, 2 * A_WIDTH, 2 * A_WIDTH + B_QK_WIDTH, 2 * A_WIDTH + 2 * B_QK_WIDTH,
                 2 * A_WIDTH + 2 * B_QK_WIDTH + B_GATE_RANK, 2 * A_WIDTH + 2 * B_QK_WIDTH + B_GATE_RANK + B_V_WIDTH]
    u_w, v_w, q_w, k_w, g_w, vb_w, og_w = jnp.split(w_in, even_cuts, axis=1)
    win_e = jnp.concatenate([u_w, v_w, q_w, k_w, vb_w, og_w, _pad_cols(g_w, LANES)], axis=1).astype(BF16)
    gup = jnp.pad(even_gate_up[0], ((0, LANES - B_GATE_RANK), (0, 0))).astype(BF16)
    gb = even_gate_bias[0].reshape(1, B_QK_WIDTH)
    tril = jnp.tril(jnp.ones((CHUNK, CHUNK), dtype=bool))
    ws = jnp.where(tril[None], even_w_s[0], 0.0).astype(BF16)
    bs = jnp.broadcast_to(even_b_s[0][:, :, None], (A_GROUPS, CHUNK, LANES))
    lng = even_ln_g[0].reshape(1, A_WIDTH)
    lnb = even_ln_b[0].reshape(1, A_WIDTH)
    hg = even_head_g[0].reshape(B_HEADS, 1, B_VAL_DIM)
    ew1 = odd_exp_w1[0].reshape(N_EXPERTS * D_MODEL, D_FF_EXPERT)
    ew3 = odd_exp_w3[0].reshape(N_EXPERTS * D_MODEL, D_FF_EXPERT)
    ew2 = odd_exp_w2[0].reshape(N_EXPERTS * D_FF_EXPERT, D_MODEL)
    x1, ew1_b, fw1_b, fw3_b, fw2_b = _even_mixer(
        xt, even_norm_mix[0].reshape(1, d), win_e, gup, gb, ws, bs, lng, lnb, hg, even_w_o[0].astype(BF16),
        [ew1, even_ffn_w1[0], even_ffn_w3[0], even_ffn_w2[0]], seq=seq, tm=tm)
    x2, win_o = _dense_ffn(x1, even_norm_ffn[0].reshape(1, d), fw1_b, fw3_b, fw2_b, odd_w_in[0], O_F, tm=tm)

    wf_o = _pad_cols(odd_w_in[0][:, O_F:], LANES).astype(BF16)
    fb = jnp.pad(odd_forget_bias[0], (0, LANES - C_HEADS)).reshape(1, LANES)
    qg = jnp.tile(odd_q_g[0], C_HEADS).reshape(1, C_WIDTH)
    kg = jnp.tile(odd_k_g[0], C_HEADS).reshape(1, C_WIDTH)
    q_t, k_ext, vt, gate = _odd_inproj(x2, odd_norm_mix[0].reshape(1, d), win_o, wf_o, fb, qg, kg, seq=seq, tm=tm,
                                       tk=tiles.attn_k)
    attn, ew2_b, ew3_b = _fox_attention(q_t, k_ext, vt, gate, [ew2, ew3], batch=batch, seq=seq, tq=tiles.attn_q,
                                        tk=tiles.attn_k)

    router = _pad_cols(odd_router[0], LANES)
    x3, h_tiles, route = _odd_out(x2, attn, odd_w_o[0].astype(BF16), odd_norm_ffn[0].reshape(1, d), router, tm=tm)

    tm_moe, tm_comb = tiles.moe, tiles.combine
    n_tiles = (2 * t) // tm_moe + N_EXPERTS + 1
    idx1 = route[:, 0].astype(jnp.int32)
    idx2 = route[:, 1].astype(jnp.int32)
    pos1, pos2, row_tok, tile_expert, n_used = _routing_tables(idx1, idx2, tm=tm_moe, n_tiles=n_tiles + 1)
    y_tiles = _moe_ffn(tile_expert[:n_tiles], n_used, row_tok.reshape(n_tiles + 1, 1, tm_moe), h_tiles,
                       ew1_b.reshape(N_EXPERTS, D_MODEL, D_FF_EXPERT), ew3_b.reshape(N_EXPERTS, D_MODEL, D_FF_EXPERT),
                       ew2_b.reshape(N_EXPERTS, D_FF_EXPERT, D_MODEL), tm=tm_moe, n_chunks=MOE_FF_CHUNKS)
    pos = jnp.stack([pos1.reshape(t // tm_comb, tm_comb), pos2.reshape(t // tm_comb, tm_comb)], axis=1)
    out = _moe_combine(pos, x3, route, final_norm.reshape(1, d), y_tiles, tm=tm_comb)
    return out.reshape(batch, seq, d)
```

```python
import collections
import functools
import math

import jax
import jax.numpy as jnp
from jax import lax
from jax.experimental import pallas as pl
from jax.experimental.pallas import tpu as pltpu

F32 = jnp.float32
BF16 = jnp.bfloat16

EPS = 1e-6
D_MODEL = 1024
CHUNK = 128
SUB = 32
N_SUB = CHUNK // SUB
A_GROUPS = 4
A_WIDTH = 512
B_HEADS = 4
B_KEY_DIM = 64
B_VAL_DIM = 128
B_QK_WIDTH = 256
B_V_WIDTH = 512
B_GATE_RANK = 16
B_GATE_NORMALIZER = 16.0
C_HEADS = 16
C_HEAD_DIM = 64
C_WIDTH = 1024
D_FF_DENSE = 2816
N_EXPERTS = 8
D_FF_EXPERT = 3584
LANES = 128
MAX_DECAY_EXP = 60.0
LOG2E = math.log2(math.e)
ROW_TILE = 8

E_U, E_V, E_Q, E_K, E_VB, E_OG, E_G, E_END = 0, 512, 1024, 1280, 1536, 2048, 2560, 2688
O_Q, O_K, O_V, O_OG, O_F, O_END = 0, 1024, 2048, 3072, 4096, 4224

V7X_VMEM_BYTES = 64 * 1024 * 1024
VMEM_LIMIT = V7X_VMEM_BYTES - 8 * 1024 * 1024

Tiles = collections.namedtuple("Tiles", ["row", "attn_q", "attn_k", "out", "moe", "combine"])


def _tiles(seq):
    return Tiles(row=min(512, seq), attn_q=min(1024, seq), attn_k=min(256, seq), out=min(1024, seq), moe=512,
                 combine=min(256, seq))


def _rms(x, g):
    ms = jnp.mean(x * x, axis=-1, keepdims=True)
    return x * lax.rsqrt(ms + EPS) * g


def _gelu_tanh(x):
    c = math.sqrt(2.0 / math.pi)
    return x * (0.5 * (1.0 + jnp.tanh(c * (x + 0.044715 * (x * x * x)))))


def _sigmoid(x):
    return 1.0 / (1.0 + jnp.exp(-x))


def _log_sigmoid(x):
    return jnp.minimum(x, 0.0) - jnp.log(1.0 + jnp.exp(-jnp.abs(x)))


def _dot(a, b):
    return jnp.dot(a, b, preferred_element_type=F32)


def _dot_nt(a, b):
    return lax.dot_general(a, b, (((1,), (1,)), ((), ())), preferred_element_type=F32)


def _split3(x):
    hi = x.astype(BF16)
    r1 = x - hi.astype(F32)
    mid = r1.astype(BF16)
    lo = (r1 - mid.astype(F32)).astype(BF16)
    return hi, mid, lo


def _cumsum_rows(tril_b, x):
    hi, mid, lo = _split3(x)
    return _dot(tril_b, hi) + _dot(tril_b, mid) + _dot(tril_b, lo)


def _const_spec(shape):
    nd = len(shape)
    return pl.BlockSpec(shape, lambda *_: (0,) * nd)


def _side_cast_specs(w2d, n_steps, index_map):
    rows = w2d.shape[0] // n_steps
    assert rows * n_steps == w2d.shape[0] and rows % 16 == 0
    spec = pl.BlockSpec((rows, w2d.shape[1]), index_map)
    return spec, spec, jax.ShapeDtypeStruct(w2d.shape, BF16)


def _side_cast_specs_1d(w2d, n_steps):
    span = 1 if (w2d.shape[0] // n_steps) % 16 == 0 and w2d.shape[0] % n_steps == 0 else 2
    return _side_cast_specs(w2d, n_steps // span, lambda i: (i // span, 0))


def _even_mixer_kernel(x_ref, nrm_ref, win_ref, gup_ref, gb_ref, ws_ref, bs_ref, lng_ref, lnb_ref,
                       hg_ref, wo_ref, *rest, tiles_per_batch, n_chunks, n_casts):
    cast_refs, o_ref, cast_out_refs = rest[:n_casts], rest[n_casts], rest[n_casts + 1:2 * n_casts + 1]
    z_ref, mix_ref, st_ref = rest[2 * n_casts + 1:]
    i = pl.program_id(0)
    for src, dst in zip(cast_refs, cast_out_refs):
        dst[...] = src[...].astype(BF16)

    @pl.when(i % tiles_per_batch == 0)
    def _():
        st_ref[...] = jnp.zeros_like(st_ref)

    h = _rms(x_ref[...], nrm_ref[...]).astype(BF16)
    z_ref[...] = _dot(h, win_ref[...])

    row = lax.broadcasted_iota(jnp.int32, (CHUNK, CHUNK), 0)
    col = lax.broadcasted_iota(jnp.int32, (CHUNK, CHUNK), 1)
    tril_b = (col <= row).astype(BF16)
    sub_row = row & (SUB - 1)
    head_lane = lax.broadcasted_iota(jnp.int32, (1, B_QK_WIDTH), 1) // B_KEY_DIM
    bd_mask = (lax.broadcasted_iota(jnp.int32, (B_V_WIDTH, B_QK_WIDTH), 0) // B_VAL_DIM
               == lax.broadcasted_iota(jnp.int32, (B_V_WIDTH, B_QK_WIDTH), 1) // B_KEY_DIM)

    def chunk_body(c, carry):
        rows = pl.ds(pl.multiple_of(c * CHUNK, CHUNK), CHUNK)

        u = _gelu_tanh(z_ref[rows, E_U:E_V])
        v = _gelu_tanh(z_ref[rows, E_V:E_Q])
        mu = jnp.mean(v, axis=-1, keepdims=True)
        vc = v - mu
        var = jnp.mean(vc * vc, axis=-1, keepdims=True)
        vln = (vc * lax.rsqrt(var + EPS) * lng_ref[...] + lnb_ref[...]).astype(BF16)
        for g in range(A_GROUPS):
            sl = slice(g * LANES, (g + 1) * LANES)
            mixed = _dot(ws_ref[g], vln[:, sl]) + bs_ref[g]
            mix_ref[rows, sl] = (u[:, sl] * mixed).astype(BF16)

        q = z_ref[rows, E_Q:E_K] * (B_KEY_DIM ** -0.5)
        k = z_ref[rows, E_K:E_VB]
        vb = z_ref[rows, E_VB:E_OG]
        og = z_ref[rows, E_OG:E_G]
        glr = z_ref[rows, E_G:E_END].astype(BF16)
        logit = _dot(glr, gup_ref[...]) + gb_ref[...]
        log_a = _log_sigmoid(logit) * (1.0 / B_GATE_NORMALIZER)
        g_cum = _cumsum_rows(tril_b, log_a)
        g_last = g_cum[CHUNK - 1:CHUNK, :]
        st = st_ref[...]
        o = _dot_nt((q * jnp.exp(g_cum)).astype(BF16), st.astype(BF16))

        p_rows = [[None] * N_SUB for _ in range(B_HEADS)]
        for s in range(N_SUB):
            gs = g_cum[s * SUB:(s + 1) * SUB, :]
            if s == 0:
                qt = q[0:SUB, :] * jnp.exp(gs)
                kt = k * jnp.exp(jnp.minimum(-g_cum, MAX_DECAY_EXP))
            else:
                ref_g = g_cum[s * SUB - 1:s * SUB, :]
                qt = q[s * SUB:(s + 1) * SUB, :] * jnp.exp(gs - ref_g)
                kt = k * jnp.exp(jnp.minimum(ref_g - g_cum, MAX_DECAY_EXP))
            qs = jnp.concatenate([jnp.where(head_lane == hh, qt, 0.0) for hh in range(B_HEADS)],
                                 axis=0).astype(BF16)
            sc = _dot_nt(qs, kt.astype(BF16))
            sc = jnp.where(col <= (s * SUB + sub_row), sc, 0.0)
            for hh in range(B_HEADS):
                p_rows[hh][s] = sc[hh * SUB:(hh + 1) * SUB, :]

        vb_b = vb.astype(BF16)
        for hh in range(B_HEADS):
            sl = slice(hh * B_VAL_DIM, (hh + 1) * B_VAL_DIM)
            ph = jnp.concatenate(p_rows[hh], axis=0).astype(BF16)
            oh = o[:, sl] + _dot(ph, vb_b[:, sl])
            on = _rms(oh, hg_ref[hh])
            ogh = og[:, sl]
            mix_ref[rows, A_WIDTH + hh * B_VAL_DIM:A_WIDTH + (hh + 1) * B_VAL_DIM] = (
                on * (ogh * _sigmoid(ogh))).astype(BF16)

        k_dec = (k * jnp.exp(g_last - g_cum)).astype(BF16)
        upd = _dot(vb.T.astype(BF16), k_dec)
        st_ref[...] = jnp.exp(g_last) * st + jnp.where(bd_mask, upd, 0.0)
        return carry

    lax.fori_loop(0, n_chunks, chunk_body, 0, unroll=True)
    o_ref[...] = x_ref[...] + _dot(mix_ref[...], wo_ref[...])


def _even_mixer(x, nrm, win, gup, gb, ws, bs, lng, lnb, hg, wo, wcasts, *, seq, tm):
    t = x.shape[0]
    kern = functools.partial(_even_mixer_kernel, tiles_per_batch=seq // tm, n_chunks=tm // CHUNK,
                             n_casts=len(wcasts))
    casts = [_side_cast_specs_1d(w, t // tm) for w in wcasts]
    return pl.pallas_call(
        kern,
        out_shape=[jax.ShapeDtypeStruct((t, D_MODEL), F32)] + [c[2] for c in casts],
        grid=(t // tm,),
        in_specs=[
            pl.BlockSpec((tm, D_MODEL), lambda i: (i, 0)),
            _const_spec(nrm.shape), _resident_spec(win.shape), _const_spec(gup.shape), _const_spec(gb.shape),
            _const_spec(ws.shape), _const_spec(bs.shape), _const_spec(lng.shape), _const_spec(lnb.shape),
            _const_spec(hg.shape), _resident_spec(wo.shape),
        ] + [c[0] for c in casts],
        out_specs=[pl.BlockSpec((tm, D_MODEL), lambda i: (i, 0))] + [c[1] for c in casts],
        scratch_shapes=[
            pltpu.VMEM((tm, E_END), F32),
            pltpu.VMEM((tm, D_MODEL), BF16),
            pltpu.VMEM((B_V_WIDTH, B_QK_WIDTH), F32),
        ],
        compiler_params=pltpu.CompilerParams(dimension_semantics=("arbitrary",), vmem_limit_bytes=VMEM_LIMIT),
        name="even_mixer",
    )(x, nrm, win, gup, gb, ws, bs, lng, lnb, hg, wo, *wcasts)


def _dense_ffn_kernel(x_ref, nrm_ref, w1_ref, w3_ref, w2_ref, wcast_ref, o_ref, wcast_out_ref):
    wcast_out_ref[...] = wcast_ref[:, :wcast_out_ref.shape[1]].astype(BF16)
    x = x_ref[...]
    h = _rms(x, nrm_ref[...]).astype(BF16)
    a = _dot(h, w1_ref[...])
    b = _dot(h, w3_ref[...])
    o_ref[...] = x + _dot((a * _sigmoid(a) * b).astype(BF16), w2_ref[...])


def _resident_spec(shape):
    nd = len(shape)
    return pl.BlockSpec(shape, lambda *_: (0,) * nd, pipeline_mode=pl.Buffered(1))


def _dense_ffn(x, nrm, w1, w3, w2, wcast, wcast_cols, *, tm):
    t = x.shape[0]
    n_steps = t // tm
    row_spec = pl.BlockSpec((tm, D_MODEL), lambda i: (i, 0))
    rows = wcast.shape[0] // n_steps
    assert rows * n_steps == wcast.shape[0] and rows % 16 == 0
    return pl.pallas_call(
        _dense_ffn_kernel,
        out_shape=[jax.ShapeDtypeStruct((t, D_MODEL), F32), jax.ShapeDtypeStruct((wcast.shape[0], wcast_cols), BF16)],
        grid=(n_steps,),
        in_specs=[row_spec, _const_spec(nrm.shape), _resident_spec(w1.shape), _resident_spec(w3.shape),
                  _resident_spec(w2.shape), pl.BlockSpec((rows, wcast.shape[1]), lambda i: (i, 0))],
        out_specs=[row_spec, pl.BlockSpec((rows, wcast_cols), lambda i: (i, 0))],
        compiler_params=pltpu.CompilerParams(dimension_semantics=("arbitrary",), vmem_limit_bytes=VMEM_LIMIT),
        name="dense_ffn",
    )(x, nrm, w1, w3, w2, wcast)


def _head_rms(x, gain):
    lo = lax.broadcasted_iota(jnp.int32, (1, LANES), 1) < C_HEAD_DIM
    outs = []
    for t in range(C_WIDTH // LANES):
        xt = x[:, t * LANES:(t + 1) * LANES]
        sq = xt * xt
        s_lo = jnp.sum(jnp.where(lo, sq, 0.0), axis=-1, keepdims=True)
        s_hi = jnp.sum(jnp.where(lo, 0.0, sq), axis=-1, keepdims=True)
        inv = jnp.where(lo, lax.rsqrt(s_lo * (1.0 / C_HEAD_DIM) + EPS), lax.rsqrt(s_hi * (1.0 / C_HEAD_DIM) + EPS))
        outs.append(xt * inv)
    return jnp.concatenate(outs, axis=-1) * gain


N_PAIRS = C_HEADS // 2
K_EXT = 2 * LANES
BIAS_PARTS = 3


def _bias_placement():
    src = jnp.arange(BIAS_PARTS * LANES)
    part, head = src // LANES, src % LANES
    dst = (head // 2) * LANES + BIAS_PARTS * (head % 2) + part
    hit = (dst[:, None] == jnp.arange(N_PAIRS * LANES)[None, :]) & (head < C_HEADS)[:, None]
    return hit.astype(BF16)


def _odd_inproj_kernel(x_ref, nrm_ref, w_ref, wf_ref, fb_ref, qg_ref, kg_ref, place_ref,
                       q_ref, k_ref, vt_ref, gate_ref, z_ref, c_ref, carry_ref, *, tiles_per_batch, n_chunks, tk):
    i = pl.program_id(0)

    @pl.when(i % tiles_per_batch == 0)
    def _():
        carry_ref[...] = jnp.zeros_like(carry_ref)

    h = _rms(x_ref[...], nrm_ref[...]).astype(BF16)
    z_ref[:, :O_F] = _dot(h, w_ref[...])
    z_ref[:, O_F:] = _dot(h, wf_ref[...])
    qn = _head_rms(z_ref[:, O_Q:O_K], qg_ref[...]) * (C_HEAD_DIM ** -0.5 * LOG2E)
    for p in range(N_PAIRS):
        q_ref[p] = qn[:, p * LANES:(p + 1) * LANES].T.astype(BF16)
    kn = _head_rms(z_ref[:, O_K:O_V], kg_ref[...]).astype(BF16)
    for p in range(N_PAIRS):
        for kb in range(vt_ref.shape[1]):
            blk = z_ref[kb * tk:(kb + 1) * tk, O_V + p * LANES:O_V + (p + 1) * LANES]
            vt_ref[p, kb] = blk.T.astype(BF16)
    gate_ref[...] = _sigmoid(z_ref[:, O_OG:O_F]).astype(BF16)

    row = lax.broadcasted_iota(jnp.int32, (CHUNK, CHUNK), 0)
    col = lax.broadcasted_iota(jnp.int32, (CHUNK, CHUNK), 1)
    tril_b = (col <= row).astype(BF16)
    carry = carry_ref[...]
    for c in range(n_chunks):
        rows = slice(c * CHUNK, (c + 1) * CHUNK)
        log_f = _log_sigmoid(z_ref[rows, O_F:O_END] + fb_ref[...])
        cs = _cumsum_rows(tril_b, log_f) + carry
        c_ref[rows, :] = cs
        carry = cs[CHUNK - 1:CHUNK, :]
    carry_ref[...] = carry

    bias = _dot(jnp.concatenate(_split3(c_ref[...] * LOG2E), axis=1), place_ref[...]).astype(BF16)
    for p in range(N_PAIRS):
        k_ref[:, p * K_EXT:p * K_EXT + LANES] = kn[:, p * LANES:(p + 1) * LANES]
        k_ref[:, p * K_EXT + LANES:(p + 1) * K_EXT] = bias[:, p * LANES:(p + 1) * LANES]


def _odd_inproj(x, nrm, w, wf, fb, qg, kg, *, seq, tm, tk):
    t = x.shape[0]
    tiles_per_batch = seq // tm
    kern = functools.partial(_odd_inproj_kernel, tiles_per_batch=tiles_per_batch, n_chunks=tm // CHUNK, tk=tk)
    row_spec = pl.BlockSpec((tm, C_WIDTH), lambda i: (i, 0))
    kext_spec = pl.BlockSpec((tm, N_PAIRS * K_EXT), lambda i: (i, 0))
    vt_spec = pl.BlockSpec((None, N_PAIRS, tm // tk, LANES, tk),
                           lambda i: (i // tiles_per_batch, 0, i % tiles_per_batch, 0, 0))
    place = _bias_placement()
    wide = jax.ShapeDtypeStruct((t, C_WIDTH), BF16)
    return pl.pallas_call(
        kern,
        out_shape=[jax.ShapeDtypeStruct((t // seq, N_PAIRS, LANES, seq), BF16),
                   jax.ShapeDtypeStruct((t, N_PAIRS * K_EXT), BF16),
                   jax.ShapeDtypeStruct((t // seq, N_PAIRS, seq // tk, LANES, tk), BF16), wide],
        grid=(t // tm,),
        in_specs=[row_spec, _const_spec(nrm.shape), _resident_spec(w.shape), _const_spec(wf.shape),
                  _const_spec(fb.shape), _const_spec(qg.shape), _const_spec(kg.shape), _const_spec(place.shape)],
        out_specs=[pl.BlockSpec((None, N_PAIRS, LANES, tm), lambda i: (i // tiles_per_batch, 0, 0, i % tiles_per_batch)),
                   kext_spec, vt_spec, row_spec],
        scratch_shapes=[pltpu.VMEM((tm, O_END), F32), pltpu.VMEM((tm, LANES), F32), pltpu.VMEM((1, LANES), F32)],
        compiler_params=pltpu.CompilerParams(dimension_semantics=("arbitrary",), vmem_limit_bytes=VMEM_LIMIT),
        name="odd_inproj",
    )(x, nrm, w, wf, fb, qg, kg, place)


NEG_BIG = -1e30


V_ROWS = 80


def _fox_kernel(q_ref, k_ref, vt_ref, gate_ref, *rest, tq, tk, n_casts):
    cast_refs, o_ref, cast_out_refs, s_ref = rest[:n_casts], rest[n_casts], rest[n_casts + 1:-1], rest[-1]
    for src, dst in zip(cast_refs, cast_out_refs):
        dst[...] = src[...].astype(BF16)
    qi = pl.program_id(2)
    ng = tq // tk
    key_i = lax.broadcasted_iota(jnp.int32, (tk, tk), 0)
    qry_i = lax.broadcasted_iota(jnp.int32, (tk, tk), 1)
    causal = key_i <= qry_i
    chains = [(hh, r) for r in range(ng) for hh in range(2)]
    feat = lax.broadcasted_iota(jnp.int32, (LANES, 1), 0)
    qms = []
    for hh, r in chains:
        q_t = q_ref[:, r * tk:(r + 1) * tk]
        own = (feat < C_HEAD_DIM) if hh == 0 else (feat >= C_HEAD_DIM)
        qh = jnp.where(own, q_t, jnp.zeros_like(q_t))
        pick = (feat >= BIAS_PARTS * hh) & (feat < BIAS_PARTS * (hh + 1))
        minus_one = jnp.broadcast_to(jnp.where(pick, -1.0, 0.0).astype(BF16), qh.shape)
        qms.append(jnp.concatenate([qh, minus_one], axis=0))

    def key_rows(j):
        return pl.ds(pl.multiple_of(j * tk, tk), tk)

    def scores_to_scratch(j, slot, live, modes):
        kb = k_ref[key_rows(j), :]
        raw = [_dot(kb, qms[idx]) for idx in live]
        maxes = []
        for s, idx in zip(raw, live):
            hh, r = chains[idx]
            if modes[r] == "diag":
                s = jnp.where(causal, s, NEG_BIG)
            s_ref[slot, idx] = s
            maxes.append(jnp.max(s, axis=0, keepdims=True))
        return maxes

    ones_rows = (lax.broadcasted_iota(jnp.int32, (V_ROWS - C_HEAD_DIM, tk), 0) == 0).astype(BF16)

    def softmax_pv(j, slot, maxes, live, state):
        vt_pair = vt_ref[j]
        vtb = [jnp.concatenate([vt_pair[hh * C_HEAD_DIM:(hh + 1) * C_HEAD_DIM, :], ones_rows], axis=0)
               for hh in range(2)]
        new = list(state)
        probs = []
        for bm, idx in zip(maxes, live):
            m = state[2 * idx]
            m_new = jnp.maximum(m, bm)
            p = jnp.exp2(s_ref[slot, idx] - m_new)
            new[2 * idx] = m_new
            probs.append((jnp.exp2(m - m_new), p.astype(BF16)))
        for (alpha, p), idx in zip(probs, live):
            hh, r = chains[idx]
            new[2 * idx + 1] = alpha * state[2 * idx + 1] + _dot(vtb[hh], p)
        return new

    assert ng % 2 == 0
    all_chains = list(range(len(chains)))
    n_state = 2 * len(chains)
    state = []
    for _ in chains:
        state += [jnp.full((1, tk), NEG_BIG, F32), jnp.zeros((V_ROWS, tk), F32)]
    n_full = qi * ng
    full_modes = ("full",) * ng

    def diag_modes(g):
        return tuple("skip" if r < g else ("diag" if r == g else "full") for r in range(ng))

    def live_chains(g):
        return [idx for idx, (hh, r) in enumerate(chains) if r >= g]

    def trip(j, carry, next_modes):
        st, mx = list(carry[:n_state]), carry[n_state:]
        for b in range(ng):
            mx_next = scores_to_scratch(j + b + 1, (b + 1) % 2, all_chains, full_modes if b + 1 < ng else next_modes)
            st = softmax_pv(j + b, b % 2, mx, all_chains, st)
            mx = mx_next
        return tuple(st) + tuple(mx)

    def with_full_blocks(_):
        first = scores_to_scratch(0, 0, all_chains, full_modes)
        carry = lax.fori_loop(0, qi - 1, lambda i, c: trip(ng * i, c, full_modes), tuple(state) + tuple(first))
        return trip(n_full - ng, carry, diag_modes(0))

    def no_full_blocks(_):
        return tuple(state) + tuple(scores_to_scratch(0, 0, all_chains, diag_modes(0)))

    carry = lax.cond(qi > 0, with_full_blocks, no_full_blocks, 0)
    state, mx = list(carry[:n_state]), carry[n_state:]
    for g in range(ng):
        if g + 1 < ng:
            mx_next = scores_to_scratch(n_full + g + 1, (g + 1) % 2, live_chains(g + 1), diag_modes(g + 1))
        state = softmax_pv(n_full + g, g % 2, mx, live_chains(g), state)
        if g + 1 < ng:
            mx = mx_next

    for r in range(ng):
        parts = []
        for hh in range(2):
            acc = state[2 * chains.index((hh, r)) + 1]
            parts.append(acc[:C_HEAD_DIM, :] / acc[C_HEAD_DIM:C_HEAD_DIM + 1, :])
        o = jnp.concatenate(parts, axis=0).T
        rows = slice(r * tk, (r + 1) * tk)
        o_ref[rows, :] = (o * gate_ref[rows, :].astype(F32)).astype(BF16)


def _fox_attention(q, k_ext, vt, gate, wcasts, *, batch, seq, tq, tk):
    t = batch * seq
    nq = seq // tq
    kern = functools.partial(_fox_kernel, tq=tq, tk=tk, n_casts=len(wcasts))
    casts = [_side_cast_specs(w, batch * N_PAIRS * nq, lambda b, p, i: ((b * N_PAIRS + p) * nq + i, 0))
             for w in wcasts]
    return pl.pallas_call(
        kern,
        out_shape=[jax.ShapeDtypeStruct((t, C_WIDTH), BF16)] + [c[2] for c in casts],
        grid=(batch, N_PAIRS, nq),
        in_specs=[
            pl.BlockSpec((None, None, LANES, tq), lambda b, p, i: (b, p, 0, i)),
            pl.BlockSpec((seq, K_EXT), lambda b, p, i: (b, p)),
            pl.BlockSpec((None, None, seq // tk, LANES, tk), lambda b, p, i: (b, p, 0, 0, 0)),
            pl.BlockSpec((tq, LANES), lambda b, p, i: (b * nq + i, p)),
        ] + [c[0] for c in casts],
        out_specs=[pl.BlockSpec((tq, LANES), lambda b, p, i: (b * nq + i, p))] + [c[1] for c in casts],
        scratch_shapes=[pltpu.VMEM((2, 2 * (tq // tk), tk, tk), F32)],
        compiler_params=pltpu.CompilerParams(dimension_semantics=("arbitrary", "arbitrary", "arbitrary"),
                                             vmem_limit_bytes=VMEM_LIMIT),
        name="fox_attn",
    )(q, k_ext, vt, gate, *wcasts)


def _store_token_tiles(dst_ref, val, n_rows):
    for s in range(ROW_TILE):
        dst_ref[pl.ds(s, n_rows, stride=ROW_TILE), :] = val[:, s * LANES:(s + 1) * LANES]


def _load_token_tiles(src_ref, n_rows):
    return jnp.concatenate([src_ref[pl.ds(s, n_rows, stride=ROW_TILE), :] for s in range(ROW_TILE)], axis=1)


def _odd_out_kernel(x_ref, a_ref, wo_ref, nrm_ref, r_ref, x3_ref, h_ref, route_ref):
    x3 = x_ref[...] + _dot(a_ref[...], wo_ref[...])
    x3_ref[...] = x3
    h = _rms(x3, nrm_ref[...])
    _store_token_tiles(h_ref, h, h.shape[0])

    h_hi = h.astype(BF16)
    h_lo = (h - h_hi.astype(F32)).astype(BF16)
    r = r_ref[...]
    r_hi = r.astype(BF16)
    r_lo = (r - r_hi.astype(F32)).astype(BF16)
    n = h.shape[0]
    prod = _dot(jnp.concatenate([h_hi, h_lo], axis=0), jnp.concatenate([r_hi, r_lo], axis=1))
    logits = prod[:n, :LANES] + (prod[n:, :LANES] + prod[:n, LANES:])

    lane = lax.broadcasted_iota(jnp.int32, logits.shape, 1).astype(F32)
    neg_inf = jnp.float32(-jnp.inf)
    lg = jnp.where(lane < N_EXPERTS, logits, neg_inf)
    m1 = jnp.max(lg, axis=-1, keepdims=True)
    i1 = jnp.min(jnp.where(lg == m1, lane, float(LANES)), axis=-1, keepdims=True)
    lg2 = jnp.where(lane == i1, neg_inf, lg)
    m2 = jnp.max(lg2, axis=-1, keepdims=True)
    i2 = jnp.min(jnp.where(lg2 == m2, lane, float(LANES)), axis=-1, keepdims=True)
    e2 = jnp.exp(m2 - m1)
    g1 = 1.0 / (1.0 + e2)
    g2 = e2 / (1.0 + e2)
    route_ref[...] = jnp.where(lane == 0, i1, jnp.where(lane == 1, i2, jnp.where(lane == 2, g1,
                               jnp.where(lane == 3, g2, 0.0))))


def _odd_out(x, a, wo, nrm, router, *, tm):
    t = x.shape[0]
    row_spec = pl.BlockSpec((tm, D_MODEL), lambda i: (i, 0))
    return pl.pallas_call(
        _odd_out_kernel,
        out_shape=[jax.ShapeDtypeStruct((t, D_MODEL), F32), jax.ShapeDtypeStruct((t * ROW_TILE, LANES), F32),
                   jax.ShapeDtypeStruct((t, LANES), F32)],
        grid=(t // tm,),
        in_specs=[row_spec, row_spec, _const_spec(wo.shape), _const_spec(nrm.shape), _const_spec(router.shape)],
        out_specs=[row_spec, pl.BlockSpec((tm * ROW_TILE, LANES), lambda i: (i, 0)),
                   pl.BlockSpec((tm, LANES), lambda i: (i, 0))],
        compiler_params=pltpu.CompilerParams(dimension_semantics=("arbitrary",), vmem_limit_bytes=VMEM_LIMIT),
        name="odd_out",
    )(x, a, wo, nrm, router)


def _row_gather_copy(src_hbm, src_row, dst_ref, dst_row, sem):
    return pltpu.make_async_copy(
        src_hbm.at[pl.ds(pl.multiple_of(src_row * ROW_TILE, ROW_TILE), ROW_TILE), :],
        dst_ref.at[pl.ds(pl.multiple_of(dst_row * ROW_TILE, ROW_TILE), ROW_TILE), :],
        sem)


GATHER_UNROLL = 8
MXU_TILE = 256
MOE_FF_CHUNKS = 2
assert (D_FF_EXPERT // MOE_FF_CHUNKS) % MXU_TILE == 0


def _start_row_gathers(src_hbm, idx_ref, idx_row, first, count, dst_ref, sem, *, inline):
    if inline:
        for u in range(count):
            _row_gather_copy(src_hbm, idx_ref[idx_row, first + u], dst_ref, first + u, sem).start(priority=u % 2)
        return

    def issue(r2, c):
        for u in range(2):
            r = 2 * r2 + u
            _row_gather_copy(src_hbm, idx_ref[idx_row, r], dst_ref, r, sem).start(priority=u)
        return c

    assert first % 2 == 0 and count % 2 == 0
    lax.fori_loop(first // 2, (first + count) // 2, issue, 0, unroll=GATHER_UNROLL // 2)


def _wait_row_gathers(src_hbm, dst_ref, sem):
    pltpu.make_async_copy(src_hbm.at[pl.ds(0, dst_ref.shape[0]), :], dst_ref, sem).wait()


def _moe_ffn_kernel(te_ref, nu_ref, tok_ref, tok_next_ref, h_hbm, w1_ref, w3_ref, w2_ref, o_ref, xs_ref, buf_ref,
                    sem, *, tm, n_chunks):
    i = pl.program_id(0)
    n_used = nu_ref[0]
    used = i < n_used
    slot = i % 2

    @pl.when(i == 0)
    def _():
        _start_row_gathers(h_hbm, tok_ref, 0, 0, tm, buf_ref.at[0], sem.at[0], inline=False)

    @pl.when(i <= n_used)
    def _():
        _wait_row_gathers(h_hbm, buf_ref.at[slot], sem.at[slot])
        xs_ref[...] = _load_token_tiles(buf_ref.at[slot], tm).astype(BF16)

    @pl.when(used)
    def _():
        n_groups = 3 * n_chunks
        per_group = tm // n_groups
        starts = [(g * per_group, per_group if g + 1 < n_groups else tm - g * per_group) for g in range(n_groups)]

        def prefetch(g):
            first, count = starts[g]
            _start_row_gathers(h_hbm, tok_next_ref, 0, first, count, buf_ref.at[1 - slot], sem.at[1 - slot],
                               inline=True)

        x = xs_ref[...]
        tf = D_FF_EXPERT // n_chunks
        y = None
        for c in range(n_chunks):
            cols = slice(c * tf, (c + 1) * tf)
            prefetch(3 * c)
            a = _dot(x, w1_ref[:, cols])
            prefetch(3 * c + 1)
            b = _dot(x, w3_ref[:, cols])
            prefetch(3 * c + 2)
            part = _dot((a * _sigmoid(a) * b).astype(BF16), w2_ref[cols, :])
            y = part if y is None else y + part
        _store_token_tiles(o_ref, y, tm)

    @pl.when(jnp.logical_not(used))
    def _():
        o_ref[...] = jnp.zeros_like(o_ref)


def _moe_ffn(tile_expert, n_used, row_tok, h_tiles, w1, w3, w2, *, tm, n_chunks):
    n_tiles = row_tok.shape[0] - 1

    def expert_spec(shape):
        return pl.BlockSpec((None,) + shape, lambda i, te, nu: (te[i], 0, 0), pipeline_mode=pl.Buffered(1))

    grid_spec = pltpu.PrefetchScalarGridSpec(
        num_scalar_prefetch=2,
        grid=(n_tiles,),
        in_specs=[
            pl.BlockSpec((None, 1, tm), lambda i, te, nu: (i, 0, 0), memory_space=pltpu.SMEM),
            pl.BlockSpec((None, 1, tm), lambda i, te, nu: (i + 1, 0, 0), memory_space=pltpu.SMEM),
            pl.BlockSpec(memory_space=pl.ANY),
            expert_spec((D_MODEL, D_FF_EXPERT)), expert_spec((D_MODEL, D_FF_EXPERT)),
            expert_spec((D_FF_EXPERT, D_MODEL)),
        ],
        out_specs=pl.BlockSpec((tm * ROW_TILE, LANES), lambda i, te, nu: (i, 0)),
        scratch_shapes=[pltpu.VMEM((tm, D_MODEL), BF16), pltpu.VMEM((2, tm * ROW_TILE, LANES), F32),
                        pltpu.SemaphoreType.DMA((2,))],
    )
    return pl.pallas_call(
        functools.partial(_moe_ffn_kernel, tm=tm, n_chunks=n_chunks),
        out_shape=jax.ShapeDtypeStruct((n_tiles * tm * ROW_TILE, LANES), F32),
        grid_spec=grid_spec,
        compiler_params=pltpu.CompilerParams(dimension_semantics=("arbitrary",), vmem_limit_bytes=VMEM_LIMIT),
        name="moe_ffn",
    )(tile_expert, n_used, row_tok, row_tok, h_tiles, w1, w3, w2)


def _moe_combine_kernel(pos_ref, pos_next_ref, x_ref, route_ref, nrm_ref, y_hbm, o_ref, buf_ref, sem, *, tm):
    i = pl.program_id(0)
    slot = i % 2

    def start_tile(idx_ref, s, inline):
        for k in range(2):
            _start_row_gathers(y_hbm, idx_ref, k, 0, tm, buf_ref.at[s, k], sem.at[s, k], inline=inline)

    @pl.when(i == 0)
    def _():
        start_tile(pos_ref, 0, False)

    @pl.when(i + 1 < pl.num_programs(0))
    def _():
        start_tile(pos_next_ref, 1 - slot, True)

    for k in range(2):
        _wait_row_gathers(y_hbm, buf_ref.at[slot, k], sem.at[slot, k])
    g1 = route_ref[:, 2:3]
    g2 = route_ref[:, 3:4]
    x = x_ref[...] + (g1 * _load_token_tiles(buf_ref.at[slot, 0], tm) + g2 * _load_token_tiles(buf_ref.at[slot, 1], tm))
    o_ref[...] = _rms(x, nrm_ref[...])


def _moe_combine(pos, x, route, nrm, y_tiles, *, tm):
    t = x.shape[0]
    n = t // tm
    row_spec = pl.BlockSpec((tm, D_MODEL), lambda i: (i, 0))
    return pl.pallas_call(
        functools.partial(_moe_combine_kernel, tm=tm),
        out_shape=jax.ShapeDtypeStruct((t, D_MODEL), F32),
        grid=(n,),
        in_specs=[pl.BlockSpec((None, 2, tm), lambda i: (i, 0, 0), memory_space=pltpu.SMEM),
                  pl.BlockSpec((None, 2, tm), lambda i: (jnp.minimum(i + 1, n - 1), 0, 0), memory_space=pltpu.SMEM),
                  row_spec, pl.BlockSpec((tm, LANES), lambda i: (i, 0)), _const_spec(nrm.shape),
                  pl.BlockSpec(memory_space=pl.ANY)],
        out_specs=row_spec,
        scratch_shapes=[pltpu.VMEM((2, 2, tm * ROW_TILE, LANES), F32), pltpu.SemaphoreType.DMA((2, 2))],
        compiler_params=pltpu.CompilerParams(dimension_semantics=("arbitrary",), vmem_limit_bytes=VMEM_LIMIT),
        name="moe_combine",
    )(pos, pos, x, route, nrm, y_tiles)


SCATTER_CHUNK = 2048


def _invert_rows_kernel(pos_ref, zeros_hbm, out_ref, sem, *, n_tokens, chunk):
    j = pl.program_id(0)

    @pl.when(j == 0)
    def _():
        fill = pltpu.make_async_copy(zeros_hbm, out_ref, sem.at[0])
        fill.start()
        fill.wait()

    base = lax.rem(j * chunk, n_tokens)

    def place(u, c):
        out_ref[pos_ref[0, u]] = base + u
        return c

    lax.fori_loop(0, chunk, place, 0, unroll=32)


def _invert_rows(pos, n_rows, n_tokens):
    chunk = min(SCATTER_CHUNK, n_tokens)
    n_chunks = pos.shape[0] // chunk
    assert n_chunks * chunk == pos.shape[0] and n_tokens % chunk == 0
    return pl.pallas_call(
        functools.partial(_invert_rows_kernel, n_tokens=n_tokens, chunk=chunk),
        out_shape=jax.ShapeDtypeStruct((n_rows,), jnp.int32),
        grid=(n_chunks,),
        in_specs=[pl.BlockSpec((None, 1, chunk), lambda j: (j, 0, 0), memory_space=pltpu.SMEM),
                  pl.BlockSpec(memory_space=pl.ANY)],
        out_specs=pl.BlockSpec(memory_space=pltpu.SMEM),
        scratch_shapes=[pltpu.SemaphoreType.DMA((1,))],
        compiler_params=pltpu.CompilerParams(dimension_semantics=("arbitrary",)),
        name="invert_rows",
    )(pos.reshape(n_chunks, 1, chunk), jnp.zeros((n_rows,), jnp.int32))


def _routing_tables(idx1, idx2, *, tm, n_tiles):
    t = idx1.shape[0]
    e_flat = jnp.concatenate([idx1, idx2])
    onehot = (e_flat[:, None] == jnp.arange(N_EXPERTS, dtype=jnp.int32)[None, :]).astype(jnp.int32)
    csum = jnp.cumsum(onehot, axis=0)
    rank = jnp.sum((csum - onehot) * onehot, axis=1)
    counts = csum[-1]
    tiles_e = (counts + tm - 1) // tm
    tile_end = jnp.cumsum(tiles_e)
    tile_start = tile_end - tiles_e
    pos = jnp.sum(onehot * tile_start[None, :], axis=1) * tm + rank
    n_used = tile_end[-1]
    tile_ids = jnp.arange(n_tiles, dtype=jnp.int32)
    te = jnp.sum((tile_ids[:, None] >= tile_end[None, :]).astype(jnp.int32), axis=1)
    te_last = jnp.sum((n_used - 1 >= tile_end).astype(jnp.int32))
    tile_expert = jnp.where(tile_ids < n_used, te, te_last).astype(jnp.int32)
    row_tok = _invert_rows(pos.astype(jnp.int32), n_tiles * tm, t)
    return pos[:t], pos[t:], row_tok, tile_expert, n_used.reshape(1).astype(jnp.int32)


def _pad_cols(w, n):
    return jnp.pad(w, ((0, 0), (0, n - w.shape[1])))


def kernel(x, even_norm_mix, even_w_in, even_gate_up, even_gate_bias, even_w_s, even_b_s, even_ln_g, even_ln_b,
           even_head_g, even_w_o, even_norm_ffn, even_ffn_w1, even_ffn_w3, even_ffn_w2, odd_norm_mix, odd_w_in,
           odd_forget_bias, odd_q_g, odd_k_g, odd_w_o, odd_norm_ffn, odd_router, odd_exp_w1, odd_exp_w3,
           odd_exp_w2, final_norm):
    batch, seq, d = x.shape
    assert d == D_MODEL and seq % CHUNK == 0
    t = batch * seq
    xt = x.reshape(t, d)
    tiles = _tiles(seq)
    tm = tiles.row

    w_in = even_w_in[0]
    even_cuts = [A_WIDTH, 2 * A_WIDTH, 2 * A_WIDTH + B_QK_WIDTH, 2 * A_WIDTH + 2 * B_QK_WIDTH,
                 2 * A_WIDTH + 2 * B_QK_WIDTH + B_GATE_RANK, 2 * A_WIDTH + 2 * B_QK_WIDTH + B_GATE_RANK + B_V_WIDTH]
    u_w, v_w, q_w, k_w, g_w, vb_w, og_w = jnp.split(w_in, even_cuts, axis=1)
    win_e = jnp.concatenate([u_w, v_w, q_w, k_w, vb_w, og_w, _pad_cols(g_w, LANES)], axis=1).astype(BF16)
    gup = jnp.pad(even_gate_up[0], ((0, LANES - B_GATE_RANK), (0, 0))).astype(BF16)
    gb = even_gate_bias[0].reshape(1, B_QK_WIDTH)
    tril = jnp.tril(jnp.ones((CHUNK, CHUNK), dtype=bool))
    ws = jnp.where(tril[None], even_w_s[0], 0.0).astype(BF16)
    bs = jnp.broadcast_to(even_b_s[0][:, :, None], (A_GROUPS, CHUNK, LANES))
    lng = even_ln_g[0].reshape(1, A_WIDTH)
    lnb = even_ln_b[0].reshape(1, A_WIDTH)
    hg = even_head_g[0].reshape(B_HEADS, 1, B_VAL_DIM)
    ew1 = odd_exp_w1[0].reshape(N_EXPERTS * D_MODEL, D_FF_EXPERT)
    ew3 = odd_exp_w3[0].reshape(N_EXPERTS * D_MODEL, D_FF_EXPERT)
    ew2 = odd_exp_w2[0].reshape(N_EXPERTS * D_FF_EXPERT, D_MODEL)
    x1, ew1_b, fw1_b, fw3_b, fw2_b = _even_mixer(
        xt, even_norm_mix[0].reshape(1, d), win_e, gup, gb, ws, bs, lng, lnb, hg, even_w_o[0].astype(BF16),
        [ew1, even_ffn_w1[0], even_ffn_w3[0], even_ffn_w2[0]], seq=seq, tm=tm)
    x2, win_o = _dense_ffn(x1, even_norm_ffn[0].reshape(1, d), fw1_b, fw3_b, fw2_b, odd_w_in[0], O_F, tm=tm)

    wf_o = _pad_cols(odd_w_in[0][:, O_F:], LANES).astype(BF16)
    fb = jnp.pad(odd_forget_bias[0], (0, LANES - C_HEADS)).reshape(1, LANES)
    qg = jnp.tile(odd_q_g[0], C_HEADS).reshape(1, C_WIDTH)
    kg = jnp.tile(odd_k_g[0], C_HEADS).reshape(1, C_WIDTH)
    q_t, k_ext, vt, gate = _odd_inproj(x2, odd_norm_mix[0].reshape(1, d), win_o, wf_o, fb, qg, kg, seq=seq, tm=tm,
                                       tk=tiles.attn_k)
    attn, ew2_b, ew3_b = _fox_attention(q_t, k_ext, vt, gate, [ew2, ew3], batch=batch, seq=seq, tq=tiles.attn_q,
                                        tk=tiles.attn_k)

    router = _pad_cols(odd_router[0], LANES)
    x3, h_tiles, route = _odd_out(x2, attn, odd_w_o[0].astype(BF16), odd_norm_ffn[0].reshape(1, d), router,
                                  tm=tiles.out)

    tm_moe, tm_comb = tiles.moe, tiles.combine
    n_tiles = (2 * t) // tm_moe + N_EXPERTS + 1
    idx1 = route[:, 0].astype(jnp.int32)
    idx2 = route[:, 1].astype(jnp.int32)
    pos1, pos2, row_tok, tile_expert, n_used = _routing_tables(idx1, idx2, tm=tm_moe, n_tiles=n_tiles + 1)
    y_tiles = _moe_ffn(tile_expert[:n_tiles], n_used, row_tok.reshape(n_tiles + 1, 1, tm_moe), h_tiles,
                       ew1_b.reshape(N_EXPERTS, D_MODEL, D_FF_EXPERT), ew3_b.reshape(N_EXPERTS, D_MODEL, D_FF_EXPERT),
                       ew2_b.reshape(N_EXPERTS, D_FF_EXPERT, D_MODEL), tm=tm_moe, n_chunks=MOE_FF_CHUNKS)
    pos = jnp.stack([pos1.reshape(t // tm_comb, tm_comb), pos2.reshape(t // tm_comb, tm_comb)], axis=1)
    out = _moe_combine(pos, x3, route, final_norm.reshape(1, d), y_tiles, tm=tm_comb)
    return out.reshape(batch, seq, d)
```

```python
import collections
import functools
import math

import jax
import jax.numpy as jnp
from jax import lax
from jax.experimental import pallas as pl
from jax.experimental.pallas import tpu as pltpu

F32 = jnp.float32
BF16 = jnp.bfloat16

EPS = 1e-6
D_MODEL = 1024
CHUNK = 128
SUB = 32
N_SUB = CHUNK // SUB
A_GROUPS = 4
A_WIDTH = 512
B_HEADS = 4
B_KEY_DIM = 64
B_VAL_DIM = 128
B_QK_WIDTH = 256
B_V_WIDTH = 512
B_GATE_RANK = 16
B_GATE_NORMALIZER = 16.0
C_HEADS = 16
C_HEAD_DIM = 64
C_WIDTH = 1024
D_FF_DENSE = 2816
N_EXPERTS = 8
D_FF_EXPERT = 3584
LANES = 128
MAX_DECAY_EXP = 60.0
LOG2E = math.log2(math.e)
ROW_TILE = 8

E_U, E_V, E_Q, E_K, E_VB, E_OG, E_G, E_END = 0, 512, 1024, 1280, 1536, 2048, 2560, 2688
O_Q, O_K, O_V, O_OG, O_F, O_END = 0, 1024, 2048, 3072, 4096, 4224

V7X_VMEM_BYTES = 64 * 1024 * 1024
VMEM_LIMIT = V7X_VMEM_BYTES - 8 * 1024 * 1024

Tiles = collections.namedtuple("Tiles", ["row", "attn_q", "attn_k", "out", "moe", "combine"])


def _tiles(seq):
    return Tiles(row=min(512, seq), attn_q=min(1024, seq), attn_k=min(256, seq), out=min(1024, seq), moe=512,
                 combine=min(256, seq))


def _rms(x, g):
    ms = jnp.mean(x * x, axis=-1, keepdims=True)
    return x * lax.rsqrt(ms + EPS) * g


def _gelu_tanh(x):
    c = math.sqrt(2.0 / math.pi)
    return x * (0.5 * (1.0 + jnp.tanh(c * (x + 0.044715 * (x * x * x)))))


def _sigmoid(x):
    return 1.0 / (1.0 + jnp.exp(-x))


def _log_sigmoid(x):
    return jnp.minimum(x, 0.0) - jnp.log(1.0 + jnp.exp(-jnp.abs(x)))


def _dot(a, b):
    return jnp.dot(a, b, preferred_element_type=F32)


def _dot_nt(a, b):
    return lax.dot_general(a, b, (((1,), (1,)), ((), ())), preferred_element_type=F32)


def _split3(x):
    hi = x.astype(BF16)
    r1 = x - hi.astype(F32)
    mid = r1.astype(BF16)
    lo = (r1 - mid.astype(F32)).astype(BF16)
    return hi, mid, lo


def _cumsum_rows(tril_b, x):
    hi, mid, lo = _split3(x)
    return _dot(tril_b, hi) + _dot(tril_b, mid) + _dot(tril_b, lo)


def _const_spec(shape):
    nd = len(shape)
    return pl.BlockSpec(shape, lambda *_: (0,) * nd)


def _side_cast_specs(w2d, n_steps, index_map):
    rows = w2d.shape[0] // n_steps
    assert rows * n_steps == w2d.shape[0] and rows % 16 == 0
    spec = pl.BlockSpec((rows, w2d.shape[1]), index_map)
    return spec, spec, jax.ShapeDtypeStruct(w2d.shape, BF16)


def _side_cast_specs_1d(w2d, n_steps):
    span = 1 if (w2d.shape[0] // n_steps) % 16 == 0 and w2d.shape[0] % n_steps == 0 else 2
    return _side_cast_specs(w2d, n_steps // span, lambda i: (i // span, 0))


def _even_mixer_kernel(x_ref, nrm_ref, win_ref, gup_ref, gb_ref, ws_ref, bs_ref, lng_ref, lnb_ref,
                       hg_ref, wo_ref, *rest, tiles_per_batch, n_chunks, n_casts):
    cast_refs, o_ref, cast_out_refs = rest[:n_casts], rest[n_casts], rest[n_casts + 1:2 * n_casts + 1]
    z_ref, mix_ref, st_ref = rest[2 * n_casts + 1:]
    i = pl.program_id(0)
    for src, dst in zip(cast_refs, cast_out_refs):
        dst[...] = src[...].astype(BF16)

    @pl.when(i % tiles_per_batch == 0)
    def _():
        st_ref[...] = jnp.zeros_like(st_ref)

    h = _rms(x_ref[...], nrm_ref[...]).astype(BF16)
    z_ref[...] = _dot(h, win_ref[...])

    row = lax.broadcasted_iota(jnp.int32, (CHUNK, CHUNK), 0)
    col = lax.broadcasted_iota(jnp.int32, (CHUNK, CHUNK), 1)
    tril_b = (col <= row).astype(BF16)
    sub_row = row & (SUB - 1)
    head_lane = lax.broadcasted_iota(jnp.int32, (1, B_QK_WIDTH), 1) // B_KEY_DIM
    bd_mask = (lax.broadcasted_iota(jnp.int32, (B_V_WIDTH, B_QK_WIDTH), 0) // B_VAL_DIM
               == lax.broadcasted_iota(jnp.int32, (B_V_WIDTH, B_QK_WIDTH), 1) // B_KEY_DIM)

    def chunk_body(c, carry):
        rows = pl.ds(pl.multiple_of(c * CHUNK, CHUNK), CHUNK)

        u = _gelu_tanh(z_ref[rows, E_U:E_V])
        v = _gelu_tanh(z_ref[rows, E_V:E_Q])
        mu = jnp.mean(v, axis=-1, keepdims=True)
        vc = v - mu
        var = jnp.mean(vc * vc, axis=-1, keepdims=True)
        vln = (vc * lax.rsqrt(var + EPS) * lng_ref[...] + lnb_ref[...]).astype(BF16)
        for g in range(A_GROUPS):
            sl = slice(g * LANES, (g + 1) * LANES)
            mixed = _dot(ws_ref[g], vln[:, sl]) + bs_ref[g]
            mix_ref[rows, sl] = (u[:, sl] * mixed).astype(BF16)

        q = z_ref[rows, E_Q:E_K] * (B_KEY_DIM ** -0.5)
        k = z_ref[rows, E_K:E_VB]
        vb = z_ref[rows, E_VB:E_OG]
        og = z_ref[rows, E_OG:E_G]
        glr = z_ref[rows, E_G:E_END].astype(BF16)
        logit = _dot(glr, gup_ref[...]) + gb_ref[...]
        log_a = _log_sigmoid(logit) * (1.0 / B_GATE_NORMALIZER)
        g_cum = _cumsum_rows(tril_b, log_a)
        g_last = g_cum[CHUNK - 1:CHUNK, :]
        st = st_ref[...]
        o = _dot_nt((q * jnp.exp(g_cum)).astype(BF16), st.astype(BF16))

        p_rows = [[None] * N_SUB for _ in range(B_HEADS)]
        for s in range(N_SUB):
            gs = g_cum[s * SUB:(s + 1) * SUB, :]
            if s == 0:
                qt = q[0:SUB, :] * jnp.exp(gs)
                kt = k * jnp.exp(jnp.minimum(-g_cum, MAX_DECAY_EXP))
            else:
                ref_g = g_cum[s * SUB - 1:s * SUB, :]
                qt = q[s * SUB:(s + 1) * SUB, :] * jnp.exp(gs - ref_g)
                kt = k * jnp.exp(jnp.minimum(ref_g - g_cum, MAX_DECAY_EXP))
            qs = jnp.concatenate([jnp.where(head_lane == hh, qt, 0.0) for hh in range(B_HEADS)],
                                 axis=0).astype(BF16)
            sc = _dot_nt(qs, kt.astype(BF16))
            sc = jnp.where(col <= (s * SUB + sub_row), sc, 0.0)
            for hh in range(B_HEADS):
                p_rows[hh][s] = sc[hh * SUB:(hh + 1) * SUB, :]

        vb_b = vb.astype(BF16)
        for hh in range(B_HEADS):
            sl = slice(hh * B_VAL_DIM, (hh + 1) * B_VAL_DIM)
            ph = jnp.concatenate(p_rows[hh], axis=0).astype(BF16)
            oh = o[:, sl] + _dot(ph, vb_b[:, sl])
            on = _rms(oh, hg_ref[hh])
            ogh = og[:, sl]
            mix_ref[rows, A_WIDTH + hh * B_VAL_DIM:A_WIDTH + (hh + 1) * B_VAL_DIM] = (
                on * (ogh * _sigmoid(ogh))).astype(BF16)

        k_dec = (k * jnp.exp(g_last - g_cum)).astype(BF16)
        upd = _dot(vb.T.astype(BF16), k_dec)
        st_ref[...] = jnp.exp(g_last) * st + jnp.where(bd_mask, upd, 0.0)
        return carry

    lax.fori_loop(0, n_chunks, chunk_body, 0, unroll=True)
    o_ref[...] = x_ref[...] + _dot(mix_ref[...], wo_ref[...])


def _even_mixer(x, nrm, win, gup, gb, ws, bs, lng, lnb, hg, wo, wcasts, *, seq, tm):
    t = x.shape[0]
    kern = functools.partial(_even_mixer_kernel, tiles_per_batch=seq // tm, n_chunks=tm // CHUNK,
                             n_casts=len(wcasts))
    casts = [_side_cast_specs_1d(w, t // tm) for w in wcasts]
    return pl.pallas_call(
        kern,
        out_shape=[jax.ShapeDtypeStruct((t, D_MODEL), F32)] + [c[2] for c in casts],
        grid=(t // tm,),
        in_specs=[
            pl.BlockSpec((tm, D_MODEL), lambda i: (i, 0)),
            _const_spec(nrm.shape), _resident_spec(win.shape), _const_spec(gup.shape), _const_spec(gb.shape),
            _const_spec(ws.shape), _const_spec(bs.shape), _const_spec(lng.shape), _const_spec(lnb.shape),
            _const_spec(hg.shape), _resident_spec(wo.shape),
        ] + [c[0] for c in casts],
        out_specs=[pl.BlockSpec((tm, D_MODEL), lambda i: (i, 0))] + [c[1] for c in casts],
        scratch_shapes=[
            pltpu.VMEM((tm, E_END), F32),
            pltpu.VMEM((tm, D_MODEL), BF16),
            pltpu.VMEM((B_V_WIDTH, B_QK_WIDTH), F32),
        ],
        compiler_params=pltpu.CompilerParams(dimension_semantics=("arbitrary",), vmem_limit_bytes=VMEM_LIMIT),
        name="even_mixer",
    )(x, nrm, win, gup, gb, ws, bs, lng, lnb, hg, wo, *wcasts)


def _dense_ffn_kernel(x_ref, nrm_ref, w1_ref, w3_ref, w2_ref, wcast_ref, o_ref, wcast_out_ref):
    wcast_out_ref[...] = wcast_ref[:, :wcast_out_ref.shape[1]].astype(BF16)
    x = x_ref[...]
    h = _rms(x, nrm_ref[...]).astype(BF16)
    a = _dot(h, w1_ref[...])
    b = _dot(h, w3_ref[...])
    o_ref[...] = x + _dot((a * _sigmoid(a) * b).astype(BF16), w2_ref[...])


def _resident_spec(shape):
    nd = len(shape)
    return pl.BlockSpec(shape, lambda *_: (0,) * nd, pipeline_mode=pl.Buffered(1))


def _dense_ffn(x, nrm, w1, w3, w2, wcast, wcast_cols, *, tm):
    t = x.shape[0]
    n_steps = t // tm
    row_spec = pl.BlockSpec((tm, D_MODEL), lambda i: (i, 0))
    rows = wcast.shape[0] // n_steps
    assert rows * n_steps == wcast.shape[0] and rows % 16 == 0
    return pl.pallas_call(
        _dense_ffn_kernel,
        out_shape=[jax.ShapeDtypeStruct((t, D_MODEL), F32), jax.ShapeDtypeStruct((wcast.shape[0], wcast_cols), BF16)],
        grid=(n_steps,),
        in_specs=[row_spec, _const_spec(nrm.shape), _resident_spec(w1.shape), _resident_spec(w3.shape),
                  _resident_spec(w2.shape), pl.BlockSpec((rows, wcast.shape[1]), lambda i: (i, 0))],
        out_specs=[row_spec, pl.BlockSpec((rows, wcast_cols), lambda i: (i, 0))],
        compiler_params=pltpu.CompilerParams(dimension_semantics=("arbitrary",), vmem_limit_bytes=VMEM_LIMIT),
        name="dense_ffn",
    )(x, nrm, w1, w3, w2, wcast)


def _head_rms(x, gain):
    lo = lax.broadcasted_iota(jnp.int32, (1, LANES), 1) < C_HEAD_DIM
    outs = []
    for t in range(C_WIDTH // LANES):
        xt = x[:, t * LANES:(t + 1) * LANES]
        sq = xt * xt
        s_lo = jnp.sum(jnp.where(lo, sq, 0.0), axis=-1, keepdims=True)
        s_hi = jnp.sum(jnp.where(lo, 0.0, sq), axis=-1, keepdims=True)
        inv = jnp.where(lo, lax.rsqrt(s_lo * (1.0 / C_HEAD_DIM) + EPS), lax.rsqrt(s_hi * (1.0 / C_HEAD_DIM) + EPS))
        outs.append(xt * inv)
    return jnp.concatenate(outs, axis=-1) * gain


N_PAIRS = C_HEADS // 2
K_EXT = 2 * LANES
BIAS_PARTS = 3


def _bias_placement():
    src = jnp.arange(BIAS_PARTS * LANES)
    part, head = src // LANES, src % LANES
    dst = (head // 2) * LANES + BIAS_PARTS * (head % 2) + part
    hit = (dst[:, None] == jnp.arange(N_PAIRS * LANES)[None, :]) & (head < C_HEADS)[:, None]
    return hit.astype(BF16)


def _odd_inproj_kernel(x_ref, nrm_ref, w_ref, wf_ref, fb_ref, qg_ref, kg_ref, place_ref,
                       q_ref, k_ref, vt_ref, gate_ref, z_ref, c_ref, carry_ref, *, tiles_per_batch, n_chunks, tk):
    i = pl.program_id(0)

    @pl.when(i % tiles_per_batch == 0)
    def _():
        carry_ref[...] = jnp.zeros_like(carry_ref)

    h = _rms(x_ref[...], nrm_ref[...]).astype(BF16)
    z_ref[:, :O_F] = _dot(h, w_ref[...])
    z_ref[:, O_F:] = _dot(h, wf_ref[...])
    qn = _head_rms(z_ref[:, O_Q:O_K], qg_ref[...]) * (C_HEAD_DIM ** -0.5 * LOG2E)
    for p in range(N_PAIRS):
        q_ref[p] = qn[:, p * LANES:(p + 1) * LANES].T.astype(BF16)
    kn = _head_rms(z_ref[:, O_K:O_V], kg_ref[...]).astype(BF16)
    for p in range(N_PAIRS):
        for kb in range(vt_ref.shape[1]):
            blk = z_ref[kb * tk:(kb + 1) * tk, O_V + p * LANES:O_V + (p + 1) * LANES]
            vt_ref[p, kb] = blk.T.astype(BF16)
    gate_ref[...] = _sigmoid(z_ref[:, O_OG:O_F]).astype(BF16)

    row = lax.broadcasted_iota(jnp.int32, (CHUNK, CHUNK), 0)
    col = lax.broadcasted_iota(jnp.int32, (CHUNK, CHUNK), 1)
    tril_b = (col <= row).astype(BF16)
    carry = carry_ref[...]
    for c in range(n_chunks):
        rows = slice(c * CHUNK, (c + 1) * CHUNK)
        log_f = _log_sigmoid(z_ref[rows, O_F:O_END] + fb_ref[...])
        cs = _cumsum_rows(tril_b, log_f) + carry
        c_ref[rows, :] = cs
        carry = cs[CHUNK - 1:CHUNK, :]
    carry_ref[...] = carry

    bias = _dot(jnp.concatenate(_split3(c_ref[...] * LOG2E), axis=1), place_ref[...]).astype(BF16)
    for p in range(N_PAIRS):
        k_ref[:, p * K_EXT:p * K_EXT + LANES] = kn[:, p * LANES:(p + 1) * LANES]
        k_ref[:, p * K_EXT + LANES:(p + 1) * K_EXT] = bias[:, p * LANES:(p + 1) * LANES]


def _odd_inproj(x, nrm, w, wf, fb, qg, kg, *, seq, tm, tk):
    t = x.shape[0]
    tiles_per_batch = seq // tm
    kern = functools.partial(_odd_inproj_kernel, tiles_per_batch=tiles_per_batch, n_chunks=tm // CHUNK, tk=tk)
    row_spec = pl.BlockSpec((tm, C_WIDTH), lambda i: (i, 0))
    kext_spec = pl.BlockSpec((tm, N_PAIRS * K_EXT), lambda i: (i, 0))
    vt_spec = pl.BlockSpec((None, N_PAIRS, tm // tk, LANES, tk),
                           lambda i: (i // tiles_per_batch, 0, i % tiles_per_batch, 0, 0))
    place = _bias_placement()
    wide = jax.ShapeDtypeStruct((t, C_WIDTH), BF16)
    return pl.pallas_call(
        kern,
        out_shape=[jax.ShapeDtypeStruct((t // seq, N_PAIRS, LANES, seq), BF16),
                   jax.ShapeDtypeStruct((t, N_PAIRS * K_EXT), BF16),
                   jax.ShapeDtypeStruct((t // seq, N_PAIRS, seq // tk, LANES, tk), BF16), wide],
        grid=(t // tm,),
        in_specs=[row_spec, _const_spec(nrm.shape), _resident_spec(w.shape), _const_spec(wf.shape),
                  _const_spec(fb.shape), _const_spec(qg.shape), _const_spec(kg.shape), _const_spec(place.shape)],
        out_specs=[pl.BlockSpec((None, N_PAIRS, LANES, tm), lambda i: (i // tiles_per_batch, 0, 0, i % tiles_per_batch)),
                   kext_spec, vt_spec, row_spec],
        scratch_shapes=[pltpu.VMEM((tm, O_END), F32), pltpu.VMEM((tm, LANES), F32), pltpu.VMEM((1, LANES), F32)],
        compiler_params=pltpu.CompilerParams(dimension_semantics=("arbitrary",), vmem_limit_bytes=VMEM_LIMIT),
        name="odd_inproj",
    )(x, nrm, w, wf, fb, qg, kg, place)


NEG_BIG = -1e30


V_ROWS = 80


def _fox_kernel(q_ref, k_ref, vt_ref, gate_ref, *rest, tq, tk, n_casts):
    cast_refs, o_ref, cast_out_refs = rest[:n_casts], rest[n_casts], rest[n_casts + 1:-3]
    s_ref, m_ref, acc_ref = rest[-3:]
    for src, dst in zip(cast_refs, cast_out_refs):
        dst[...] = src[...].astype(BF16)
    qi = pl.program_id(2)
    ng = tq // tk
    key_i = lax.broadcasted_iota(jnp.int32, (tk, tk), 0)
    qry_i = lax.broadcasted_iota(jnp.int32, (tk, tk), 1)
    causal = key_i <= qry_i
    chains = [(hh, r) for r in range(ng) for hh in range(2)]
    feat = lax.broadcasted_iota(jnp.int32, (LANES, 1), 0)
    qms = []
    for hh, r in chains:
        q_t = q_ref[:, r * tk:(r + 1) * tk]
        own = (feat < C_HEAD_DIM) if hh == 0 else (feat >= C_HEAD_DIM)
        qh = jnp.where(own, q_t, jnp.zeros_like(q_t))
        pick = (feat >= BIAS_PARTS * hh) & (feat < BIAS_PARTS * (hh + 1))
        minus_one = jnp.broadcast_to(jnp.where(pick, -1.0, 0.0).astype(BF16), qh.shape)
        qms.append(jnp.concatenate([qh, minus_one], axis=0))

    def key_rows(j):
        return pl.ds(pl.multiple_of(j * tk, tk), tk)

    def scores_to_scratch(j, slot, live, modes):
        kb = k_ref[key_rows(j), :]
        raw = [_dot(kb, qms[idx]) for idx in live]
        maxes = []
        for s, idx in zip(raw, live):
            hh, r = chains[idx]
            if modes[r] == "diag":
                s = jnp.where(causal, s, NEG_BIG)
            s_ref[slot, idx] = s
            maxes.append(jnp.max(s, axis=0, keepdims=True))
        return maxes

    ones_rows = (lax.broadcasted_iota(jnp.int32, (V_ROWS - C_HEAD_DIM, tk), 0) == 0).astype(BF16)

    def softmax_pv(j, slot, maxes, live):
        vt_pair = vt_ref[j]
        vtb = [jnp.concatenate([vt_pair[hh * C_HEAD_DIM:(hh + 1) * C_HEAD_DIM, :], ones_rows], axis=0)
               for hh in range(2)]
        probs = []
        for bm, idx in zip(maxes, live):
            m = m_ref[idx]
            m_new = jnp.maximum(m, bm)
            p = jnp.exp2(s_ref[slot, idx] - m_new)
            m_ref[idx] = m_new
            probs.append((jnp.exp2(m - m_new), p.astype(BF16)))
        for (alpha, p), idx in zip(probs, live):
            hh, r = chains[idx]
            acc_ref[idx] = alpha * acc_ref[idx] + _dot(vtb[hh], p)

    assert ng % 2 == 0
    all_chains = list(range(len(chains)))
    m_ref[...] = jnp.full(m_ref.shape, NEG_BIG, F32)
    acc_ref[...] = jnp.zeros(acc_ref.shape, F32)
    n_full = qi * ng
    full_modes = ("full",) * ng

    def diag_modes(g):
        return tuple("skip" if r < g else ("diag" if r == g else "full") for r in range(ng))

    def live_chains(g):
        return [idx for idx, (hh, r) in enumerate(chains) if r >= g]

    def trip(j, carry, next_modes):
        mx = carry
        for b in range(ng):
            mx_next = scores_to_scratch(j + b + 1, (b + 1) % 2, all_chains, full_modes if b + 1 < ng else next_modes)
            softmax_pv(j + b, b % 2, mx, all_chains)
            mx = mx_next
        return tuple(mx)

    def with_full_blocks(_):
        first = scores_to_scratch(0, 0, all_chains, full_modes)
        carry = lax.fori_loop(0, qi - 1, lambda i, c: trip(ng * i, c, full_modes), tuple(first))
        return trip(n_full - ng, carry, diag_modes(0))

    def no_full_blocks(_):
        return tuple(scores_to_scratch(0, 0, all_chains, diag_modes(0)))

    mx = lax.cond(qi > 0, with_full_blocks, no_full_blocks, 0)
    for g in range(ng):
        if g + 1 < ng:
            mx_next = scores_to_scratch(n_full + g + 1, (g + 1) % 2, live_chains(g + 1), diag_modes(g + 1))
        softmax_pv(n_full + g, g % 2, mx, live_chains(g))
        if g + 1 < ng:
            mx = mx_next

    for r in range(ng):
        parts = []
        for hh in range(2):
            acc = acc_ref[chains.index((hh, r))]
            parts.append(acc[:C_HEAD_DIM, :] / acc[C_HEAD_DIM:C_HEAD_DIM + 1, :])
        o = jnp.concatenate(parts, axis=0).T
        rows = slice(r * tk, (r + 1) * tk)
        o_ref[rows, :] = (o * gate_ref[rows, :].astype(F32)).astype(BF16)


def _fox_attention(q, k_ext, vt, gate, wcasts, *, batch, seq, tq, tk):
    t = batch * seq
    nq = seq // tq
    kern = functools.partial(_fox_kernel, tq=tq, tk=tk, n_casts=len(wcasts))
    casts = [_side_cast_specs(w, batch * N_PAIRS * nq, lambda b, p, i: ((b * N_PAIRS + p) * nq + i, 0))
             for w in wcasts]
    return pl.pallas_call(
        kern,
        out_shape=[jax.ShapeDtypeStruct((t, C_WIDTH), BF16)] + [c[2] for c in casts],
        grid=(batch, N_PAIRS, nq),
        in_specs=[
            pl.BlockSpec((None, None, LANES, tq), lambda b, p, i: (b, p, 0, i)),
            pl.BlockSpec((seq, K_EXT), lambda b, p, i: (b, p)),
            pl.BlockSpec((None, None, seq // tk, LANES, tk), lambda b, p, i: (b, p, 0, 0, 0)),
            pl.BlockSpec((tq, LANES), lambda b, p, i: (b * nq + i, p)),
        ] + [c[0] for c in casts],
        out_specs=[pl.BlockSpec((tq, LANES), lambda b, p, i: (b * nq + i, p))] + [c[1] for c in casts],
        scratch_shapes=[pltpu.VMEM((2, 2 * (tq // tk), tk, tk), F32), pltpu.VMEM((2 * (tq // tk), 1, tk), F32),
                        pltpu.VMEM((2 * (tq // tk), V_ROWS, tk), F32)],
        compiler_params=pltpu.CompilerParams(dimension_semantics=("arbitrary", "arbitrary", "arbitrary"),
                                             vmem_limit_bytes=VMEM_LIMIT),
        name="fox_attn",
    )(q, k_ext, vt, gate, *wcasts)


def _store_token_tiles(dst_ref, val, n_rows):
    for s in range(ROW_TILE):
        dst_ref[pl.ds(s, n_rows, stride=ROW_TILE), :] = val[:, s * LANES:(s + 1) * LANES]


def _load_token_tiles(src_ref, n_rows):
    return jnp.concatenate([src_ref[pl.ds(s, n_rows, stride=ROW_TILE), :] for s in range(ROW_TILE)], axis=1)


def _odd_out_kernel(x_ref, a_ref, wo_ref, nrm_ref, r_ref, x3_ref, h_ref, route_ref):
    x3 = x_ref[...] + _dot(a_ref[...], wo_ref[...])
    x3_ref[...] = x3
    h = _rms(x3, nrm_ref[...])
    _store_token_tiles(h_ref, h, h.shape[0])

    h_hi = h.astype(BF16)
    h_lo = (h - h_hi.astype(F32)).astype(BF16)
    r = r_ref[...]
    r_hi = r.astype(BF16)
    r_lo = (r - r_hi.astype(F32)).astype(BF16)
    n = h.shape[0]
    prod = _dot(jnp.concatenate([h_hi, h_lo], axis=0), jnp.concatenate([r_hi, r_lo], axis=1))
    logits = prod[:n, :LANES] + (prod[n:, :LANES] + prod[:n, LANES:])

    lane = lax.broadcasted_iota(jnp.int32, logits.shape, 1).astype(F32)
    neg_inf = jnp.float32(-jnp.inf)
    lg = jnp.where(lane < N_EXPERTS, logits, neg_inf)
    m1 = jnp.max(lg, axis=-1, keepdims=True)
    i1 = jnp.min(jnp.where(lg == m1, lane, float(LANES)), axis=-1, keepdims=True)
    lg2 = jnp.where(lane == i1, neg_inf, lg)
    m2 = jnp.max(lg2, axis=-1, keepdims=True)
    i2 = jnp.min(jnp.where(lg2 == m2, lane, float(LANES)), axis=-1, keepdims=True)
    e2 = jnp.exp(m2 - m1)
    g1 = 1.0 / (1.0 + e2)
    g2 = e2 / (1.0 + e2)
    route_ref[...] = jnp.where(lane == 0, i1, jnp.where(lane == 1, i2, jnp.where(lane == 2, g1,
                               jnp.where(lane == 3, g2, 0.0))))


def _odd_out(x, a, wo, nrm, router, *, tm):
    t = x.shape[0]
    row_spec = pl.BlockSpec((tm, D_MODEL), lambda i: (i, 0))
    return pl.pallas_call(
        _odd_out_kernel,
        out_shape=[jax.ShapeDtypeStruct((t, D_MODEL), F32), jax.ShapeDtypeStruct((t * ROW_TILE, LANES), F32),
                   jax.ShapeDtypeStruct((t, LANES), F32)],
        grid=(t // tm,),
        in_specs=[row_spec, row_spec, _const_spec(wo.shape), _const_spec(nrm.shape), _const_spec(router.shape)],
        out_specs=[row_spec, pl.BlockSpec((tm * ROW_TILE, LANES), lambda i: (i, 0)),
                   pl.BlockSpec((tm, LANES), lambda i: (i, 0))],
        compiler_params=pltpu.CompilerParams(dimension_semantics=("arbitrary",), vmem_limit_bytes=VMEM_LIMIT),
        name="odd_out",
    )(x, a, wo, nrm, router)


def _row_gather_copy(src_hbm, src_row, dst_ref, dst_row, sem):
    return pltpu.make_async_copy(
        src_hbm.at[pl.ds(pl.multiple_of(src_row * ROW_TILE, ROW_TILE), ROW_TILE), :],
        dst_ref.at[pl.ds(pl.multiple_of(dst_row * ROW_TILE, ROW_TILE), ROW_TILE), :],
        sem)


GATHER_UNROLL = 8
MXU_TILE = 256
MOE_FF_CHUNKS = 2
assert (D_FF_EXPERT // MOE_FF_CHUNKS) % MXU_TILE == 0


def _start_row_gathers(src_hbm, idx_ref, idx_row, first, count, dst_ref, sem, *, inline):
    if inline:
        for u in range(count):
            _row_gather_copy(src_hbm, idx_ref[idx_row, first + u], dst_ref, first + u, sem).start(priority=u % 2)
        return

    def issue(r2, c):
        for u in range(2):
            r = 2 * r2 + u
            _row_gather_copy(src_hbm, idx_ref[idx_row, r], dst_ref, r, sem).start(priority=u)
        return c

    assert first % 2 == 0 and count % 2 == 0
    lax.fori_loop(first // 2, (first + count) // 2, issue, 0, unroll=GATHER_UNROLL // 2)


def _wait_row_gathers(src_hbm, dst_ref, sem):
    pltpu.make_async_copy(src_hbm.at[pl.ds(0, dst_ref.shape[0]), :], dst_ref, sem).wait()


def _moe_ffn_kernel(te_ref, nu_ref, tok_ref, tok_next_ref, h_hbm, w1_ref, w3_ref, w2_ref, o_ref, xs_ref, buf_ref,
                    sem, *, tm, n_chunks):
    i = pl.program_id(0)
    n_used = nu_ref[0]
    used = i < n_used
    slot = i % 2

    @pl.when(i == 0)
    def _():
        _start_row_gathers(h_hbm, tok_ref, 0, 0, tm, buf_ref.at[0], sem.at[0], inline=False)

    @pl.when(i <= n_used)
    def _():
        _wait_row_gathers(h_hbm, buf_ref.at[slot], sem.at[slot])
        xs_ref[...] = _load_token_tiles(buf_ref.at[slot], tm).astype(BF16)

    @pl.when(used)
    def _():
        n_groups = 3 * n_chunks
        per_group = tm // n_groups
        starts = [(g * per_group, per_group if g + 1 < n_groups else tm - g * per_group) for g in range(n_groups)]

        def prefetch(g):
            first, count = starts[g]
            _start_row_gathers(h_hbm, tok_next_ref, 0, first, count, buf_ref.at[1 - slot], sem.at[1 - slot],
                               inline=True)

        x = xs_ref[...]
        tf = D_FF_EXPERT // n_chunks
        y = None
        for c in range(n_chunks):
            cols = slice(c * tf, (c + 1) * tf)
            prefetch(3 * c)
            a = _dot(x, w1_ref[:, cols])
            prefetch(3 * c + 1)
            b = _dot(x, w3_ref[:, cols])
            prefetch(3 * c + 2)
            part = _dot((a * _sigmoid(a) * b).astype(BF16), w2_ref[cols, :])
            y = part if y is None else y + part
        _store_token_tiles(o_ref, y, tm)

    @pl.when(jnp.logical_not(used))
    def _():
        o_ref[...] = jnp.zeros_like(o_ref)


def _moe_ffn(tile_expert, n_used, row_tok, h_tiles, w1, w3, w2, *, tm, n_chunks):
    n_tiles = row_tok.shape[0] - 1

    def expert_spec(shape):
        return pl.BlockSpec((None,) + shape, lambda i, te, nu: (te[i], 0, 0), pipeline_mode=pl.Buffered(1))

    grid_spec = pltpu.PrefetchScalarGridSpec(
        num_scalar_prefetch=2,
        grid=(n_tiles,),
        in_specs=[
            pl.BlockSpec((None, 1, tm), lambda i, te, nu: (i, 0, 0), memory_space=pltpu.SMEM),
            pl.BlockSpec((None, 1, tm), lambda i, te, nu: (i + 1, 0, 0), memory_space=pltpu.SMEM),
            pl.BlockSpec(memory_space=pl.ANY),
            expert_spec((D_MODEL, D_FF_EXPERT)), expert_spec((D_MODEL, D_FF_EXPERT)),
            expert_spec((D_FF_EXPERT, D_MODEL)),
        ],
        out_specs=pl.BlockSpec((tm * ROW_TILE, LANES), lambda i, te, nu: (i, 0)),
        scratch_shapes=[pltpu.VMEM((tm, D_MODEL), BF16), pltpu.VMEM((2, tm * ROW_TILE, LANES), F32),
                        pltpu.SemaphoreType.DMA((2,))],
    )
    return pl.pallas_call(
        functools.partial(_moe_ffn_kernel, tm=tm, n_chunks=n_chunks),
        out_shape=jax.ShapeDtypeStruct((n_tiles * tm * ROW_TILE, LANES), F32),
        grid_spec=grid_spec,
        compiler_params=pltpu.CompilerParams(dimension_semantics=("arbitrary",), vmem_limit_bytes=VMEM_LIMIT),
        name="moe_ffn",
    )(tile_expert, n_used, row_tok, row_tok, h_tiles, w1, w3, w2)


def _moe_combine_kernel(pos_ref, pos_next_ref, x_ref, route_ref, nrm_ref, y_hbm, o_ref, buf_ref, sem, *, tm):
    i = pl.program_id(0)
    slot = i % 2

    def start_tile(idx_ref, s, inline):
        for k in range(2):
            _start_row_gathers(y_hbm, idx_ref, k, 0, tm, buf_ref.at[s, k], sem.at[s, k], inline=inline)

    @pl.when(i == 0)
    def _():
        start_tile(pos_ref, 0, False)

    @pl.when(i + 1 < pl.num_programs(0))
    def _():
        start_tile(pos_next_ref, 1 - slot, True)

    for k in range(2):
        _wait_row_gathers(y_hbm, buf_ref.at[slot, k], sem.at[slot, k])
    g1 = route_ref[:, 2:3]
    g2 = route_ref[:, 3:4]
    x = x_ref[...] + (g1 * _load_token_tiles(buf_ref.at[slot, 0], tm) + g2 * _load_token_tiles(buf_ref.at[slot, 1], tm))
    o_ref[...] = _rms(x, nrm_ref[...])


def _moe_combine(pos, x, route, nrm, y_tiles, *, tm):
    t = x.shape[0]
    n = t // tm
    row_spec = pl.BlockSpec((tm, D_MODEL), lambda i: (i, 0))
    return pl.pallas_call(
        functools.partial(_moe_combine_kernel, tm=tm),
        out_shape=jax.ShapeDtypeStruct((t, D_MODEL), F32),
        grid=(n,),
        in_specs=[pl.BlockSpec((None, 2, tm), lambda i: (i, 0, 0), memory_space=pltpu.SMEM),
                  pl.BlockSpec((None, 2, tm), lambda i: (jnp.minimum(i + 1, n - 1), 0, 0), memory_space=pltpu.SMEM),
                  row_spec, pl.BlockSpec((tm, LANES), lambda i: (i, 0)), _const_spec(nrm.shape),
                  pl.BlockSpec(memory_space=pl.ANY)],
        out_specs=row_spec,
        scratch_shapes=[pltpu.VMEM((2, 2, tm * ROW_TILE, LANES), F32), pltpu.SemaphoreType.DMA((2, 2))],
        compiler_params=pltpu.CompilerParams(dimension_semantics=("arbitrary",), vmem_limit_bytes=VMEM_LIMIT),
        name="moe_combine",
    )(pos, pos, x, route, nrm, y_tiles)


SCATTER_CHUNK = 2048


def _invert_rows_kernel(pos_ref, zeros_hbm, out_ref, sem, *, n_tokens, chunk):
    j = pl.program_id(0)

    @pl.when(j == 0)
    def _():
        fill = pltpu.make_async_copy(zeros_hbm, out_ref, sem.at[0])
        fill.start()
        fill.wait()

    base = lax.rem(j * chunk, n_tokens)

    def place(u, c):
        out_ref[pos_ref[0, u]] = base + u
        return c

    lax.fori_loop(0, chunk, place, 0, unroll=32)


def _invert_rows(pos, n_rows, n_tokens):
    chunk = min(SCATTER_CHUNK, n_tokens)
    n_chunks = pos.shape[0] // chunk
    assert n_chunks * chunk == pos.shape[0] and n_tokens % chunk == 0
    return pl.pallas_call(
        functools.partial(_invert_rows_kernel, n_tokens=n_tokens, chunk=chunk),
        out_shape=jax.ShapeDtypeStruct((n_rows,), jnp.int32),
        grid=(n_chunks,),
        in_specs=[pl.BlockSpec((None, 1, chunk), lambda j: (j, 0, 0), memory_space=pltpu.SMEM),
                  pl.BlockSpec(memory_space=pl.ANY)],
        out_specs=pl.BlockSpec(memory_space=pltpu.SMEM),
        scratch_shapes=[pltpu.SemaphoreType.DMA((1,))],
        compiler_params=pltpu.CompilerParams(dimension_semantics=("arbitrary",)),
        name="invert_rows",
    )(pos.reshape(n_chunks, 1, chunk), jnp.zeros((n_rows,), jnp.int32))


def _routing_tables(idx1, idx2, *, tm, n_tiles):
    t = idx1.shape[0]
    e_flat = jnp.concatenate([idx1, idx2])
    onehot = (e_flat[:, None] == jnp.arange(N_EXPERTS, dtype=jnp.int32)[None, :]).astype(jnp.int32)
    csum = jnp.cumsum(onehot, axis=0)
    rank = jnp.sum((csum - onehot) * onehot, axis=1)
    counts = csum[-1]
    tiles_e = (counts + tm - 1) // tm
    tile_end = jnp.cumsum(tiles_e)
    tile_start = tile_end - tiles_e
    pos = jnp.sum(onehot * tile_start[None, :], axis=1) * tm + rank
    n_used = tile_end[-1]
    tile_ids = jnp.arange(n_tiles, dtype=jnp.int32)
    te = jnp.sum((tile_ids[:, None] >= tile_end[None, :]).astype(jnp.int32), axis=1)
    te_last = jnp.sum((n_used - 1 >= tile_end).astype(jnp.int32))
    tile_expert = jnp.where(tile_ids < n_used, te, te_last).astype(jnp.int32)
    row_tok = _invert_rows(pos.astype(jnp.int32), n_tiles * tm, t)
    return pos[:t], pos[t:], row_tok, tile_expert, n_used.reshape(1).astype(jnp.int32)


def _pad_cols(w, n):
    return jnp.pad(w, ((0, 0), (0, n - w.shape[1])))


def kernel(x, even_norm_mix, even_w_in, even_gate_up, even_gate_bias, even_w_s, even_b_s, even_ln_g, even_ln_b,
           even_head_g, even_w_o, even_norm_ffn, even_ffn_w1, even_ffn_w3, even_ffn_w2, odd_norm_mix, odd_w_in,
           odd_forget_bias, odd_q_g, odd_k_g, odd_w_o, odd_norm_ffn, odd_router, odd_exp_w1, odd_exp_w3,
           odd_exp_w2, final_norm):
    batch, seq, d = x.shape
    assert d == D_MODEL and seq % CHUNK == 0
    t = batch * seq
    xt = x.reshape(t, d)
    tiles = _tiles(seq)
    tm = tiles.row

    w_in = even_w_in[0]
    even_cuts = [A_WIDTH, 2 * A_WIDTH, 2 * A_WIDTH + B_QK_WIDTH, 2 * A_WIDTH + 2 * B_QK_WIDTH,
                 2 * A_WIDTH + 2 * B_QK_WIDTH + B_GATE_RANK, 2 * A_WIDTH + 2 * B_QK_WIDTH + B_GATE_RANK + B_V_WIDTH]
    u_w, v_w, q_w, k_w, g_w, vb_w, og_w = jnp.split(w_in, even_cuts, axis=1)
    win_e = jnp.concatenate([u_w, v_w, q_w, k_w, vb_w, og_w, _pad_cols(g_w, LANES)], axis=1).astype(BF16)
    gup = jnp.pad(even_gate_up[0], ((0, LANES - B_GATE_RANK), (0, 0))).astype(BF16)
    gb = even_gate_bias[0].reshape(1, B_QK_WIDTH)
    tril = jnp.tril(jnp.ones((CHUNK, CHUNK), dtype=bool))
    ws = jnp.where(tril[None], even_w_s[0], 0.0).astype(BF16)
    bs = jnp.broadcast_to(even_b_s[0][:, :, None], (A_GROUPS, CHUNK, LANES))
    lng = even_ln_g[0].reshape(1, A_WIDTH)
    lnb = even_ln_b[0].reshape(1, A_WIDTH)
    hg = even_head_g[0].reshape(B_HEADS, 1, B_VAL_DIM)
    ew1 = odd_exp_w1[0].reshape(N_EXPERTS * D_MODEL, D_FF_EXPERT)
    ew3 = odd_exp_w3[0].reshape(N_EXPERTS * D_MODEL, D_FF_EXPERT)
    ew2 = odd_exp_w2[0].reshape(N_EXPERTS * D_FF_EXPERT, D_MODEL)
    x1, ew1_b, fw1_b, fw3_b, fw2_b = _even_mixer(
        xt, even_norm_mix[0].reshape(1, d), win_e, gup, gb, ws, bs, lng, lnb, hg, even_w_o[0].astype(BF16),
        [ew1, even_ffn_w1[0], even_ffn_w3[0], even_ffn_w2[0]], seq=seq, tm=tm)
    x2, win_o = _dense_ffn(x1, even_norm_ffn[0].reshape(1, d), fw1_b, fw3_b, fw2_b, odd_w_in[0], O_F, tm=tm)

    wf_o = _pad_cols(odd_w_in[0][:, O_F:], LANES).astype(BF16)
    fb = jnp.pad(odd_forget_bias[0], (0, LANES - C_HEADS)).reshape(1, LANES)
    qg = jnp.tile(odd_q_g[0], C_HEADS).reshape(1, C_WIDTH)
    kg = jnp.tile(odd_k_g[0], C_HEADS).reshape(1, C_WIDTH)
    q_t, k_ext, vt, gate = _odd_inproj(x2, odd_norm_mix[0].reshape(1, d), win_o, wf_o, fb, qg, kg, seq=seq, tm=tm,
                                       tk=tiles.attn_k)
    attn, ew2_b, ew3_b = _fox_attention(q_t, k_ext, vt, gate, [ew2, ew3], batch=batch, seq=seq, tq=tiles.attn_q,
                                        tk=tiles.attn_k)

    router = _pad_cols(odd_router[0], LANES)
    x3, h_tiles, route = _odd_out(x2, attn, odd_w_o[0].astype(BF16), odd_norm_ffn[0].reshape(1, d), router,
                                  tm=tiles.out)

    tm_moe, tm_comb = tiles.moe, tiles.combine
    n_tiles = (2 * t) // tm_moe + N_EXPERTS + 1
    idx1 = route[:, 0].astype(jnp.int32)
    idx2 = route[:, 1].astype(jnp.int32)
    pos1, pos2, row_tok, tile_expert, n_used = _routing_tables(idx1, idx2, tm=tm_moe, n_tiles=n_tiles + 1)
    y_tiles = _moe_ffn(tile_expert[:n_tiles], n_used, row_tok.reshape(n_tiles + 1, 1, tm_moe), h_tiles,
                       ew1_b.reshape(N_EXPERTS, D_MODEL, D_FF_EXPERT), ew3_b.reshape(N_EXPERTS, D_MODEL, D_FF_EXPERT),
                       ew2_b.reshape(N_EXPERTS, D_FF_EXPERT, D_MODEL), tm=tm_moe, n_chunks=MOE_FF_CHUNKS)
    pos = jnp.stack([pos1.reshape(t // tm_comb, tm_comb), pos2.reshape(t // tm_comb, tm_comb)], axis=1)
    out = _moe_combine(pos, x3, route, final_norm.reshape(1, d), y_tiles, tm=tm_comb)
    return out.reshape(batch, seq, d)
```
